```python
import jax, jax.numpy as jnp
from jax import lax
import numpy as np

D_MODEL = 1024
BATCH = 8
SEQ = 2048
DEPTH = 2

CTX_LEN = 256
GRID_W = 64
HEAD_DIM = 64
ATTN_HEADS = 8
ATTN_KV_HEADS = 2
ATTN_GROUP = ATTN_HEADS // ATTN_KV_HEADS
WINDOW = 128
ATTN_BLOCK = 128
RET_HEADS = 8
RET_DK = 64
RET_DV = 64
RET_CHUNK = 128
FOURIER_GROUPS = 4
FOURIER_DIM = 128
ATTN_WIDTH = ATTN_HEADS * HEAD_DIM
KV_WIDTH = ATTN_KV_HEADS * HEAD_DIM
RET_QK_WIDTH = RET_HEADS * RET_DK
RET_WIDTH = RET_HEADS * RET_DV
FOURIER_WIDTH = FOURIER_GROUPS * FOURIER_DIM
N_BRANCH = 3
IN_WIDTHS = (ATTN_WIDTH, KV_WIDTH, KV_WIDTH, RET_QK_WIDTH, RET_QK_WIDTH, RET_WIDTH, RET_WIDTH, FOURIER_WIDTH, N_BRANCH * D_MODEL)
IN_COLS = sum(IN_WIDTHS)
N_GROUPS = 4
EXPERTS_PER_GROUP = 8
N_EXPERTS = N_GROUPS * EXPERTS_PER_GROUP
TOP_K = 2
EXPERT_HIDDEN = 512
MOE_BLOCK = 128
ROPE_BASE = 10000.0
NORM_EPS = 1e-6
GN_EPS = 1e-5

kernel_name = "hybrid_flow_backbone"

F32 = jnp.float32


def split_points():
    pts, acc = [], 0
    for w in IN_WIDTHS[:-1]:
        acc += w
        pts.append(acc)
    return pts


def rmsnorm(x, g):
    xf = x.astype(F32)
    y = xf * lax.rsqrt(jnp.mean(xf * xf, axis=-1, keepdims=True) + NORM_EPS)
    return (y * g.astype(F32)).astype(x.dtype)


def modulate(h, shift, scale):
    return h * (1.0 + scale) + shift


def rope_table(pos, n_freq):
    inv = ROPE_BASE ** (-jnp.arange(n_freq, dtype=F32) / n_freq)
    ang = pos[:, None] * inv[None, :]
    return jnp.cos(ang), jnp.sin(ang)


def rotate(x, cos, sin):
    f = x.shape[-1] // 2
    x1, x2 = x[..., :f], x[..., f:]
    c = cos[None, :, None, :].astype(x.dtype)
    s = sin[None, :, None, :].astype(x.dtype)
    return jnp.concatenate([x1 * c - x2 * s, x1 * s + x2 * c], axis=-1)


def axial_rope(x, row_cs, col_cs):
    h = x.shape[-1] // 2
    return jnp.concatenate([rotate(x[..., :h], *row_cs), rotate(x[..., h:], *col_cs)], axis=-1)


def window_attention(q, k, v, kc, vc, sink):
    B, N, H, Dh = q.shape
    L = kc.shape[1]
    nb = N // ATTN_BLOCK
    nw = 3 * ATTN_BLOCK
    scale = Dh ** -0.5
    qb = q.reshape(B, nb, ATTN_BLOCK, ATTN_KV_HEADS, ATTN_GROUP, Dh)

    def windows(t):
        tp = jnp.pad(t, ((0, 0), (ATTN_BLOCK, ATTN_BLOCK), (0, 0), (0, 0)))
        tb = tp.reshape(B, nb + 2, ATTN_BLOCK, ATTN_KV_HEADS, Dh)
        return jnp.concatenate([tb[:, :-2], tb[:, 1:-1], tb[:, 2:]], axis=2)

    kw, vw = windows(k), windows(v)
    kpos = jnp.arange(-ATTN_BLOCK, N + ATTN_BLOCK).reshape(nb + 2, ATTN_BLOCK)
    kpos = jnp.concatenate([kpos[:-2], kpos[1:-1], kpos[2:]], axis=1)
    qpos = jnp.arange(N).reshape(nb, ATTN_BLOCK)
    kp = kpos[:, None, :]
    valid = (jnp.abs(qpos[:, :, None] - kp) <= WINDOW) & (kp >= 0) & (kp < N)
    s_loc = jnp.einsum('bnqhgd,bnkhd->bnhgqk', qb, kw).astype(F32) * scale
    s_loc = jnp.where(valid[None, :, None, None], s_loc, -jnp.inf)
    s_ctx = jnp.einsum('bnqhgd,blhd->bnhgql', qb, kc).astype(F32) * scale
    s_snk = jnp.broadcast_to(sink.astype(F32).reshape(1, 1, ATTN_KV_HEADS, ATTN_GROUP, 1, 1), s_loc.shape[:-1] + (1,))
    p = jax.nn.softmax(jnp.concatenate([s_loc, s_ctx, s_snk], axis=-1), axis=-1).astype(v.dtype)
    o = (jnp.einsum('bnhgqk,bnkhd->bnqhgd', p[..., :nw], vw)
         + jnp.einsum('bnhgql,blhd->bnqhgd', p[..., nw:nw + L], vc))
    return o.reshape(B, N, H * Dh)


def context_attention(q, k, v, sink):
    B, L, H, Dh = q.shape
    qg = q.reshape(B, L, ATTN_KV_HEADS, ATTN_GROUP, Dh)
    s = jnp.einsum('bqhgd,bkhd->bhgqk', qg, k).astype(F32) * (Dh ** -0.5)
    s_snk = jnp.broadcast_to(sink.astype(F32).reshape(1, ATTN_KV_HEADS, ATTN_GROUP, 1, 1), s.shape[:-1] + (1,))
    p = jax.nn.softmax(jnp.concatenate([s, s_snk], axis=-1), axis=-1)[..., :L].astype(v.dtype)
    o = jnp.einsum('bhgqk,bkhd->bqhgd', p, v)
    return o.reshape(B, L, H * Dh)


def fourier_mix(u):
    B, N, _ = u.shape
    ug = u.astype(F32).reshape(B, N, FOURIER_GROUPS, FOURIER_DIM)
    z = jnp.fft.fft2(ug, axes=(1, 3), norm='ortho')
    return jnp.real(z).reshape(B, N, FOURIER_WIDTH).astype(u.dtype)


def retention_chunked(q, k, v, lg, s0, inclusive):
    B, N, H, dk = q.shape
    dv = v.shape[-1]
    nc = N // RET_CHUNK
    qc = q.astype(F32).reshape(B, nc, RET_CHUNK, H, dk)
    kc = k.astype(F32).reshape(B, nc, RET_CHUNK, H, dk)
    vc = v.astype(F32).reshape(B, nc, RET_CHUNK, H, dv)
    i = jnp.arange(RET_CHUNK)
    diff = i[:, None] - i[None, :]
    mask = (diff >= 0) if inclusive else (diff > 0)
    dec = jnp.where(mask[None], jnp.exp(jnp.where(mask, diff, 0).astype(F32)[None] * lg[:, None, None]), 0.0)
    inner = jnp.einsum('bcihd,bcjhd->bchij', qc, kc) * dec[None, None]
    o_inner = jnp.einsum('bchij,bcjhe->bcihe', inner, vc)
    fi = i.astype(F32)
    zeta = jnp.exp((RET_CHUNK - 1 - fi)[:, None] * lg[None, :])
    u = jnp.einsum('bcjhd,jh,bcjhe->bchde', kc, zeta, vc)
    g_chunk = jnp.exp(RET_CHUNK * lg)[None, :, None, None]

    def step(s, u_c):
        return g_chunk * s + u_c, s

    _, s_prev = lax.scan(step, s0.astype(F32), jnp.moveaxis(u, 1, 0))
    s_prev = jnp.moveaxis(s_prev, 0, 1)
    xi = jnp.exp((fi + 1.0)[:, None] * lg[None, :])
    o_cross = jnp.einsum('bcihd,ih,bchde->bcihe', qc, xi, s_prev)
    return (o_inner + o_cross).reshape(B, N, H, dv)


def context_states(k, v, lg_f, lg_b):
    L = k.shape[1]
    m = jnp.arange(L, dtype=F32)
    w_f = jnp.exp((L - 1 - m)[:, None] * lg_f[None, :])
    w_b = jnp.exp(m[:, None] * lg_b[None, :])
    kf, vf = k.astype(F32), v.astype(F32)
    s_f = jnp.einsum('blhd,lh,blhe->bhde', kf, w_f, vf)
    s_b = jnp.einsum('blhd,lh,blhe->bhde', kf, w_b, vf)
    return s_f, s_b


def bidir_retention(q, k, v, lg_f, lg_b, s_f, s_b):
    flip = lambda t: jnp.flip(t, axis=1)
    o_f = retention_chunked(q, k, v, lg_f, s_f, True)
    o_b = flip(retention_chunked(flip(q), flip(k), flip(v), lg_b, s_b, False))
    return o_f + o_b


def retention_out(o, g):
    B, T = o.shape[0], o.shape[1]
    mu = jnp.mean(o, axis=-1, keepdims=True)
    var = jnp.mean(jnp.square(o - mu), axis=-1, keepdims=True)
    on = ((o - mu) * lax.rsqrt(var + GN_EPS)).reshape(B, T, RET_WIDTH)
    return (jax.nn.silu(g.astype(F32)) * on).astype(g.dtype)


def token_mixer(hc, hl, w_in, sink, dec_f, dec_b, w_ba, w_bf, w_br, w_out, row_cs, col_cs, ret_cs, need_ctx):
    B, L, _ = hc.shape
    h = jnp.concatenate([hc, hl], axis=1)
    T = h.shape[1]
    proj = h @ w_in
    qa, ka, va, qr, kr, vr, gr, fu, gates = jnp.split(proj, split_points(), axis=-1)
    qa = qa.reshape(B, T, ATTN_HEADS, HEAD_DIM)
    ka = ka.reshape(B, T, ATTN_KV_HEADS, HEAD_DIM)
    va = va.reshape(B, T, ATTN_KV_HEADS, HEAD_DIM)
    qr = qr.reshape(B, T, RET_HEADS, RET_DK)
    kr = kr.reshape(B, T, RET_HEADS, RET_DK) * (RET_DK ** -0.5)
    vr = vr.reshape(B, T, RET_HEADS, RET_DV)
    oa_l = window_attention(axial_rope(qa[:, L:], row_cs, col_cs), axial_rope(ka[:, L:], row_cs, col_cs),
                            va[:, L:], ka[:, :L], va[:, :L], sink)
    of_l = fourier_mix(fu[:, L:])
    lg_f = jax.nn.log_sigmoid(dec_f.astype(F32))
    lg_b = jax.nn.log_sigmoid(dec_b.astype(F32))
    s_f, s_b = context_states(kr[:, :L], vr[:, :L], lg_f, lg_b)
    or_l = bidir_retention(rotate(qr[:, L:], *ret_cs), rotate(kr[:, L:], *ret_cs), vr[:, L:], lg_f, lg_b, s_f, s_b)
    if need_ctx:
        zero = jnp.zeros_like(s_f)
        oa = jnp.concatenate([context_attention(qa[:, :L], ka[:, :L], va[:, :L], sink), oa_l], axis=1)
        of = jnp.concatenate([fourier_mix(fu[:, :L]), of_l], axis=1)
        orr = jnp.concatenate([bidir_retention(qr[:, :L], kr[:, :L], vr[:, :L], lg_f, lg_b, zero, zero), or_l], axis=1)
        g_r, g_m = gr, gates
    else:
        oa, of, orr = oa_l, of_l, or_l
        g_r, g_m = gr[:, L:], gates[:, L:]
    ret = retention_out(orr, g_r)
    ga, gf, gt = jnp.split(jax.nn.sigmoid(g_m), 3, axis=-1)
    y = (ga * (oa @ w_ba) + gf * (of @ w_bf) + gt * (ret @ w_br)) @ w_out
    if need_ctx:
        return y[:, :L], y[:, L:]
    return None, y


def expert_ffn(xb, e, w_g, w_u, w_d):
    return (jax.nn.silu(xb @ w_g[e]) * (xb @ w_u[e])) @ w_d[e]


def hier_moe(h, w_rg, b_rg, w_re, b_re, w_g, w_u, w_d):
    T, D = h.shape
    hf = h.astype(F32)
    lg_grp = hf @ w_rg.astype(F32) + b_rg.astype(F32)
    p_grp = jax.nn.softmax(lg_grp, axis=-1)
    grp = jnp.argmax(lg_grp, axis=-1)
    pg = jnp.take_along_axis(p_grp, grp[:, None], axis=1)
    lg_e = (hf @ w_re.astype(F32) + b_re.astype(F32)).reshape(T, N_GROUPS, EXPERTS_PER_GROUP)
    lg_in = jnp.take_along_axis(lg_e, grp[:, None, None], axis=1)[:, 0]
    top_v, top_i = lax.top_k(lg_in, TOP_K)
    w = pg * jax.nn.softmax(top_v, axis=-1)
    flat_e = (grp[:, None] * EXPERTS_PER_GROUP + top_i).reshape(-1).astype(jnp.int32)
    flat_w = w.reshape(-1)
    A = T * TOP_K
    flat_tok = jnp.repeat(jnp.arange(T, dtype=jnp.int32), TOP_K)
    order = jnp.argsort(flat_e)
    se, st, sw = flat_e[order], flat_tok[order], flat_w[order]
    counts = jnp.zeros((N_EXPERTS,), jnp.int32).at[flat_e].add(1)
    pcounts = (counts + MOE_BLOCK - 1) // MOE_BLOCK * MOE_BLOCK
    start = jnp.cumsum(counts) - counts
    pend = jnp.cumsum(pcounts)
    pstart = pend - pcounts
    dest = pstart[se] + jnp.arange(A, dtype=jnp.int32) - start[se]
    n_blocks = -(-A // MOE_BLOCK) + N_EXPERTS
    P = n_blocks * MOE_BLOCK
    buf_tok = jnp.full((P,), T, jnp.int32).at[dest].set(st)
    buf_w = jnp.zeros((P,), F32).at[dest].set(sw)
    blk_e = jnp.minimum(jnp.searchsorted(pend, jnp.arange(n_blocks, dtype=jnp.int32) * MOE_BLOCK, side='right'), N_EXPERTS - 1)
    h_pad = jnp.concatenate([h, jnp.zeros((1, D), h.dtype)], axis=0)
    xb = h_pad[buf_tok].reshape(n_blocks, MOE_BLOCK, D)
    y = lax.map(lambda a: expert_ffn(a[0], a[1], w_g, w_u, w_d), (xb, blk_e))
    out = jnp.zeros((T + 1, D), F32).at[buf_tok].add(y.reshape(P, D).astype(F32) * buf_w[:, None])
    return out[:T].astype(h.dtype)


def setup_inputs(seed: int = 0) -> dict:
    key = jax.random.key(seed)
    ks = jax.random.split(key, 24)
    D = D_MODEL
    nrm = lambda k, shape, s: jax.random.normal(k, shape, F32) * s
    base_logit = jnp.asarray(np.log(2.0 ** (5 + np.arange(RET_HEADS)) - 1.0).astype(np.float32))
    return {
        'x': nrm(ks[0], (BATCH, SEQ, D), 1.0),
        'c': nrm(ks[1], (BATCH, D), 1.0),
        'ctx': nrm(ks[2], (BATCH, CTX_LEN, D), 1.0),
        'c_ctx': nrm(ks[3], (D,), 1.0),
        'norm_mix': 1.0 + nrm(ks[4], (DEPTH, D), 0.05),
        'norm_ffn': 1.0 + nrm(ks[5], (DEPTH, D), 0.05),
        'w_ada': nrm(ks[6], (DEPTH, D, 6 * D), 0.5 * D ** -0.5),
        'b_ada': nrm(ks[7], (DEPTH, 6 * D), 0.02),
        'w_in': nrm(ks[8], (DEPTH, D, IN_COLS), D ** -0.5),
        'attn_sink': nrm(ks[9], (DEPTH, ATTN_HEADS), 0.5),
        'ret_decay_fwd': base_logit[None, :] + nrm(ks[10], (DEPTH, RET_HEADS), 0.1),
        'ret_decay_bwd': base_logit[None, :] + nrm(ks[11], (DEPTH, RET_HEADS), 0.1),
        'w_branch_attn': nrm(ks[12], (DEPTH, ATTN_WIDTH, D), ATTN_WIDTH ** -0.5),
        'w_branch_fourier': nrm(ks[13], (DEPTH, FOURIER_WIDTH, D), FOURIER_WIDTH ** -0.5),
        'w_branch_ret': nrm(ks[14], (DEPTH, RET_WIDTH, D), RET_WIDTH ** -0.5),
        'w_out': nrm(ks[15], (DEPTH, D, D), D ** -0.5),
        'w_router_group': nrm(ks[16], (DEPTH, D, N_GROUPS), D ** -0.5),
        'b_router_group': nrm(ks[17], (DEPTH, N_GROUPS), 0.01),
        'w_router_expert': nrm(ks[18], (DEPTH, D, N_EXPERTS), D ** -0.5),
        'b_router_expert': nrm(ks[19], (DEPTH, N_EXPERTS), 0.01),
        'w_exp_gate': nrm(ks[20], (DEPTH, N_EXPERTS, D, EXPERT_HIDDEN), D ** -0.5),
        'w_exp_up': nrm(ks[21], (DEPTH, N_EXPERTS, D, EXPERT_HIDDEN), D ** -0.5),
        'w_exp_down': nrm(ks[22], (DEPTH, N_EXPERTS, EXPERT_HIDDEN, D), EXPERT_HIDDEN ** -0.5),
        'norm_final': 1.0 + nrm(ks[23], (D,), 0.05),
    }


def reference(x, c, ctx, c_ctx, norm_mix, norm_ffn, w_ada, b_ada, w_in, attn_sink, ret_decay_fwd, ret_decay_bwd,
              w_branch_attn, w_branch_fourier, w_branch_ret, w_out, w_router_group, b_router_group,
              w_router_expert, b_router_expert, w_exp_gate, w_exp_up, w_exp_down, norm_final):
    B, N, D = x.shape
    L = ctx.shape[1]
    ROWS = N // GRID_W
    row = jnp.repeat(jnp.arange(ROWS, dtype=F32), GRID_W)
    col = jnp.tile(jnp.arange(GRID_W, dtype=F32), ROWS)
    row_cs = rope_table(row, HEAD_DIM // 4)
    col_cs = rope_table(col, HEAD_DIM // 4)
    ret_cs = rope_table(jnp.arange(N, dtype=F32), RET_DK // 2)
    xc, xl = ctx, x
    for l in range(DEPTH):
        need_ctx = l < DEPTH - 1
        mod_l = (jax.nn.silu(c) @ w_ada[l] + b_ada[l])[:, None, :]
        mod_c = (jax.nn.silu(c_ctx) @ w_ada[l] + b_ada[l])[None, None, :]
        sh1_l, sc1_l, g1_l, sh2_l, sc2_l, g2_l = jnp.split(mod_l, 6, axis=-1)
        sh1_c, sc1_c, g1_c, sh2_c, sc2_c, g2_c = jnp.split(mod_c, 6, axis=-1)
        hl = modulate(rmsnorm(xl, norm_mix[l]), sh1_l, sc1_l)
        hc = modulate(rmsnorm(xc, norm_mix[l]), sh1_c, sc1_c)
        yc, yl = token_mixer(hc, hl, w_in[l], attn_sink[l], ret_decay_fwd[l], ret_decay_bwd[l],
                             w_branch_attn[l], w_branch_fourier[l], w_branch_ret[l], w_out[l],
                             row_cs, col_cs, ret_cs, need_ctx)
        xl = xl + g1_l * yl
        hl2 = modulate(rmsnorm(xl, norm_ffn[l]), sh2_l, sc2_l)
        moe_args = (w_router_group[l], b_router_group[l], w_router_expert[l], b_router_expert[l],
                    w_exp_gate[l], w_exp_up[l], w_exp_down[l])
        if need_ctx:
            xc = xc + g1_c * yc
            hc2 = modulate(rmsnorm(xc, norm_ffn[l]), sh2_c, sc2_c)
            h2 = jnp.concatenate([hc2, hl2], axis=1)
            f = hier_moe(h2.reshape(-1, D), *moe_args).reshape(B, L + N, D)
            xc = xc + g2_c * f[:, :L]
            xl = xl + g2_l * f[:, L:]
        else:
            f = hier_moe(hl2.reshape(-1, D), *moe_args).reshape(B, N, D)
            xl = xl + g2_l * f
    return rmsnorm(xl, norm_final)
```

```python
import functools

import numpy as np
import jax
import jax.numpy as jnp
from jax import lax
from jax.experimental import pallas as pl
from jax.experimental.pallas import tpu as pltpu

F32 = jnp.float32
BF16 = jnp.bfloat16

D_MODEL = 1024
BATCH = 8
SEQ = 2048
DEPTH = 2
CTX_LEN = 256
GRID_W = 64
HEAD_DIM = 64
ATTN_HEADS = 8
ATTN_KV_HEADS = 2
ATTN_GROUP = ATTN_HEADS // ATTN_KV_HEADS
ATTN_BLOCK = 128
RET_HEADS = 8
RET_DK = 64
RET_CHUNK = 128
FOURIER_GROUPS = 4
FOURIER_DIM = 128
N_GROUPS = 4
EXPERTS_PER_GROUP = 8
N_EXPERTS = N_GROUPS * EXPERTS_PER_GROUP
EXPERT_HIDDEN = 512
ROPE_BASE = 10000.0
NORM_EPS = 1e-6
GN_EPS = 1e-5

W = 512
IN_COLS = 6400
RC = BATCH * CTX_LEN
RL = BATCH * SEQ
R = RC + RL
MOD_ROWS = 16
CTX_MOD_ROW = 8

VMEM_LIMIT = 52 * 1024 * 1024

TM_PROJ = 512
TM_MERGE = 512
TN_ADA = 1536
TR_FOURIER = 512
T_DISPATCH = 256
T_COMBINE = 256
MOE_BM = 256
NEG = -1e30
ROUTER_ROWS = 40


def _dot(a, b):
    return jnp.dot(a, b, preferred_element_type=F32)


def _dot_nt(a, b):
    return lax.dot_general(a, b, (((1,), (1,)), ((), ())), preferred_element_type=F32)


def _dot_tn(a, b):
    return lax.dot_general(a, b, (((0,), (0,)), ((), ())), preferred_element_type=F32)


def _split(x):
    hi = x.astype(BF16)
    lo = (x - hi.astype(F32)).astype(BF16)
    return hi, lo


def _sigmoid(x):
    return 1.0 / (1.0 + jnp.exp(-x))


def _params(sem, vmem=VMEM_LIMIT):
    return pltpu.CompilerParams(dimension_semantics=sem, vmem_limit_bytes=vmem)


def _mod_row(row0, tm):
    def f(i):
        g0 = i * tm + row0
        return jnp.where(g0 < RC, CTX_MOD_ROW, (g0 - RC) // SEQ)
    return f


def _rope_tables():
    pos = np.arange(SEQ, dtype=np.float64)
    row = np.floor(pos / GRID_W)
    col = pos % GRID_W

    def cs(p, nf):
        inv = ROPE_BASE ** (-np.arange(nf, dtype=np.float64) / nf)
        ang = p[:, None] * inv[None, :]
        return np.cos(ang), np.sin(ang)

    rc, rs = cs(row, HEAD_DIM // 4)
    cc, cs_ = cs(col, HEAD_DIM // 4)
    cos_a = np.concatenate([rc, rc, cc, cc], axis=1)
    sin_a = np.concatenate([-rs, rs, -cs_, cs_], axis=1)
    tc, ts = cs(pos, RET_DK // 2)
    cos_r = np.concatenate([tc, tc], axis=1)
    sin_r = np.concatenate([-ts, ts], axis=1)

    def full(t, ident):
        t2 = np.concatenate([t, t], axis=1)
        return np.concatenate([np.full_like(t2, ident), t2], axis=0).astype(np.float32)

    return full(cos_a, 1.0), full(sin_a, 0.0), full(cos_r, 1.0), full(sin_r, 0.0)


def _dft_tables():
    def cs(n):
        k = np.arange(n, dtype=np.int64)
        m = (k[:, None] * k[None, :]) % n
        ang = 2.0 * np.pi * m.astype(np.float64) / n
        return np.cos(ang), np.sin(ang)

    c128, s128 = cs(FOURIER_DIM)
    eye = np.eye(FOURIER_GROUPS)
    bdc = np.kron(eye, c128).astype(np.float32)
    bds = np.kron(eye, s128).astype(np.float32)
    cn, sn = cs(SEQ)
    w2 = np.concatenate([cn, -sn], axis=1).astype(np.float32)
    cl, sl = cs(CTX_LEN)
    w2c = np.concatenate([cl, -sl], axis=1).astype(np.float32)
    return bdc, bds, w2, w2c


def _retention_tables(dec_f, dec_b):
    lg_f = jax.nn.log_sigmoid(dec_f.astype(F32))
    lg_b = jax.nn.log_sigmoid(dec_b.astype(F32))
    i = jnp.arange(RET_CHUNK)
    diff = (i[:, None] - i[None, :]).astype(F32)
    fwd = jnp.exp(jnp.maximum(diff, 0.0)[None] * lg_f[:, None, None])
    bwd = jnp.exp(jnp.maximum(-diff, 0.0)[None] * lg_b[:, None, None])
    dcomb = jnp.where((diff >= 0)[None], fwd, bwd)
    fi = i.astype(F32)
    lanes = lambda t: jnp.repeat(t, RET_DK, axis=1)
    xi_f = lanes(jnp.exp((fi + 1.0)[:, None] * lg_f[None, :]))
    zt_f = lanes(jnp.exp((RET_CHUNK - 1 - fi)[:, None] * lg_f[None, :]))
    xi_b = lanes(jnp.exp((RET_CHUNK - fi)[:, None] * lg_b[None, :]))
    zt_b = lanes(jnp.exp(fi[:, None] * lg_b[None, :]))
    g_f = jnp.repeat(jnp.exp(RET_CHUNK * lg_f), RET_DK).reshape(2, 256, 1)
    g_b = jnp.repeat(jnp.exp(RET_CHUNK * lg_b), RET_DK).reshape(2, 256, 1)
    g_f = jnp.broadcast_to(g_f, (2, 256, 256))
    g_b = jnp.broadcast_to(g_b, (2, 256, 256))
    return dcomb, xi_f, zt_f, xi_b, zt_b, g_f, g_b


def _ada_body(c_ref, w_ref, b_ref, o_ref):
    c = c_ref[...]
    s = c * _sigmoid(c)
    sh, sl = _split(s)
    wh, wl = _split(w_ref[...])
    o_ref[...] = _dot(sh, wh) + _dot(sl, wh) + _dot(sh, wl) + b_ref[...]


def _ada(cc, w_ada, b_ada):
    nt = 6 * D_MODEL // TN_ADA
    return pl.pallas_call(
        _ada_body,
        grid=(DEPTH, nt),
        in_specs=[
            pl.BlockSpec((MOD_ROWS, D_MODEL), lambda l, j: (0, 0)),
            pl.BlockSpec((None, D_MODEL, TN_ADA), lambda l, j: (l, 0, j)),
            pl.BlockSpec((None, 1, TN_ADA), lambda l, j: (l, 0, j)),
        ],
        out_specs=pl.BlockSpec((None, MOD_ROWS, TN_ADA), lambda l, j: (l, 0, j)),
        out_shape=jax.ShapeDtypeStruct((DEPTH, MOD_ROWS, 6 * D_MODEL), F32),
        compiler_params=_params(("arbitrary", "arbitrary")),
        name="ada_mod",
    )(cc, w_ada, b_ada.reshape(DEPTH, 1, 6 * D_MODEL))


def _rope(xc, cos, sin, half):
    fwd = pltpu.roll(xc, 128 - half, axis=1)
    bwd = pltpu.roll(xc, half, axis=1)
    lane = lax.broadcasted_iota(jnp.int32, xc.shape, 1)
    first = (lane & (2 * half - 1)) < half
    return xc * cos + jnp.where(first, fwd, bwd) * sin


def _proj_body(x_ref, mod_ref, gn_ref, w_ref, ca_ref, sa_ref, cr_ref, sr_ref,
               qa_ref, ka_ref, va_ref, qr_ref, kr_ref, vr_ref, gr_ref, fu_ref, gm_ref):
    x = x_ref[...]
    ms = jnp.mean(x * x, axis=-1, keepdims=True)
    y = x * lax.rsqrt(ms + NORM_EPS) * gn_ref[...]
    h = y * (1.0 + mod_ref[1:2, :]) + mod_ref[0:1, :]
    hb = h.astype(BF16)

    def proj(c0, width):
        return _dot(hb, w_ref[:, c0:c0 + width])

    ca, sa, cr, sr = ca_ref[...], sa_ref[...], cr_ref[...], sr_ref[...]

    qa = proj(0, W) * (HEAD_DIM ** -0.5)
    for c in range(W // 128):
        qa_ref[:, c * 128:(c + 1) * 128] = _rope(qa[:, c * 128:(c + 1) * 128], ca, sa, 16).astype(BF16)
    kv = proj(W, 256)
    ka = _rope(kv[:, 0:128], ca, sa, 16).astype(BF16)
    ka_ref[0] = ka[:, 0:64]
    ka_ref[1] = ka[:, 64:128]
    va = kv[:, 128:256].astype(BF16)
    va_ref[0] = va[:, 0:64]
    va_ref[1] = va[:, 64:128]
    qr = proj(768, W)
    kr = proj(1280, W) * (RET_DK ** -0.5)
    for c in range(W // 128):
        sl = slice(c * 128, (c + 1) * 128)
        qr_ref[:, sl] = _rope(qr[:, sl], cr, sr, 32).astype(BF16)
        kr_ref[:, sl] = _rope(kr[:, sl], cr, sr, 32).astype(BF16)
    vr_ref[...] = proj(1792, W).astype(BF16)
    g = proj(2304, W)
    gr_ref[...] = (g * _sigmoid(g)).astype(BF16)
    fu_ref[...] = proj(2816, W).astype(BF16)
    for c in range(3):
        gm_ref[:, c * D_MODEL:(c + 1) * D_MODEL] = _sigmoid(proj(3328 + c * D_MODEL, D_MODEL)).astype(BF16)


def _proj(x, mod_l, gnorm, w_in_bf, tabs):
    tm = TM_PROJ
    nt = R // tm
    nc = RC // tm

    def tab_map(i):
        return (jnp.where(i < nc, i, nc + (i - nc) % (SEQ // tm)), 0)

    row = lambda i: (i, 0)
    wide = lambda n: pl.BlockSpec((tm, n), row)
    kv_spec = pl.BlockSpec((2, tm, 64), lambda i: (0, i, 0))
    sds = lambda n: jax.ShapeDtypeStruct((R, n), BF16)
    kv_sds = jax.ShapeDtypeStruct((2, R, 64), BF16)
    mrow = _mod_row(0, tm)
    return pl.pallas_call(
        _proj_body,
        grid=(nt,),
        in_specs=[
            wide(D_MODEL),
            pl.BlockSpec((None, 6, D_MODEL), lambda i: (mrow(i), 0, 0)),
            pl.BlockSpec((1, D_MODEL), lambda i: (0, 0)),
            pl.BlockSpec((D_MODEL, IN_COLS), lambda i: (0, 0), pipeline_mode=pl.Buffered(1)),
        ] + [pl.BlockSpec((tm, 128), tab_map)] * 4,
        out_specs=[wide(W), kv_spec, kv_spec, wide(W), wide(W), wide(W), wide(W), wide(W), wide(3 * D_MODEL)],
        out_shape=[sds(W), kv_sds, kv_sds, sds(W), sds(W), sds(W), sds(W), sds(W), sds(3 * D_MODEL)],
        compiler_params=_params(("arbitrary",)),
        name="in_proj",
    )(x, mod_l, gnorm, w_in_bf, *tabs)


def _attn_body(sink_ref, q_ref, kc_ref, kp_ref, kk_ref, kn_ref, vc_ref, vp_ref, vk_ref, vn_ref, o_ref, *, q_lo):
    h = pl.program_id(1)
    qi = pl.program_id(2) + q_lo
    is_lat = qi >= 2
    m = qi - 2
    q = q_ref[...]
    q4 = jnp.concatenate([q[:, g * 64:(g + 1) * 64] for g in range(ATTN_GROUP)], axis=0)
    nr = ATTN_GROUP * ATTN_BLOCK
    ri = lax.broadcasted_iota(jnp.int32, (nr, ATTN_BLOCK), 0) & (ATTN_BLOCK - 1)
    ci = lax.broadcasted_iota(jnp.int32, (nr, ATTN_BLOCK), 1)
    far = 4 * ATTN_BLOCK
    off_p = jnp.where(is_lat & (m >= 1), 0, far)
    off_k = jnp.where(is_lat, 0, far)
    off_n = jnp.where(is_lat & (m <= SEQ // ATTN_BLOCK - 2), 0, far)
    ok_p = ci >= ri + off_p
    ok_k = ci >= off_k
    ok_n = ci + off_n <= ri
    s_c = _dot_nt(q4, kc_ref[...])
    s_p = jnp.where(ok_p, _dot_nt(q4, kp_ref[...]), NEG)
    s_k = jnp.where(ok_k, _dot_nt(q4, kk_ref[...]), NEG)
    s_n = jnp.where(ok_n, _dot_nt(q4, kn_ref[...]), NEG)
    rg = lax.broadcasted_iota(jnp.int32, (nr, 1), 0) >> 7
    sk = jnp.zeros((nr, 1), F32)
    for g in range(ATTN_GROUP):
        sk = jnp.where(rg == g, sink_ref[h * ATTN_GROUP + g], sk)
    rmax = lambda s: jnp.max(s, axis=-1, keepdims=True)
    mx = jnp.maximum(jnp.maximum(rmax(s_c), rmax(s_p)), jnp.maximum(rmax(s_k), rmax(s_n)))
    mx = jnp.maximum(mx, sk)
    p_c, p_p, p_k, p_n = (jnp.exp(s - mx) for s in (s_c, s_p, s_k, s_n))
    rsum = lambda p: jnp.sum(p, axis=-1, keepdims=True)
    den = rsum(p_c) + rsum(p_p) + rsum(p_k) + rsum(p_n) + jnp.exp(sk - mx)
    o = (_dot(p_c.astype(BF16), vc_ref[...]) + _dot(p_p.astype(BF16), vp_ref[...])
         + _dot(p_k.astype(BF16), vk_ref[...]) + _dot(p_n.astype(BF16), vn_ref[...]))
    o = o / den
    o_ref[...] = jnp.concatenate(
        [o[g * ATTN_BLOCK:(g + 1) * ATTN_BLOCK] for g in range(ATTN_GROUP)], axis=1).astype(BF16)


def _attention(sink, qa, ka, va, need_ctx):
    q_lo = 0 if need_ctx else 2
    nq = SEQ // ATTN_BLOCK + (2 if need_ctx else 0)
    nb = SEQ // ATTN_BLOCK
    lat0 = RC // ATTN_BLOCK

    def qrow(b, qi):
        return jnp.where(qi < 2, b * 2 + qi, lat0 + b * nb + qi - 2)

    def q_map(b, h, i):
        return (qrow(b, i + q_lo), h)

    def loc(delta):
        def f(b, h, i):
            m = jnp.clip(i + q_lo - 2 + delta, 0, nb - 1)
            return (h, lat0 + b * nb + m, 0)
        return f

    ctx_spec = pl.BlockSpec((None, CTX_LEN, 64), lambda b, h, i: (h, b, 0))
    loc_spec = lambda d: pl.BlockSpec((None, ATTN_BLOCK, 64), loc(d))
    kv_specs = [ctx_spec, loc_spec(-1), loc_spec(0), loc_spec(1)]
    return pl.pallas_call(
        functools.partial(_attn_body, q_lo=q_lo),
        grid=(BATCH, ATTN_KV_HEADS, nq),
        in_specs=[pl.BlockSpec(memory_space=pltpu.SMEM),
                  pl.BlockSpec((ATTN_BLOCK, 256), q_map)] + kv_specs + kv_specs,
        out_specs=pl.BlockSpec((ATTN_BLOCK, 256), q_map),
        out_shape=jax.ShapeDtypeStruct((R, W), BF16),
        compiler_params=_params(("arbitrary", "arbitrary", "arbitrary")),
        name="window_attn",
    )(sink, qa, ka, ka, ka, ka, va, va, va, va)


def _fourier_lat_body(u_ref, bdc_ref, bds_ref, w2_ref, o_ref, as_ref):
    @pl.when(pl.program_id(1) == 0)
    def _():
        u = u_ref[...]
        as_ref[0:SEQ, :] = _dot(u, bdc_ref[...]).astype(BF16)
        as_ref[SEQ:2 * SEQ, :] = _dot(u, bds_ref[...]).astype(BF16)

    o_ref[...] = (_dot(w2_ref[...], as_ref[...]) * ((SEQ * FOURIER_DIM) ** -0.5)).astype(BF16)


def _fourier_ctx_body(u_ref, bdc_ref, bds_ref, w2_ref, of_in_ref, o_ref):
    del of_in_ref
    u = u_ref[...]
    a = _dot(u, bdc_ref[...]).astype(BF16)
    s = _dot(u, bds_ref[...]).astype(BF16)
    z = _dot(w2_ref[...], jnp.concatenate([a, s], axis=0))
    o_ref[...] = (z * ((CTX_LEN * FOURIER_DIM) ** -0.5)).astype(BF16)


def _fourier(fu, dft, need_ctx):
    bdc, bds, w2, w2c = dft
    tr = TR_FOURIER
    nj = SEQ // tr
    full = lambda a: pl.BlockSpec(a.shape, lambda *_: (0,) * a.ndim)
    of = pl.pallas_call(
        _fourier_lat_body,
        grid=(BATCH, nj),
        in_specs=[pl.BlockSpec((SEQ, W), lambda b, j: (1 + b, 0)), full(bdc), full(bds),
                  pl.BlockSpec((tr, 2 * SEQ), lambda b, j: (j, 0))],
        out_specs=pl.BlockSpec((tr, W), lambda b, j: (RC // tr + b * nj + j, 0)),
        out_shape=jax.ShapeDtypeStruct((R, W), BF16),
        scratch_shapes=[pltpu.VMEM((2 * SEQ, W), BF16)],
        compiler_params=_params(("arbitrary", "arbitrary")),
        name="fourier_latent",
    )(fu, bdc, bds, w2)
    if not need_ctx:
        return of
    return pl.pallas_call(
        _fourier_ctx_body,
        grid=(BATCH,),
        in_specs=[pl.BlockSpec((CTX_LEN, W), lambda b: (b, 0)), full(bdc), full(bds), full(w2c),
                  pl.BlockSpec(memory_space=pl.ANY)],
        out_specs=pl.BlockSpec((CTX_LEN, W), lambda b: (b, 0)),
        out_shape=jax.ShapeDtypeStruct((R, W), BF16),
        input_output_aliases={4: 0},
        compiler_params=_params(("arbitrary",)),
        name="fourier_ctx",
    )(fu, bdc, bds, w2c, of)


def _retention_body(qc_ref, kc_ref, vc_ref, ql_ref, kl_ref, vl_ref, g_ref,
                    dcomb_ref, xif_ref, ztf_ref, xib_ref, ztb_ref, gf_ref, gb_ref, mbd_ref, avg_ref,
                    o_ref, os_ref, st_ref):
    j = pl.program_id(1)
    C = RET_CHUNK
    nl = SEQ // C

    def group_parts(q, k, v, gi):
        sl = slice(gi * 256, (gi + 1) * 256)
        return q[:, sl], k[:, sl], v[:, sl], sl

    lane_head = lax.broadcasted_iota(jnp.int32, (C, 256), 1) // RET_DK

    def chunk_fwd(q, k, v, r0):
        for gi in range(2):
            q4, k4, v4, sl = group_parts(q, k, v, gi)
            s_prev = st_ref[gi]
            q4f = q4.astype(F32)
            o4 = _dot((q4f * xif_ref[:, sl]).astype(BF16), s_prev.astype(BF16))
            intra = jnp.zeros((C, 256), F32)
            for hh in range(4):
                qm = jnp.where(lane_head == hh, q4f, 0.0).astype(BF16)
                p = (_dot_nt(qm, k4) * dcomb_ref[gi * 4 + hh]).astype(BF16)
                intra = jnp.where(lane_head == hh, _dot(p, v4), intra)
            os_ref[pl.ds(r0, C), sl] = o4 + intra
            u = _dot_tn(k4, (v4.astype(F32) * ztf_ref[:, sl]).astype(BF16))
            st_ref[gi] = gf_ref[gi] * s_prev + mbd_ref[...] * u

    def chunk_bwd(q, k, v, r0):
        for gi in range(2):
            q4, k4, v4, sl = group_parts(q, k, v, gi)
            s_prev = st_ref[gi]
            cross = _dot((q4.astype(F32) * xib_ref[:, sl]).astype(BF16), s_prev.astype(BF16))
            os_ref[pl.ds(r0, C), sl] = os_ref[pl.ds(r0, C), sl] + cross
            u = _dot_tn(k4, (v4.astype(F32) * ztb_ref[:, sl]).astype(BF16))
            st_ref[gi] = gb_ref[gi] * s_prev + mbd_ref[...] * u

    @pl.when(j == 0)
    def _():
        st_ref[...] = jnp.zeros_like(st_ref)
        for c in range(CTX_LEN // C):
            rs = slice(c * C, (c + 1) * C)
            chunk_fwd(qc_ref[rs, :], kc_ref[rs, :], vc_ref[rs, :], c * C)

        def fbody(c, carry):
            r0 = pl.multiple_of(c * C, C)
            rs = pl.ds(r0, C)
            chunk_fwd(ql_ref[rs, :], kl_ref[rs, :], vl_ref[rs, :], CTX_LEN + r0)
            return carry

        lax.fori_loop(0, nl, fbody, 0)

        st_ref[...] = jnp.zeros_like(st_ref)
        for c in reversed(range(CTX_LEN // C)):
            rs = slice(c * C, (c + 1) * C)
            chunk_bwd(qc_ref[rs, :], kc_ref[rs, :], vc_ref[rs, :], c * C)

        def bbody(t, carry):
            r0 = pl.multiple_of((nl - 1 - t) * C, C)
            rs = pl.ds(r0, C)
            chunk_bwd(ql_ref[rs, :], kl_ref[rs, :], vl_ref[rs, :], CTX_LEN + r0)
            return carry

        lax.fori_loop(0, nl, bbody, 0)

    o = os_ref[pl.ds(pl.multiple_of(j * 256, 256), 256), :]
    avg = avg_ref[...]
    oh, ol = _split(o)
    mu = _dot(oh, avg) + _dot(ol, avg)
    d = o - mu
    vh, vl = _split(d * d)
    var = _dot(vh, avg) + _dot(vl, avg)
    o_ref[...] = (g_ref[...].astype(F32) * d * lax.rsqrt(var + GN_EPS)).astype(BF16)


def _retention(qr, kr, vr, gr, rtabs):
    nj = 1 + SEQ // 256

    def out_map(b, j):
        return (jnp.where(j == 0, b, RC // 256 + b * (SEQ // 256) + j - 1), 0)

    ctx = pl.BlockSpec((CTX_LEN, W), lambda b, j: (b, 0))
    lat = pl.BlockSpec((SEQ, W), lambda b, j: (1 + b, 0))
    full = lambda a: pl.BlockSpec(a.shape, lambda *_: (0,) * a.ndim)
    avg = jnp.asarray(np.kron(np.eye(RET_HEADS), np.full((RET_DK, RET_DK), 1.0 / RET_DK)).astype(np.float32)).astype(BF16)
    mbd = jnp.asarray(np.kron(np.eye(4), np.ones((RET_DK, RET_DK))).astype(np.float32))
    tabs = list(rtabs) + [mbd, avg]
    return pl.pallas_call(
        _retention_body,
        grid=(BATCH, nj),
        in_specs=[ctx, ctx, ctx, lat, lat, lat, pl.BlockSpec((256, W), out_map)] + [full(t) for t in tabs],
        out_specs=pl.BlockSpec((256, W), out_map),
        out_shape=jax.ShapeDtypeStruct((R, W), BF16),
        scratch_shapes=[pltpu.VMEM((CTX_LEN + SEQ, W), F32), pltpu.VMEM((2, 256, 256), F32)],
        compiler_params=_params(("arbitrary", "arbitrary")),
        name="retention",
    )(qr, kr, vr, qr, kr, vr, gr, *tabs)


def _merge_body(oa_ref, of_ref, rt_ref, gm_ref, x_ref, mod_ref, gn_ref, wba_ref, wbf_ref, wbr_ref, wout_ref,
                wrh_ref, wrl_ref, br_ref, xo_ref, h2_ref, ro_ref):
    gm = gm_ref[...].astype(F32)
    z = (gm[:, 0:D_MODEL] * _dot(oa_ref[...], wba_ref[...])
         + gm[:, D_MODEL:2 * D_MODEL] * _dot(of_ref[...], wbf_ref[...])
         + gm[:, 2 * D_MODEL:3 * D_MODEL] * _dot(rt_ref[...], wbr_ref[...]))
    y = _dot(z.astype(BF16), wout_ref[...])
    x = x_ref[...] + mod_ref[2:3, :] * y
    xo_ref[...] = x
    ms = jnp.mean(x * x, axis=-1, keepdims=True)
    hn = x * lax.rsqrt(ms + NORM_EPS) * gn_ref[...]
    h2 = hn * (1.0 + mod_ref[4:5, :]) + mod_ref[3:4, :]
    h2_ref[...] = h2
    hh, hl = _split(h2)
    wh, wl = wrh_ref[...], wrl_ref[...]
    lg = _dot_nt(wh, hh) + _dot_nt(wh, hl) + _dot_nt(wl, hh) + br_ref[...]
    tm = lg.shape[1]
    row8 = lax.broadcasted_iota(jnp.int32, (8, tm), 0)
    lgg = lg[0:8, :]
    mg = jnp.max(lgg, axis=0, keepdims=True)
    grp = jnp.min(jnp.where(lgg == mg, row8, 8), axis=0, keepdims=True)
    pg = 1.0 / jnp.sum(jnp.exp(lgg - mg), axis=0, keepdims=True)
    lin = jnp.zeros((8, tm), F32)
    for g in range(N_GROUPS):
        lin = jnp.where(grp == g, lg[8 + 8 * g:16 + 8 * g, :], lin)
    v1 = jnp.max(lin, axis=0, keepdims=True)
    i1 = jnp.min(jnp.where(lin == v1, row8, 8), axis=0, keepdims=True)
    rest = jnp.where(row8 == i1, -jnp.inf, lin)
    v2 = jnp.max(rest, axis=0, keepdims=True)
    i2 = jnp.min(jnp.where(rest == v2, row8, 8), axis=0, keepdims=True)
    e2 = jnp.exp(v2 - v1)
    w1 = pg / (1.0 + e2)
    w2 = pg * e2 / (1.0 + e2)
    e_1 = (grp * EXPERTS_PER_GROUP + i1).astype(F32)
    e_2 = (grp * EXPERTS_PER_GROUP + i2).astype(F32)
    out = jnp.where(row8 == 0, e_1, jnp.where(row8 == 1, e_2, jnp.where(row8 == 2, w1, jnp.where(row8 == 3, w2, 0.0))))
    ro_ref[...] = out


def _merge(oa, of, ret, gm, x, mod_l, gnorm, wba, wbf, wbr, wout, wrh, wrl, brb, row0):
    tm = TM_MERGE
    rm = R - row0
    nt = rm // tm
    off = row0 // tm
    src = lambda n: pl.BlockSpec((tm, n), lambda i: (i + off, 0))
    dst = lambda n: pl.BlockSpec((tm, n), lambda i: (i, 0))
    full = lambda a: pl.BlockSpec(a.shape, lambda *_: (0,) * a.ndim)
    mrow = _mod_row(row0, tm)
    return pl.pallas_call(
        _merge_body,
        grid=(nt,),
        in_specs=[src(W), src(W), src(W), src(3 * D_MODEL), src(D_MODEL),
                  pl.BlockSpec((None, 6, D_MODEL), lambda i: (mrow(i), 0, 0)),
                  full(gnorm), full(wba), full(wbf), full(wbr), full(wout), full(wrh), full(wrl), full(brb)],
        out_specs=[dst(D_MODEL), dst(D_MODEL), pl.BlockSpec((8, tm), lambda i: (0, i))],
        out_shape=[jax.ShapeDtypeStruct((rm, D_MODEL), F32), jax.ShapeDtypeStruct((rm, D_MODEL), F32),
                   jax.ShapeDtypeStruct((8, rm), F32)],
        compiler_params=_params(("arbitrary",)),
        name="merge_router",
    )(oa, of, ret, gm, x, mod_l, gnorm, wba, wbf, wbr, wout, wrh, wrl, brb)


def _dispatch_body(dest_ref, h_ref, xs_in_ref, xs_ref, sem):
    del xs_in_ref
    td = h_ref.shape[0]

    def body(r, carry):
        for k in range(2):
            pltpu.make_async_copy(h_ref.at[pl.ds(r, 1)], xs_ref.at[pl.ds(dest_ref[k, r], 1)], sem.at[k]).start()
        return carry

    lax.fori_loop(0, td, body, 0, unroll=8)
    for k in range(2):
        pltpu.make_async_copy(h_ref, xs_ref.at[pl.ds(0, td)], sem.at[k]).wait()


def _dispatch(dest3, h2, p_rows):
    td = T_DISPATCH
    rm = h2.shape[0]
    xs0 = jnp.zeros((p_rows, D_MODEL), F32)
    return pl.pallas_call(
        _dispatch_body,
        grid=(rm // td,),
        in_specs=[pl.BlockSpec((None, 2, td), lambda i: (i, 0, 0), memory_space=pltpu.SMEM),
                  pl.BlockSpec((td, D_MODEL), lambda i: (i, 0)),
                  pl.BlockSpec(memory_space=pl.ANY)],
        out_specs=pl.BlockSpec(memory_space=pl.ANY),
        out_shape=jax.ShapeDtypeStruct((p_rows, D_MODEL), F32),
        scratch_shapes=[pltpu.SemaphoreType.DMA((2,))],
        input_output_aliases={2: 0},
        compiler_params=_params(("arbitrary",)),
        name="moe_dispatch",
    )(dest3, h2, xs0)


def _ffn_body(be_ref, nu_ref, xs_ref, wg_ref, wu_ref, wd_ref, y_ref, wgb, wub, wdb):
    b = pl.program_id(0)
    prev = be_ref[jnp.maximum(b - 1, 0)]
    fresh = (b == 0) | (be_ref[b] != prev)

    @pl.when(fresh)
    def _():
        wgb[...] = wg_ref[...].astype(BF16)
        wub[...] = wu_ref[...].astype(BF16)
        wdb[...] = wd_ref[...].astype(BF16)

    @pl.when(b < nu_ref[0])
    def _():
        x = xs_ref[...].astype(BF16)
        g = _dot(x, wgb[...])
        u = _dot(x, wub[...])
        hmid = (g * _sigmoid(g) * u).astype(BF16)
        y_ref[...] = _dot(hmid, wdb[...])


def _ffn(blk_e, n_used, xs, w_g, w_u, w_d):
    bm = MOE_BM
    p_rows = xs.shape[0]
    nb = p_rows // bm
    rows = lambda b, be, nu: (jnp.minimum(b, nu[0] - 1), 0)
    wmap = lambda b, be, nu: (be[b], 0, 0)
    grid_spec = pltpu.PrefetchScalarGridSpec(
        num_scalar_prefetch=2,
        grid=(nb,),
        in_specs=[pl.BlockSpec((bm, D_MODEL), rows),
                  pl.BlockSpec((None, D_MODEL, EXPERT_HIDDEN), wmap),
                  pl.BlockSpec((None, D_MODEL, EXPERT_HIDDEN), wmap),
                  pl.BlockSpec((None, EXPERT_HIDDEN, D_MODEL), wmap)],
        out_specs=pl.BlockSpec((bm, D_MODEL), rows),
        scratch_shapes=[pltpu.VMEM((D_MODEL, EXPERT_HIDDEN), BF16), pltpu.VMEM((D_MODEL, EXPERT_HIDDEN), BF16),
                        pltpu.VMEM((EXPERT_HIDDEN, D_MODEL), BF16)],
    )
    return pl.pallas_call(
        _ffn_body,
        grid_spec=grid_spec,
        out_shape=jax.ShapeDtypeStruct((p_rows, D_MODEL), F32),
        compiler_params=_params(("arbitrary",)),
        name="moe_experts",
    )(blk_e, n_used, xs, w_g, w_u, w_d)


def _combine_body(dcur_ref, dnxt_ref, x_ref, mod_ref, wb_ref, gn_ref, y_ref, o_ref, ybuf, sem, *, final):
    i = pl.program_id(0)
    n = pl.num_programs(0)
    tc = x_ref.shape[0]
    slot = i % 2

    def issue(dref, sl):
        def body(r, carry):
            for k in range(2):
                pltpu.make_async_copy(y_ref.at[pl.ds(dref[k, r], 1)], ybuf.at[sl, k, pl.ds(r, 1)], sem.at[sl]).start()
            return carry
        lax.fori_loop(0, tc, body, 0, unroll=8)

    @pl.when(i == 0)
    def _():
        issue(dcur_ref, 0)

    @pl.when(i + 1 < n)
    def _():
        issue(dnxt_ref, 1 - slot)

    for k in range(2):
        pltpu.make_async_copy(y_ref.at[pl.ds(0, tc)], ybuf.at[slot, k], sem.at[slot]).wait()
    wb = wb_ref[...]
    f = wb[:, 0:1] * ybuf[slot, 0] + wb[:, 1:2] * ybuf[slot, 1]
    x = x_ref[...] + mod_ref[5:6, :] * f
    if final:
        ms = jnp.mean(x * x, axis=-1, keepdims=True)
        x = x * lax.rsqrt(ms + NORM_EPS) * gn_ref[...]
    o_ref[...] = x


def _combine(dest3, x, mod_l, wpad, gnorm, y, row0, final):
    tc = T_COMBINE
    rm = x.shape[0]
    nt = rm // tc
    mrow = _mod_row(row0, tc)
    dspec = lambda f: pl.BlockSpec((None, 2, tc), f, memory_space=pltpu.SMEM)
    return pl.pallas_call(
        functools.partial(_combine_body, final=final),
        grid=(nt,),
        in_specs=[dspec(lambda i: (i, 0, 0)), dspec(lambda i: (jnp.minimum(i + 1, nt - 1), 0, 0)),
                  pl.BlockSpec((tc, D_MODEL), lambda i: (i, 0)),
                  pl.BlockSpec((None, 6, D_MODEL), lambda i: (mrow(i), 0, 0)),
                  pl.BlockSpec((tc, 128), lambda i: (i, 0)),
                  pl.BlockSpec((1, D_MODEL), lambda i: (0, 0)),
                  pl.BlockSpec(memory_space=pl.ANY)],
        out_specs=pl.BlockSpec((tc, D_MODEL), lambda i: (i, 0)),
        out_shape=jax.ShapeDtypeStruct((rm, D_MODEL), F32),
        scratch_shapes=[pltpu.VMEM((2, 2, tc, D_MODEL), F32), pltpu.SemaphoreType.DMA((2,))],
        compiler_params=_params(("arbitrary",)),
        name="moe_combine",
    )(dest3, dest3, x, mod_l, wpad, gnorm, y)


def _moe(route, h2, x, mod_l, gnorm, w_g, w_u, w_d, row0, final):
    rm = h2.shape[0]
    bm = MOE_BM
    a = 2 * rm
    nb = a // bm + N_EXPERTS
    p_rows = nb * bm
    flat_e = route[0:2].astype(jnp.int32).reshape(a)
    onehot = (flat_e[:, None] == jnp.arange(N_EXPERTS, dtype=jnp.int32)[None, :]).astype(jnp.int32)
    csum = jnp.cumsum(onehot, axis=0)
    rank = jnp.take_along_axis(csum, flat_e[:, None], axis=1)[:, 0] - 1
    counts = csum[-1]
    pcounts = (counts + bm - 1) // bm * bm
    pend = jnp.cumsum(pcounts)
    pstart = pend - pcounts
    dest = pstart[flat_e] + rank
    blk_e = jnp.minimum(jnp.searchsorted(pend, jnp.arange(nb, dtype=jnp.int32) * bm, side='right'),
                        N_EXPERTS - 1).astype(jnp.int32)
    n_used = (pend[-1] // bm).astype(jnp.int32).reshape(1)

    def tiles(t):
        return dest.reshape(2, rm // t, t).transpose(1, 0, 2)

    xs = _dispatch(tiles(T_DISPATCH), h2, p_rows)
    y = _ffn(blk_e, n_used, xs, w_g, w_u, w_d)
    wpad = jnp.pad(route[2:4].T, ((0, 0), (0, 126)))
    return _combine(tiles(T_COMBINE), x, mod_l, wpad, gnorm, y, row0, final)


def kernel(x, c, ctx, c_ctx, norm_mix, norm_ffn, w_ada, b_ada, w_in, attn_sink, ret_decay_fwd, ret_decay_bwd,
           w_branch_attn, w_branch_fourier, w_branch_ret, w_out, w_router_group, b_router_group,
           w_router_expert, b_router_expert, w_exp_gate, w_exp_up, w_exp_down, norm_final):
    tabs = [jnp.asarray(t) for t in _rope_tables()]
    dft = [jnp.asarray(t).astype(BF16) for t in _dft_tables()]

    cc = jnp.zeros((MOD_ROWS, D_MODEL), F32).at[0:BATCH].set(c).at[CTX_MOD_ROW].set(c_ctx)
    mod = _ada(cc, w_ada, b_ada).reshape(DEPTH, MOD_ROWS, 6, D_MODEL)

    xf = jnp.concatenate([ctx.reshape(RC, D_MODEL), x.reshape(RL, D_MODEL)], axis=0)
    for l in range(DEPTH):
        need_ctx = l < DEPTH - 1
        row0 = 0 if need_ctx else RC
        mod_l = mod[l]
        qa, ka, va, qr, kr, vr, gr, fu, gm = _proj(xf, mod_l, norm_mix[l][None, :], w_in[l].astype(BF16), tabs)
        oa = _attention(attn_sink[l], qa, ka, va, need_ctx)
        of = _fourier(fu, dft, need_ctx)
        ret = _retention(qr, kr, vr, gr, _retention_tables(ret_decay_fwd[l], ret_decay_bwd[l]))
        wr = jnp.zeros((ROUTER_ROWS, D_MODEL), F32)
        wr = wr.at[0:N_GROUPS].set(w_router_group[l].T).at[8:8 + N_EXPERTS].set(w_router_expert[l].T)
        br = jnp.full((ROUTER_ROWS,), NEG, F32)
        br = br.at[0:N_GROUPS].set(b_router_group[l]).at[8:8 + N_EXPERTS].set(b_router_expert[l])
        wrh, wrl = _split(wr)
        brb = jnp.broadcast_to(br[:, None], (ROUTER_ROWS, TM_MERGE))
        x_mid, h2, route = _merge(oa, of, ret, gm, xf, mod_l, norm_ffn[l][None, :],
                                  w_branch_attn[l].astype(BF16), w_branch_fourier[l].astype(BF16),
                                  w_branch_ret[l].astype(BF16), w_out[l].astype(BF16), wrh, wrl, brb, row0)
        final = l == DEPTH - 1
        xf = _moe(route, h2, x_mid, mod_l, norm_final[None, :], w_exp_gate[l], w_exp_up[l], w_exp_down[l],
                  row0, final)
    return xf.reshape(BATCH, SEQ, D_MODEL)
```

```python
import functools

import numpy as np
import jax
import jax.numpy as jnp
from jax import lax
from jax.experimental import pallas as pl
from jax.experimental.pallas import tpu as pltpu

F32 = jnp.float32
BF16 = jnp.bfloat16

D_MODEL = 1024
BATCH = 8
SEQ = 2048
DEPTH = 2
CTX_LEN = 256
GRID_W = 64
HEAD_DIM = 64
ATTN_HEADS = 8
ATTN_KV_HEADS = 2
ATTN_GROUP = ATTN_HEADS // ATTN_KV_HEADS
ATTN_BLOCK = 128
RET_HEADS = 8
RET_DK = 64
RET_CHUNK = 128
FOURIER_GROUPS = 4
FOURIER_DIM = 128
N_GROUPS = 4
EXPERTS_PER_GROUP = 8
N_EXPERTS = N_GROUPS * EXPERTS_PER_GROUP
EXPERT_HIDDEN = 512
ROPE_BASE = 10000.0
NORM_EPS = 1e-6
GN_EPS = 1e-5

W = 512
IN_COLS = 6400
RC = BATCH * CTX_LEN
RL = BATCH * SEQ
R = RC + RL
MOD_ROWS = 16
CTX_MOD_ROW = 8

VMEM_LIMIT = 52 * 1024 * 1024

TM_PROJ = 512
TM_MERGE = 512
TN_ADA = 1536
TR_FOURIER = 512
T_DISPATCH = 256
T_COMBINE = 256
MOE_BM = 256
NEG = -1e30
ROUTER_ROWS = 40


def _dot(a, b):
    return jnp.dot(a, b, preferred_element_type=F32)


def _dot_nt(a, b):
    return lax.dot_general(a, b, (((1,), (1,)), ((), ())), preferred_element_type=F32)


def _dot_tn(a, b):
    return lax.dot_general(a, b, (((0,), (0,)), ((), ())), preferred_element_type=F32)


def _split(x):
    hi = x.astype(BF16)
    lo = (x - hi.astype(F32)).astype(BF16)
    return hi, lo


def _sigmoid(x):
    return 1.0 / (1.0 + jnp.exp(-x))


def _params(sem, vmem=VMEM_LIMIT):
    return pltpu.CompilerParams(dimension_semantics=sem, vmem_limit_bytes=vmem)


def _mod_row(row0, tm):
    def f(i):
        g0 = i * tm + row0
        return jnp.where(g0 < RC, CTX_MOD_ROW, (g0 - RC) // SEQ)
    return f


def _rope_tables():
    pos = np.arange(SEQ, dtype=np.float64)
    row = np.floor(pos / GRID_W)
    col = pos % GRID_W

    def cs(p, nf):
        inv = ROPE_BASE ** (-np.arange(nf, dtype=np.float64) / nf)
        ang = p[:, None] * inv[None, :]
        return np.cos(ang), np.sin(ang)

    rc, rs = cs(row, HEAD_DIM // 4)
    cc, cs_ = cs(col, HEAD_DIM // 4)
    cos_a = np.concatenate([rc, rc, cc, cc], axis=1)
    sin_a = np.concatenate([-rs, rs, -cs_, cs_], axis=1)
    tc, ts = cs(pos, RET_DK // 2)
    cos_r = np.concatenate([tc, tc], axis=1)
    sin_r = np.concatenate([-ts, ts], axis=1)

    def full(t, ident):
        t2 = np.concatenate([t, t], axis=1)
        return np.concatenate([np.full_like(t2, ident), t2], axis=0).astype(np.float32)

    return full(cos_a, 1.0), full(sin_a, 0.0), full(cos_r, 1.0), full(sin_r, 0.0)


def _dft_tables():
    def cs(n):
        k = np.arange(n, dtype=np.int64)
        m = (k[:, None] * k[None, :]) % n
        ang = 2.0 * np.pi * m.astype(np.float64) / n
        return np.cos(ang), np.sin(ang)

    c128, s128 = cs(FOURIER_DIM)
    eye = np.eye(FOURIER_GROUPS)
    bdc = np.kron(eye, c128).astype(np.float32)
    bds = np.kron(eye, s128).astype(np.float32)
    cn, sn = cs(SEQ)
    w2 = np.concatenate([cn, -sn], axis=1).astype(np.float32)
    cl, sl = cs(CTX_LEN)
    w2c = np.concatenate([cl, -sl], axis=1).astype(np.float32)
    return bdc, bds, w2, w2c


def _retention_tables(dec_f, dec_b):
    lg_f = jax.nn.log_sigmoid(dec_f.astype(F32))
    lg_b = jax.nn.log_sigmoid(dec_b.astype(F32))
    i = jnp.arange(RET_CHUNK)
    diff = (i[:, None] - i[None, :]).astype(F32)
    fwd = jnp.exp(jnp.maximum(diff, 0.0)[None] * lg_f[:, None, None])
    bwd = jnp.exp(jnp.maximum(-diff, 0.0)[None] * lg_b[:, None, None])
    dcomb = jnp.where((diff >= 0)[None], fwd, bwd)
    fi = i.astype(F32)
    lanes = lambda t: jnp.repeat(t, RET_DK, axis=1)
    xi_f = lanes(jnp.exp((fi + 1.0)[:, None] * lg_f[None, :]))
    zt_f = lanes(jnp.exp((RET_CHUNK - 1 - fi)[:, None] * lg_f[None, :]))
    xi_b = lanes(jnp.exp((RET_CHUNK - fi)[:, None] * lg_b[None, :]))
    zt_b = lanes(jnp.exp(fi[:, None] * lg_b[None, :]))
    g_f = jnp.repeat(jnp.exp(RET_CHUNK * lg_f), RET_DK).reshape(2, 256, 1)
    g_b = jnp.repeat(jnp.exp(RET_CHUNK * lg_b), RET_DK).reshape(2, 256, 1)
    g_f = jnp.broadcast_to(g_f, (2, 256, 256))
    g_b = jnp.broadcast_to(g_b, (2, 256, 256))
    return dcomb, xi_f, zt_f, xi_b, zt_b, g_f, g_b


def _ada_body(c_ref, w_ref, b_ref, o_ref):
    c = c_ref[...]
    s = c * _sigmoid(c)
    sh, sl = _split(s)
    wh, wl = _split(w_ref[...])
    o_ref[...] = _dot(sh, wh) + _dot(sl, wh) + _dot(sh, wl) + b_ref[...]


def _ada(cc, w_ada, b_ada):
    nt = 6 * D_MODEL // TN_ADA
    return pl.pallas_call(
        _ada_body,
        grid=(DEPTH, nt),
        in_specs=[
            pl.BlockSpec((MOD_ROWS, D_MODEL), lambda l, j: (0, 0)),
            pl.BlockSpec((None, D_MODEL, TN_ADA), lambda l, j: (l, 0, j)),
            pl.BlockSpec((None, 1, TN_ADA), lambda l, j: (l, 0, j)),
        ],
        out_specs=pl.BlockSpec((None, MOD_ROWS, TN_ADA), lambda l, j: (l, 0, j)),
        out_shape=jax.ShapeDtypeStruct((DEPTH, MOD_ROWS, 6 * D_MODEL), F32),
        compiler_params=_params(("arbitrary", "arbitrary")),
        name="ada_mod",
    )(cc, w_ada, b_ada.reshape(DEPTH, 1, 6 * D_MODEL))


def _rope(xc, cos, sin, half):
    fwd = pltpu.roll(xc, 128 - half, axis=1)
    bwd = pltpu.roll(xc, half, axis=1)
    lane = lax.broadcasted_iota(jnp.int32, xc.shape, 1)
    first = (lane & (2 * half - 1)) < half
    return xc * cos + jnp.where(first, fwd, bwd) * sin


def _proj_body(x_ref, mod_ref, gn_ref, w_ref, ca_ref, sa_ref, cr_ref, sr_ref,
               qa_ref, ka_ref, va_ref, qr_ref, kr_ref, vr_ref, gr_ref, fu_ref, gm_ref):
    x = x_ref[...]
    ms = jnp.mean(x * x, axis=-1, keepdims=True)
    y = x * lax.rsqrt(ms + NORM_EPS) * gn_ref[...]
    h = y * (1.0 + mod_ref[1:2, :]) + mod_ref[0:1, :]
    hb = h.astype(BF16)

    def proj(c0, width):
        return _dot(hb, w_ref[:, c0:c0 + width])

    ca, sa, cr, sr = ca_ref[...], sa_ref[...], cr_ref[...], sr_ref[...]

    qa = proj(0, W) * (HEAD_DIM ** -0.5)
    for c in range(W // 128):
        qa_ref[:, c * 128:(c + 1) * 128] = _rope(qa[:, c * 128:(c + 1) * 128], ca, sa, 16).astype(BF16)
    kv = proj(W, 256)
    ka = _rope(kv[:, 0:128], ca, sa, 16).astype(BF16)
    ka_ref[0] = ka[:, 0:64]
    ka_ref[1] = ka[:, 64:128]
    va = kv[:, 128:256].astype(BF16)
    va_ref[0] = va[:, 0:64]
    va_ref[1] = va[:, 64:128]
    qr = proj(768, W)
    kr = proj(1280, W) * (RET_DK ** -0.5)
    for c in range(W // 128):
        sl = slice(c * 128, (c + 1) * 128)
        qr_ref[:, sl] = _rope(qr[:, sl], cr, sr, 32).astype(BF16)
        kr_ref[:, sl] = _rope(kr[:, sl], cr, sr, 32).astype(BF16)
    vr_ref[...] = proj(1792, W).astype(BF16)
    g = proj(2304, W)
    gr_ref[...] = (g * _sigmoid(g)).astype(BF16)
    fu_ref[...] = proj(2816, W).astype(BF16)
    for c in range(3):
        gm_ref[:, c * D_MODEL:(c + 1) * D_MODEL] = _sigmoid(proj(3328 + c * D_MODEL, D_MODEL)).astype(BF16)


def _proj(x, mod_l, gnorm, w_in_bf, tabs, layer):
    tm = TM_PROJ
    nt = R // tm
    nc = RC // tm

    def tab_map(i):
        return (jnp.where(i < nc, i, nc + (i - nc) % (SEQ // tm)), 0)

    row = lambda i: (i, 0)
    wide = lambda n: pl.BlockSpec((tm, n), row)
    kv_spec = pl.BlockSpec((2, tm, 64), lambda i: (0, i, 0))
    sds = lambda n: jax.ShapeDtypeStruct((R, n), BF16)
    kv_sds = jax.ShapeDtypeStruct((2, R, 64), BF16)
    mrow = _mod_row(0, tm)
    return pl.pallas_call(
        _proj_body,
        grid=(nt,),
        in_specs=[
            wide(D_MODEL),
            pl.BlockSpec((None, 6, D_MODEL), lambda i: (mrow(i), 0, 0)),
            pl.BlockSpec((1, D_MODEL), lambda i: (0, 0)),
            pl.BlockSpec((None, D_MODEL, IN_COLS), lambda i: (layer, 0, 0), pipeline_mode=pl.Buffered(1)),
        ] + [pl.BlockSpec((tm, 128), tab_map)] * 4,
        out_specs=[wide(W), kv_spec, kv_spec, wide(W), wide(W), wide(W), wide(W), wide(W), wide(3 * D_MODEL)],
        out_shape=[sds(W), kv_sds, kv_sds, sds(W), sds(W), sds(W), sds(W), sds(W), sds(3 * D_MODEL)],
        compiler_params=_params(("arbitrary",)),
        name="in_proj",
    )(x, mod_l, gnorm, w_in_bf, *tabs)


def _attn_body(sink_ref, q_ref, kc_ref, kp_ref, kk_ref, kn_ref, vc_ref, vp_ref, vk_ref, vn_ref, o_ref, *, q_lo):
    h = pl.program_id(1)
    qi = pl.program_id(2) + q_lo
    is_lat = qi >= 2
    m = qi - 2
    q = q_ref[...]
    q4 = jnp.concatenate([q[:, g * 64:(g + 1) * 64] for g in range(ATTN_GROUP)], axis=0)
    nr = ATTN_GROUP * ATTN_BLOCK
    ri = lax.broadcasted_iota(jnp.int32, (nr, ATTN_BLOCK), 0) & (ATTN_BLOCK - 1)
    ci = lax.broadcasted_iota(jnp.int32, (nr, ATTN_BLOCK), 1)
    far = 4 * ATTN_BLOCK
    off_p = jnp.where(is_lat & (m >= 1), 0, far)
    off_k = jnp.where(is_lat, 0, far)
    off_n = jnp.where(is_lat & (m <= SEQ // ATTN_BLOCK - 2), 0, far)
    ok_p = ci >= ri + off_p
    ok_k = ci >= off_k
    ok_n = ci + off_n <= ri
    s_c = _dot_nt(q4, kc_ref[...])
    s_p = jnp.where(ok_p, _dot_nt(q4, kp_ref[...]), NEG)
    s_k = jnp.where(ok_k, _dot_nt(q4, kk_ref[...]), NEG)
    s_n = jnp.where(ok_n, _dot_nt(q4, kn_ref[...]), NEG)
    rg = lax.broadcasted_iota(jnp.int32, (nr, 1), 0) >> 7
    sk = jnp.zeros((nr, 1), F32)
    for g in range(ATTN_GROUP):
        sk = jnp.where(rg == g, sink_ref[h * ATTN_GROUP + g], sk)
    rmax = lambda s: jnp.max(s, axis=-1, keepdims=True)
    mx = jnp.maximum(jnp.maximum(rmax(s_c), rmax(s_p)), jnp.maximum(rmax(s_k), rmax(s_n)))
    mx = jnp.maximum(mx, sk)
    p_c, p_p, p_k, p_n = (jnp.exp(s - mx) for s in (s_c, s_p, s_k, s_n))
    rsum = lambda p: jnp.sum(p, axis=-1, keepdims=True)
    den = rsum(p_c) + rsum(p_p) + rsum(p_k) + rsum(p_n) + jnp.exp(sk - mx)
    o = (_dot(p_c.astype(BF16), vc_ref[...]) + _dot(p_p.astype(BF16), vp_ref[...])
         + _dot(p_k.astype(BF16), vk_ref[...]) + _dot(p_n.astype(BF16), vn_ref[...]))
    o = o / den
    o_ref[...] = jnp.concatenate(
        [o[g * ATTN_BLOCK:(g + 1) * ATTN_BLOCK] for g in range(ATTN_GROUP)], axis=1).astype(BF16)


def _attention(sink, qa, ka, va, need_ctx):
    q_lo = 0 if need_ctx else 2
    nq = SEQ // ATTN_BLOCK + (2 if need_ctx else 0)
    nb = SEQ // ATTN_BLOCK
    lat0 = RC // ATTN_BLOCK

    def qrow(b, qi):
        return jnp.where(qi < 2, b * 2 + qi, lat0 + b * nb + qi - 2)

    def q_map(b, h, i):
        return (qrow(b, i + q_lo), h)

    def loc(delta):
        def f(b, h, i):
            m = jnp.clip(i + q_lo - 2 + delta, 0, nb - 1)
            return (h, lat0 + b * nb + m, 0)
        return f

    ctx_spec = pl.BlockSpec((None, CTX_LEN, 64), lambda b, h, i: (h, b, 0))
    loc_spec = lambda d: pl.BlockSpec((None, ATTN_BLOCK, 64), loc(d))
    kv_specs = [ctx_spec, loc_spec(-1), loc_spec(0), loc_spec(1)]
    return pl.pallas_call(
        functools.partial(_attn_body, q_lo=q_lo),
        grid=(BATCH, ATTN_KV_HEADS, nq),
        in_specs=[pl.BlockSpec(memory_space=pltpu.SMEM),
                  pl.BlockSpec((ATTN_BLOCK, 256), q_map)] + kv_specs + kv_specs,
        out_specs=pl.BlockSpec((ATTN_BLOCK, 256), q_map),
        out_shape=jax.ShapeDtypeStruct((R, W), BF16),
        compiler_params=_params(("arbitrary", "arbitrary", "arbitrary")),
        name="window_attn",
    )(sink, qa, ka, ka, ka, ka, va, va, va, va)


def _fourier_lat_body(u_ref, bdc_ref, bds_ref, w2_ref, o_ref, as_ref):
    @pl.when(pl.program_id(1) == 0)
    def _():
        u = u_ref[...]
        as_ref[0:SEQ, :] = _dot(u, bdc_ref[...]).astype(BF16)
        as_ref[SEQ:2 * SEQ, :] = _dot(u, bds_ref[...]).astype(BF16)

    o_ref[...] = (_dot(w2_ref[...], as_ref[...]) * ((SEQ * FOURIER_DIM) ** -0.5)).astype(BF16)


def _fourier_ctx_body(u_ref, bdc_ref, bds_ref, w2_ref, of_in_ref, o_ref):
    del of_in_ref
    u = u_ref[...]
    a = _dot(u, bdc_ref[...]).astype(BF16)
    s = _dot(u, bds_ref[...]).astype(BF16)
    z = _dot(w2_ref[...], jnp.concatenate([a, s], axis=0))
    o_ref[...] = (z * ((CTX_LEN * FOURIER_DIM) ** -0.5)).astype(BF16)


def _fourier(fu, dft, need_ctx):
    bdc, bds, w2, w2c = dft
    tr = TR_FOURIER
    nj = SEQ // tr
    full = lambda a: pl.BlockSpec(a.shape, lambda *_: (0,) * a.ndim)
    of = pl.pallas_call(
        _fourier_lat_body,
        grid=(BATCH, nj),
        in_specs=[pl.BlockSpec((SEQ, W), lambda b, j: (1 + b, 0)), full(bdc), full(bds),
                  pl.BlockSpec((tr, 2 * SEQ), lambda b, j: (j, 0))],
        out_specs=pl.BlockSpec((tr, W), lambda b, j: (RC // tr + b * nj + j, 0)),
        out_shape=jax.ShapeDtypeStruct((R, W), BF16),
        scratch_shapes=[pltpu.VMEM((2 * SEQ, W), BF16)],
        compiler_params=_params(("arbitrary", "arbitrary")),
        name="fourier_latent",
    )(fu, bdc, bds, w2)
    if not need_ctx:
        return of
    return pl.pallas_call(
        _fourier_ctx_body,
        grid=(BATCH,),
        in_specs=[pl.BlockSpec((CTX_LEN, W), lambda b: (b, 0)), full(bdc), full(bds), full(w2c),
                  pl.BlockSpec(memory_space=pl.ANY)],
        out_specs=pl.BlockSpec((CTX_LEN, W), lambda b: (b, 0)),
        out_shape=jax.ShapeDtypeStruct((R, W), BF16),
        input_output_aliases={4: 0},
        compiler_params=_params(("arbitrary",)),
        name="fourier_ctx",
    )(fu, bdc, bds, w2c, of)


def _retention_body(qc_ref, kc_ref, vc_ref, ql_ref, kl_ref, vl_ref, g_ref,
                    dcomb_ref, xif_ref, ztf_ref, xib_ref, ztb_ref, gf_ref, gb_ref, mbd_ref, avg_ref,
                    o_ref, os_ref, st_ref):
    j = pl.program_id(1)
    C = RET_CHUNK
    nl = SEQ // C

    def group_parts(q, k, v, gi):
        sl = slice(gi * 256, (gi + 1) * 256)
        return q[:, sl], k[:, sl], v[:, sl], sl

    lane_head = lax.broadcasted_iota(jnp.int32, (C, 256), 1) // RET_DK

    def chunk_fwd(q, k, v, r0):
        for gi in range(2):
            q4, k4, v4, sl = group_parts(q, k, v, gi)
            s_prev = st_ref[gi]
            q4f = q4.astype(F32)
            o4 = _dot((q4f * xif_ref[:, sl]).astype(BF16), s_prev.astype(BF16))
            intra = jnp.zeros((C, 256), F32)
            for hh in range(4):
                qm = jnp.where(lane_head == hh, q4f, 0.0).astype(BF16)
                p = (_dot_nt(qm, k4) * dcomb_ref[gi * 4 + hh]).astype(BF16)
                intra = jnp.where(lane_head == hh, _dot(p, v4), intra)
            os_ref[pl.ds(r0, C), sl] = o4 + intra
            u = _dot_tn(k4, (v4.astype(F32) * ztf_ref[:, sl]).astype(BF16))
            st_ref[gi] = gf_ref[gi] * s_prev + mbd_ref[...] * u

    def chunk_bwd(q, k, v, r0):
        for gi in range(2):
            q4, k4, v4, sl = group_parts(q, k, v, gi)
            s_prev = st_ref[gi]
            cross = _dot((q4.astype(F32) * xib_ref[:, sl]).astype(BF16), s_prev.astype(BF16))
            os_ref[pl.ds(r0, C), sl] = os_ref[pl.ds(r0, C), sl] + cross
            u = _dot_tn(k4, (v4.astype(F32) * ztb_ref[:, sl]).astype(BF16))
            st_ref[gi] = gb_ref[gi] * s_prev + mbd_ref[...] * u

    @pl.when(j == 0)
    def _():
        st_ref[...] = jnp.zeros_like(st_ref)
        for c in range(CTX_LEN // C):
            rs = slice(c * C, (c + 1) * C)
            chunk_fwd(qc_ref[rs, :], kc_ref[rs, :], vc_ref[rs, :], c * C)

        def fbody(c, carry):
            r0 = pl.multiple_of(c * C, C)
            rs = pl.ds(r0, C)
            chunk_fwd(ql_ref[rs, :], kl_ref[rs, :], vl_ref[rs, :], CTX_LEN + r0)
            return carry

        lax.fori_loop(0, nl, fbody, 0)

        st_ref[...] = jnp.zeros_like(st_ref)
        for c in reversed(range(CTX_LEN // C)):
            rs = slice(c * C, (c + 1) * C)
            chunk_bwd(qc_ref[rs, :], kc_ref[rs, :], vc_ref[rs, :], c * C)

        def bbody(t, carry):
            r0 = pl.multiple_of((nl - 1 - t) * C, C)
            rs = pl.ds(r0, C)
            chunk_bwd(ql_ref[rs, :], kl_ref[rs, :], vl_ref[rs, :], CTX_LEN + r0)
            return carry

        lax.fori_loop(0, nl, bbody, 0)

    o = os_ref[pl.ds(pl.multiple_of(j * 256, 256), 256), :]
    avg = avg_ref[...]
    oh, ol = _split(o)
    mu = _dot(oh, avg) + _dot(ol, avg)
    d = o - mu
    vh, vl = _split(d * d)
    var = _dot(vh, avg) + _dot(vl, avg)
    o_ref[...] = (g_ref[...].astype(F32) * d * lax.rsqrt(var + GN_EPS)).astype(BF16)


def _retention(qr, kr, vr, gr, rtabs):
    nj = 1 + SEQ // 256

    def out_map(b, j):
        return (jnp.where(j == 0, b, RC // 256 + b * (SEQ // 256) + j - 1), 0)

    ctx = pl.BlockSpec((CTX_LEN, W), lambda b, j: (b, 0))
    lat = pl.BlockSpec((SEQ, W), lambda b, j: (1 + b, 0))
    full = lambda a: pl.BlockSpec(a.shape, lambda *_: (0,) * a.ndim)
    avg = jnp.asarray(np.kron(np.eye(RET_HEADS), np.full((RET_DK, RET_DK), 1.0 / RET_DK)).astype(np.float32)).astype(BF16)
    mbd = jnp.asarray(np.kron(np.eye(4), np.ones((RET_DK, RET_DK))).astype(np.float32))
    tabs = list(rtabs) + [mbd, avg]
    return pl.pallas_call(
        _retention_body,
        grid=(BATCH, nj),
        in_specs=[ctx, ctx, ctx, lat, lat, lat, pl.BlockSpec((256, W), out_map)] + [full(t) for t in tabs],
        out_specs=pl.BlockSpec((256, W), out_map),
        out_shape=jax.ShapeDtypeStruct((R, W), BF16),
        scratch_shapes=[pltpu.VMEM((CTX_LEN + SEQ, W), F32), pltpu.VMEM((2, 256, 256), F32)],
        compiler_params=_params(("arbitrary", "arbitrary")),
        name="retention",
    )(qr, kr, vr, qr, kr, vr, gr, *tabs)


def _merge_body(oa_ref, of_ref, rt_ref, gm_ref, x_ref, mod_ref, gn_ref, wba_ref, wbf_ref, wbr_ref, wout_ref,
                wrh_ref, wrl_ref, br_ref, xo_ref, h2_ref, ro_ref):
    gm = gm_ref[...].astype(F32)
    z = (gm[:, 0:D_MODEL] * _dot(oa_ref[...], wba_ref[...])
         + gm[:, D_MODEL:2 * D_MODEL] * _dot(of_ref[...], wbf_ref[...])
         + gm[:, 2 * D_MODEL:3 * D_MODEL] * _dot(rt_ref[...], wbr_ref[...]))
    y = _dot(z.astype(BF16), wout_ref[...])
    x = x_ref[...] + mod_ref[2:3, :] * y
    xo_ref[...] = x
    ms = jnp.mean(x * x, axis=-1, keepdims=True)
    hn = x * lax.rsqrt(ms + NORM_EPS) * gn_ref[...]
    h2 = hn * (1.0 + mod_ref[4:5, :]) + mod_ref[3:4, :]
    h2_ref[...] = h2
    hh, hl = _split(h2)
    wh, wl = wrh_ref[...], wrl_ref[...]
    lg = _dot_nt(wh, hh) + _dot_nt(wh, hl) + _dot_nt(wl, hh) + br_ref[...]
    tm = lg.shape[1]
    row8 = lax.broadcasted_iota(jnp.int32, (8, tm), 0)
    lgg = lg[0:8, :]
    mg = jnp.max(lgg, axis=0, keepdims=True)
    grp = jnp.min(jnp.where(lgg == mg, row8, 8), axis=0, keepdims=True)
    pg = 1.0 / jnp.sum(jnp.exp(lgg - mg), axis=0, keepdims=True)
    lin = jnp.zeros((8, tm), F32)
    for g in range(N_GROUPS):
        lin = jnp.where(grp == g, lg[8 + 8 * g:16 + 8 * g, :], lin)
    v1 = jnp.max(lin, axis=0, keepdims=True)
    i1 = jnp.min(jnp.where(lin == v1, row8, 8), axis=0, keepdims=True)
    rest = jnp.where(row8 == i1, -jnp.inf, lin)
    v2 = jnp.max(rest, axis=0, keepdims=True)
    i2 = jnp.min(jnp.where(rest == v2, row8, 8), axis=0, keepdims=True)
    e2 = jnp.exp(v2 - v1)
    w1 = pg / (1.0 + e2)
    w2 = pg * e2 / (1.0 + e2)
    e_1 = (grp * EXPERTS_PER_GROUP + i1).astype(F32)
    e_2 = (grp * EXPERTS_PER_GROUP + i2).astype(F32)
    out = jnp.where(row8 == 0, e_1, jnp.where(row8 == 1, e_2, jnp.where(row8 == 2, w1, jnp.where(row8 == 3, w2, 0.0))))
    ro_ref[...] = out


def _merge(oa, of, ret, gm, x, mod_l, gnorm, wba, wbf, wbr, wout, wrh, wrl, brb, row0):
    tm = TM_MERGE
    rm = R - row0
    nt = rm // tm
    off = row0 // tm
    src = lambda n: pl.BlockSpec((tm, n), lambda i: (i + off, 0))
    dst = lambda n: pl.BlockSpec((tm, n), lambda i: (i, 0))
    full = lambda a: pl.BlockSpec(a.shape, lambda *_: (0,) * a.ndim)
    mrow = _mod_row(row0, tm)
    return pl.pallas_call(
        _merge_body,
        grid=(nt,),
        in_specs=[src(W), src(W), src(W), src(3 * D_MODEL), src(D_MODEL),
                  pl.BlockSpec((None, 6, D_MODEL), lambda i: (mrow(i), 0, 0)),
                  full(gnorm), full(wba), full(wbf), full(wbr), full(wout), full(wrh), full(wrl), full(brb)],
        out_specs=[dst(D_MODEL), dst(D_MODEL), pl.BlockSpec((8, tm), lambda i: (0, i))],
        out_shape=[jax.ShapeDtypeStruct((rm, D_MODEL), F32), jax.ShapeDtypeStruct((rm, D_MODEL), F32),
                   jax.ShapeDtypeStruct((8, rm), F32)],
        compiler_params=_params(("arbitrary",)),
        name="merge_router",
    )(oa, of, ret, gm, x, mod_l, gnorm, wba, wbf, wbr, wout, wrh, wrl, brb)


def _dispatch_body(dest_ref, h_ref, xs_in_ref, xs_ref, sem):
    del xs_in_ref
    td = h_ref.shape[0]

    def body(r, carry):
        for k in range(2):
            pltpu.make_async_copy(h_ref.at[pl.ds(r, 1)], xs_ref.at[pl.ds(dest_ref[k, r], 1)], sem.at[k]).start()
        return carry

    lax.fori_loop(0, td, body, 0, unroll=8)
    for k in range(2):
        pltpu.make_async_copy(h_ref, xs_ref.at[pl.ds(0, td)], sem.at[k]).wait()


def _dispatch(dest3, h2, p_rows):
    td = T_DISPATCH
    rm = h2.shape[0]
    xs0 = jnp.zeros((p_rows, D_MODEL), F32)
    return pl.pallas_call(
        _dispatch_body,
        grid=(rm // td,),
        in_specs=[pl.BlockSpec((None, 2, td), lambda i: (i, 0, 0), memory_space=pltpu.SMEM),
                  pl.BlockSpec((td, D_MODEL), lambda i: (i, 0)),
                  pl.BlockSpec(memory_space=pl.ANY)],
        out_specs=pl.BlockSpec(memory_space=pl.ANY),
        out_shape=jax.ShapeDtypeStruct((p_rows, D_MODEL), F32),
        scratch_shapes=[pltpu.SemaphoreType.DMA((2,))],
        input_output_aliases={2: 0},
        compiler_params=_params(("arbitrary",)),
        name="moe_dispatch",
    )(dest3, h2, xs0)


def _ffn_body(be_ref, nu_ref, xs_ref, wg_ref, wu_ref, wd_ref, y_ref, wgb, wub, wdb):
    b = pl.program_id(0)
    prev = be_ref[jnp.maximum(b - 1, 0)]
    fresh = (b == 0) | (be_ref[b] != prev)

    @pl.when(fresh)
    def _():
        wgb[...] = wg_ref[...].astype(BF16)
        wub[...] = wu_ref[...].astype(BF16)
        wdb[...] = wd_ref[...].astype(BF16)

    @pl.when(b < nu_ref[0])
    def _():
        x = xs_ref[...].astype(BF16)
        g = _dot(x, wgb[...])
        u = _dot(x, wub[...])
        hmid = (g * _sigmoid(g) * u).astype(BF16)
        y_ref[...] = _dot(hmid, wdb[...])


def _ffn(blk_e, n_used, xs, w_g, w_u, w_d, layer):
    bm = MOE_BM
    p_rows = xs.shape[0]
    nb = p_rows // bm
    rows = lambda b, be, nu: (jnp.minimum(b, nu[0] - 1), 0)
    wmap = lambda b, be, nu: (layer, be[b], 0, 0)
    grid_spec = pltpu.PrefetchScalarGridSpec(
        num_scalar_prefetch=2,
        grid=(nb,),
        in_specs=[pl.BlockSpec((bm, D_MODEL), rows),
                  pl.BlockSpec((None, None, D_MODEL, EXPERT_HIDDEN), wmap),
                  pl.BlockSpec((None, None, D_MODEL, EXPERT_HIDDEN), wmap),
                  pl.BlockSpec((None, None, EXPERT_HIDDEN, D_MODEL), wmap)],
        out_specs=pl.BlockSpec((bm, D_MODEL), rows),
        scratch_shapes=[pltpu.VMEM((D_MODEL, EXPERT_HIDDEN), BF16), pltpu.VMEM((D_MODEL, EXPERT_HIDDEN), BF16),
                        pltpu.VMEM((EXPERT_HIDDEN, D_MODEL), BF16)],
    )
    return pl.pallas_call(
        _ffn_body,
        grid_spec=grid_spec,
        out_shape=jax.ShapeDtypeStruct((p_rows, D_MODEL), F32),
        compiler_params=_params(("arbitrary",)),
        name="moe_experts",
    )(blk_e, n_used, xs, w_g, w_u, w_d)


def _combine_body(dcur_ref, dnxt_ref, x_ref, mod_ref, wb_ref, gn_ref, y_ref, o_ref, ybuf, sem, *, final):
    i = pl.program_id(0)
    n = pl.num_programs(0)
    tc = x_ref.shape[0]
    slot = i % 2

    def issue(dref, sl):
        def body(r, carry):
            for k in range(2):
                pltpu.make_async_copy(y_ref.at[pl.ds(dref[k, r], 1)], ybuf.at[sl, k, pl.ds(r, 1)], sem.at[sl]).start()
            return carry
        lax.fori_loop(0, tc, body, 0, unroll=8)

    @pl.when(i == 0)
    def _():
        issue(dcur_ref, 0)

    @pl.when(i + 1 < n)
    def _():
        issue(dnxt_ref, 1 - slot)

    for k in range(2):
        pltpu.make_async_copy(y_ref.at[pl.ds(0, tc)], ybuf.at[slot, k], sem.at[slot]).wait()
    wb = wb_ref[...]
    f = wb[:, 0:1] * ybuf[slot, 0] + wb[:, 1:2] * ybuf[slot, 1]
    x = x_ref[...] + mod_ref[5:6, :] * f
    if final:
        ms = jnp.mean(x * x, axis=-1, keepdims=True)
        x = x * lax.rsqrt(ms + NORM_EPS) * gn_ref[...]
    o_ref[...] = x


def _combine(dest3, x, mod_l, wpad, gnorm, y, row0, final):
    tc = T_COMBINE
    rm = x.shape[0]
    nt = rm // tc
    mrow = _mod_row(row0, tc)
    dspec = lambda f: pl.BlockSpec((None, 2, tc), f, memory_space=pltpu.SMEM)
    return pl.pallas_call(
        functools.partial(_combine_body, final=final),
        grid=(nt,),
        in_specs=[dspec(lambda i: (i, 0, 0)), dspec(lambda i: (jnp.minimum(i + 1, nt - 1), 0, 0)),
                  pl.BlockSpec((tc, D_MODEL), lambda i: (i, 0)),
                  pl.BlockSpec((None, 6, D_MODEL), lambda i: (mrow(i), 0, 0)),
                  pl.BlockSpec((tc, 128), lambda i: (i, 0)),
                  pl.BlockSpec((1, D_MODEL), lambda i: (0, 0)),
                  pl.BlockSpec(memory_space=pl.ANY)],
        out_specs=pl.BlockSpec((tc, D_MODEL), lambda i: (i, 0)),
        out_shape=jax.ShapeDtypeStruct((rm, D_MODEL), F32),
        scratch_shapes=[pltpu.VMEM((2, 2, tc, D_MODEL), F32), pltpu.SemaphoreType.DMA((2,))],
        compiler_params=_params(("arbitrary",)),
        name="moe_combine",
    )(dest3, dest3, x, mod_l, wpad, gnorm, y)


def _moe(route, h2, x, mod_l, gnorm, w_g, w_u, w_d, row0, final, layer):
    rm = h2.shape[0]
    bm = MOE_BM
    a = 2 * rm
    nb = a // bm + N_EXPERTS
    p_rows = nb * bm
    flat_e = route[0:2].astype(jnp.int32).reshape(a)
    onehot = (flat_e[:, None] == jnp.arange(N_EXPERTS, dtype=jnp.int32)[None, :]).astype(jnp.int32)
    csum = jnp.cumsum(onehot, axis=0)
    rank = jnp.take_along_axis(csum, flat_e[:, None], axis=1)[:, 0] - 1
    counts = csum[-1]
    pcounts = (counts + bm - 1) // bm * bm
    pend = jnp.cumsum(pcounts)
    pstart = pend - pcounts
    dest = pstart[flat_e] + rank
    blk_e = jnp.minimum(jnp.searchsorted(pend, jnp.arange(nb, dtype=jnp.int32) * bm, side='right'),
                        N_EXPERTS - 1).astype(jnp.int32)
    n_used = (pend[-1] // bm).astype(jnp.int32).reshape(1)

    def tiles(t):
        return dest.reshape(2, rm // t, t).transpose(1, 0, 2)

    xs = _dispatch(tiles(T_DISPATCH), h2, p_rows)
    y = _ffn(blk_e, n_used, xs, w_g, w_u, w_d, layer)
    wpad = jnp.pad(route[2:4].T, ((0, 0), (0, 126)))
    return _combine(tiles(T_COMBINE), x, mod_l, wpad, gnorm, y, row0, final)


def kernel(x, c, ctx, c_ctx, norm_mix, norm_ffn, w_ada, b_ada, w_in, attn_sink, ret_decay_fwd, ret_decay_bwd,
           w_branch_attn, w_branch_fourier, w_branch_ret, w_out, w_router_group, b_router_group,
           w_router_expert, b_router_expert, w_exp_gate, w_exp_up, w_exp_down, norm_final):
    tabs = [jnp.asarray(t) for t in _rope_tables()]
    dft = [jnp.asarray(t).astype(BF16) for t in _dft_tables()]

    cc = jnp.zeros((MOD_ROWS, D_MODEL), F32).at[0:BATCH].set(c).at[CTX_MOD_ROW].set(c_ctx)
    mod = _ada(cc, w_ada, b_ada).reshape(DEPTH, MOD_ROWS, 6, D_MODEL)

    xf = jnp.concatenate([ctx.reshape(RC, D_MODEL), x.reshape(RL, D_MODEL)], axis=0)
    w_in_bf = w_in.astype(BF16)
    for l in range(DEPTH):
        need_ctx = l < DEPTH - 1
        row0 = 0 if need_ctx else RC
        mod_l = mod[l]
        qa, ka, va, qr, kr, vr, gr, fu, gm = _proj(xf, mod_l, norm_mix[l][None, :], w_in_bf, tabs, l)
        oa = _attention(attn_sink[l], qa, ka, va, need_ctx)
        of = _fourier(fu, dft, need_ctx)
        ret = _retention(qr, kr, vr, gr, _retention_tables(ret_decay_fwd[l], ret_decay_bwd[l]))
        wr = jnp.zeros((ROUTER_ROWS, D_MODEL), F32)
        wr = wr.at[0:N_GROUPS].set(w_router_group[l].T).at[8:8 + N_EXPERTS].set(w_router_expert[l].T)
        br = jnp.full((ROUTER_ROWS,), NEG, F32)
        br = br.at[0:N_GROUPS].set(b_router_group[l]).at[8:8 + N_EXPERTS].set(b_router_expert[l])
        wrh, wrl = _split(wr)
        brb = jnp.broadcast_to(br[:, None], (ROUTER_ROWS, TM_MERGE))
        x_mid, h2, route = _merge(oa, of, ret, gm, xf, mod_l, norm_ffn[l][None, :],
                                  w_branch_attn[l].astype(BF16), w_branch_fourier[l].astype(BF16),
                                  w_branch_ret[l].astype(BF16), w_out[l].astype(BF16), wrh, wrl, brb, row0)
        final = l == DEPTH - 1
        xf = _moe(route, h2, x_mid, mod_l, norm_final[None, :], w_exp_gate, w_exp_up, w_exp_down,
                  row0, final, l)
    return xf.reshape(BATCH, SEQ, D_MODEL)
```

```python
import functools

import numpy as np
import jax
import jax.numpy as jnp
from jax import lax
from jax.experimental import pallas as pl
from jax.experimental.pallas import tpu as pltpu

F32 = jnp.float32
BF16 = jnp.bfloat16

D_MODEL = 1024
BATCH = 8
SEQ = 2048
DEPTH = 2
CTX_LEN = 256
GRID_W = 64
HEAD_DIM = 64
ATTN_HEADS = 8
ATTN_KV_HEADS = 2
ATTN_GROUP = ATTN_HEADS // ATTN_KV_HEADS
ATTN_BLOCK = 128
RET_HEADS = 8
RET_DK = 64
RET_CHUNK = 128
FOURIER_GROUPS = 4
FOURIER_DIM = 128
N_GROUPS = 4
EXPERTS_PER_GROUP = 8
N_EXPERTS = N_GROUPS * EXPERTS_PER_GROUP
EXPERT_HIDDEN = 512
ROPE_BASE = 10000.0
NORM_EPS = 1e-6
GN_EPS = 1e-5

W = 512
IN_COLS = 6400
RC = BATCH * CTX_LEN
RL = BATCH * SEQ
R = RC + RL
MOD_ROWS = 16
CTX_MOD_ROW = 8

VMEM_LIMIT = 52 * 1024 * 1024

TM_PROJ = 512
TM_MERGE = 512
TN_ADA = 1536
TR_FOURIER = 512
MOE_BM = 256
CHUNK = 8
CHUNK_SHIFT = 3
CPB = MOE_BM // CHUNK
SLOTS = 1280
XS_COLS = D_MODEL // 2 + 128
NEG = -1e30
ROUTER_ROWS = 40


def _dot(a, b):
    return jnp.dot(a, b, preferred_element_type=F32)


def _dot_nt(a, b):
    return lax.dot_general(a, b, (((1,), (1,)), ((), ())), preferred_element_type=F32)


def _dot_tn(a, b):
    return lax.dot_general(a, b, (((0,), (0,)), ((), ())), preferred_element_type=F32)


def _split(x):
    hi = x.astype(BF16)
    lo = (x - hi.astype(F32)).astype(BF16)
    return hi, lo


def _sigmoid(x):
    return 1.0 / (1.0 + jnp.exp(-x))


def _params(sem, vmem=VMEM_LIMIT):
    return pltpu.CompilerParams(dimension_semantics=sem, vmem_limit_bytes=vmem)


def _mod_row(row0, tm):
    def f(i):
        g0 = i * tm + row0
        return jnp.where(g0 < RC, CTX_MOD_ROW, (g0 - RC) // SEQ)
    return f


def _rope_tables():
    pos = np.arange(SEQ, dtype=np.float64)
    row = np.floor(pos / GRID_W)
    col = pos % GRID_W

    def cs(p, nf):
        inv = ROPE_BASE ** (-np.arange(nf, dtype=np.float64) / nf)
        ang = p[:, None] * inv[None, :]
        return np.cos(ang), np.sin(ang)

    rc, rs = cs(row, HEAD_DIM // 4)
    cc, cs_ = cs(col, HEAD_DIM // 4)
    cos_a = np.concatenate([rc, rc, cc, cc], axis=1)
    sin_a = np.concatenate([-rs, rs, -cs_, cs_], axis=1)
    tc, ts = cs(pos, RET_DK // 2)
    cos_r = np.concatenate([tc, tc], axis=1)
    sin_r = np.concatenate([-ts, ts], axis=1)

    def full(t, ident):
        t2 = np.concatenate([t, t], axis=1)
        return np.concatenate([np.full_like(t2, ident), t2], axis=0).astype(np.float32)

    return full(cos_a, 1.0), full(sin_a, 0.0), full(cos_r, 1.0), full(sin_r, 0.0)


def _dft_tables():
    def cs(n):
        k = np.arange(n, dtype=np.int64)
        m = (k[:, None] * k[None, :]) % n
        ang = 2.0 * np.pi * m.astype(np.float64) / n
        return np.cos(ang), np.sin(ang)

    c128, s128 = cs(FOURIER_DIM)
    eye = np.eye(FOURIER_GROUPS)
    bdc = np.kron(eye, c128).astype(np.float32)
    bds = np.kron(eye, s128).astype(np.float32)
    cn, sn = cs(SEQ)
    w2 = np.concatenate([cn, -sn], axis=1).astype(np.float32)
    cl, sl = cs(CTX_LEN)
    w2c = np.concatenate([cl, -sl], axis=1).astype(np.float32)
    return bdc, bds, w2, w2c


def _retention_tables(dec_f, dec_b):
    lg_f = jax.nn.log_sigmoid(dec_f.astype(F32))
    lg_b = jax.nn.log_sigmoid(dec_b.astype(F32))
    i = jnp.arange(RET_CHUNK)
    diff = (i[:, None] - i[None, :]).astype(F32)
    fwd = jnp.exp(jnp.maximum(diff, 0.0)[None] * lg_f[:, None, None])
    bwd = jnp.exp(jnp.maximum(-diff, 0.0)[None] * lg_b[:, None, None])
    dcomb = jnp.where((diff >= 0)[None], fwd, bwd)
    fi = i.astype(F32)
    lanes = lambda t: jnp.repeat(t, RET_DK, axis=1)
    xi_f = lanes(jnp.exp((fi + 1.0)[:, None] * lg_f[None, :]))
    zt_f = lanes(jnp.exp((RET_CHUNK - 1 - fi)[:, None] * lg_f[None, :]))
    xi_b = lanes(jnp.exp((RET_CHUNK - fi)[:, None] * lg_b[None, :]))
    zt_b = lanes(jnp.exp(fi[:, None] * lg_b[None, :]))
    g_f = jnp.repeat(jnp.exp(RET_CHUNK * lg_f), RET_DK).reshape(2, 256, 1)
    g_b = jnp.repeat(jnp.exp(RET_CHUNK * lg_b), RET_DK).reshape(2, 256, 1)
    g_f = jnp.broadcast_to(g_f, (2, 256, 256))
    g_b = jnp.broadcast_to(g_b, (2, 256, 256))
    return dcomb, xi_f, zt_f, xi_b, zt_b, g_f, g_b


def _ada_body(c_ref, w_ref, b_ref, o_ref):
    c = c_ref[...]
    s = c * _sigmoid(c)
    sh, sl = _split(s)
    wh, wl = _split(w_ref[...])
    o_ref[...] = _dot(sh, wh) + _dot(sl, wh) + _dot(sh, wl) + b_ref[...]


def _ada(cc, w_ada, b_ada):
    nt = 6 * D_MODEL // TN_ADA
    return pl.pallas_call(
        _ada_body,
        grid=(DEPTH, nt),
        in_specs=[
            pl.BlockSpec((MOD_ROWS, D_MODEL), lambda l, j: (0, 0)),
            pl.BlockSpec((None, D_MODEL, TN_ADA), lambda l, j: (l, 0, j)),
            pl.BlockSpec((None, 1, TN_ADA), lambda l, j: (l, 0, j)),
        ],
        out_specs=pl.BlockSpec((None, MOD_ROWS, TN_ADA), lambda l, j: (l, 0, j)),
        out_shape=jax.ShapeDtypeStruct((DEPTH, MOD_ROWS, 6 * D_MODEL), F32),
        compiler_params=_params(("arbitrary", "arbitrary")),
        name="ada_mod",
    )(cc, w_ada, b_ada.reshape(DEPTH, 1, 6 * D_MODEL))


def _rope(xc, cos, sin, half):
    fwd = pltpu.roll(xc, 128 - half, axis=1)
    bwd = pltpu.roll(xc, half, axis=1)
    lane = lax.broadcasted_iota(jnp.int32, xc.shape, 1)
    first = (lane & (2 * half - 1)) < half
    return xc * cos + jnp.where(first, fwd, bwd) * sin


def _proj_body(x_ref, mod_ref, gn_ref, w_ref, ca_ref, sa_ref, cr_ref, sr_ref,
               qa_ref, ka_ref, va_ref, qr_ref, kr_ref, vr_ref, gr_ref, fu_ref, gm_ref):
    x = x_ref[...]
    ms = jnp.mean(x * x, axis=-1, keepdims=True)
    y = x * lax.rsqrt(ms + NORM_EPS) * gn_ref[...]
    h = y * (1.0 + mod_ref[1:2, :]) + mod_ref[0:1, :]
    hb = h.astype(BF16)

    def proj(c0, width):
        return _dot(hb, w_ref[:, c0:c0 + width])

    ca, sa, cr, sr = ca_ref[...], sa_ref[...], cr_ref[...], sr_ref[...]

    qa = proj(0, W) * (HEAD_DIM ** -0.5)
    for c in range(W // 128):
        qa_ref[:, c * 128:(c + 1) * 128] = _rope(qa[:, c * 128:(c + 1) * 128], ca, sa, 16).astype(BF16)
    kv = proj(W, 256)
    ka = _rope(kv[:, 0:128], ca, sa, 16).astype(BF16)
    ka_ref[0] = ka[:, 0:64]
    ka_ref[1] = ka[:, 64:128]
    va = kv[:, 128:256].astype(BF16)
    va_ref[0] = va[:, 0:64]
    va_ref[1] = va[:, 64:128]
    qr = proj(768, W)
    kr = proj(1280, W) * (RET_DK ** -0.5)
    for c in range(W // 128):
        sl = slice(c * 128, (c + 1) * 128)
        qr_ref[:, sl] = _rope(qr[:, sl], cr, sr, 32).astype(BF16)
        kr_ref[:, sl] = _rope(kr[:, sl], cr, sr, 32).astype(BF16)
    vr_ref[...] = proj(1792, W).astype(BF16)
    g = proj(2304, W)
    gr_ref[...] = (g * _sigmoid(g)).astype(BF16)
    fu_ref[...] = proj(2816, W).astype(BF16)
    for c in range(3):
        gm_ref[:, c * D_MODEL:(c + 1) * D_MODEL] = _sigmoid(proj(3328 + c * D_MODEL, D_MODEL)).astype(BF16)


def _proj(x, mod_l, gnorm, w_in_bf, tabs, layer):
    tm = TM_PROJ
    nt = R // tm
    nc = RC // tm

    def tab_map(i):
        return (jnp.where(i < nc, i, nc + (i - nc) % (SEQ // tm)), 0)

    row = lambda i: (i, 0)
    wide = lambda n: pl.BlockSpec((tm, n), row)
    kv_spec = pl.BlockSpec((2, tm, 64), lambda i: (0, i, 0))
    sds = lambda n: jax.ShapeDtypeStruct((R, n), BF16)
    kv_sds = jax.ShapeDtypeStruct((2, R, 64), BF16)
    mrow = _mod_row(0, tm)
    return pl.pallas_call(
        _proj_body,
        grid=(nt,),
        in_specs=[
            wide(D_MODEL),
            pl.BlockSpec((None, 6, D_MODEL), lambda i: (mrow(i), 0, 0)),
            pl.BlockSpec((1, D_MODEL), lambda i: (0, 0)),
            pl.BlockSpec((None, D_MODEL, IN_COLS), lambda i: (layer, 0, 0), pipeline_mode=pl.Buffered(1)),
        ] + [pl.BlockSpec((tm, 128), tab_map)] * 4,
        out_specs=[wide(W), kv_spec, kv_spec, wide(W), wide(W), wide(W), wide(W), wide(W), wide(3 * D_MODEL)],
        out_shape=[sds(W), kv_sds, kv_sds, sds(W), sds(W), sds(W), sds(W), sds(W), sds(3 * D_MODEL)],
        compiler_params=_params(("arbitrary",)),
        name="in_proj",
    )(x, mod_l, gnorm, w_in_bf, *tabs)


def _attn_body(sink_ref, q_ref, kc_ref, kp_ref, kk_ref, kn_ref, vc_ref, vp_ref, vk_ref, vn_ref, o_ref, *, q_lo):
    h = pl.program_id(1)
    qi = pl.program_id(2) + q_lo
    is_lat = qi >= 2
    m = qi - 2
    q = q_ref[...]
    q4 = jnp.concatenate([q[:, g * 64:(g + 1) * 64] for g in range(ATTN_GROUP)], axis=0)
    nr = ATTN_GROUP * ATTN_BLOCK
    ri = lax.broadcasted_iota(jnp.int32, (nr, ATTN_BLOCK), 0) & (ATTN_BLOCK - 1)
    ci = lax.broadcasted_iota(jnp.int32, (nr, ATTN_BLOCK), 1)
    far = 4 * ATTN_BLOCK
    off_p = jnp.where(is_lat & (m >= 1), 0, far)
    off_k = jnp.where(is_lat, 0, far)
    off_n = jnp.where(is_lat & (m <= SEQ // ATTN_BLOCK - 2), 0, far)
    ok_p = ci >= ri + off_p
    ok_k = ci >= off_k
    ok_n = ci + off_n <= ri
    s_c = _dot_nt(q4, kc_ref[...])
    s_p = jnp.where(ok_p, _dot_nt(q4, kp_ref[...]), NEG)
    s_k = jnp.where(ok_k, _dot_nt(q4, kk_ref[...]), NEG)
    s_n = jnp.where(ok_n, _dot_nt(q4, kn_ref[...]), NEG)
    rg = lax.broadcasted_iota(jnp.int32, (nr, 1), 0) >> 7
    sk = jnp.zeros((nr, 1), F32)
    for g in range(ATTN_GROUP):
        sk = jnp.where(rg == g, sink_ref[h * ATTN_GROUP + g], sk)
    rmax = lambda s: jnp.max(s, axis=-1, keepdims=True)
    mx = jnp.maximum(jnp.maximum(rmax(s_c), rmax(s_p)), jnp.maximum(rmax(s_k), rmax(s_n)))
    mx = jnp.maximum(mx, sk)
    p_c, p_p, p_k, p_n = (jnp.exp(s - mx) for s in (s_c, s_p, s_k, s_n))
    rsum = lambda p: jnp.sum(p, axis=-1, keepdims=True)
    den = rsum(p_c) + rsum(p_p) + rsum(p_k) + rsum(p_n) + jnp.exp(sk - mx)
    o = (_dot(p_c.astype(BF16), vc_ref[...]) + _dot(p_p.astype(BF16), vp_ref[...])
         + _dot(p_k.astype(BF16), vk_ref[...]) + _dot(p_n.astype(BF16), vn_ref[...]))
    o = o / den
    o_ref[...] = jnp.concatenate(
        [o[g * ATTN_BLOCK:(g + 1) * ATTN_BLOCK] for g in range(ATTN_GROUP)], axis=1).astype(BF16)


def _attention(sink, qa, ka, va, need_ctx):
    q_lo = 0 if need_ctx else 2
    nq = SEQ // ATTN_BLOCK + (2 if need_ctx else 0)
    nb = SEQ // ATTN_BLOCK
    lat0 = RC // ATTN_BLOCK

    def qrow(b, qi):
        return jnp.where(qi < 2, b * 2 + qi, lat0 + b * nb + qi - 2)

    def q_map(b, h, i):
        return (qrow(b, i + q_lo), h)

    def loc(delta):
        def f(b, h, i):
            m = jnp.clip(i + q_lo - 2 + delta, 0, nb - 1)
            return (h, lat0 + b * nb + m, 0)
        return f

    ctx_spec = pl.BlockSpec((None, CTX_LEN, 64), lambda b, h, i: (h, b, 0))
    loc_spec = lambda d: pl.BlockSpec((None, ATTN_BLOCK, 64), loc(d))
    kv_specs = [ctx_spec, loc_spec(-1), loc_spec(0), loc_spec(1)]
    return pl.pallas_call(
        functools.partial(_attn_body, q_lo=q_lo),
        grid=(BATCH, ATTN_KV_HEADS, nq),
        in_specs=[pl.BlockSpec(memory_space=pltpu.SMEM),
                  pl.BlockSpec((ATTN_BLOCK, 256), q_map)] + kv_specs + kv_specs,
        out_specs=pl.BlockSpec((ATTN_BLOCK, 256), q_map),
        out_shape=jax.ShapeDtypeStruct((R, W), BF16),
        compiler_params=_params(("arbitrary", "arbitrary", "arbitrary")),
        name="window_attn",
    )(sink, qa, ka, ka, ka, ka, va, va, va, va)


def _fourier_lat_body(u_ref, bdc_ref, bds_ref, w2_ref, o_ref, as_ref):
    @pl.when(pl.program_id(1) == 0)
    def _():
        u = u_ref[...]
        as_ref[0:SEQ, :] = _dot(u, bdc_ref[...]).astype(BF16)
        as_ref[SEQ:2 * SEQ, :] = _dot(u, bds_ref[...]).astype(BF16)

    o_ref[...] = (_dot(w2_ref[...], as_ref[...]) * ((SEQ * FOURIER_DIM) ** -0.5)).astype(BF16)


def _fourier_ctx_body(u_ref, bdc_ref, bds_ref, w2_ref, of_in_ref, o_ref):
    del of_in_ref
    u = u_ref[...]
    a = _dot(u, bdc_ref[...]).astype(BF16)
    s = _dot(u, bds_ref[...]).astype(BF16)
    z = _dot(w2_ref[...], jnp.concatenate([a, s], axis=0))
    o_ref[...] = (z * ((CTX_LEN * FOURIER_DIM) ** -0.5)).astype(BF16)


def _fourier(fu, dft, need_ctx):
    bdc, bds, w2, w2c = dft
    tr = TR_FOURIER
    nj = SEQ // tr
    full = lambda a: pl.BlockSpec(a.shape, lambda *_: (0,) * a.ndim)
    of = pl.pallas_call(
        _fourier_lat_body,
        grid=(BATCH, nj),
        in_specs=[pl.BlockSpec((SEQ, W), lambda b, j: (1 + b, 0)), full(bdc), full(bds),
                  pl.BlockSpec((tr, 2 * SEQ), lambda b, j: (j, 0))],
        out_specs=pl.BlockSpec((tr, W), lambda b, j: (RC // tr + b * nj + j, 0)),
        out_shape=jax.ShapeDtypeStruct((R, W), BF16),
        scratch_shapes=[pltpu.VMEM((2 * SEQ, W), BF16)],
        compiler_params=_params(("arbitrary", "arbitrary")),
        name="fourier_latent",
    )(fu, bdc, bds, w2)
    if not need_ctx:
        return of
    return pl.pallas_call(
        _fourier_ctx_body,
        grid=(BATCH,),
        in_specs=[pl.BlockSpec((CTX_LEN, W), lambda b: (b, 0)), full(bdc), full(bds), full(w2c),
                  pl.BlockSpec(memory_space=pl.ANY)],
        out_specs=pl.BlockSpec((CTX_LEN, W), lambda b: (b, 0)),
        out_shape=jax.ShapeDtypeStruct((R, W), BF16),
        input_output_aliases={4: 0},
        compiler_params=_params(("arbitrary",)),
        name="fourier_ctx",
    )(fu, bdc, bds, w2c, of)


def _retention_body(qc_ref, kc_ref, vc_ref, ql_ref, kl_ref, vl_ref, g_ref,
                    dcomb_ref, xif_ref, ztf_ref, xib_ref, ztb_ref, gf_ref, gb_ref, mbd_ref, avg_ref,
                    o_ref, os_ref, st_ref):
    j = pl.program_id(1)
    C = RET_CHUNK
    nl = SEQ // C

    def group_parts(q, k, v, gi):
        sl = slice(gi * 256, (gi + 1) * 256)
        return q[:, sl], k[:, sl], v[:, sl], sl

    lane_head = lax.broadcasted_iota(jnp.int32, (C, 256), 1) // RET_DK

    def chunk_fwd(q, k, v, r0):
        for gi in range(2):
            q4, k4, v4, sl = group_parts(q, k, v, gi)
            s_prev = st_ref[gi]
            q4f = q4.astype(F32)
            o4 = _dot((q4f * xif_ref[:, sl]).astype(BF16), s_prev.astype(BF16))
            intra = jnp.zeros((C, 256), F32)
            for hh in range(4):
                qm = jnp.where(lane_head == hh, q4f, 0.0).astype(BF16)
                p = (_dot_nt(qm, k4) * dcomb_ref[gi * 4 + hh]).astype(BF16)
                intra = jnp.where(lane_head == hh, _dot(p, v4), intra)
            os_ref[pl.ds(r0, C), sl] = o4 + intra
            u = _dot_tn(k4, (v4.astype(F32) * ztf_ref[:, sl]).astype(BF16))
            st_ref[gi] = gf_ref[gi] * s_prev + mbd_ref[...] * u

    def chunk_bwd(q, k, v, r0):
        for gi in range(2):
            q4, k4, v4, sl = group_parts(q, k, v, gi)
            s_prev = st_ref[gi]
            cross = _dot((q4.astype(F32) * xib_ref[:, sl]).astype(BF16), s_prev.astype(BF16))
            os_ref[pl.ds(r0, C), sl] = os_ref[pl.ds(r0, C), sl] + cross
            u = _dot_tn(k4, (v4.astype(F32) * ztb_ref[:, sl]).astype(BF16))
            st_ref[gi] = gb_ref[gi] * s_prev + mbd_ref[...] * u

    @pl.when(j == 0)
    def _():
        st_ref[...] = jnp.zeros_like(st_ref)
        for c in range(CTX_LEN // C):
            rs = slice(c * C, (c + 1) * C)
            chunk_fwd(qc_ref[rs, :], kc_ref[rs, :], vc_ref[rs, :], c * C)

        def fbody(c, carry):
            r0 = pl.multiple_of(c * C, C)
            rs = pl.ds(r0, C)
            chunk_fwd(ql_ref[rs, :], kl_ref[rs, :], vl_ref[rs, :], CTX_LEN + r0)
            return carry

        lax.fori_loop(0, nl, fbody, 0)

        st_ref[...] = jnp.zeros_like(st_ref)
        for c in reversed(range(CTX_LEN // C)):
            rs = slice(c * C, (c + 1) * C)
            chunk_bwd(qc_ref[rs, :], kc_ref[rs, :], vc_ref[rs, :], c * C)

        def bbody(t, carry):
            r0 = pl.multiple_of((nl - 1 - t) * C, C)
            rs = pl.ds(r0, C)
            chunk_bwd(ql_ref[rs, :], kl_ref[rs, :], vl_ref[rs, :], CTX_LEN + r0)
            return carry

        lax.fori_loop(0, nl, bbody, 0)

    o = os_ref[pl.ds(pl.multiple_of(j * 256, 256), 256), :]
    avg = avg_ref[...]
    oh, ol = _split(o)
    mu = _dot(oh, avg) + _dot(ol, avg)
    d = o - mu
    vh, vl = _split(d * d)
    var = _dot(vh, avg) + _dot(vl, avg)
    o_ref[...] = (g_ref[...].astype(F32) * d * lax.rsqrt(var + GN_EPS)).astype(BF16)


def _retention(qr, kr, vr, gr, rtabs):
    nj = 1 + SEQ // 256

    def out_map(b, j):
        return (jnp.where(j == 0, b, RC // 256 + b * (SEQ // 256) + j - 1), 0)

    ctx = pl.BlockSpec((CTX_LEN, W), lambda b, j: (b, 0))
    lat = pl.BlockSpec((SEQ, W), lambda b, j: (1 + b, 0))
    full = lambda a: pl.BlockSpec(a.shape, lambda *_: (0,) * a.ndim)
    avg = jnp.asarray(np.kron(np.eye(RET_HEADS), np.full((RET_DK, RET_DK), 1.0 / RET_DK)).astype(np.float32)).astype(BF16)
    mbd = jnp.asarray(np.kron(np.eye(4), np.ones((RET_DK, RET_DK))).astype(np.float32))
    tabs = list(rtabs) + [mbd, avg]
    return pl.pallas_call(
        _retention_body,
        grid=(BATCH, nj),
        in_specs=[ctx, ctx, ctx, lat, lat, lat, pl.BlockSpec((256, W), out_map)] + [full(t) for t in tabs],
        out_specs=pl.BlockSpec((256, W), out_map),
        out_shape=jax.ShapeDtypeStruct((R, W), BF16),
        scratch_shapes=[pltpu.VMEM((CTX_LEN + SEQ, W), F32), pltpu.VMEM((2, 256, 256), F32)],
        compiler_params=_params(("arbitrary", "arbitrary")),
        name="retention",
    )(qr, kr, vr, qr, kr, vr, gr, *tabs)


def _pack_pair(a, b):
    ua = lax.bitcast_convert_type(a, jnp.uint32) >> 16
    ub = lax.bitcast_convert_type(b, jnp.uint32) & jnp.uint32(0xFFFF0000)
    return ua | ub


def _unpack_pair(w):
    a = lax.bitcast_convert_type(w << 16, F32)
    b = lax.bitcast_convert_type(w & jnp.uint32(0xFFFF0000), F32)
    return a, b


def _slot_onehot(s0, s1, n):
    srow = lax.broadcasted_iota(jnp.int32, (n, s0.shape[1]), 0)
    p0 = jnp.where(srow == s0, 1.0, 0.0).astype(BF16)
    p1 = jnp.where(srow == s1, 1.0, 0.0).astype(BF16)
    return p0, p1


def _merge_body(oa_ref, of_ref, rt_ref, gm_ref, x_ref, mod_ref, gn_ref, wba_ref, wbf_ref, wbr_ref, wout_ref,
                wrh_ref, wrl_ref, br_ref, tri_ref, ltri_ref, xo_ref, xs_ref, ro_ref, nch_ref):
    gm = gm_ref[...].astype(F32)
    z = (gm[:, 0:D_MODEL] * _dot(oa_ref[...], wba_ref[...])
         + gm[:, D_MODEL:2 * D_MODEL] * _dot(of_ref[...], wbf_ref[...])
         + gm[:, 2 * D_MODEL:3 * D_MODEL] * _dot(rt_ref[...], wbr_ref[...]))
    y = _dot(z.astype(BF16), wout_ref[...])
    x = x_ref[...] + mod_ref[2:3, :] * y
    xo_ref[...] = x
    ms = jnp.mean(x * x, axis=-1, keepdims=True)
    hn = x * lax.rsqrt(ms + NORM_EPS) * gn_ref[...]
    h2 = hn * (1.0 + mod_ref[4:5, :]) + mod_ref[3:4, :]
    hh, hl = _split(h2)
    wh, wl = wrh_ref[...], wrl_ref[...]
    lg = _dot_nt(wh, hh) + _dot_nt(wh, hl) + _dot_nt(wl, hh) + br_ref[...]
    tm = lg.shape[1]
    row8 = lax.broadcasted_iota(jnp.int32, (8, tm), 0)
    lgg = lg[0:8, :]
    mg = jnp.max(lgg, axis=0, keepdims=True)
    grp = jnp.min(jnp.where(lgg == mg, row8, 8), axis=0, keepdims=True)
    pg = 1.0 / jnp.sum(jnp.exp(lgg - mg), axis=0, keepdims=True)
    lin = jnp.zeros((8, tm), F32)
    for g in range(N_GROUPS):
        lin = jnp.where(grp == g, lg[8 + 8 * g:16 + 8 * g, :], lin)
    v1 = jnp.max(lin, axis=0, keepdims=True)
    i1 = jnp.min(jnp.where(lin == v1, row8, 8), axis=0, keepdims=True)
    rest = jnp.where(row8 == i1, -jnp.inf, lin)
    v2 = jnp.max(rest, axis=0, keepdims=True)
    i2 = jnp.min(jnp.where(rest == v2, row8, 8), axis=0, keepdims=True)
    e2 = jnp.exp(v2 - v1)
    w1 = pg / (1.0 + e2)
    w2 = pg * e2 / (1.0 + e2)
    e_1 = grp * EXPERTS_PER_GROUP + i1
    e_2 = grp * EXPERTS_PER_GROUP + i2

    row32 = lax.broadcasted_iota(jnp.int32, (N_EXPERTS, tm), 0)
    oh0 = jnp.where(row32 == e_1, 1.0, 0.0)
    oh1 = jnp.where(row32 == e_2, 1.0, 0.0)
    tri = tri_ref[...]
    cum0 = _dot(oh0.astype(BF16), tri)
    cum1 = _dot(oh1.astype(BF16), tri)
    tot0 = jnp.sum(oh0, axis=1, keepdims=True)
    tot1 = jnp.sum(oh1, axis=1, keepdims=True)
    nch = ((tot0 + tot1).astype(jnp.int32) + (CHUNK - 1)) >> CHUNK_SHIFT
    nch_b = jnp.broadcast_to(nch.astype(F32), (N_EXPERTS, 128))
    nch_ref[...] = nch_b.astype(jnp.int32)
    base = CHUNK * _dot(ltri_ref[...], nch_b.astype(BF16))[:, 0:1]
    s0 = jnp.sum(oh0 * (base + cum0), axis=0, keepdims=True).astype(jnp.int32)
    s1 = jnp.sum(oh1 * (base + tot0 + cum1), axis=0, keepdims=True).astype(jnp.int32)
    p0, p1 = _slot_onehot(s0, s1, SLOTS)
    xs = _dot(p0 + p1, hh)
    xs_ref[:, 0:D_MODEL // 2] = _pack_pair(xs[:, 0:D_MODEL // 2], xs[:, D_MODEL // 2:D_MODEL])

    def wrows(w):
        hi, lo = _split(w)
        return jnp.where(row8 == 0, hi.astype(F32), jnp.where(row8 == 1, lo.astype(F32), 0.0)).astype(BF16)

    wc = _dot_nt(p0, wrows(w1)) + _dot_nt(p1, wrows(w2))
    wcol = jnp.broadcast_to(wc[:, 0:1] + wc[:, 1:2], (SLOTS, 128))
    xs_ref[:, D_MODEL // 2:XS_COLS] = lax.bitcast_convert_type(wcol, jnp.uint32)
    s0f, s1f = s0.astype(F32), s1.astype(F32)
    ro_ref[...] = jnp.where(row8 == 0, s0f, jnp.where(row8 == 1, s1f, 0.0))


def _merge(oa, of, ret, gm, x, mod_l, gnorm, wba, wbf, wbr, wout, wrh, wrl, brb, row0):
    tm = TM_MERGE
    rm = R - row0
    nt = rm // tm
    off = row0 // tm
    src = lambda n: pl.BlockSpec((tm, n), lambda i: (i + off, 0))
    dst = lambda n: pl.BlockSpec((tm, n), lambda i: (i, 0))
    full = lambda a: pl.BlockSpec(a.shape, lambda *_: (0,) * a.ndim, pipeline_mode=pl.Buffered(1))
    mrow = _mod_row(row0, tm)
    tri = jnp.asarray(np.triu(np.ones((tm, tm), np.float32), 1)).astype(BF16)
    ltri = jnp.asarray(np.tril(np.ones((N_EXPERTS, N_EXPERTS), np.float32), -1)).astype(BF16)
    return pl.pallas_call(
        _merge_body,
        grid=(nt,),
        in_specs=[src(W), src(W), src(W), src(3 * D_MODEL), src(D_MODEL),
                  pl.BlockSpec((None, 6, D_MODEL), lambda i: (mrow(i), 0, 0)),
                  full(gnorm), full(wba), full(wbf), full(wbr), full(wout), full(wrh), full(wrl), full(brb),
                  full(tri), full(ltri)],
        out_specs=[dst(D_MODEL), pl.BlockSpec((SLOTS, XS_COLS), lambda i: (i, 0)),
                   pl.BlockSpec((8, tm), lambda i: (0, i)),
                   pl.BlockSpec((None, N_EXPERTS, 128), lambda i: (i, 0, 0))],
        out_shape=[jax.ShapeDtypeStruct((rm, D_MODEL), F32),
                   jax.ShapeDtypeStruct((nt * SLOTS, XS_COLS), jnp.uint32),
                   jax.ShapeDtypeStruct((8, rm), F32),
                   jax.ShapeDtypeStruct((nt, N_EXPERTS, 128), jnp.int32)],
        compiler_params=_params(("arbitrary",)),
        name="merge_router",
    )(oa, of, ret, gm, x, mod_l, gnorm, wba, wbf, wbr, wout, wrh, wrl, brb, tri, ltri)


def _moe_plan(nch, nb):
    nt = nch.shape[0]
    choff = jnp.cumsum(nch, axis=1) - nch
    used_ch = jnp.sum(nch, axis=1)
    cum_t = jnp.cumsum(nch, axis=0)
    tot = cum_t[-1]
    ptot = (tot + CPB - 1) // CPB * CPB
    pend = jnp.cumsum(ptot)
    pstart = pend - ptot
    n_used = pend[-1] // CPB
    blk = jnp.arange(nb, dtype=jnp.int32)
    lane = jnp.arange(CPB, dtype=jnp.int32)
    blk_e = jnp.minimum(jnp.sum((blk[:, None] * CPB >= pend[None, :]).astype(jnp.int32), axis=1), N_EXPERTS - 1)
    oe = (blk_e[:, None] == jnp.arange(N_EXPERTS, dtype=jnp.int32)[None, :]).astype(jnp.int32)
    sel = lambda tab: jnp.sum(oe[:, :, None] * tab.T[None, :, :], axis=1)
    pstart_b = jnp.sum(oe * pstart[None, :], axis=1)
    tot_b = jnp.sum(oe * tot[None, :], axis=1)
    cum_b, nch_b, choff_b = sel(cum_t), sel(nch), sel(choff)
    i = blk[:, None] * CPB + lane[None, :] - pstart_b[:, None]
    valid = (i < tot_b[:, None]) & (blk[:, None] < n_used)
    t = jnp.minimum(jnp.sum((i[:, :, None] >= cum_b[:, None, :]).astype(jnp.int32), axis=2), nt - 1)
    tiles = jnp.arange(nt, dtype=jnp.int32)[None, None, :]
    before = jnp.sum(jnp.where(tiles < t[:, :, None], nch_b[:, None, :], 0), axis=2)
    coff = jnp.sum(jnp.where(tiles == t[:, :, None], choff_b[:, None, :], 0), axis=2)
    row = t * SLOTS + CHUNK * (coff + i - before)
    src = jnp.where(valid, row, SLOTS - CHUNK)
    dummy = nt * SLOTS + CHUNK * ((blk[:, None] % 2) * CPB + lane[None, :])
    dst = jnp.where(valid, row, dummy)
    return (blk_e.astype(jnp.int32), n_used.astype(jnp.int32).reshape(1), src.reshape(-1).astype(jnp.int32),
            dst.reshape(-1).astype(jnp.int32), used_ch.astype(jnp.int32))


def _ffn_body(be_ref, nu_ref, src_ref, dst_ref, uc_ref, xs_ref, wg_ref, wu_ref, wd_ref, ys_ref,
              xbuf, ybuf, zbuf, wgb, wub, wdb, sem_in, sem_out, sem_z, *, nt):
    b = pl.program_id(0)
    nu = nu_ref[0]
    slot = b % 2
    half = D_MODEL // 2

    def gather(blk, sl):
        for c in range(CPB):
            r = pl.multiple_of(src_ref[blk * CPB + c], CHUNK)
            pltpu.make_async_copy(xs_ref.at[pl.ds(r, CHUNK)], xbuf.at[sl, pl.ds(c * CHUNK, CHUNK)],
                                  sem_in.at[sl]).start()

    def scatter(blk, sl):
        for c in range(CPB):
            r = pl.multiple_of(dst_ref[blk * CPB + c], CHUNK)
            pltpu.make_async_copy(ybuf.at[sl, pl.ds(c * CHUNK, CHUNK)], ys_ref.at[pl.ds(r, CHUNK)],
                                  sem_out.at[sl]).start()

    def wait_gather(sl):
        pltpu.make_async_copy(xs_ref.at[pl.ds(0, MOE_BM)], xbuf.at[sl], sem_in.at[sl]).wait()

    def wait_scatter(sl):
        pltpu.make_async_copy(ybuf.at[sl], ys_ref.at[pl.ds(0, MOE_BM)], sem_out.at[sl]).wait()

    def zero_copy(r):
        return pltpu.make_async_copy(zbuf, ys_ref.at[pl.ds(pl.multiple_of(r, CHUNK), CHUNK)], sem_z)

    @pl.when(b == 0)
    def _():
        zbuf[...] = jnp.zeros_like(zbuf)

        def tails(fn):
            def per_tile(t, carry):
                def per_chunk(c, carry2):
                    fn(t * SLOTS + c * CHUNK)
                    return carry2
                lax.fori_loop(uc_ref[t], SLOTS // CHUNK, per_chunk, 0)
                return carry
            lax.fori_loop(0, nt, per_tile, 0)
            for c in range(2 * CPB):
                fn(nt * SLOTS + c * CHUNK)

        tails(lambda r: zero_copy(r).start())
        tails(lambda r: zero_copy(r).wait())
        gather(0, 0)

    prev = be_ref[jnp.maximum(b - 1, 0)]
    fresh = (b == 0) | (be_ref[b] != prev)

    @pl.when(fresh)
    def _():
        wgb[...] = wg_ref[...].astype(BF16)
        wub[...] = wu_ref[...].astype(BF16)
        wdb[...] = wd_ref[...].astype(BF16)

    @pl.when(b < nu)
    def _():
        @pl.when(b + 1 < nu)
        def _():
            gather(b + 1, 1 - slot)

        wait_gather(slot)

        @pl.when(b >= 2)
        def _():
            wait_scatter(slot)

        xw = xbuf[slot]
        xa, xb = _unpack_pair(xw[:, 0:half])
        x = jnp.concatenate([xa, xb], axis=1).astype(BF16)
        wt = lax.bitcast_convert_type(xw[:, half:XS_COLS], F32)
        g = _dot(x, wgb[...])
        u = _dot(x, wub[...])
        hmid = (g * _sigmoid(g) * u).astype(BF16)
        y = _dot(hmid, wdb[...]) * jnp.concatenate([wt] * (D_MODEL // 128), axis=1)
        yb = y.astype(BF16).astype(F32)
        ybuf[slot] = _pack_pair(yb[:, 0:half], yb[:, half:D_MODEL])
        scatter(b, slot)

        @pl.when(b == nu - 1)
        def _():
            wait_scatter(slot)

            @pl.when(b >= 1)
            def _():
                wait_scatter(1 - slot)


def _ffn(plan, xs, w_g, w_u, w_d, layer, nt, nb):
    wmap = lambda b, be, *_: (layer, be[b], 0, 0)
    half = D_MODEL // 2
    grid_spec = pltpu.PrefetchScalarGridSpec(
        num_scalar_prefetch=5,
        grid=(nb,),
        in_specs=[pl.BlockSpec(memory_space=pl.ANY),
                  pl.BlockSpec((None, None, D_MODEL, EXPERT_HIDDEN), wmap),
                  pl.BlockSpec((None, None, D_MODEL, EXPERT_HIDDEN), wmap),
                  pl.BlockSpec((None, None, EXPERT_HIDDEN, D_MODEL), wmap)],
        out_specs=pl.BlockSpec(memory_space=pl.ANY),
        scratch_shapes=[pltpu.VMEM((2, MOE_BM, XS_COLS), jnp.uint32), pltpu.VMEM((2, MOE_BM, half), jnp.uint32),
                        pltpu.VMEM((CHUNK, half), jnp.uint32),
                        pltpu.VMEM((D_MODEL, EXPERT_HIDDEN), BF16), pltpu.VMEM((D_MODEL, EXPERT_HIDDEN), BF16),
                        pltpu.VMEM((EXPERT_HIDDEN, D_MODEL), BF16),
                        pltpu.SemaphoreType.DMA((2,)), pltpu.SemaphoreType.DMA((2,)), pltpu.SemaphoreType.DMA(())],
    )
    return pl.pallas_call(
        functools.partial(_ffn_body, nt=nt),
        grid_spec=grid_spec,
        out_shape=jax.ShapeDtypeStruct((nt * SLOTS + 2 * CPB * CHUNK, half), jnp.uint32),
        compiler_params=_params(("arbitrary",)),
        name="moe_experts",
    )(*plan, xs, w_g, w_u, w_d)


def _combine_body(ys_ref, ro_ref, x_ref, mod_ref, gn_ref, o_ref, *, final):
    s = ro_ref[...]
    p0, p1 = _slot_onehot(s[0:1, :].astype(jnp.int32), s[1:2, :].astype(jnp.int32), SLOTS)
    ya, yb = _unpack_pair(ys_ref[...])
    y = jnp.concatenate([ya, yb], axis=1).astype(BF16)
    f = _dot_tn(p0 + p1, y)
    x = x_ref[...] + mod_ref[5:6, :] * f
    if final:
        ms = jnp.mean(x * x, axis=-1, keepdims=True)
        x = x * lax.rsqrt(ms + NORM_EPS) * gn_ref[...]
    o_ref[...] = x


def _combine(ys, route, x, mod_l, gnorm, row0, final):
    tm = TM_MERGE
    rm = x.shape[0]
    mrow = _mod_row(row0, tm)
    return pl.pallas_call(
        functools.partial(_combine_body, final=final),
        grid=(rm // tm,),
        in_specs=[pl.BlockSpec((SLOTS, D_MODEL // 2), lambda i: (i, 0)),
                  pl.BlockSpec((8, tm), lambda i: (0, i)),
                  pl.BlockSpec((tm, D_MODEL), lambda i: (i, 0)),
                  pl.BlockSpec((None, 6, D_MODEL), lambda i: (mrow(i), 0, 0)),
                  pl.BlockSpec((1, D_MODEL), lambda i: (0, 0))],
        out_specs=pl.BlockSpec((tm, D_MODEL), lambda i: (i, 0)),
        out_shape=jax.ShapeDtypeStruct((rm, D_MODEL), F32),
        compiler_params=_params(("arbitrary",)),
        name="moe_combine",
    )(ys, route, x, mod_l, gnorm)


def _moe(xs, route, nch3, x, mod_l, gnorm, w_g, w_u, w_d, row0, final, layer):
    nt = nch3.shape[0]
    max_chunks = nt * ((2 * TM_MERGE + N_EXPERTS * (CHUNK - 1)) // CHUNK)
    nb = -(-max_chunks // CPB) + N_EXPERTS
    plan = _moe_plan(nch3[:, :, 0], nb)
    ys = _ffn(plan, xs, w_g, w_u, w_d, layer, nt, nb)
    return _combine(ys, route, x, mod_l, gnorm, row0, final)


def kernel(x, c, ctx, c_ctx, norm_mix, norm_ffn, w_ada, b_ada, w_in, attn_sink, ret_decay_fwd, ret_decay_bwd,
           w_branch_attn, w_branch_fourier, w_branch_ret, w_out, w_router_group, b_router_group,
           w_router_expert, b_router_expert, w_exp_gate, w_exp_up, w_exp_down, norm_final):
    tabs = [jnp.asarray(t) for t in _rope_tables()]
    dft = [jnp.asarray(t).astype(BF16) for t in _dft_tables()]

    cc = jnp.zeros((MOD_ROWS, D_MODEL), F32).at[0:BATCH].set(c).at[CTX_MOD_ROW].set(c_ctx)
    mod = _ada(cc, w_ada, b_ada).reshape(DEPTH, MOD_ROWS, 6, D_MODEL)

    xf = jnp.concatenate([ctx.reshape(RC, D_MODEL), x.reshape(RL, D_MODEL)], axis=0)
    w_in_bf = w_in.astype(BF16)
    for l in range(DEPTH):
        need_ctx = l < DEPTH - 1
        row0 = 0 if need_ctx else RC
        mod_l = mod[l]
        qa, ka, va, qr, kr, vr, gr, fu, gm = _proj(xf, mod_l, norm_mix[l][None, :], w_in_bf, tabs, l)
        oa = _attention(attn_sink[l], qa, ka, va, need_ctx)
        of = _fourier(fu, dft, need_ctx)
        ret = _retention(qr, kr, vr, gr, _retention_tables(ret_decay_fwd[l], ret_decay_bwd[l]))
        wr = jnp.zeros((ROUTER_ROWS, D_MODEL), F32)
        wr = wr.at[0:N_GROUPS].set(w_router_group[l].T).at[8:8 + N_EXPERTS].set(w_router_expert[l].T)
        br = jnp.full((ROUTER_ROWS,), NEG, F32)
        br = br.at[0:N_GROUPS].set(b_router_group[l]).at[8:8 + N_EXPERTS].set(b_router_expert[l])
        wrh, wrl = _split(wr)
        brb = jnp.broadcast_to(br[:, None], (ROUTER_ROWS, TM_MERGE))
        x_mid, xs, route, nch3 = _merge(oa, of, ret, gm, xf, mod_l, norm_ffn[l][None, :],
                                        w_branch_attn[l].astype(BF16), w_branch_fourier[l].astype(BF16),
                                        w_branch_ret[l].astype(BF16), w_out[l].astype(BF16), wrh, wrl, brb, row0)
        final = l == DEPTH - 1
        xf = _moe(xs, route, nch3, x_mid, mod_l, norm_final[None, :], w_exp_gate, w_exp_up, w_exp_down,
                  row0, final, l)
    return xf.reshape(BATCH, SEQ, D_MODEL)
```

```python
import functools

import numpy as np
import jax
import jax.numpy as jnp
from jax import lax
from jax.experimental import pallas as pl
from jax.experimental.pallas import tpu as pltpu

F32 = jnp.float32
BF16 = jnp.bfloat16

D_MODEL = 1024
BATCH = 8
SEQ = 2048
DEPTH = 2
CTX_LEN = 256
GRID_W = 64
HEAD_DIM = 64
ATTN_HEADS = 8
ATTN_KV_HEADS = 2
ATTN_GROUP = ATTN_HEADS // ATTN_KV_HEADS
ATTN_BLOCK = 128
RET_HEADS = 8
RET_DK = 64
RET_CHUNK = 128
FOURIER_GROUPS = 4
FOURIER_DIM = 128
N_GROUPS = 4
EXPERTS_PER_GROUP = 8
N_EXPERTS = N_GROUPS * EXPERTS_PER_GROUP
EXPERT_HIDDEN = 512
ROPE_BASE = 10000.0
NORM_EPS = 1e-6
GN_EPS = 1e-5

W = 512
IN_COLS = 6400
RC = BATCH * CTX_LEN
RL = BATCH * SEQ
R = RC + RL
MOD_ROWS = 16
CTX_MOD_ROW = 8

VMEM_LIMIT = 52 * 1024 * 1024

TM_PROJ = 512
TM_MERGE = 512
TN_ADA = 1536
TR_FOURIER = 512
MOE_BM = 256
CHUNK = 8
CHUNK_SHIFT = 3
CPB = MOE_BM // CHUNK
SLOTS = 1280
XS_COLS = D_MODEL // 2 + 128
NEG = -1e30
LOG2E = 1.4426950408889634
ROUTER_ROWS = 40


def _dot(a, b):
    return jnp.dot(a, b, preferred_element_type=F32)


def _dot_nt(a, b):
    return lax.dot_general(a, b, (((1,), (1,)), ((), ())), preferred_element_type=F32)


def _dot_tn(a, b):
    return lax.dot_general(a, b, (((0,), (0,)), ((), ())), preferred_element_type=F32)


def _split(x):
    hi = x.astype(BF16)
    lo = (x - hi.astype(F32)).astype(BF16)
    return hi, lo


def _sigmoid(x):
    return 1.0 / (1.0 + jnp.exp(-x))


def _params(sem, vmem=VMEM_LIMIT):
    return pltpu.CompilerParams(dimension_semantics=sem, vmem_limit_bytes=vmem)


def _mod_row(row0, tm):
    def f(i):
        g0 = i * tm + row0
        return jnp.where(g0 < RC, CTX_MOD_ROW, (g0 - RC) // SEQ)
    return f


def _rope_tables():
    pos = np.arange(SEQ, dtype=np.float64)
    row = np.floor(pos / GRID_W)
    col = pos % GRID_W

    def cs(p, nf):
        inv = ROPE_BASE ** (-np.arange(nf, dtype=np.float64) / nf)
        ang = p[:, None] * inv[None, :]
        return np.cos(ang), np.sin(ang)

    rc, rs = cs(row, HEAD_DIM // 4)
    cc, cs_ = cs(col, HEAD_DIM // 4)
    cos_a = np.concatenate([rc, rc, cc, cc], axis=1)
    sin_a = np.concatenate([-rs, rs, -cs_, cs_], axis=1)
    tc, ts = cs(pos, RET_DK // 2)
    cos_r = np.concatenate([tc, tc], axis=1)
    sin_r = np.concatenate([-ts, ts], axis=1)

    def full(t, ident):
        t2 = np.concatenate([t, t], axis=1)
        return np.concatenate([np.full_like(t2, ident), t2], axis=0).astype(np.float32)

    return full(cos_a, 1.0), full(sin_a, 0.0), full(cos_r, 1.0), full(sin_r, 0.0)


def _dft_tables():
    def cs(n):
        k = np.arange(n, dtype=np.int64)
        m = (k[:, None] * k[None, :]) % n
        ang = 2.0 * np.pi * m.astype(np.float64) / n
        return np.cos(ang), np.sin(ang)

    c128, s128 = cs(FOURIER_DIM)
    eye = np.eye(FOURIER_GROUPS)
    bdc = np.kron(eye, c128).astype(np.float32)
    bds = np.kron(eye, s128).astype(np.float32)
    cn, sn = cs(SEQ)
    w2 = np.concatenate([cn, -sn], axis=1).astype(np.float32)
    cl, sl = cs(CTX_LEN)
    w2c = np.concatenate([cl, -sl], axis=1).astype(np.float32)
    return bdc, bds, w2, w2c


def _retention_tables(dec_f, dec_b):
    lg_f = jax.nn.log_sigmoid(dec_f.astype(F32))
    lg_b = jax.nn.log_sigmoid(dec_b.astype(F32))
    i = jnp.arange(RET_CHUNK)
    diff = (i[:, None] - i[None, :]).astype(F32)
    fwd = jnp.exp(jnp.maximum(diff, 0.0)[None] * lg_f[:, None, None])
    bwd = jnp.exp(jnp.maximum(-diff, 0.0)[None] * lg_b[:, None, None])
    dcomb = jnp.where((diff >= 0)[None], fwd, bwd).reshape(2, 4 * RET_CHUNK, RET_CHUNK)
    fi = i.astype(F32)
    lanes = lambda t: jnp.repeat(t, RET_DK, axis=1)
    xi_f = lanes(jnp.exp((fi + 1.0)[:, None] * lg_f[None, :]))
    zt_f = lanes(jnp.exp((RET_CHUNK - 1 - fi)[:, None] * lg_f[None, :]))
    xi_b = lanes(jnp.exp((RET_CHUNK - fi)[:, None] * lg_b[None, :]))
    zt_b = lanes(jnp.exp(fi[:, None] * lg_b[None, :]))
    g_f = jnp.repeat(jnp.exp(RET_CHUNK * lg_f), RET_DK).reshape(2, 256, 1)
    g_b = jnp.repeat(jnp.exp(RET_CHUNK * lg_b), RET_DK).reshape(2, 256, 1)
    g_f = jnp.broadcast_to(g_f, (2, 256, 256))
    g_b = jnp.broadcast_to(g_b, (2, 256, 256))
    return dcomb, xi_f, zt_f, xi_b, zt_b, g_f, g_b


def _ada_body(c_ref, w_ref, b_ref, o_ref):
    c = c_ref[...]
    s = c * _sigmoid(c)
    sh, sl = _split(s)
    wh, wl = _split(w_ref[...])
    o_ref[...] = _dot(sh, wh) + _dot(sl, wh) + _dot(sh, wl) + b_ref[...]


def _ada(cc, w_ada, b_ada):
    nt = 6 * D_MODEL // TN_ADA
    return pl.pallas_call(
        _ada_body,
        grid=(DEPTH, nt),
        in_specs=[
            pl.BlockSpec((MOD_ROWS, D_MODEL), lambda l, j: (0, 0)),
            pl.BlockSpec((None, D_MODEL, TN_ADA), lambda l, j: (l, 0, j)),
            pl.BlockSpec((None, 1, TN_ADA), lambda l, j: (l, 0, j)),
        ],
        out_specs=pl.BlockSpec((None, MOD_ROWS, TN_ADA), lambda l, j: (l, 0, j)),
        out_shape=jax.ShapeDtypeStruct((DEPTH, MOD_ROWS, 6 * D_MODEL), F32),
        compiler_params=_params(("arbitrary", "arbitrary")),
        name="ada_mod",
    )(cc, w_ada, b_ada.reshape(DEPTH, 1, 6 * D_MODEL))


def _rope(xc, cos, sin, half):
    fwd = pltpu.roll(xc, 128 - half, axis=1)
    bwd = pltpu.roll(xc, half, axis=1)
    lane = lax.broadcasted_iota(jnp.int32, xc.shape, 1)
    first = (lane & (2 * half - 1)) < half
    return xc * cos + jnp.where(first, fwd, bwd) * sin


def _proj_body(*refs, split):
    if split:
        x = _pick(pl.program_id(0) < RC // TM_PROJ, refs[0], refs[1])
        refs = refs[2:]
    else:
        x = refs[0][...]
        refs = refs[1:]
    (mod_ref, gn_ref, w_ref, ca_ref, sa_ref, cr_ref, sr_ref,
     qa_ref, ka_ref, va_ref, qr_ref, kr_ref, vr_ref, gr_ref, fu_ref, gm_ref) = refs
    ms = jnp.mean(x * x, axis=-1, keepdims=True)
    y = x * lax.rsqrt(ms + NORM_EPS) * gn_ref[...]
    h = y * (1.0 + mod_ref[1:2, :]) + mod_ref[0:1, :]
    hb = h.astype(BF16)

    def proj(c0, width):
        return _dot(hb, w_ref[:, c0:c0 + width])

    ca, sa, cr, sr = ca_ref[...], sa_ref[...], cr_ref[...], sr_ref[...]

    qa = proj(0, W) * (HEAD_DIM ** -0.5 * LOG2E)
    for c in range(W // 128):
        qa_ref[:, c * 128:(c + 1) * 128] = _rope(qa[:, c * 128:(c + 1) * 128], ca, sa, 16).astype(BF16)
    kv = proj(W, 256)
    ka = _rope(kv[:, 0:128], ca, sa, 16).astype(BF16)
    ka_ref[0] = ka[:, 0:64]
    ka_ref[1] = ka[:, 64:128]
    va = kv[:, 128:256].astype(BF16)
    ones_col = jnp.where(lax.broadcasted_iota(jnp.int32, (va.shape[0], 64), 1) == 0, 1.0, 0.0).astype(BF16)
    va_ref[0] = jnp.concatenate([va[:, 0:64], ones_col], axis=1)
    va_ref[1] = jnp.concatenate([va[:, 64:128], ones_col], axis=1)
    qr = proj(768, W)
    kr = proj(1280, W) * (RET_DK ** -0.5)
    for c in range(W // 128):
        sl = slice(c * 128, (c + 1) * 128)
        qr_ref[:, sl] = _rope(qr[:, sl], cr, sr, 32).astype(BF16)
        kr_ref[:, sl] = _rope(kr[:, sl], cr, sr, 32).astype(BF16)
    vr_ref[...] = proj(1792, W).astype(BF16)
    g = proj(2304, W)
    gr_ref[...] = (g * _sigmoid(g)).astype(BF16)
    fu_ref[...] = proj(2816, W).astype(BF16)
    for c in range(3):
        gm_ref[:, c * D_MODEL:(c + 1) * D_MODEL] = _sigmoid(proj(3328 + c * D_MODEL, D_MODEL)).astype(BF16)


def _proj(x, mod_l, gnorm, w_in_bf, tabs, layer):
    tm = TM_PROJ
    nt = R // tm
    nc = RC // tm
    split = isinstance(x, tuple)
    if split:
        xs = list(x)
        x_specs = [pl.BlockSpec((tm, D_MODEL), lambda i: (jnp.minimum(i, nc - 1), 0)),
                   pl.BlockSpec((tm, D_MODEL), lambda i: (jnp.maximum(i - nc, 0), 0))]
    else:
        xs = [x]
        x_specs = [pl.BlockSpec((tm, D_MODEL), lambda i: (i, 0))]

    def tab_map(i):
        return (jnp.where(i < nc, i, nc + (i - nc) % (SEQ // tm)), 0)

    row = lambda i: (i, 0)
    wide = lambda n: pl.BlockSpec((tm, n), row)
    kv_spec = lambda n: pl.BlockSpec((2, tm, n), lambda i: (0, i, 0))
    sds = lambda n: jax.ShapeDtypeStruct((R, n), BF16)
    kv_sds = lambda n: jax.ShapeDtypeStruct((2, R, n), BF16)
    mrow = _mod_row(0, tm)
    return pl.pallas_call(
        functools.partial(_proj_body, split=split),
        grid=(nt,),
        in_specs=x_specs + [
            pl.BlockSpec((None, 6, D_MODEL), lambda i: (mrow(i), 0, 0)),
            pl.BlockSpec((1, D_MODEL), lambda i: (0, 0)),
            pl.BlockSpec((None, D_MODEL, IN_COLS), lambda i: (layer, 0, 0), pipeline_mode=pl.Buffered(1)),
        ] + [pl.BlockSpec((tm, 128), tab_map)] * 4,
        out_specs=[wide(W), kv_spec(64), kv_spec(128), wide(W), wide(W), wide(W), wide(W), wide(W),
                   wide(3 * D_MODEL)],
        out_shape=[sds(W), kv_sds(64), kv_sds(128), sds(W), sds(W), sds(W), sds(W), sds(W), sds(3 * D_MODEL)],
        compiler_params=_params(("arbitrary",)),
        name="in_proj",
    )(*xs, mod_l, gnorm, w_in_bf, *tabs)


def _attn_body(sink_ref, q_ref, kc_ref, kp_ref, kk_ref, kn_ref, vc_ref, vp_ref, vk_ref, vn_ref, o_ref, *, q_lo):
    qi = pl.program_id(1) + q_lo
    is_lat = qi >= 2
    m = qi - 2
    nr = ATTN_GROUP * ATTN_BLOCK
    ri = lax.broadcasted_iota(jnp.int32, (nr, ATTN_BLOCK), 0) & (ATTN_BLOCK - 1)
    ci = lax.broadcasted_iota(jnp.int32, (nr, ATTN_BLOCK), 1)
    far = 4 * ATTN_BLOCK
    off_p = jnp.where(is_lat & (m >= 1), 0, far)
    off_k = jnp.where(is_lat, 0, far)
    off_n = jnp.where(is_lat & (m <= SEQ // ATTN_BLOCK - 2), 0, far)
    ok_p = ci >= ri + off_p
    ok_k = ci >= off_k
    ok_n = ci + off_n <= ri
    rg = lax.broadcasted_iota(jnp.int32, (nr, 1), 0) >> 7
    rmax = lambda s: jnp.max(s, axis=-1, keepdims=True)
    outs = []
    for h in range(ATTN_KV_HEADS):
        q = q_ref[:, h * 256:(h + 1) * 256]
        q4 = jnp.concatenate([q[:, g * 64:(g + 1) * 64] for g in range(ATTN_GROUP)], axis=0)
        s_c = _dot_nt(q4, kc_ref[h])
        s_p = jnp.where(ok_p, _dot_nt(q4, kp_ref[h]), NEG)
        s_k = jnp.where(ok_k, _dot_nt(q4, kk_ref[h]), NEG)
        s_n = jnp.where(ok_n, _dot_nt(q4, kn_ref[h]), NEG)
        sk = jnp.zeros((nr, 1), F32)
        for g in range(ATTN_GROUP):
            sk = jnp.where(rg == g, sink_ref[h * ATTN_GROUP + g] * LOG2E, sk)
        mx = jnp.maximum(jnp.maximum(rmax(s_c), rmax(s_p)), jnp.maximum(rmax(s_k), rmax(s_n)))
        mx = jnp.maximum(mx, sk)
        p_c, p_p, p_k, p_n = (jnp.exp2(s - mx).astype(BF16) for s in (s_c, s_p, s_k, s_n))
        oa = _dot(p_c, vc_ref[h]) + _dot(p_p, vp_ref[h]) + _dot(p_k, vk_ref[h]) + _dot(p_n, vn_ref[h])
        o = oa[:, 0:HEAD_DIM] / (oa[:, HEAD_DIM:HEAD_DIM + 1] + jnp.exp2(sk - mx))
        outs += [o[g * ATTN_BLOCK:(g + 1) * ATTN_BLOCK] for g in range(ATTN_GROUP)]
    o_ref[...] = jnp.concatenate(outs, axis=1).astype(BF16)


def _attention(sink, qa, ka, va, need_ctx):
    q_lo = 0 if need_ctx else 2
    nq = SEQ // ATTN_BLOCK + (2 if need_ctx else 0)
    nb = SEQ // ATTN_BLOCK
    lat0 = RC // ATTN_BLOCK
    out0 = 0 if need_ctx else lat0

    def qrow(b, qi):
        return jnp.where(qi < 2, b * 2 + qi, lat0 + b * nb + qi - 2)

    def loc(delta):
        def f(b, i):
            m = jnp.clip(i + q_lo - 2 + delta, 0, nb - 1)
            return (0, lat0 + b * nb + m, 0)
        return f

    def kv_specs(width):
        ctx_spec = pl.BlockSpec((ATTN_KV_HEADS, CTX_LEN, width), lambda b, i: (0, b, 0))
        loc_spec = lambda d: pl.BlockSpec((ATTN_KV_HEADS, ATTN_BLOCK, width), loc(d))
        return [ctx_spec, loc_spec(-1), loc_spec(0), loc_spec(1)]

    return pl.pallas_call(
        functools.partial(_attn_body, q_lo=q_lo),
        grid=(BATCH, nq),
        in_specs=[pl.BlockSpec(memory_space=pltpu.SMEM),
                  pl.BlockSpec((ATTN_BLOCK, W), lambda b, i: (qrow(b, i + q_lo), 0))]
                 + kv_specs(HEAD_DIM) + kv_specs(128),
        out_specs=pl.BlockSpec((ATTN_BLOCK, W), lambda b, i: (qrow(b, i + q_lo) - out0, 0)),
        out_shape=jax.ShapeDtypeStruct((R - out0 * ATTN_BLOCK, W), BF16),
        compiler_params=_params(("arbitrary", "arbitrary")),
        name="window_attn",
    )(sink, qa, ka, ka, ka, ka, va, va, va, va)


def _fourier_body(*refs, has_ctx):
    if has_ctx:
        uc_ref, ul_ref, bdc_ref, bds_ref, w2_ref, w2c_ref, oc_ref, ol_ref, as_ref = refs
    else:
        ul_ref, bdc_ref, bds_ref, w2_ref, ol_ref, as_ref = refs
    j = pl.program_id(1)
    first = 1 if has_ctx else 0

    if has_ctx:
        @pl.when(j == 0)
        def _():
            u = uc_ref[...]
            a = _dot(u, bdc_ref[...]).astype(BF16)
            s = _dot(u, bds_ref[...]).astype(BF16)
            z = _dot(w2c_ref[...], jnp.concatenate([a, s], axis=0))
            oc_ref[...] = (z * ((CTX_LEN * FOURIER_DIM) ** -0.5)).astype(BF16)

    @pl.when(j == first)
    def _():
        u = ul_ref[...]
        as_ref[0:SEQ, :] = _dot(u, bdc_ref[...]).astype(BF16)
        as_ref[SEQ:2 * SEQ, :] = _dot(u, bds_ref[...]).astype(BF16)

    @pl.when(j >= first)
    def _():
        ol_ref[...] = (_dot(w2_ref[...], as_ref[...]) * ((SEQ * FOURIER_DIM) ** -0.5)).astype(BF16)


def _fourier(fu, dft, need_ctx):
    bdc, bds, w2, w2c = dft
    tr = TR_FOURIER
    nj = SEQ // tr
    first = 1 if need_ctx else 0
    full = lambda a: pl.BlockSpec(a.shape, lambda *_: (0,) * a.ndim)
    lat_tile = lambda j: jnp.maximum(j - first, 0)
    ul_spec = pl.BlockSpec((SEQ, W), lambda b, j: (1 + b, 0))
    w2_spec = pl.BlockSpec((tr, 2 * SEQ), lambda b, j: (lat_tile(j), 0))
    ol_spec = pl.BlockSpec((tr, W), lambda b, j: (b * nj + lat_tile(j), 0))
    ol_shape = jax.ShapeDtypeStruct((RL, W), BF16)
    if need_ctx:
        in_specs = [pl.BlockSpec((CTX_LEN, W), lambda b, j: (b, 0)), ul_spec, full(bdc), full(bds), w2_spec,
                    full(w2c)]
        out_specs = [pl.BlockSpec((CTX_LEN, W), lambda b, j: (b, 0)), ol_spec]
        out_shape = [jax.ShapeDtypeStruct((RC, W), BF16), ol_shape]
        args = (fu, fu, bdc, bds, w2, w2c)
    else:
        in_specs = [ul_spec, full(bdc), full(bds), w2_spec]
        out_specs = [ol_spec]
        out_shape = [ol_shape]
        args = (fu, bdc, bds, w2)
    outs = pl.pallas_call(
        functools.partial(_fourier_body, has_ctx=need_ctx),
        grid=(BATCH, nj + first),
        in_specs=in_specs,
        out_specs=out_specs,
        out_shape=out_shape,
        scratch_shapes=[pltpu.VMEM((2 * SEQ, W), BF16)],
        compiler_params=_params(("arbitrary", "arbitrary")),
        name="fourier_mix",
    )(*args)
    return (outs[0], outs[1]) if need_ctx else (None, outs[0])


def _retention_body(qc_ref, kc_ref, vc_ref, ql_ref, kl_ref, vl_ref, g_ref,
                    dcomb_ref, xif_ref, ztf_ref, xib_ref, ztb_ref, gf_ref, gb_ref, mbd_ref, avg_ref,
                    o_ref, os_ref, st_ref):
    j = pl.program_id(1)
    C = RET_CHUNK
    nl = SEQ // C

    def group_parts(q, k, v, gi):
        sl = slice(gi * 256, (gi + 1) * 256)
        return q[:, sl], k[:, sl], v[:, sl], sl

    own = ((lax.broadcasted_iota(jnp.int32, (4 * C, 256), 0) >> 7)
           == (lax.broadcasted_iota(jnp.int32, (4 * C, 256), 1) >> 6))

    def chunk_fwd(q, k, v, r0):
        for gi in range(2):
            q4, k4, v4, sl = group_parts(q, k, v, gi)
            s_prev = st_ref[gi]
            q4f = q4.astype(F32)
            o4 = _dot((q4f * xif_ref[:, sl]).astype(BF16), s_prev.astype(BF16))
            qstack = jnp.where(own, jnp.concatenate([q4f] * 4, axis=0), 0.0).astype(BF16)
            p = (_dot_nt(qstack, k4) * dcomb_ref[gi]).astype(BF16)
            ov = jnp.where(own, _dot(p, v4), 0.0)
            intra = ov[0:C] + ov[C:2 * C] + ov[2 * C:3 * C] + ov[3 * C:4 * C]
            os_ref[pl.ds(r0, C), sl] = o4 + intra
            u = _dot_tn(k4, (v4.astype(F32) * ztf_ref[:, sl]).astype(BF16))
            st_ref[gi] = gf_ref[gi] * s_prev + mbd_ref[...] * u

    def chunk_bwd(q, k, v, r0):
        for gi in range(2):
            q4, k4, v4, sl = group_parts(q, k, v, gi)
            s_prev = st_ref[gi]
            cross = _dot((q4.astype(F32) * xib_ref[:, sl]).astype(BF16), s_prev.astype(BF16))
            os_ref[pl.ds(r0, C), sl] = os_ref[pl.ds(r0, C), sl] + cross
            u = _dot_tn(k4, (v4.astype(F32) * ztb_ref[:, sl]).astype(BF16))
            st_ref[gi] = gb_ref[gi] * s_prev + mbd_ref[...] * u

    @pl.when(j == 0)
    def _():
        st_ref[...] = jnp.zeros_like(st_ref)
        for c in range(CTX_LEN // C):
            rs = slice(c * C, (c + 1) * C)
            chunk_fwd(qc_ref[rs, :], kc_ref[rs, :], vc_ref[rs, :], c * C)

        def fbody(c, carry):
            r0 = pl.multiple_of(c * C, C)
            rs = pl.ds(r0, C)
            chunk_fwd(ql_ref[rs, :], kl_ref[rs, :], vl_ref[rs, :], CTX_LEN + r0)
            return carry

        lax.fori_loop(0, nl, fbody, 0)

        st_ref[...] = jnp.zeros_like(st_ref)
        for c in reversed(range(CTX_LEN // C)):
            rs = slice(c * C, (c + 1) * C)
            chunk_bwd(qc_ref[rs, :], kc_ref[rs, :], vc_ref[rs, :], c * C)

        def bbody(t, carry):
            r0 = pl.multiple_of((nl - 1 - t) * C, C)
            rs = pl.ds(r0, C)
            chunk_bwd(ql_ref[rs, :], kl_ref[rs, :], vl_ref[rs, :], CTX_LEN + r0)
            return carry

        lax.fori_loop(0, nl, bbody, 0)

    o = os_ref[pl.ds(pl.multiple_of(j * 256, 256), 256), :]
    avg = avg_ref[...]
    oh, ol = _split(o)
    mu = _dot(oh, avg) + _dot(ol, avg)
    d = o - mu
    vh, vl = _split(d * d)
    var = _dot(vh, avg) + _dot(vl, avg)
    o_ref[...] = (g_ref[...].astype(F32) * d * lax.rsqrt(var + GN_EPS)).astype(BF16)


def _retention(qr, kr, vr, gr, rtabs):
    nj = 1 + SEQ // 256

    def out_map(b, j):
        return (jnp.where(j == 0, b, RC // 256 + b * (SEQ // 256) + j - 1), 0)

    ctx = pl.BlockSpec((CTX_LEN, W), lambda b, j: (b, 0))
    lat = pl.BlockSpec((SEQ, W), lambda b, j: (1 + b, 0))
    full = lambda a: pl.BlockSpec(a.shape, lambda *_: (0,) * a.ndim)
    avg = jnp.asarray(np.kron(np.eye(RET_HEADS), np.full((RET_DK, RET_DK), 1.0 / RET_DK)).astype(np.float32)).astype(BF16)
    mbd = jnp.asarray(np.kron(np.eye(4), np.ones((RET_DK, RET_DK))).astype(np.float32))
    tabs = list(rtabs) + [mbd, avg]
    return pl.pallas_call(
        _retention_body,
        grid=(BATCH, nj),
        in_specs=[ctx, ctx, ctx, lat, lat, lat, pl.BlockSpec((256, W), out_map)] + [full(t) for t in tabs],
        out_specs=pl.BlockSpec((256, W), out_map),
        out_shape=jax.ShapeDtypeStruct((R, W), BF16),
        scratch_shapes=[pltpu.VMEM((CTX_LEN + SEQ, W), F32), pltpu.VMEM((2, 256, 256), F32)],
        compiler_params=_params(("arbitrary", "arbitrary")),
        name="retention",
    )(qr, kr, vr, qr, kr, vr, gr, *tabs)


def _pack_pair(a, b):
    ua = lax.bitcast_convert_type(a, jnp.uint32) >> 16
    ub = lax.bitcast_convert_type(b, jnp.uint32) & jnp.uint32(0xFFFF0000)
    return ua | ub


def _unpack_pair(w):
    a = lax.bitcast_convert_type(w << 16, F32)
    b = lax.bitcast_convert_type(w & jnp.uint32(0xFFFF0000), F32)
    return a, b


def _slot_onehot(s0, s1, n):
    srow = lax.broadcasted_iota(jnp.int32, (n, s0.shape[1]), 0)
    p0 = jnp.where(srow == s0, 1.0, 0.0).astype(BF16)
    p1 = jnp.where(srow == s1, 1.0, 0.0).astype(BF16)
    return p0, p1


def _pick(first, a_ref, b_ref):
    a = a_ref[...]
    flag = jnp.zeros(a.shape, jnp.int32) + first.astype(jnp.int32)
    return jnp.where(flag > 0, a, b_ref[...])


def _merge_body(*refs, layer0):
    if layer0:
        oa_ref, ofc_ref, ofl_ref, rt_ref, gm_ref, xc_ref, xl_ref = refs[:7]
        rest = refs[7:]
        is_ctx = pl.program_id(0) < RC // TM_MERGE
        of_in = _pick(is_ctx, ofc_ref, ofl_ref)
        x_in = _pick(is_ctx, xc_ref, xl_ref)
    else:
        oa_ref, ofl_ref, rt_ref, gm_ref, x_ref = refs[:5]
        rest = refs[5:]
        of_in = ofl_ref[...]
        x_in = x_ref[...]
    (mod_ref, gn_ref, wba_ref, wbf_ref, wbr_ref, wout_ref, wrh_ref, wrl_ref, br_ref, tri_ref, ltri_ref,
     xo_ref, xs_ref, ro_ref, nch_ref) = rest
    gm = gm_ref[...].astype(F32)
    z = (gm[:, 0:D_MODEL] * _dot(oa_ref[...], wba_ref[...])
         + gm[:, D_MODEL:2 * D_MODEL] * _dot(of_in, wbf_ref[...])
         + gm[:, 2 * D_MODEL:3 * D_MODEL] * _dot(rt_ref[...], wbr_ref[...]))
    y = _dot(z.astype(BF16), wout_ref[...])
    x = x_in + mod_ref[2:3, :] * y
    xo_ref[...] = x
    ms = jnp.mean(x * x, axis=-1, keepdims=True)
    hn = x * lax.rsqrt(ms + NORM_EPS) * gn_ref[...]
    h2 = hn * (1.0 + mod_ref[4:5, :]) + mod_ref[3:4, :]
    hh, hl = _split(h2)
    wh, wl = wrh_ref[...], wrl_ref[...]
    lg = _dot_nt(wh, hh) + _dot_nt(wh, hl) + _dot_nt(wl, hh) + br_ref[...]
    tm = lg.shape[1]
    row8 = lax.broadcasted_iota(jnp.int32, (8, tm), 0)
    lgg = lg[0:8, :]
    mg = jnp.max(lgg, axis=0, keepdims=True)
    grp = jnp.min(jnp.where(lgg == mg, row8, 8), axis=0, keepdims=True)
    pg = 1.0 / jnp.sum(jnp.exp(lgg - mg), axis=0, keepdims=True)
    lin = jnp.zeros((8, tm), F32)
    for g in range(N_GROUPS):
        lin = jnp.where(grp == g, lg[8 + 8 * g:16 + 8 * g, :], lin)
    v1 = jnp.max(lin, axis=0, keepdims=True)
    i1 = jnp.min(jnp.where(lin == v1, row8, 8), axis=0, keepdims=True)
    rest = jnp.where(row8 == i1, -jnp.inf, lin)
    v2 = jnp.max(rest, axis=0, keepdims=True)
    i2 = jnp.min(jnp.where(rest == v2, row8, 8), axis=0, keepdims=True)
    e2 = jnp.exp(v2 - v1)
    w1 = pg / (1.0 + e2)
    w2 = pg * e2 / (1.0 + e2)
    e_1 = grp * EXPERTS_PER_GROUP + i1
    e_2 = grp * EXPERTS_PER_GROUP + i2

    row32 = lax.broadcasted_iota(jnp.int32, (N_EXPERTS, tm), 0)
    oh0 = jnp.where(row32 == e_1, 1.0, 0.0)
    oh1 = jnp.where(row32 == e_2, 1.0, 0.0)
    tri = tri_ref[...]
    cum0 = _dot(oh0.astype(BF16), tri)
    cum1 = _dot(oh1.astype(BF16), tri)
    tot0 = jnp.sum(oh0, axis=1, keepdims=True)
    tot1 = jnp.sum(oh1, axis=1, keepdims=True)
    nch = ((tot0 + tot1).astype(jnp.int32) + (CHUNK - 1)) >> CHUNK_SHIFT
    nch_b = jnp.broadcast_to(nch.astype(F32), (N_EXPERTS, 128))
    nch_ref[...] = nch_b.astype(jnp.int32)
    base = CHUNK * _dot(ltri_ref[...], nch_b.astype(BF16))[:, 0:1]
    s0 = jnp.sum(oh0 * (base + cum0), axis=0, keepdims=True).astype(jnp.int32)
    s1 = jnp.sum(oh1 * (base + tot0 + cum1), axis=0, keepdims=True).astype(jnp.int32)
    p0, p1 = _slot_onehot(s0, s1, SLOTS)
    xs = _dot(p0 + p1, hh)
    xs_ref[:, 0:D_MODEL // 2] = _pack_pair(xs[:, 0:D_MODEL // 2], xs[:, D_MODEL // 2:D_MODEL])

    def wrows(w):
        hi, lo = _split(w)
        return jnp.where(row8 == 0, hi.astype(F32), jnp.where(row8 == 1, lo.astype(F32), 0.0)).astype(BF16)

    wc = _dot_nt(p0, wrows(w1)) + _dot_nt(p1, wrows(w2))
    wcol = jnp.broadcast_to(wc[:, 0:1] + wc[:, 1:2], (SLOTS, 128))
    xs_ref[:, D_MODEL // 2:XS_COLS] = lax.bitcast_convert_type(wcol, jnp.uint32)
    s0f, s1f = s0.astype(F32), s1.astype(F32)
    ro_ref[...] = jnp.where(row8 == 0, s0f, jnp.where(row8 == 1, s1f, 0.0))


def _merge(oa, of_c, of_l, ret, gm, xs_in, mod_l, gnorm, wba, wbf, wbr, wout, wrh, wrl, brb, layer0):
    tm = TM_MERGE
    row0 = 0 if layer0 else RC
    rm = R - row0
    nt = rm // tm
    off = row0 // tm
    nc = RC // tm
    src = lambda n: pl.BlockSpec((tm, n), lambda i: (i + off, 0))
    dst = lambda n: pl.BlockSpec((tm, n), lambda i: (i, 0))
    ctx_rows = lambda n: pl.BlockSpec((tm, n), lambda i: (jnp.minimum(i, nc - 1), 0))
    lat_rows = lambda n: pl.BlockSpec((tm, n), lambda i: (jnp.maximum(i - nc, 0), 0))
    full = lambda a: pl.BlockSpec(a.shape, lambda *_: (0,) * a.ndim, pipeline_mode=pl.Buffered(1))
    mrow = _mod_row(row0, tm)
    tri = jnp.asarray(np.triu(np.ones((tm, tm), np.float32), 1)).astype(BF16)
    ltri = jnp.asarray(np.tril(np.ones((N_EXPERTS, N_EXPERTS), np.float32), -1)).astype(BF16)
    if layer0:
        acts = [oa, of_c, of_l, ret, gm, xs_in[0], xs_in[1]]
        act_specs = [src(W), ctx_rows(W), lat_rows(W), src(W), src(3 * D_MODEL), ctx_rows(D_MODEL),
                     lat_rows(D_MODEL)]
    else:
        acts = [oa, of_l, ret, gm, xs_in]
        act_specs = [dst(W), dst(W), src(W), src(3 * D_MODEL), src(D_MODEL)]
    return pl.pallas_call(
        functools.partial(_merge_body, layer0=layer0),
        grid=(nt,),
        in_specs=act_specs + [
                  pl.BlockSpec((None, 6, D_MODEL), lambda i: (mrow(i), 0, 0)),
                  full(gnorm), full(wba), full(wbf), full(wbr), full(wout), full(wrh), full(wrl), full(brb),
                  full(tri), full(ltri)],
        out_specs=[dst(D_MODEL), pl.BlockSpec((SLOTS, XS_COLS), lambda i: (i, 0)),
                   pl.BlockSpec((8, tm), lambda i: (0, i)),
                   pl.BlockSpec((None, N_EXPERTS, 128), lambda i: (i, 0, 0))],
        out_shape=[jax.ShapeDtypeStruct((rm, D_MODEL), F32),
                   jax.ShapeDtypeStruct((nt * SLOTS, XS_COLS), jnp.uint32),
                   jax.ShapeDtypeStruct((8, rm), F32),
                   jax.ShapeDtypeStruct((nt, N_EXPERTS, 128), jnp.int32)],
        compiler_params=_params(("arbitrary",)),
        name="merge_router",
    )(*acts, mod_l, gnorm, wba, wbf, wbr, wout, wrh, wrl, brb, tri, ltri)


def _moe_plan(nch, nb):
    nt = nch.shape[0]
    choff = jnp.cumsum(nch, axis=1) - nch
    used_ch = jnp.sum(nch, axis=1)
    cum_t = jnp.cumsum(nch, axis=0)
    tot = cum_t[-1]
    ptot = (tot + CPB - 1) // CPB * CPB
    pend = jnp.cumsum(ptot)
    pstart = pend - ptot
    n_used = pend[-1] // CPB
    blk = jnp.arange(nb, dtype=jnp.int32)
    lane = jnp.arange(CPB, dtype=jnp.int32)
    blk_e = jnp.minimum(jnp.sum((blk[:, None] * CPB >= pend[None, :]).astype(jnp.int32), axis=1), N_EXPERTS - 1)
    oe = (blk_e[:, None] == jnp.arange(N_EXPERTS, dtype=jnp.int32)[None, :]).astype(jnp.int32)
    sel = lambda tab: jnp.sum(oe[:, :, None] * tab.T[None, :, :], axis=1)
    pstart_b = jnp.sum(oe * pstart[None, :], axis=1)
    tot_b = jnp.sum(oe * tot[None, :], axis=1)
    cum_b, nch_b, choff_b = sel(cum_t), sel(nch), sel(choff)
    i = blk[:, None] * CPB + lane[None, :] - pstart_b[:, None]
    valid = (i < tot_b[:, None]) & (blk[:, None] < n_used)
    t = jnp.minimum(jnp.sum((i[:, :, None] >= cum_b[:, None, :]).astype(jnp.int32), axis=2), nt - 1)
    tiles = jnp.arange(nt, dtype=jnp.int32)[None, None, :]
    before = jnp.sum(jnp.where(tiles < t[:, :, None], nch_b[:, None, :], 0), axis=2)
    coff = jnp.sum(jnp.where(tiles == t[:, :, None], choff_b[:, None, :], 0), axis=2)
    row = t * SLOTS + CHUNK * (coff + i - before)
    src = jnp.where(valid, row, SLOTS - CHUNK)
    dummy = nt * SLOTS + CHUNK * ((blk[:, None] % 2) * CPB + lane[None, :])
    dst = jnp.where(valid, row, dummy)
    return (blk_e.astype(jnp.int32), n_used.astype(jnp.int32).reshape(1), src.reshape(-1).astype(jnp.int32),
            dst.reshape(-1).astype(jnp.int32), used_ch.astype(jnp.int32))


def _ffn_body(be_ref, nu_ref, src_ref, dst_ref, uc_ref, xs_ref, wg_ref, wu_ref, wd_ref, ys_ref,
              xbuf, ybuf, zbuf, wgb, wub, wdb, sem_in, sem_out, sem_z, *, nt):
    b = pl.program_id(0)
    nu = nu_ref[0]
    slot = b % 2
    half = D_MODEL // 2

    def gather(blk, sl):
        for c in range(CPB):
            r = pl.multiple_of(src_ref[blk * CPB + c], CHUNK)
            pltpu.make_async_copy(xs_ref.at[pl.ds(r, CHUNK)], xbuf.at[sl, pl.ds(c * CHUNK, CHUNK)],
                                  sem_in.at[sl]).start()

    def scatter(blk, sl):
        for c in range(CPB):
            r = pl.multiple_of(dst_ref[blk * CPB + c], CHUNK)
            pltpu.make_async_copy(ybuf.at[sl, pl.ds(c * CHUNK, CHUNK)], ys_ref.at[pl.ds(r, CHUNK)],
                                  sem_out.at[sl]).start()

    def wait_gather(sl):
        pltpu.make_async_copy(xs_ref.at[pl.ds(0, MOE_BM)], xbuf.at[sl], sem_in.at[sl]).wait()

    def wait_scatter(sl):
        pltpu.make_async_copy(ybuf.at[sl], ys_ref.at[pl.ds(0, MOE_BM)], sem_out.at[sl]).wait()

    def zero_copy(r):
        return pltpu.make_async_copy(zbuf, ys_ref.at[pl.ds(pl.multiple_of(r, CHUNK), CHUNK)], sem_z)

    @pl.when(b == 0)
    def _():
        zbuf[...] = jnp.zeros_like(zbuf)

        def tails(fn):
            def per_tile(t, carry):
                def per_chunk(c, carry2):
                    fn(t * SLOTS + c * CHUNK)
                    return carry2
                lax.fori_loop(uc_ref[t], SLOTS // CHUNK, per_chunk, 0)
                return carry
            lax.fori_loop(0, nt, per_tile, 0)
            for c in range(2 * CPB):
                fn(nt * SLOTS + c * CHUNK)

        tails(lambda r: zero_copy(r).start())
        tails(lambda r: zero_copy(r).wait())
        gather(0, 0)

    prev = be_ref[jnp.maximum(b - 1, 0)]
    fresh = (b == 0) | (be_ref[b] != prev)

    @pl.when(fresh)
    def _():
        wgb[...] = wg_ref[...].astype(BF16)
        wub[...] = wu_ref[...].astype(BF16)
        wdb[...] = wd_ref[...].astype(BF16)

    @pl.when(b < nu)
    def _():
        @pl.when(b + 1 < nu)
        def _():
            gather(b + 1, 1 - slot)

        wait_gather(slot)

        @pl.when(b >= 2)
        def _():
            wait_scatter(slot)

        xw = xbuf[slot]
        xa, xb = _unpack_pair(xw[:, 0:half])
        x = jnp.concatenate([xa, xb], axis=1).astype(BF16)
        wt = lax.bitcast_convert_type(xw[:, half:XS_COLS], F32)
        g = _dot(x, wgb[...])
        u = _dot(x, wub[...])
        hmid = (g * _sigmoid(g) * u).astype(BF16)
        y = _dot(hmid, wdb[...]) * jnp.concatenate([wt] * (D_MODEL // 128), axis=1)
        yb = y.astype(BF16).astype(F32)
        ybuf[slot] = _pack_pair(yb[:, 0:half], yb[:, half:D_MODEL])
        scatter(b, slot)

        @pl.when(b == nu - 1)
        def _():
            wait_scatter(slot)

            @pl.when(b >= 1)
            def _():
                wait_scatter(1 - slot)


def _ffn(plan, xs, w_g, w_u, w_d, layer, nt, nb):
    wmap = lambda b, be, *_: (layer, be[b], 0, 0)
    half = D_MODEL // 2
    grid_spec = pltpu.PrefetchScalarGridSpec(
        num_scalar_prefetch=5,
        grid=(nb,),
        in_specs=[pl.BlockSpec(memory_space=pl.ANY),
                  pl.BlockSpec((None, None, D_MODEL, EXPERT_HIDDEN), wmap),
                  pl.BlockSpec((None, None, D_MODEL, EXPERT_HIDDEN), wmap),
                  pl.BlockSpec((None, None, EXPERT_HIDDEN, D_MODEL), wmap)],
        out_specs=pl.BlockSpec(memory_space=pl.ANY),
        scratch_shapes=[pltpu.VMEM((2, MOE_BM, XS_COLS), jnp.uint32), pltpu.VMEM((2, MOE_BM, half), jnp.uint32),
                        pltpu.VMEM((CHUNK, half), jnp.uint32),
                        pltpu.VMEM((D_MODEL, EXPERT_HIDDEN), BF16), pltpu.VMEM((D_MODEL, EXPERT_HIDDEN), BF16),
                        pltpu.VMEM((EXPERT_HIDDEN, D_MODEL), BF16),
                        pltpu.SemaphoreType.DMA((2,)), pltpu.SemaphoreType.DMA((2,)), pltpu.SemaphoreType.DMA(())],
    )
    return pl.pallas_call(
        functools.partial(_ffn_body, nt=nt),
        grid_spec=grid_spec,
        out_shape=jax.ShapeDtypeStruct((nt * SLOTS + 2 * CPB * CHUNK, half), jnp.uint32),
        compiler_params=_params(("arbitrary",)),
        name="moe_experts",
    )(*plan, xs, w_g, w_u, w_d)


def _combine_body(ys_ref, ro_ref, x_ref, mod_ref, gn_ref, o_ref, *, final):
    s = ro_ref[...]
    p0, p1 = _slot_onehot(s[0:1, :].astype(jnp.int32), s[1:2, :].astype(jnp.int32), SLOTS)
    ya, yb = _unpack_pair(ys_ref[...])
    y = jnp.concatenate([ya, yb], axis=1).astype(BF16)
    f = _dot_tn(p0 + p1, y)
    x = x_ref[...] + mod_ref[5:6, :] * f
    if final:
        ms = jnp.mean(x * x, axis=-1, keepdims=True)
        x = x * lax.rsqrt(ms + NORM_EPS) * gn_ref[...]
    o_ref[...] = x


def _combine(ys, route, x, mod_l, gnorm, row0, final):
    tm = TM_MERGE
    rm = x.shape[0]
    mrow = _mod_row(row0, tm)
    return pl.pallas_call(
        functools.partial(_combine_body, final=final),
        grid=(rm // tm,),
        in_specs=[pl.BlockSpec((SLOTS, D_MODEL // 2), lambda i: (i, 0)),
                  pl.BlockSpec((8, tm), lambda i: (0, i)),
                  pl.BlockSpec((tm, D_MODEL), lambda i: (i, 0)),
                  pl.BlockSpec((None, 6, D_MODEL), lambda i: (mrow(i), 0, 0)),
                  pl.BlockSpec((1, D_MODEL), lambda i: (0, 0))],
        out_specs=pl.BlockSpec((tm, D_MODEL), lambda i: (i, 0)),
        out_shape=jax.ShapeDtypeStruct((rm, D_MODEL), F32),
        compiler_params=_params(("arbitrary",)),
        name="moe_combine",
    )(ys, route, x, mod_l, gnorm)


def _moe(xs, route, nch3, x, mod_l, gnorm, w_g, w_u, w_d, row0, final, layer):
    nt = nch3.shape[0]
    max_chunks = nt * ((2 * TM_MERGE + N_EXPERTS * (CHUNK - 1)) // CHUNK)
    nb = -(-max_chunks // CPB) + N_EXPERTS
    plan = _moe_plan(nch3[:, :, 0], nb)
    ys = _ffn(plan, xs, w_g, w_u, w_d, layer, nt, nb)
    return _combine(ys, route, x, mod_l, gnorm, row0, final)


def kernel(x, c, ctx, c_ctx, norm_mix, norm_ffn, w_ada, b_ada, w_in, attn_sink, ret_decay_fwd, ret_decay_bwd,
           w_branch_attn, w_branch_fourier, w_branch_ret, w_out, w_router_group, b_router_group,
           w_router_expert, b_router_expert, w_exp_gate, w_exp_up, w_exp_down, norm_final):
    tabs = [jnp.asarray(t) for t in _rope_tables()]
    dft = [jnp.asarray(t).astype(BF16) for t in _dft_tables()]

    cc = jnp.zeros((MOD_ROWS, D_MODEL), F32).at[0:BATCH].set(c).at[CTX_MOD_ROW].set(c_ctx)
    mod = _ada(cc, w_ada, b_ada).reshape(DEPTH, MOD_ROWS, 6, D_MODEL)

    xf = (ctx.reshape(RC, D_MODEL), x.reshape(RL, D_MODEL))
    w_in_bf = w_in.astype(BF16)
    for l in range(DEPTH):
        need_ctx = l < DEPTH - 1
        row0 = 0 if need_ctx else RC
        mod_l = mod[l]
        qa, ka, va, qr, kr, vr, gr, fu, gm = _proj(xf, mod_l, norm_mix[l][None, :], w_in_bf, tabs, l)
        oa = _attention(attn_sink[l], qa, ka, va, need_ctx)
        of_c, of_l = _fourier(fu, dft, need_ctx)
        ret = _retention(qr, kr, vr, gr, _retention_tables(ret_decay_fwd[l], ret_decay_bwd[l]))
        wr = jnp.zeros((ROUTER_ROWS, D_MODEL), F32)
        wr = wr.at[0:N_GROUPS].set(w_router_group[l].T).at[8:8 + N_EXPERTS].set(w_router_expert[l].T)
        br = jnp.full((ROUTER_ROWS,), NEG, F32)
        br = br.at[0:N_GROUPS].set(b_router_group[l]).at[8:8 + N_EXPERTS].set(b_router_expert[l])
        wrh, wrl = _split(wr)
        brb = jnp.broadcast_to(br[:, None], (ROUTER_ROWS, TM_MERGE))
        x_mid, xs, route, nch3 = _merge(oa, of_c, of_l, ret, gm, xf, mod_l, norm_ffn[l][None, :],
                                        w_branch_attn[l].astype(BF16), w_branch_fourier[l].astype(BF16),
                                        w_branch_ret[l].astype(BF16), w_out[l].astype(BF16), wrh, wrl, brb,
                                        need_ctx)
        final = l == DEPTH - 1
        xf = _moe(xs, route, nch3, x_mid, mod_l, norm_final[None, :], w_exp_gate, w_exp_up, w_exp_down,
                  row0, final, l)
    return xf.reshape(BATCH, SEQ, D_MODEL)
```

```python
import functools

import numpy as np
import jax
import jax.numpy as jnp
from jax import lax
from jax.experimental import pallas as pl
from jax.experimental.pallas import tpu as pltpu

F32 = jnp.float32
BF16 = jnp.bfloat16

D_MODEL = 1024
BATCH = 8
SEQ = 2048
DEPTH = 2
CTX_LEN = 256
GRID_W = 64
HEAD_DIM = 64
ATTN_HEADS = 8
ATTN_KV_HEADS = 2
ATTN_GROUP = ATTN_HEADS // ATTN_KV_HEADS
ATTN_BLOCK = 128
RET_HEADS = 8
RET_DK = 64
RET_CHUNK = 128
FOURIER_GROUPS = 4
FOURIER_DIM = 128
N_GROUPS = 4
EXPERTS_PER_GROUP = 8
N_EXPERTS = N_GROUPS * EXPERTS_PER_GROUP
EXPERT_HIDDEN = 512
ROPE_BASE = 10000.0
NORM_EPS = 1e-6
GN_EPS = 1e-5

W = 512
IN_COLS = 6400
RC = BATCH * CTX_LEN
RL = BATCH * SEQ
R = RC + RL
MOD_ROWS = 16
CTX_MOD_ROW = 8

VMEM_LIMIT = 52 * 1024 * 1024

TM_PROJ = 512
TM_MERGE = 512
TN_ADA = 1536
TR_FOURIER = 512
MOE_BM = 256
CHUNK = 8
CHUNK_SHIFT = 3
CPB = MOE_BM // CHUNK
SLOTS = 1280
XS_COLS = D_MODEL // 2 + 128
NEG = -1e30
LOG2E = 1.4426950408889634
ROUTER_ROWS = 40


def _dot(a, b):
    return jnp.dot(a, b, preferred_element_type=F32)


def _dot_nt(a, b):
    return lax.dot_general(a, b, (((1,), (1,)), ((), ())), preferred_element_type=F32)


def _dot_tn(a, b):
    return lax.dot_general(a, b, (((0,), (0,)), ((), ())), preferred_element_type=F32)


def _split(x):
    hi = x.astype(BF16)
    lo = (x - hi.astype(F32)).astype(BF16)
    return hi, lo


def _sigmoid(x):
    return 1.0 / (1.0 + jnp.exp(-x))


def _params(sem, vmem=VMEM_LIMIT):
    return pltpu.CompilerParams(dimension_semantics=sem, vmem_limit_bytes=vmem)


def _mod_row(row0, tm):
    def f(i):
        g0 = i * tm + row0
        return jnp.where(g0 < RC, CTX_MOD_ROW, (g0 - RC) // SEQ)
    return f


def _rope_tables():
    pos = np.arange(SEQ, dtype=np.float64)
    row = np.floor(pos / GRID_W)
    col = pos % GRID_W

    def cs(p, nf):
        inv = ROPE_BASE ** (-np.arange(nf, dtype=np.float64) / nf)
        ang = p[:, None] * inv[None, :]
        return np.cos(ang), np.sin(ang)

    rc, rs = cs(row, HEAD_DIM // 4)
    cc, cs_ = cs(col, HEAD_DIM // 4)
    cos_a = np.concatenate([rc, rc, cc, cc], axis=1)
    sin_a = np.concatenate([-rs, rs, -cs_, cs_], axis=1)
    tc, ts = cs(pos, RET_DK // 2)
    cos_r = np.concatenate([tc, tc], axis=1)
    sin_r = np.concatenate([-ts, ts], axis=1)

    def full(t, ident):
        t2 = np.concatenate([t, t], axis=1)
        return np.concatenate([np.full_like(t2, ident), t2], axis=0).astype(np.float32)

    return full(cos_a, 1.0), full(sin_a, 0.0), full(cos_r, 1.0), full(sin_r, 0.0)


def _dft_tables():
    def cs(n):
        k = np.arange(n, dtype=np.int64)
        m = (k[:, None] * k[None, :]) % n
        ang = 2.0 * np.pi * m.astype(np.float64) / n
        return np.cos(ang), np.sin(ang)

    c128, s128 = cs(FOURIER_DIM)
    eye = np.eye(FOURIER_GROUPS)
    bdc = np.kron(eye, c128).astype(np.float32)
    bds = np.kron(eye, s128).astype(np.float32)
    cn, sn = cs(SEQ)
    w2 = np.concatenate([cn, -sn], axis=1).astype(np.float32)
    cl, sl = cs(CTX_LEN)
    w2c = np.concatenate([cl, -sl], axis=1).astype(np.float32)
    return bdc, bds, w2, w2c


def _retention_tables(dec_f, dec_b):
    lg_f = jax.nn.log_sigmoid(dec_f.astype(F32))
    lg_b = jax.nn.log_sigmoid(dec_b.astype(F32))
    i = jnp.arange(RET_CHUNK)
    diff = (i[:, None] - i[None, :]).astype(F32)
    fwd = jnp.exp(jnp.maximum(diff, 0.0)[None] * lg_f[:, None, None])
    bwd = jnp.exp(jnp.maximum(-diff, 0.0)[None] * lg_b[:, None, None])
    dcomb = jnp.where((diff >= 0)[None], fwd, bwd).reshape(2, 4 * RET_CHUNK, RET_CHUNK)
    fi = i.astype(F32)
    lanes = lambda t: jnp.repeat(t, RET_DK, axis=1)
    xi_f = lanes(jnp.exp((fi + 1.0)[:, None] * lg_f[None, :]))
    zt_f = lanes(jnp.exp((RET_CHUNK - 1 - fi)[:, None] * lg_f[None, :]))
    xi_b = lanes(jnp.exp((RET_CHUNK - fi)[:, None] * lg_b[None, :]))
    zt_b = lanes(jnp.exp(fi[:, None] * lg_b[None, :]))
    g_f = jnp.repeat(jnp.exp(RET_CHUNK * lg_f), RET_DK).reshape(2, 256, 1)
    g_b = jnp.repeat(jnp.exp(RET_CHUNK * lg_b), RET_DK).reshape(2, 256, 1)
    g_f = jnp.broadcast_to(g_f, (2, 256, 256))
    g_b = jnp.broadcast_to(g_b, (2, 256, 256))
    return dcomb, xi_f, zt_f, xi_b, zt_b, g_f, g_b


def _ada_body(c_ref, w_ref, b_ref, o_ref):
    c = c_ref[...]
    s = c * _sigmoid(c)
    sh, sl = _split(s)
    wh, wl = _split(w_ref[...])
    o_ref[...] = _dot(sh, wh) + _dot(sl, wh) + _dot(sh, wl) + b_ref[...]


def _ada(cc, w_ada, b_ada):
    nt = 6 * D_MODEL // TN_ADA
    return pl.pallas_call(
        _ada_body,
        grid=(DEPTH, nt),
        in_specs=[
            pl.BlockSpec((MOD_ROWS, D_MODEL), lambda l, j: (0, 0)),
            pl.BlockSpec((None, D_MODEL, TN_ADA), lambda l, j: (l, 0, j)),
            pl.BlockSpec((None, 1, TN_ADA), lambda l, j: (l, 0, j)),
        ],
        out_specs=pl.BlockSpec((None, MOD_ROWS, TN_ADA), lambda l, j: (l, 0, j)),
        out_shape=jax.ShapeDtypeStruct((DEPTH, MOD_ROWS, 6 * D_MODEL), F32),
        compiler_params=_params(("arbitrary", "arbitrary")),
        name="ada_mod",
    )(cc, w_ada, b_ada.reshape(DEPTH, 1, 6 * D_MODEL))


def _rope(xc, cos, sin, half):
    fwd = pltpu.roll(xc, 128 - half, axis=1)
    bwd = pltpu.roll(xc, half, axis=1)
    lane = lax.broadcasted_iota(jnp.int32, xc.shape, 1)
    first = (lane & (2 * half - 1)) < half
    return xc * cos + jnp.where(first, fwd, bwd) * sin


def _proj_body(*refs, split):
    if split:
        x = _pick(pl.program_id(0) < RC // TM_PROJ, refs[0], refs[1])
        refs = refs[2:]
    else:
        x = refs[0][...]
        refs = refs[1:]
    (mod_ref, gn_ref, w_ref, ca_ref, sa_ref, cr_ref, sr_ref,
     qa_ref, ka_ref, va_ref, qr_ref, kr_ref, vr_ref, gr_ref, fu_ref, gm_ref) = refs
    ms = jnp.mean(x * x, axis=-1, keepdims=True)
    y = x * lax.rsqrt(ms + NORM_EPS) * gn_ref[...]
    h = y * (1.0 + mod_ref[1:2, :]) + mod_ref[0:1, :]
    hb = h.astype(BF16)

    def proj(c0, width):
        return _dot(hb, w_ref[:, c0:c0 + width])

    ca, sa, cr, sr = ca_ref[...], sa_ref[...], cr_ref[...], sr_ref[...]

    qa = proj(0, W) * (HEAD_DIM ** -0.5 * LOG2E)
    for c in range(W // 128):
        qa_ref[:, c * 128:(c + 1) * 128] = _rope(qa[:, c * 128:(c + 1) * 128], ca, sa, 16).astype(BF16)
    kv = proj(W, 256)
    ka = _rope(kv[:, 0:128], ca, sa, 16).astype(BF16)
    ka_ref[0] = ka[:, 0:64]
    ka_ref[1] = ka[:, 64:128]
    va = kv[:, 128:256].astype(BF16)
    ones_col = jnp.where(lax.broadcasted_iota(jnp.int32, (va.shape[0], 64), 1) == 0, 1.0, 0.0).astype(BF16)
    va_ref[0] = jnp.concatenate([va[:, 0:64], ones_col], axis=1)
    va_ref[1] = jnp.concatenate([va[:, 64:128], ones_col], axis=1)
    qr = proj(768, W)
    kr = proj(1280, W) * (RET_DK ** -0.5)
    for c in range(W // 128):
        sl = slice(c * 128, (c + 1) * 128)
        qr_ref[:, sl] = _rope(qr[:, sl], cr, sr, 32).astype(BF16)
        kr_ref[:, sl] = _rope(kr[:, sl], cr, sr, 32).astype(BF16)
    vr_ref[...] = proj(1792, W).astype(BF16)
    g = proj(2304, W)
    gr_ref[...] = (g * _sigmoid(g)).astype(BF16)
    fu_ref[...] = proj(2816, W).astype(BF16)
    for c in range(3):
        gm_ref[:, c * D_MODEL:(c + 1) * D_MODEL] = _sigmoid(proj(3328 + c * D_MODEL, D_MODEL)).astype(BF16)


def _proj(x, mod_l, gnorm, w_in_bf, tabs, layer):
    tm = TM_PROJ
    nt = R // tm
    nc = RC // tm
    split = isinstance(x, tuple)
    if split:
        xs = list(x)
        x_specs = [pl.BlockSpec((tm, D_MODEL), lambda i: (jnp.minimum(i, nc - 1), 0)),
                   pl.BlockSpec((tm, D_MODEL), lambda i: (jnp.maximum(i - nc, 0), 0))]
    else:
        xs = [x]
        x_specs = [pl.BlockSpec((tm, D_MODEL), lambda i: (i, 0))]

    def tab_map(i):
        return (jnp.where(i < nc, i, nc + (i - nc) % (SEQ // tm)), 0)

    row = lambda i: (i, 0)
    wide = lambda n: pl.BlockSpec((tm, n), row)
    kv_spec = lambda n: pl.BlockSpec((2, tm, n), lambda i: (0, i, 0))
    sds = lambda n: jax.ShapeDtypeStruct((R, n), BF16)
    kv_sds = lambda n: jax.ShapeDtypeStruct((2, R, n), BF16)
    mrow = _mod_row(0, tm)
    return pl.pallas_call(
        functools.partial(_proj_body, split=split),
        grid=(nt,),
        in_specs=x_specs + [
            pl.BlockSpec((None, 6, D_MODEL), lambda i: (mrow(i), 0, 0)),
            pl.BlockSpec((1, D_MODEL), lambda i: (0, 0)),
            pl.BlockSpec((None, D_MODEL, IN_COLS), lambda i: (layer, 0, 0), pipeline_mode=pl.Buffered(1)),
        ] + [pl.BlockSpec((tm, 128), tab_map)] * 4,
        out_specs=[wide(W), kv_spec(64), kv_spec(128), wide(W), wide(W), wide(W), wide(W), wide(W),
                   wide(3 * D_MODEL)],
        out_shape=[sds(W), kv_sds(64), kv_sds(128), sds(W), sds(W), sds(W), sds(W), sds(W), sds(3 * D_MODEL)],
        compiler_params=_params(("arbitrary",)),
        name="in_proj",
    )(*xs, mod_l, gnorm, w_in_bf, *tabs)


def _attn_body(sink_ref, q_ref, kc_ref, kp_ref, kk_ref, kn_ref, vc_ref, vp_ref, vk_ref, vn_ref, o_ref, *, q_lo):
    qi = pl.program_id(1) + q_lo
    m = qi - 2
    nr = ATTN_GROUP * ATTN_BLOCK
    groups = [slice(g * ATTN_BLOCK, (g + 1) * ATTN_BLOCK) for g in range(ATTN_GROUP)]

    def attend(pieces):
        outs = []
        for h in range(ATTN_KV_HEADS):
            q = q_ref[:, h * 256:(h + 1) * 256]
            q4 = jnp.concatenate([q[:, g * 64:(g + 1) * 64] for g in range(ATTN_GROUP)], axis=0)
            sinks = [sink_ref[h * ATTN_GROUP + g] * LOG2E for g in range(ATTN_GROUP)]
            ss = []
            for k_ref, _, ok in pieces:
                s = _dot_nt(q4, k_ref[h])
                ss.append(s if ok is None else jnp.where(ok, s, NEG))
            mxs, ps = [], [[] for _ in pieces]
            for g, r in enumerate(groups):
                mx = functools.reduce(jnp.maximum, [jnp.max(s[r], axis=-1, keepdims=True) for s in ss])
                mx = jnp.maximum(mx, sinks[g])
                mxs.append(mx)
                for i, s in enumerate(ss):
                    ps[i].append(jnp.exp2(s[r] - mx).astype(BF16))
            oa = sum(_dot(jnp.concatenate(p, axis=0), v_ref[h]) for p, (_, v_ref, _) in zip(ps, pieces))
            for g, r in enumerate(groups):
                den = oa[r, HEAD_DIM:HEAD_DIM + 1] + jnp.exp2(sinks[g] - mxs[g])
                outs.append(oa[r, 0:HEAD_DIM] / den)
        o_ref[...] = jnp.concatenate(outs, axis=1).astype(BF16)

    @pl.when(qi >= 2)
    def _():
        ri = lax.broadcasted_iota(jnp.int32, (nr, ATTN_BLOCK), 0) & (ATTN_BLOCK - 1)
        ci = lax.broadcasted_iota(jnp.int32, (nr, ATTN_BLOCK), 1)
        far = 4 * ATTN_BLOCK
        ok_p = ci >= ri + jnp.where(m >= 1, 0, far)
        ok_n = ci + jnp.where(m <= SEQ // ATTN_BLOCK - 2, 0, far) <= ri
        attend([(kc_ref, vc_ref, None), (kp_ref, vp_ref, ok_p), (kk_ref, vk_ref, None), (kn_ref, vn_ref, ok_n)])

    if q_lo == 0:
        @pl.when(qi < 2)
        def _():
            attend([(kc_ref, vc_ref, None)])


def _attention(sink, qa, ka, va, need_ctx):
    q_lo = 0 if need_ctx else 2
    nq = SEQ // ATTN_BLOCK + (2 if need_ctx else 0)
    nb = SEQ // ATTN_BLOCK
    lat0 = RC // ATTN_BLOCK
    out0 = 0 if need_ctx else lat0

    def qrow(b, qi):
        return jnp.where(qi < 2, b * 2 + qi, lat0 + b * nb + qi - 2)

    def loc(delta):
        def f(b, i):
            m = jnp.clip(i + q_lo - 2 + delta, 0, nb - 1)
            return (0, lat0 + b * nb + m, 0)
        return f

    def kv_specs(width):
        ctx_spec = pl.BlockSpec((ATTN_KV_HEADS, CTX_LEN, width), lambda b, i: (0, b, 0))
        loc_spec = lambda d: pl.BlockSpec((ATTN_KV_HEADS, ATTN_BLOCK, width), loc(d))
        return [ctx_spec, loc_spec(-1), loc_spec(0), loc_spec(1)]

    return pl.pallas_call(
        functools.partial(_attn_body, q_lo=q_lo),
        grid=(BATCH, nq),
        in_specs=[pl.BlockSpec(memory_space=pltpu.SMEM),
                  pl.BlockSpec((ATTN_BLOCK, W), lambda b, i: (qrow(b, i + q_lo), 0))]
                 + kv_specs(HEAD_DIM) + kv_specs(128),
        out_specs=pl.BlockSpec((ATTN_BLOCK, W), lambda b, i: (qrow(b, i + q_lo) - out0, 0)),
        out_shape=jax.ShapeDtypeStruct((R - out0 * ATTN_BLOCK, W), BF16),
        compiler_params=_params(("arbitrary", "arbitrary")),
        name="window_attn",
    )(sink, qa, ka, ka, ka, ka, va, va, va, va)


def _fourier_body(*refs, has_ctx):
    if has_ctx:
        uc_ref, ul_ref, bdc_ref, bds_ref, w2_ref, w2c_ref, oc_ref, ol_ref, as_ref = refs
    else:
        ul_ref, bdc_ref, bds_ref, w2_ref, ol_ref, as_ref = refs
    j = pl.program_id(1)
    first = 1 if has_ctx else 0

    if has_ctx:
        @pl.when(j == 0)
        def _():
            u = uc_ref[...]
            a = _dot(u, bdc_ref[...]).astype(BF16)
            s = _dot(u, bds_ref[...]).astype(BF16)
            z = _dot(w2c_ref[...], jnp.concatenate([a, s], axis=0))
            oc_ref[...] = (z * ((CTX_LEN * FOURIER_DIM) ** -0.5)).astype(BF16)

    @pl.when(j == first)
    def _():
        u = ul_ref[...]
        as_ref[0:SEQ, :] = _dot(u, bdc_ref[...]).astype(BF16)
        as_ref[SEQ:2 * SEQ, :] = _dot(u, bds_ref[...]).astype(BF16)

    @pl.when(j >= first)
    def _():
        ol_ref[...] = (_dot(w2_ref[...], as_ref[...]) * ((SEQ * FOURIER_DIM) ** -0.5)).astype(BF16)


def _fourier(fu, dft, need_ctx):
    bdc, bds, w2, w2c = dft
    tr = TR_FOURIER
    nj = SEQ // tr
    first = 1 if need_ctx else 0
    full = lambda a: pl.BlockSpec(a.shape, lambda *_: (0,) * a.ndim)
    lat_tile = lambda j: jnp.maximum(j - first, 0)
    ul_spec = pl.BlockSpec((SEQ, W), lambda b, j: (1 + b, 0))
    w2_spec = pl.BlockSpec((tr, 2 * SEQ), lambda b, j: (lat_tile(j), 0))
    ol_spec = pl.BlockSpec((tr, W), lambda b, j: (b * nj + lat_tile(j), 0))
    ol_shape = jax.ShapeDtypeStruct((RL, W), BF16)
    if need_ctx:
        in_specs = [pl.BlockSpec((CTX_LEN, W), lambda b, j: (b, 0)), ul_spec, full(bdc), full(bds), w2_spec,
                    full(w2c)]
        out_specs = [pl.BlockSpec((CTX_LEN, W), lambda b, j: (b, 0)), ol_spec]
        out_shape = [jax.ShapeDtypeStruct((RC, W), BF16), ol_shape]
        args = (fu, fu, bdc, bds, w2, w2c)
    else:
        in_specs = [ul_spec, full(bdc), full(bds), w2_spec]
        out_specs = [ol_spec]
        out_shape = [ol_shape]
        args = (fu, bdc, bds, w2)
    outs = pl.pallas_call(
        functools.partial(_fourier_body, has_ctx=need_ctx),
        grid=(BATCH, nj + first),
        in_specs=in_specs,
        out_specs=out_specs,
        out_shape=out_shape,
        scratch_shapes=[pltpu.VMEM((2 * SEQ, W), BF16)],
        compiler_params=_params(("arbitrary", "arbitrary")),
        name="fourier_mix",
    )(*args)
    return (outs[0], outs[1]) if need_ctx else (None, outs[0])


def _retention_body(qc_ref, kc_ref, vc_ref, ql_ref, kl_ref, vl_ref, g_ref,
                    dcomb_ref, xif_ref, ztf_ref, xib_ref, ztb_ref, gf_ref, gb_ref, mbd_ref, avg_ref,
                    o_ref, os_ref, st_ref):
    j = pl.program_id(1)
    C = RET_CHUNK
    nl = SEQ // C

    def group_parts(q, k, v, gi):
        sl = slice(gi * 256, (gi + 1) * 256)
        return q[:, sl], k[:, sl], v[:, sl], sl

    own = ((lax.broadcasted_iota(jnp.int32, (4 * C, 256), 0) >> 7)
           == (lax.broadcasted_iota(jnp.int32, (4 * C, 256), 1) >> 6))

    def chunk_fwd(q, k, v, r0):
        for gi in range(2):
            q4, k4, v4, sl = group_parts(q, k, v, gi)
            s_prev = st_ref[gi]
            q4f = q4.astype(F32)
            o4 = _dot((q4f * xif_ref[:, sl]).astype(BF16), s_prev.astype(BF16))
            qstack = jnp.where(own, jnp.concatenate([q4f] * 4, axis=0), 0.0).astype(BF16)
            p = (_dot_nt(qstack, k4) * dcomb_ref[gi]).astype(BF16)
            ov = jnp.where(own, _dot(p, v4), 0.0)
            intra = ov[0:C] + ov[C:2 * C] + ov[2 * C:3 * C] + ov[3 * C:4 * C]
            os_ref[pl.ds(r0, C), sl] = o4 + intra
            u = _dot_tn(k4, (v4.astype(F32) * ztf_ref[:, sl]).astype(BF16))
            st_ref[gi] = gf_ref[gi] * s_prev + mbd_ref[...] * u

    def chunk_bwd(q, k, v, r0):
        for gi in range(2):
            q4, k4, v4, sl = group_parts(q, k, v, gi)
            s_prev = st_ref[gi]
            cross = _dot((q4.astype(F32) * xib_ref[:, sl]).astype(BF16), s_prev.astype(BF16))
            os_ref[pl.ds(r0, C), sl] = os_ref[pl.ds(r0, C), sl] + cross
            u = _dot_tn(k4, (v4.astype(F32) * ztb_ref[:, sl]).astype(BF16))
            st_ref[gi] = gb_ref[gi] * s_prev + mbd_ref[...] * u

    @pl.when(j == 0)
    def _():
        st_ref[...] = jnp.zeros_like(st_ref)
        for c in range(CTX_LEN // C):
            rs = slice(c * C, (c + 1) * C)
            chunk_fwd(qc_ref[rs, :], kc_ref[rs, :], vc_ref[rs, :], c * C)

        def fbody(c, carry):
            r0 = pl.multiple_of(c * C, C)
            rs = pl.ds(r0, C)
            chunk_fwd(ql_ref[rs, :], kl_ref[rs, :], vl_ref[rs, :], CTX_LEN + r0)
            return carry

        lax.fori_loop(0, nl, fbody, 0)

        st_ref[...] = jnp.zeros_like(st_ref)
        for c in reversed(range(CTX_LEN // C)):
            rs = slice(c * C, (c + 1) * C)
            chunk_bwd(qc_ref[rs, :], kc_ref[rs, :], vc_ref[rs, :], c * C)

        def bbody(t, carry):
            r0 = pl.multiple_of((nl - 1 - t) * C, C)
            rs = pl.ds(r0, C)
            chunk_bwd(ql_ref[rs, :], kl_ref[rs, :], vl_ref[rs, :], CTX_LEN + r0)
            return carry

        lax.fori_loop(0, nl, bbody, 0)

    o = os_ref[pl.ds(pl.multiple_of(j * 256, 256), 256), :]
    avg = avg_ref[...]
    oh, ol = _split(o)
    mu = _dot(oh, avg) + _dot(ol, avg)
    d = o - mu
    vh, vl = _split(d * d)
    var = _dot(vh, avg) + _dot(vl, avg)
    o_ref[...] = (g_ref[...].astype(F32) * d * lax.rsqrt(var + GN_EPS)).astype(BF16)


def _retention(qr, kr, vr, gr, rtabs):
    nj = 1 + SEQ // 256

    def out_map(b, j):
        return (jnp.where(j == 0, b, RC // 256 + b * (SEQ // 256) + j - 1), 0)

    ctx = pl.BlockSpec((CTX_LEN, W), lambda b, j: (b, 0))
    lat = pl.BlockSpec((SEQ, W), lambda b, j: (1 + b, 0))
    full = lambda a: pl.BlockSpec(a.shape, lambda *_: (0,) * a.ndim)
    avg = jnp.asarray(np.kron(np.eye(RET_HEADS), np.full((RET_DK, RET_DK), 1.0 / RET_DK)).astype(np.float32)).astype(BF16)
    mbd = jnp.asarray(np.kron(np.eye(4), np.ones((RET_DK, RET_DK))).astype(np.float32))
    tabs = list(rtabs) + [mbd, avg]
    return pl.pallas_call(
        _retention_body,
        grid=(BATCH, nj),
        in_specs=[ctx, ctx, ctx, lat, lat, lat, pl.BlockSpec((256, W), out_map)] + [full(t) for t in tabs],
        out_specs=pl.BlockSpec((256, W), out_map),
        out_shape=jax.ShapeDtypeStruct((R, W), BF16),
        scratch_shapes=[pltpu.VMEM((CTX_LEN + SEQ, W), F32), pltpu.VMEM((2, 256, 256), F32)],
        compiler_params=_params(("arbitrary", "arbitrary")),
        name="retention",
    )(qr, kr, vr, qr, kr, vr, gr, *tabs)


def _pack_pair(a, b):
    ua = lax.bitcast_convert_type(a, jnp.uint32) >> 16
    ub = lax.bitcast_convert_type(b, jnp.uint32) & jnp.uint32(0xFFFF0000)
    return ua | ub


def _unpack_pair(w):
    a = lax.bitcast_convert_type(w << 16, F32)
    b = lax.bitcast_convert_type(w & jnp.uint32(0xFFFF0000), F32)
    return a, b


def _slot_onehot(s0, s1, n):
    srow = lax.broadcasted_iota(jnp.int32, (n, s0.shape[1]), 0)
    p0 = jnp.where(srow == s0, 1.0, 0.0).astype(BF16)
    p1 = jnp.where(srow == s1, 1.0, 0.0).astype(BF16)
    return p0, p1


def _pick(first, a_ref, b_ref):
    a = a_ref[...]
    flag = jnp.zeros(a.shape, jnp.int32) + first.astype(jnp.int32)
    return jnp.where(flag > 0, a, b_ref[...])


def _merge_body(*refs, layer0):
    if layer0:
        oa_ref, ofc_ref, ofl_ref, rt_ref, gm_ref, xc_ref, xl_ref = refs[:7]
        rest = refs[7:]
        is_ctx = pl.program_id(0) < RC // TM_MERGE
        of_in = _pick(is_ctx, ofc_ref, ofl_ref)
        x_in = _pick(is_ctx, xc_ref, xl_ref)
    else:
        oa_ref, ofl_ref, rt_ref, gm_ref, x_ref = refs[:5]
        rest = refs[5:]
        of_in = ofl_ref[...]
        x_in = x_ref[...]
    (mod_ref, gn_ref, wba_ref, wbf_ref, wbr_ref, wout_ref, wrh_ref, wrl_ref, br_ref, tri_ref, ltri_ref,
     xo_ref, xs_ref, ro_ref, nch_ref) = rest
    gm = gm_ref[...].astype(F32)
    z = (gm[:, 0:D_MODEL] * _dot(oa_ref[...], wba_ref[...])
         + gm[:, D_MODEL:2 * D_MODEL] * _dot(of_in, wbf_ref[...])
         + gm[:, 2 * D_MODEL:3 * D_MODEL] * _dot(rt_ref[...], wbr_ref[...]))
    y = _dot(z.astype(BF16), wout_ref[...])
    x = x_in + mod_ref[2:3, :] * y
    xo_ref[...] = x
    ms = jnp.mean(x * x, axis=-1, keepdims=True)
    hn = x * lax.rsqrt(ms + NORM_EPS) * gn_ref[...]
    h2 = hn * (1.0 + mod_ref[4:5, :]) + mod_ref[3:4, :]
    hh, hl = _split(h2)
    wh, wl = wrh_ref[...], wrl_ref[...]
    lg = _dot_nt(wh, hh) + _dot_nt(wh, hl) + _dot_nt(wl, hh) + br_ref[...]
    tm = lg.shape[1]
    row8 = lax.broadcasted_iota(jnp.int32, (8, tm), 0)
    lgg = lg[0:8, :]
    mg = jnp.max(lgg, axis=0, keepdims=True)
    grp = jnp.min(jnp.where(lgg == mg, row8, 8), axis=0, keepdims=True)
    pg = 1.0 / jnp.sum(jnp.exp(lgg - mg), axis=0, keepdims=True)
    lin = jnp.zeros((8, tm), F32)
    for g in range(N_GROUPS):
        lin = jnp.where(grp == g, lg[8 + 8 * g:16 + 8 * g, :], lin)
    v1 = jnp.max(lin, axis=0, keepdims=True)
    i1 = jnp.min(jnp.where(lin == v1, row8, 8), axis=0, keepdims=True)
    rest = jnp.where(row8 == i1, -jnp.inf, lin)
    v2 = jnp.max(rest, axis=0, keepdims=True)
    i2 = jnp.min(jnp.where(rest == v2, row8, 8), axis=0, keepdims=True)
    e2 = jnp.exp(v2 - v1)
    w1 = pg / (1.0 + e2)
    w2 = pg * e2 / (1.0 + e2)
    e_1 = grp * EXPERTS_PER_GROUP + i1
    e_2 = grp * EXPERTS_PER_GROUP + i2

    row32 = lax.broadcasted_iota(jnp.int32, (N_EXPERTS, tm), 0)
    oh0 = jnp.where(row32 == e_1, 1.0, 0.0)
    oh1 = jnp.where(row32 == e_2, 1.0, 0.0)
    tri = tri_ref[...]
    cum0 = _dot(oh0.astype(BF16), tri)
    cum1 = _dot(oh1.astype(BF16), tri)
    tot0 = jnp.sum(oh0, axis=1, keepdims=True)
    tot1 = jnp.sum(oh1, axis=1, keepdims=True)
    nch = ((tot0 + tot1).astype(jnp.int32) + (CHUNK - 1)) >> CHUNK_SHIFT
    nch_b = jnp.broadcast_to(nch.astype(F32), (N_EXPERTS, 128))
    nch_ref[...] = nch_b.astype(jnp.int32)
    base = CHUNK * _dot(ltri_ref[...], nch_b.astype(BF16))[:, 0:1]
    s0 = jnp.sum(oh0 * (base + cum0), axis=0, keepdims=True).astype(jnp.int32)
    s1 = jnp.sum(oh1 * (base + tot0 + cum1), axis=0, keepdims=True).astype(jnp.int32)
    p0, p1 = _slot_onehot(s0, s1, SLOTS)
    xs = _dot(p0 + p1, hh)
    xs_ref[:, 0:D_MODEL // 2] = _pack_pair(xs[:, 0:D_MODEL // 2], xs[:, D_MODEL // 2:D_MODEL])

    def wrows(w):
        hi, lo = _split(w)
        return jnp.where(row8 == 0, hi.astype(F32), jnp.where(row8 == 1, lo.astype(F32), 0.0)).astype(BF16)

    wc = _dot_nt(p0, wrows(w1)) + _dot_nt(p1, wrows(w2))
    wcol = jnp.broadcast_to(wc[:, 0:1] + wc[:, 1:2], (SLOTS, 128))
    xs_ref[:, D_MODEL // 2:XS_COLS] = lax.bitcast_convert_type(wcol, jnp.uint32)
    s0f, s1f = s0.astype(F32), s1.astype(F32)
    ro_ref[...] = jnp.where(row8 == 0, s0f, jnp.where(row8 == 1, s1f, 0.0))


def _merge(oa, of_c, of_l, ret, gm, xs_in, mod_l, gnorm, wba, wbf, wbr, wout, wrh, wrl, brb, layer0):
    tm = TM_MERGE
    row0 = 0 if layer0 else RC
    rm = R - row0
    nt = rm // tm
    off = row0 // tm
    nc = RC // tm
    src = lambda n: pl.BlockSpec((tm, n), lambda i: (i + off, 0))
    dst = lambda n: pl.BlockSpec((tm, n), lambda i: (i, 0))
    ctx_rows = lambda n: pl.BlockSpec((tm, n), lambda i: (jnp.minimum(i, nc - 1), 0))
    lat_rows = lambda n: pl.BlockSpec((tm, n), lambda i: (jnp.maximum(i - nc, 0), 0))
    full = lambda a: pl.BlockSpec(a.shape, lambda *_: (0,) * a.ndim, pipeline_mode=pl.Buffered(1))
    mrow = _mod_row(row0, tm)
    tri = jnp.asarray(np.triu(np.ones((tm, tm), np.float32), 1)).astype(BF16)
    ltri = jnp.asarray(np.tril(np.ones((N_EXPERTS, N_EXPERTS), np.float32), -1)).astype(BF16)
    if layer0:
        acts = [oa, of_c, of_l, ret, gm, xs_in[0], xs_in[1]]
        act_specs = [src(W), ctx_rows(W), lat_rows(W), src(W), src(3 * D_MODEL), ctx_rows(D_MODEL),
                     lat_rows(D_MODEL)]
    else:
        acts = [oa, of_l, ret, gm, xs_in]
        act_specs = [dst(W), dst(W), src(W), src(3 * D_MODEL), src(D_MODEL)]
    return pl.pallas_call(
        functools.partial(_merge_body, layer0=layer0),
        grid=(nt,),
        in_specs=act_specs + [
                  pl.BlockSpec((None, 6, D_MODEL), lambda i: (mrow(i), 0, 0)),
                  full(gnorm), full(wba), full(wbf), full(wbr), full(wout), full(wrh), full(wrl), full(brb),
                  full(tri), full(ltri)],
        out_specs=[dst(D_MODEL), pl.BlockSpec((SLOTS, XS_COLS), lambda i: (i, 0)),
                   pl.BlockSpec((8, tm), lambda i: (0, i)),
                   pl.BlockSpec((None, N_EXPERTS, 128), lambda i: (i, 0, 0))],
        out_shape=[jax.ShapeDtypeStruct((rm, D_MODEL), F32),
                   jax.ShapeDtypeStruct((nt * SLOTS, XS_COLS), jnp.uint32),
                   jax.ShapeDtypeStruct((8, rm), F32),
                   jax.ShapeDtypeStruct((nt, N_EXPERTS, 128), jnp.int32)],
        compiler_params=_params(("arbitrary",)),
        name="merge_router",
    )(*acts, mod_l, gnorm, wba, wbf, wbr, wout, wrh, wrl, brb, tri, ltri)


def _moe_plan(nch, nb):
    nt = nch.shape[0]
    choff = jnp.cumsum(nch, axis=1) - nch
    used_ch = jnp.sum(nch, axis=1)
    cum_t = jnp.cumsum(nch, axis=0)
    tot = cum_t[-1]
    ptot = (tot + CPB - 1) // CPB * CPB
    pend = jnp.cumsum(ptot)
    pstart = pend - ptot
    n_used = pend[-1] // CPB
    blk = jnp.arange(nb, dtype=jnp.int32)
    lane = jnp.arange(CPB, dtype=jnp.int32)
    blk_e = jnp.minimum(jnp.sum((blk[:, None] * CPB >= pend[None, :]).astype(jnp.int32), axis=1), N_EXPERTS - 1)
    oe = (blk_e[:, None] == jnp.arange(N_EXPERTS, dtype=jnp.int32)[None, :]).astype(jnp.int32)
    sel = lambda tab: jnp.sum(oe[:, :, None] * tab.T[None, :, :], axis=1)
    pstart_b = jnp.sum(oe * pstart[None, :], axis=1)
    tot_b = jnp.sum(oe * tot[None, :], axis=1)
    cum_b, nch_b, choff_b = sel(cum_t), sel(nch), sel(choff)
    i = blk[:, None] * CPB + lane[None, :] - pstart_b[:, None]
    valid = (i < tot_b[:, None]) & (blk[:, None] < n_used)
    t = jnp.minimum(jnp.sum((i[:, :, None] >= cum_b[:, None, :]).astype(jnp.int32), axis=2), nt - 1)
    tiles = jnp.arange(nt, dtype=jnp.int32)[None, None, :]
    before = jnp.sum(jnp.where(tiles < t[:, :, None], nch_b[:, None, :], 0), axis=2)
    coff = jnp.sum(jnp.where(tiles == t[:, :, None], choff_b[:, None, :], 0), axis=2)
    row = t * SLOTS + CHUNK * (coff + i - before)
    src = jnp.where(valid, row, SLOTS - CHUNK)
    dummy = nt * SLOTS + CHUNK * ((blk[:, None] % 2) * CPB + lane[None, :])
    dst = jnp.where(valid, row, dummy)
    blk_start = jnp.concatenate([pstart, pend[-1:]]) // CPB
    return (blk_start.astype(jnp.int32), n_used.astype(jnp.int32).reshape(1), src.reshape(-1).astype(jnp.int32),
            dst.reshape(-1).astype(jnp.int32), used_ch.astype(jnp.int32))


def _ffn_body(bs_ref, nu_ref, src_ref, dst_ref, uc_ref, xs_ref, wg_ref, wu_ref, wd_ref, ys_ref,
              xbuf, ybuf, zbuf, wgb, wub, wdb, sem_in, sem_out, sem_z, *, nt):
    e = pl.program_id(0)
    nu = nu_ref[0]
    half = D_MODEL // 2

    def gather(blk, sl):
        for c in range(CPB):
            r = pl.multiple_of(src_ref[blk * CPB + c], CHUNK)
            pltpu.make_async_copy(xs_ref.at[pl.ds(r, CHUNK)], xbuf.at[sl, pl.ds(c * CHUNK, CHUNK)],
                                  sem_in.at[sl]).start()

    def scatter(blk, sl):
        for c in range(CPB):
            r = pl.multiple_of(dst_ref[blk * CPB + c], CHUNK)
            pltpu.make_async_copy(ybuf.at[sl, pl.ds(c * CHUNK, CHUNK)], ys_ref.at[pl.ds(r, CHUNK)],
                                  sem_out.at[sl]).start()

    def wait_gather(sl):
        pltpu.make_async_copy(xs_ref.at[pl.ds(0, MOE_BM)], xbuf.at[sl], sem_in.at[sl]).wait()

    def wait_scatter(sl):
        pltpu.make_async_copy(ybuf.at[sl], ys_ref.at[pl.ds(0, MOE_BM)], sem_out.at[sl]).wait()

    def zero_copy(r):
        return pltpu.make_async_copy(zbuf, ys_ref.at[pl.ds(pl.multiple_of(r, CHUNK), CHUNK)], sem_z)

    @pl.when(e == 0)
    def _():
        zbuf[...] = jnp.zeros_like(zbuf)

        def tails(fn):
            def per_tile(t, carry):
                def per_chunk(c, carry2):
                    fn(t * SLOTS + c * CHUNK)
                    return carry2
                lax.fori_loop(uc_ref[t], SLOTS // CHUNK, per_chunk, 0)
                return carry
            lax.fori_loop(0, nt, per_tile, 0)
            for c in range(2 * CPB):
                fn(nt * SLOTS + c * CHUNK)

        tails(lambda r: zero_copy(r).start())
        tails(lambda r: zero_copy(r).wait())
        gather(0, 0)

    b0, b1 = bs_ref[e], bs_ref[e + 1]

    @pl.when(b1 > b0)
    def _():
        wgb[...] = wg_ref[...].astype(BF16)
        wub[...] = wu_ref[...].astype(BF16)
        wdb[...] = wd_ref[...].astype(BF16)

        def block(b, carry):
            slot = b % 2

            @pl.when(b + 1 < nu)
            def _():
                gather(b + 1, 1 - slot)

            wait_gather(slot)

            @pl.when(b >= 2)
            def _():
                wait_scatter(slot)

            xw = xbuf[slot]
            xa, xb = _unpack_pair(xw[:, 0:half])
            x = jnp.concatenate([xa, xb], axis=1).astype(BF16)
            wt = lax.bitcast_convert_type(xw[:, half:XS_COLS], F32)
            g = _dot(x, wgb[...])
            u = _dot(x, wub[...])
            hmid = (g * _sigmoid(g) * u).astype(BF16)
            y = _dot(hmid, wdb[...]) * jnp.concatenate([wt] * (D_MODEL // 128), axis=1)
            yb = y.astype(BF16).astype(F32)
            ybuf[slot] = _pack_pair(yb[:, 0:half], yb[:, half:D_MODEL])
            scatter(b, slot)
            return carry

        lax.fori_loop(b0, b1, block, 0)

    @pl.when(e == N_EXPERTS - 1)
    def _():
        wait_scatter((nu - 1) % 2)

        @pl.when(nu >= 2)
        def _():
            wait_scatter(nu % 2)


def _ffn(plan, xs, w_g, w_u, w_d, layer, nt):
    wmap = lambda e, *_: (layer, e, 0, 0)
    half = D_MODEL // 2
    grid_spec = pltpu.PrefetchScalarGridSpec(
        num_scalar_prefetch=5,
        grid=(N_EXPERTS,),
        in_specs=[pl.BlockSpec(memory_space=pl.ANY),
                  pl.BlockSpec((None, None, D_MODEL, EXPERT_HIDDEN), wmap),
                  pl.BlockSpec((None, None, D_MODEL, EXPERT_HIDDEN), wmap),
                  pl.BlockSpec((None, None, EXPERT_HIDDEN, D_MODEL), wmap)],
        out_specs=pl.BlockSpec(memory_space=pl.ANY),
        scratch_shapes=[pltpu.VMEM((2, MOE_BM, XS_COLS), jnp.uint32), pltpu.VMEM((2, MOE_BM, half), jnp.uint32),
                        pltpu.VMEM((CHUNK, half), jnp.uint32),
                        pltpu.VMEM((D_MODEL, EXPERT_HIDDEN), BF16), pltpu.VMEM((D_MODEL, EXPERT_HIDDEN), BF16),
                        pltpu.VMEM((EXPERT_HIDDEN, D_MODEL), BF16),
                        pltpu.SemaphoreType.DMA((2,)), pltpu.SemaphoreType.DMA((2,)), pltpu.SemaphoreType.DMA(())],
    )
    return pl.pallas_call(
        functools.partial(_ffn_body, nt=nt),
        grid_spec=grid_spec,
        out_shape=jax.ShapeDtypeStruct((nt * SLOTS + 2 * CPB * CHUNK, half), jnp.uint32),
        compiler_params=_params(("arbitrary",)),
        name="moe_experts",
    )(*plan, xs, w_g, w_u, w_d)


def _combine_body(ys_ref, ro_ref, x_ref, mod_ref, gn_ref, o_ref, *, final):
    s = ro_ref[...]
    p0, p1 = _slot_onehot(s[0:1, :].astype(jnp.int32), s[1:2, :].astype(jnp.int32), SLOTS)
    ya, yb = _unpack_pair(ys_ref[...])
    y = jnp.concatenate([ya, yb], axis=1).astype(BF16)
    f = _dot_tn(p0 + p1, y)
    x = x_ref[...] + mod_ref[5:6, :] * f
    if final:
        ms = jnp.mean(x * x, axis=-1, keepdims=True)
        x = x * lax.rsqrt(ms + NORM_EPS) * gn_ref[...]
    o_ref[...] = x


def _combine(ys, route, x, mod_l, gnorm, row0, final):
    tm = TM_MERGE
    rm = x.shape[0]
    mrow = _mod_row(row0, tm)
    return pl.pallas_call(
        functools.partial(_combine_body, final=final),
        grid=(rm // tm,),
        in_specs=[pl.BlockSpec((SLOTS, D_MODEL // 2), lambda i: (i, 0)),
                  pl.BlockSpec((8, tm), lambda i: (0, i)),
                  pl.BlockSpec((tm, D_MODEL), lambda i: (i, 0)),
                  pl.BlockSpec((None, 6, D_MODEL), lambda i: (mrow(i), 0, 0)),
                  pl.BlockSpec((1, D_MODEL), lambda i: (0, 0))],
        out_specs=pl.BlockSpec((tm, D_MODEL), lambda i: (i, 0)),
        out_shape=jax.ShapeDtypeStruct((rm, D_MODEL), F32),
        compiler_params=_params(("arbitrary",)),
        name="moe_combine",
    )(ys, route, x, mod_l, gnorm)


def _moe(xs, route, nch3, x, mod_l, gnorm, w_g, w_u, w_d, row0, final, layer):
    nt = nch3.shape[0]
    max_chunks = nt * ((2 * TM_MERGE + N_EXPERTS * (CHUNK - 1)) // CHUNK)
    nb = -(-max_chunks // CPB) + N_EXPERTS
    plan = _moe_plan(nch3[:, :, 0], nb)
    ys = _ffn(plan, xs, w_g, w_u, w_d, layer, nt)
    return _combine(ys, route, x, mod_l, gnorm, row0, final)


def kernel(x, c, ctx, c_ctx, norm_mix, norm_ffn, w_ada, b_ada, w_in, attn_sink, ret_decay_fwd, ret_decay_bwd,
           w_branch_attn, w_branch_fourier, w_branch_ret, w_out, w_router_group, b_router_group,
           w_router_expert, b_router_expert, w_exp_gate, w_exp_up, w_exp_down, norm_final):
    tabs = [jnp.asarray(t) for t in _rope_tables()]
    dft = [jnp.asarray(t).astype(BF16) for t in _dft_tables()]

    cc = jnp.zeros((MOD_ROWS, D_MODEL), F32).at[0:BATCH].set(c).at[CTX_MOD_ROW].set(c_ctx)
    mod = _ada(cc, w_ada, b_ada).reshape(DEPTH, MOD_ROWS, 6, D_MODEL)

    xf = (ctx.reshape(RC, D_MODEL), x.reshape(RL, D_MODEL))
    w_in_bf = w_in.astype(BF16)
    for l in range(DEPTH):
        need_ctx = l < DEPTH - 1
        row0 = 0 if need_ctx else RC
        mod_l = mod[l]
        qa, ka, va, qr, kr, vr, gr, fu, gm = _proj(xf, mod_l, norm_mix[l][None, :], w_in_bf, tabs, l)
        oa = _attention(attn_sink[l], qa, ka, va, need_ctx)
        of_c, of_l = _fourier(fu, dft, need_ctx)
        ret = _retention(qr, kr, vr, gr, _retention_tables(ret_decay_fwd[l], ret_decay_bwd[l]))
        wr = jnp.zeros((ROUTER_ROWS, D_MODEL), F32)
        wr = wr.at[0:N_GROUPS].set(w_router_group[l].T).at[8:8 + N_EXPERTS].set(w_router_expert[l].T)
        br = jnp.full((ROUTER_ROWS,), NEG, F32)
        br = br.at[0:N_GROUPS].set(b_router_group[l]).at[8:8 + N_EXPERTS].set(b_router_expert[l])
        wrh, wrl = _split(wr)
        brb = jnp.broadcast_to(br[:, None], (ROUTER_ROWS, TM_MERGE))
        x_mid, xs, route, nch3 = _merge(oa, of_c, of_l, ret, gm, xf, mod_l, norm_ffn[l][None, :],
                                        w_branch_attn[l].astype(BF16), w_branch_fourier[l].astype(BF16),
                                        w_branch_ret[l].astype(BF16), w_out[l].astype(BF16), wrh, wrl, brb,
                                        need_ctx)
        final = l == DEPTH - 1
        xf = _moe(xs, route, nch3, x_mid, mod_l, norm_final[None, :], w_exp_gate, w_exp_up, w_exp_down,
                  row0, final, l)
    return xf.reshape(BATCH, SEQ, D_MODEL)
```

```python
import functools

import numpy as np
import jax
import jax.numpy as jnp
from jax import lax
from jax.experimental import pallas as pl
from jax.experimental.pallas import tpu as pltpu

F32 = jnp.float32
BF16 = jnp.bfloat16

D_MODEL = 1024
BATCH = 8
SEQ = 2048
DEPTH = 2
CTX_LEN = 256
GRID_W = 64
HEAD_DIM = 64
ATTN_HEADS = 8
ATTN_KV_HEADS = 2
ATTN_GROUP = ATTN_HEADS // ATTN_KV_HEADS
ATTN_BLOCK = 128
RET_HEADS = 8
RET_DK = 64
RET_CHUNK = 128
FOURIER_GROUPS = 4
FOURIER_DIM = 128
N_GROUPS = 4
EXPERTS_PER_GROUP = 8
N_EXPERTS = N_GROUPS * EXPERTS_PER_GROUP
EXPERT_HIDDEN = 512
ROPE_BASE = 10000.0
NORM_EPS = 1e-6
GN_EPS = 1e-5

W = 512
IN_COLS = 6400
RC = BATCH * CTX_LEN
RL = BATCH * SEQ
R = RC + RL
MOD_ROWS = 16
CTX_MOD_ROW = 8

VMEM_LIMIT = 52 * 1024 * 1024

TM_PROJ = 512
TM_MERGE = 512
TN_ADA = 1536
TR_FOURIER = 512
MOE_BM = 512
CHUNK = 8
CHUNK_SHIFT = 3
CPB = MOE_BM // CHUNK
SLOTS = 1280
XS_COLS = D_MODEL // 2 + 128
NEG = -1e30
LOG2E = 1.4426950408889634
ROUTER_ROWS = 40


def _dot(a, b):
    return jnp.dot(a, b, preferred_element_type=F32)


def _dot_nt(a, b):
    return lax.dot_general(a, b, (((1,), (1,)), ((), ())), preferred_element_type=F32)


def _dot_tn(a, b):
    return lax.dot_general(a, b, (((0,), (0,)), ((), ())), preferred_element_type=F32)


def _split(x):
    hi = x.astype(BF16)
    lo = (x - hi.astype(F32)).astype(BF16)
    return hi, lo


def _sigmoid(x):
    return 1.0 / (1.0 + jnp.exp(-x))


def _params(sem, vmem=VMEM_LIMIT):
    return pltpu.CompilerParams(dimension_semantics=sem, vmem_limit_bytes=vmem)


def _mod_row(row0, tm):
    def f(i):
        g0 = i * tm + row0
        return jnp.where(g0 < RC, CTX_MOD_ROW, (g0 - RC) // SEQ)
    return f


def _rope_tables():
    pos = np.arange(SEQ, dtype=np.float64)
    row = np.floor(pos / GRID_W)
    col = pos % GRID_W

    def cs(p, nf):
        inv = ROPE_BASE ** (-np.arange(nf, dtype=np.float64) / nf)
        ang = p[:, None] * inv[None, :]
        return np.cos(ang), np.sin(ang)

    rc, rs = cs(row, HEAD_DIM // 4)
    cc, cs_ = cs(col, HEAD_DIM // 4)
    cos_a = np.concatenate([rc, rc, cc, cc], axis=1)
    sin_a = np.concatenate([-rs, rs, -cs_, cs_], axis=1)
    tc, ts = cs(pos, RET_DK // 2)
    cos_r = np.concatenate([tc, tc], axis=1)
    sin_r = np.concatenate([-ts, ts], axis=1)

    def full(t, ident):
        t2 = np.concatenate([t, t], axis=1)
        return np.concatenate([np.full_like(t2, ident), t2], axis=0).astype(np.float32)

    return full(cos_a, 1.0), full(sin_a, 0.0), full(cos_r, 1.0), full(sin_r, 0.0)


def _dft_tables():
    def cs(n):
        k = np.arange(n, dtype=np.int64)
        m = (k[:, None] * k[None, :]) % n
        ang = 2.0 * np.pi * m.astype(np.float64) / n
        return np.cos(ang), np.sin(ang)

    c128, s128 = cs(FOURIER_DIM)
    eye = np.eye(FOURIER_GROUPS)
    bdc = np.kron(eye, c128).astype(np.float32)
    bds = np.kron(eye, s128).astype(np.float32)
    cn, sn = cs(SEQ)
    w2 = np.concatenate([cn, -sn], axis=1).astype(np.float32)
    cl, sl = cs(CTX_LEN)
    w2c = np.concatenate([cl, -sl], axis=1).astype(np.float32)
    return bdc, bds, w2, w2c


def _retention_tables(dec_f, dec_b):
    lg_f = jax.nn.log_sigmoid(dec_f.astype(F32))
    lg_b = jax.nn.log_sigmoid(dec_b.astype(F32))
    i = jnp.arange(RET_CHUNK)
    diff = (i[:, None] - i[None, :]).astype(F32)
    fwd = jnp.exp(jnp.maximum(diff, 0.0)[None] * lg_f[:, None, None])
    bwd = jnp.exp(jnp.maximum(-diff, 0.0)[None] * lg_b[:, None, None])
    dcomb = jnp.where((diff >= 0)[None], fwd, bwd).reshape(2, 4 * RET_CHUNK, RET_CHUNK)
    fi = i.astype(F32)
    lanes = lambda t: jnp.repeat(t, RET_DK, axis=1)
    xi_f = lanes(jnp.exp((fi + 1.0)[:, None] * lg_f[None, :]))
    zt_f = lanes(jnp.exp((RET_CHUNK - 1 - fi)[:, None] * lg_f[None, :]))
    xi_b = lanes(jnp.exp((RET_CHUNK - fi)[:, None] * lg_b[None, :]))
    zt_b = lanes(jnp.exp(fi[:, None] * lg_b[None, :]))
    g_f = jnp.repeat(jnp.exp(RET_CHUNK * lg_f), RET_DK).reshape(2, 256, 1)
    g_b = jnp.repeat(jnp.exp(RET_CHUNK * lg_b), RET_DK).reshape(2, 256, 1)
    g_f = jnp.broadcast_to(g_f, (2, 256, 256))
    g_b = jnp.broadcast_to(g_b, (2, 256, 256))
    return dcomb, xi_f, zt_f, xi_b, zt_b, g_f, g_b


def _ada_body(c_ref, w_ref, b_ref, o_ref):
    c = c_ref[...]
    s = c * _sigmoid(c)
    sh, sl = _split(s)
    wh, wl = _split(w_ref[...])
    o_ref[...] = _dot(sh, wh) + _dot(sl, wh) + _dot(sh, wl) + b_ref[...]


def _ada(cc, w_ada, b_ada):
    nt = 6 * D_MODEL // TN_ADA
    return pl.pallas_call(
        _ada_body,
        grid=(DEPTH, nt),
        in_specs=[
            pl.BlockSpec((MOD_ROWS, D_MODEL), lambda l, j: (0, 0)),
            pl.BlockSpec((None, D_MODEL, TN_ADA), lambda l, j: (l, 0, j)),
            pl.BlockSpec((None, 1, TN_ADA), lambda l, j: (l, 0, j)),
        ],
        out_specs=pl.BlockSpec((None, MOD_ROWS, TN_ADA), lambda l, j: (l, 0, j)),
        out_shape=jax.ShapeDtypeStruct((DEPTH, MOD_ROWS, 6 * D_MODEL), F32),
        compiler_params=_params(("arbitrary", "arbitrary")),
        name="ada_mod",
    )(cc, w_ada, b_ada.reshape(DEPTH, 1, 6 * D_MODEL))


def _rope(xc, cos, sin, half):
    fwd = pltpu.roll(xc, 128 - half, axis=1)
    bwd = pltpu.roll(xc, half, axis=1)
    lane = lax.broadcasted_iota(jnp.int32, xc.shape, 1)
    first = (lane & (2 * half - 1)) < half
    return xc * cos + jnp.where(first, fwd, bwd) * sin


def _proj_body(*refs, split):
    if split:
        x = _pick(pl.program_id(0) < RC // TM_PROJ, refs[0], refs[1])
        refs = refs[2:]
    else:
        x = refs[0][...]
        refs = refs[1:]
    (mod_ref, gn_ref, w_ref, ca_ref, sa_ref, cr_ref, sr_ref,
     qa_ref, ka_ref, va_ref, qr_ref, kr_ref, vr_ref, gr_ref, fu_ref, gm_ref) = refs
    ms = jnp.mean(x * x, axis=-1, keepdims=True)
    y = x * lax.rsqrt(ms + NORM_EPS) * gn_ref[...]
    h = y * (1.0 + mod_ref[1:2, :]) + mod_ref[0:1, :]
    hb = h.astype(BF16)

    def proj(c0, width):
        return _dot(hb, w_ref[:, c0:c0 + width])

    ca, sa, cr, sr = ca_ref[...], sa_ref[...], cr_ref[...], sr_ref[...]

    qa = proj(0, W) * (HEAD_DIM ** -0.5 * LOG2E)
    for c in range(W // 128):
        qa_ref[:, c * 128:(c + 1) * 128] = _rope(qa[:, c * 128:(c + 1) * 128], ca, sa, 16).astype(BF16)
    kv = proj(W, 256)
    ka = _rope(kv[:, 0:128], ca, sa, 16).astype(BF16)
    ka_ref[0] = ka[:, 0:64]
    ka_ref[1] = ka[:, 64:128]
    va = kv[:, 128:256].astype(BF16)
    ones_col = jnp.where(lax.broadcasted_iota(jnp.int32, (va.shape[0], 64), 1) == 0, 1.0, 0.0).astype(BF16)
    va_ref[0] = jnp.concatenate([va[:, 0:64], ones_col], axis=1)
    va_ref[1] = jnp.concatenate([va[:, 64:128], ones_col], axis=1)
    qr = proj(768, W)
    kr = proj(1280, W) * (RET_DK ** -0.5)
    for c in range(W // 128):
        sl = slice(c * 128, (c + 1) * 128)
        qr_ref[:, sl] = _rope(qr[:, sl], cr, sr, 32).astype(BF16)
        kr_ref[:, sl] = _rope(kr[:, sl], cr, sr, 32).astype(BF16)
    vr_ref[...] = proj(1792, W).astype(BF16)
    g = proj(2304, W)
    gr_ref[...] = (g * _sigmoid(g)).astype(BF16)
    fu_ref[...] = proj(2816, W).astype(BF16)
    for c in range(3):
        gm_ref[:, c * D_MODEL:(c + 1) * D_MODEL] = _sigmoid(proj(3328 + c * D_MODEL, D_MODEL)).astype(BF16)


def _proj(x, mod_l, gnorm, w_in_bf, tabs, layer):
    tm = TM_PROJ
    nt = R // tm
    nc = RC // tm
    split = isinstance(x, tuple)
    if split:
        xs = list(x)
        x_specs = [pl.BlockSpec((tm, D_MODEL), lambda i: (jnp.minimum(i, nc - 1), 0)),
                   pl.BlockSpec((tm, D_MODEL), lambda i: (jnp.maximum(i - nc, 0), 0))]
    else:
        xs = [x]
        x_specs = [pl.BlockSpec((tm, D_MODEL), lambda i: (i, 0))]

    def tab_map(i):
        return (jnp.where(i < nc, i, nc + (i - nc) % (SEQ // tm)), 0)

    row = lambda i: (i, 0)
    wide = lambda n: pl.BlockSpec((tm, n), row)
    kv_spec = lambda n: pl.BlockSpec((2, tm, n), lambda i: (0, i, 0))
    sds = lambda n: jax.ShapeDtypeStruct((R, n), BF16)
    kv_sds = lambda n: jax.ShapeDtypeStruct((2, R, n), BF16)
    mrow = _mod_row(0, tm)
    return pl.pallas_call(
        functools.partial(_proj_body, split=split),
        grid=(nt,),
        in_specs=x_specs + [
            pl.BlockSpec((None, 6, D_MODEL), lambda i: (mrow(i), 0, 0)),
            pl.BlockSpec((1, D_MODEL), lambda i: (0, 0)),
            pl.BlockSpec((None, D_MODEL, IN_COLS), lambda i: (layer, 0, 0), pipeline_mode=pl.Buffered(1)),
        ] + [pl.BlockSpec((tm, 128), tab_map)] * 4,
        out_specs=[wide(W), kv_spec(64), kv_spec(128), wide(W), wide(W), wide(W), wide(W), wide(W),
                   wide(3 * D_MODEL)],
        out_shape=[sds(W), kv_sds(64), kv_sds(128), sds(W), sds(W), sds(W), sds(W), sds(W), sds(3 * D_MODEL)],
        compiler_params=_params(("arbitrary",)),
        name="in_proj",
    )(*xs, mod_l, gnorm, w_in_bf, *tabs)


def _attn_body(sink_ref, q_ref, kc_ref, kp_ref, kk_ref, kn_ref, vc_ref, vp_ref, vk_ref, vn_ref, o_ref, *, q_lo):
    qi = pl.program_id(1) + q_lo
    m = qi - 2
    nr = ATTN_GROUP * ATTN_BLOCK
    groups = [slice(g * ATTN_BLOCK, (g + 1) * ATTN_BLOCK) for g in range(ATTN_GROUP)]

    def attend(pieces):
        outs = []
        for h in range(ATTN_KV_HEADS):
            q = q_ref[:, h * 256:(h + 1) * 256]
            q4 = jnp.concatenate([q[:, g * 64:(g + 1) * 64] for g in range(ATTN_GROUP)], axis=0)
            sinks = [sink_ref[h * ATTN_GROUP + g] * LOG2E for g in range(ATTN_GROUP)]
            ss = []
            for k_ref, _, ok in pieces:
                s = _dot_nt(q4, k_ref[h])
                ss.append(s if ok is None else jnp.where(ok, s, NEG))
            mxs, ps = [], [[] for _ in pieces]
            for g, r in enumerate(groups):
                mx = functools.reduce(jnp.maximum, [jnp.max(s[r], axis=-1, keepdims=True) for s in ss])
                mx = jnp.maximum(mx, sinks[g])
                mxs.append(mx)
                for i, s in enumerate(ss):
                    ps[i].append(jnp.exp2(s[r] - mx).astype(BF16))
            oa = sum(_dot(jnp.concatenate(p, axis=0), v_ref[h]) for p, (_, v_ref, _) in zip(ps, pieces))
            for g, r in enumerate(groups):
                den = oa[r, HEAD_DIM:HEAD_DIM + 1] + jnp.exp2(sinks[g] - mxs[g])
                outs.append(oa[r, 0:HEAD_DIM] / den)
        o_ref[...] = jnp.concatenate(outs, axis=1).astype(BF16)

    @pl.when(qi >= 2)
    def _():
        ri = lax.broadcasted_iota(jnp.int32, (nr, ATTN_BLOCK), 0) & (ATTN_BLOCK - 1)
        ci = lax.broadcasted_iota(jnp.int32, (nr, ATTN_BLOCK), 1)
        far = 4 * ATTN_BLOCK
        ok_p = ci >= ri + jnp.where(m >= 1, 0, far)
        ok_n = ci + jnp.where(m <= SEQ // ATTN_BLOCK - 2, 0, far) <= ri
        attend([(kc_ref, vc_ref, None), (kp_ref, vp_ref, ok_p), (kk_ref, vk_ref, None), (kn_ref, vn_ref, ok_n)])

    if q_lo == 0:
        @pl.when(qi < 2)
        def _():
            attend([(kc_ref, vc_ref, None)])


def _attention(sink, qa, ka, va, need_ctx):
    q_lo = 0 if need_ctx else 2
    nq = SEQ // ATTN_BLOCK + (2 if need_ctx else 0)
    nb = SEQ // ATTN_BLOCK
    lat0 = RC // ATTN_BLOCK
    out0 = 0 if need_ctx else lat0

    def qrow(b, qi):
        return jnp.where(qi < 2, b * 2 + qi, lat0 + b * nb + qi - 2)

    def loc(delta):
        def f(b, i):
            m = jnp.clip(i + q_lo - 2 + delta, 0, nb - 1)
            return (0, lat0 + b * nb + m, 0)
        return f

    def kv_specs(width):
        ctx_spec = pl.BlockSpec((ATTN_KV_HEADS, CTX_LEN, width), lambda b, i: (0, b, 0))
        loc_spec = lambda d: pl.BlockSpec((ATTN_KV_HEADS, ATTN_BLOCK, width), loc(d))
        return [ctx_spec, loc_spec(-1), loc_spec(0), loc_spec(1)]

    return pl.pallas_call(
        functools.partial(_attn_body, q_lo=q_lo),
        grid=(BATCH, nq),
        in_specs=[pl.BlockSpec(memory_space=pltpu.SMEM),
                  pl.BlockSpec((ATTN_BLOCK, W), lambda b, i: (qrow(b, i + q_lo), 0))]
                 + kv_specs(HEAD_DIM) + kv_specs(128),
        out_specs=pl.BlockSpec((ATTN_BLOCK, W), lambda b, i: (qrow(b, i + q_lo) - out0, 0)),
        out_shape=jax.ShapeDtypeStruct((R - out0 * ATTN_BLOCK, W), BF16),
        compiler_params=_params(("arbitrary", "arbitrary")),
        name="window_attn",
    )(sink, qa, ka, ka, ka, ka, va, va, va, va)


def _fourier_body(*refs, has_ctx):
    if has_ctx:
        uc_ref, ul_ref, bdc_ref, bds_ref, w2_ref, w2c_ref, oc_ref, ol_ref, as_ref = refs
    else:
        ul_ref, bdc_ref, bds_ref, w2_ref, ol_ref, as_ref = refs
    j = pl.program_id(1)
    first = 1 if has_ctx else 0

    if has_ctx:
        @pl.when(j == 0)
        def _():
            u = uc_ref[...]
            a = _dot(u, bdc_ref[...]).astype(BF16)
            s = _dot(u, bds_ref[...]).astype(BF16)
            z = _dot(w2c_ref[...], jnp.concatenate([a, s], axis=0))
            oc_ref[...] = (z * ((CTX_LEN * FOURIER_DIM) ** -0.5)).astype(BF16)

    @pl.when(j == first)
    def _():
        u = ul_ref[...]
        as_ref[0:SEQ, :] = _dot(u, bdc_ref[...]).astype(BF16)
        as_ref[SEQ:2 * SEQ, :] = _dot(u, bds_ref[...]).astype(BF16)

    @pl.when(j >= first)
    def _():
        ol_ref[...] = (_dot(w2_ref[...], as_ref[...]) * ((SEQ * FOURIER_DIM) ** -0.5)).astype(BF16)


def _fourier(fu, dft, need_ctx):
    bdc, bds, w2, w2c = dft
    tr = TR_FOURIER
    nj = SEQ // tr
    first = 1 if need_ctx else 0
    full = lambda a: pl.BlockSpec(a.shape, lambda *_: (0,) * a.ndim)
    lat_tile = lambda j: jnp.maximum(j - first, 0)
    ul_spec = pl.BlockSpec((SEQ, W), lambda b, j: (1 + b, 0))
    w2_spec = pl.BlockSpec((tr, 2 * SEQ), lambda b, j: (lat_tile(j), 0))
    ol_spec = pl.BlockSpec((tr, W), lambda b, j: (b * nj + lat_tile(j), 0))
    ol_shape = jax.ShapeDtypeStruct((RL, W), BF16)
    if need_ctx:
        in_specs = [pl.BlockSpec((CTX_LEN, W), lambda b, j: (b, 0)), ul_spec, full(bdc), full(bds), w2_spec,
                    full(w2c)]
        out_specs = [pl.BlockSpec((CTX_LEN, W), lambda b, j: (b, 0)), ol_spec]
        out_shape = [jax.ShapeDtypeStruct((RC, W), BF16), ol_shape]
        args = (fu, fu, bdc, bds, w2, w2c)
    else:
        in_specs = [ul_spec, full(bdc), full(bds), w2_spec]
        out_specs = [ol_spec]
        out_shape = [ol_shape]
        args = (fu, bdc, bds, w2)
    outs = pl.pallas_call(
        functools.partial(_fourier_body, has_ctx=need_ctx),
        grid=(BATCH, nj + first),
        in_specs=in_specs,
        out_specs=out_specs,
        out_shape=out_shape,
        scratch_shapes=[pltpu.VMEM((2 * SEQ, W), BF16)],
        compiler_params=_params(("arbitrary", "arbitrary")),
        name="fourier_mix",
    )(*args)
    return (outs[0], outs[1]) if need_ctx else (None, outs[0])


def _retention_body(qc_ref, kc_ref, vc_ref, ql_ref, kl_ref, vl_ref, g_ref,
                    dcomb_ref, xif_ref, ztf_ref, xib_ref, ztb_ref, gf_ref, gb_ref, mbd_ref, avg_ref,
                    o_ref, os_ref, st_ref):
    j = pl.program_id(1)
    C = RET_CHUNK
    nl = SEQ // C

    def group_parts(q, k, v, gi):
        sl = slice(gi * 256, (gi + 1) * 256)
        return q[:, sl], k[:, sl], v[:, sl], sl

    own = ((lax.broadcasted_iota(jnp.int32, (4 * C, 256), 0) >> 7)
           == (lax.broadcasted_iota(jnp.int32, (4 * C, 256), 1) >> 6))

    def chunk_fwd(q, k, v, r0):
        for gi in range(2):
            q4, k4, v4, sl = group_parts(q, k, v, gi)
            s_prev = st_ref[gi]
            q4f = q4.astype(F32)
            o4 = _dot((q4f * xif_ref[:, sl]).astype(BF16), s_prev.astype(BF16))
            qstack = jnp.where(own, jnp.concatenate([q4f] * 4, axis=0), 0.0).astype(BF16)
            p = (_dot_nt(qstack, k4) * dcomb_ref[gi]).astype(BF16)
            ov = jnp.where(own, _dot(p, v4), 0.0)
            intra = ov[0:C] + ov[C:2 * C] + ov[2 * C:3 * C] + ov[3 * C:4 * C]
            os_ref[pl.ds(r0, C), sl] = o4 + intra
            u = _dot_tn(k4, (v4.astype(F32) * ztf_ref[:, sl]).astype(BF16))
            st_ref[gi] = gf_ref[gi] * s_prev + mbd_ref[...] * u

    def chunk_bwd(q, k, v, r0):
        for gi in range(2):
            q4, k4, v4, sl = group_parts(q, k, v, gi)
            s_prev = st_ref[gi]
            cross = _dot((q4.astype(F32) * xib_ref[:, sl]).astype(BF16), s_prev.astype(BF16))
            os_ref[pl.ds(r0, C), sl] = os_ref[pl.ds(r0, C), sl] + cross
            u = _dot_tn(k4, (v4.astype(F32) * ztb_ref[:, sl]).astype(BF16))
            st_ref[gi] = gb_ref[gi] * s_prev + mbd_ref[...] * u

    @pl.when(j == 0)
    def _():
        st_ref[...] = jnp.zeros_like(st_ref)
        for c in range(CTX_LEN // C):
            rs = slice(c * C, (c + 1) * C)
            chunk_fwd(qc_ref[rs, :], kc_ref[rs, :], vc_ref[rs, :], c * C)

        def fbody(c, carry):
            r0 = pl.multiple_of(c * C, C)
            rs = pl.ds(r0, C)
            chunk_fwd(ql_ref[rs, :], kl_ref[rs, :], vl_ref[rs, :], CTX_LEN + r0)
            return carry

        lax.fori_loop(0, nl, fbody, 0)

        st_ref[...] = jnp.zeros_like(st_ref)
        for c in reversed(range(CTX_LEN // C)):
            rs = slice(c * C, (c + 1) * C)
            chunk_bwd(qc_ref[rs, :], kc_ref[rs, :], vc_ref[rs, :], c * C)

        def bbody(t, carry):
            r0 = pl.multiple_of((nl - 1 - t) * C, C)
            rs = pl.ds(r0, C)
            chunk_bwd(ql_ref[rs, :], kl_ref[rs, :], vl_ref[rs, :], CTX_LEN + r0)
            return carry

        lax.fori_loop(0, nl, bbody, 0)

    o = os_ref[pl.ds(pl.multiple_of(j * 256, 256), 256), :]
    avg = avg_ref[...]
    oh, ol = _split(o)
    mu = _dot(oh, avg) + _dot(ol, avg)
    d = o - mu
    vh, vl = _split(d * d)
    var = _dot(vh, avg) + _dot(vl, avg)
    o_ref[...] = (g_ref[...].astype(F32) * d * lax.rsqrt(var + GN_EPS)).astype(BF16)


def _retention(qr, kr, vr, gr, rtabs):
    nj = 1 + SEQ // 256

    def out_map(b, j):
        return (jnp.where(j == 0, b, RC // 256 + b * (SEQ // 256) + j - 1), 0)

    ctx = pl.BlockSpec((CTX_LEN, W), lambda b, j: (b, 0))
    lat = pl.BlockSpec((SEQ, W), lambda b, j: (1 + b, 0))
    full = lambda a: pl.BlockSpec(a.shape, lambda *_: (0,) * a.ndim)
    avg = jnp.asarray(np.kron(np.eye(RET_HEADS), np.full((RET_DK, RET_DK), 1.0 / RET_DK)).astype(np.float32)).astype(BF16)
    mbd = jnp.asarray(np.kron(np.eye(4), np.ones((RET_DK, RET_DK))).astype(np.float32))
    tabs = list(rtabs) + [mbd, avg]
    return pl.pallas_call(
        _retention_body,
        grid=(BATCH, nj),
        in_specs=[ctx, ctx, ctx, lat, lat, lat, pl.BlockSpec((256, W), out_map)] + [full(t) for t in tabs],
        out_specs=pl.BlockSpec((256, W), out_map),
        out_shape=jax.ShapeDtypeStruct((R, W), BF16),
        scratch_shapes=[pltpu.VMEM((CTX_LEN + SEQ, W), F32), pltpu.VMEM((2, 256, 256), F32)],
        compiler_params=_params(("arbitrary", "arbitrary")),
        name="retention",
    )(qr, kr, vr, qr, kr, vr, gr, *tabs)


def _pack_pair(a, b):
    ua = lax.bitcast_convert_type(a, jnp.uint32) >> 16
    ub = lax.bitcast_convert_type(b, jnp.uint32) & jnp.uint32(0xFFFF0000)
    return ua | ub


def _unpack_pair(w):
    a = lax.bitcast_convert_type(w << 16, F32)
    b = lax.bitcast_convert_type(w & jnp.uint32(0xFFFF0000), F32)
    return a, b


def _slot_onehot(s0, s1, n):
    srow = lax.broadcasted_iota(jnp.int32, (n, s0.shape[1]), 0)
    p0 = jnp.where(srow == s0, 1.0, 0.0).astype(BF16)
    p1 = jnp.where(srow == s1, 1.0, 0.0).astype(BF16)
    return p0, p1


def _pick(first, a_ref, b_ref):
    a = a_ref[...]
    flag = jnp.zeros(a.shape, jnp.int32) + first.astype(jnp.int32)
    return jnp.where(flag > 0, a, b_ref[...])


def _merge_body(*refs, layer0):
    if layer0:
        oa_ref, ofc_ref, ofl_ref, rt_ref, gm_ref, xc_ref, xl_ref = refs[:7]
        rest = refs[7:]
        is_ctx = pl.program_id(0) < RC // TM_MERGE
        of_in = _pick(is_ctx, ofc_ref, ofl_ref)
        x_in = _pick(is_ctx, xc_ref, xl_ref)
    else:
        oa_ref, ofl_ref, rt_ref, gm_ref, x_ref = refs[:5]
        rest = refs[5:]
        of_in = ofl_ref[...]
        x_in = x_ref[...]
    (mod_ref, gn_ref, wba_ref, wbf_ref, wbr_ref, wout_ref, wrh_ref, wrl_ref, br_ref, tri_ref, ltri_ref,
     xo_ref, xs_ref, ro_ref, nch_ref) = rest
    gm = gm_ref[...].astype(F32)
    z = (gm[:, 0:D_MODEL] * _dot(oa_ref[...], wba_ref[...])
         + gm[:, D_MODEL:2 * D_MODEL] * _dot(of_in, wbf_ref[...])
         + gm[:, 2 * D_MODEL:3 * D_MODEL] * _dot(rt_ref[...], wbr_ref[...]))
    y = _dot(z.astype(BF16), wout_ref[...])
    x = x_in + mod_ref[2:3, :] * y
    xo_ref[...] = x
    ms = jnp.mean(x * x, axis=-1, keepdims=True)
    hn = x * lax.rsqrt(ms + NORM_EPS) * gn_ref[...]
    h2 = hn * (1.0 + mod_ref[4:5, :]) + mod_ref[3:4, :]
    hh, hl = _split(h2)
    wh, wl = wrh_ref[...], wrl_ref[...]
    lg = _dot_nt(wh, hh) + _dot_nt(wh, hl) + _dot_nt(wl, hh) + br_ref[...]
    tm = lg.shape[1]
    row8 = lax.broadcasted_iota(jnp.int32, (8, tm), 0)
    lgg = lg[0:8, :]
    mg = jnp.max(lgg, axis=0, keepdims=True)
    grp = jnp.min(jnp.where(lgg == mg, row8, 8), axis=0, keepdims=True)
    pg = 1.0 / jnp.sum(jnp.exp(lgg - mg), axis=0, keepdims=True)
    lin = jnp.zeros((8, tm), F32)
    for g in range(N_GROUPS):
        lin = jnp.where(grp == g, lg[8 + 8 * g:16 + 8 * g, :], lin)
    v1 = jnp.max(lin, axis=0, keepdims=True)
    i1 = jnp.min(jnp.where(lin == v1, row8, 8), axis=0, keepdims=True)
    rest = jnp.where(row8 == i1, -jnp.inf, lin)
    v2 = jnp.max(rest, axis=0, keepdims=True)
    i2 = jnp.min(jnp.where(rest == v2, row8, 8), axis=0, keepdims=True)
    e2 = jnp.exp(v2 - v1)
    w1 = pg / (1.0 + e2)
    w2 = pg * e2 / (1.0 + e2)
    e_1 = grp * EXPERTS_PER_GROUP + i1
    e_2 = grp * EXPERTS_PER_GROUP + i2

    row32 = lax.broadcasted_iota(jnp.int32, (N_EXPERTS, tm), 0)
    oh0 = jnp.where(row32 == e_1, 1.0, 0.0)
    oh1 = jnp.where(row32 == e_2, 1.0, 0.0)
    tri = tri_ref[...]
    cum0 = _dot(oh0.astype(BF16), tri)
    cum1 = _dot(oh1.astype(BF16), tri)
    tot0 = jnp.sum(oh0, axis=1, keepdims=True)
    tot1 = jnp.sum(oh1, axis=1, keepdims=True)
    nch = ((tot0 + tot1).astype(jnp.int32) + (CHUNK - 1)) >> CHUNK_SHIFT
    nch_b = jnp.broadcast_to(nch.astype(F32), (N_EXPERTS, 128))
    nch_ref[...] = nch_b.astype(jnp.int32)
    base = CHUNK * _dot(ltri_ref[...], nch_b.astype(BF16))[:, 0:1]
    s0 = jnp.sum(oh0 * (base + cum0), axis=0, keepdims=True).astype(jnp.int32)
    s1 = jnp.sum(oh1 * (base + tot0 + cum1), axis=0, keepdims=True).astype(jnp.int32)
    p0, p1 = _slot_onehot(s0, s1, SLOTS)
    xs = _dot(p0 + p1, hh)
    xs_ref[:, 0:D_MODEL // 2] = _pack_pair(xs[:, 0:D_MODEL // 2], xs[:, D_MODEL // 2:D_MODEL])

    def wrows(w):
        hi, lo = _split(w)
        return jnp.where(row8 == 0, hi.astype(F32), jnp.where(row8 == 1, lo.astype(F32), 0.0)).astype(BF16)

    wc = _dot_nt(p0, wrows(w1)) + _dot_nt(p1, wrows(w2))
    wcol = jnp.broadcast_to(wc[:, 0:1] + wc[:, 1:2], (SLOTS, 128))
    xs_ref[:, D_MODEL // 2:XS_COLS] = lax.bitcast_convert_type(wcol, jnp.uint32)
    s0f, s1f = s0.astype(F32), s1.astype(F32)
    ro_ref[...] = jnp.where(row8 == 0, s0f, jnp.where(row8 == 1, s1f, 0.0))


def _merge(oa, of_c, of_l, ret, gm, xs_in, mod_l, gnorm, wba, wbf, wbr, wout, wrh, wrl, brb, layer0):
    tm = TM_MERGE
    row0 = 0 if layer0 else RC
    rm = R - row0
    nt = rm // tm
    off = row0 // tm
    nc = RC // tm
    src = lambda n: pl.BlockSpec((tm, n), lambda i: (i + off, 0))
    dst = lambda n: pl.BlockSpec((tm, n), lambda i: (i, 0))
    ctx_rows = lambda n: pl.BlockSpec((tm, n), lambda i: (jnp.minimum(i, nc - 1), 0))
    lat_rows = lambda n: pl.BlockSpec((tm, n), lambda i: (jnp.maximum(i - nc, 0), 0))
    full = lambda a: pl.BlockSpec(a.shape, lambda *_: (0,) * a.ndim, pipeline_mode=pl.Buffered(1))
    mrow = _mod_row(row0, tm)
    tri = jnp.asarray(np.triu(np.ones((tm, tm), np.float32), 1)).astype(BF16)
    ltri = jnp.asarray(np.tril(np.ones((N_EXPERTS, N_EXPERTS), np.float32), -1)).astype(BF16)
    if layer0:
        acts = [oa, of_c, of_l, ret, gm, xs_in[0], xs_in[1]]
        act_specs = [src(W), ctx_rows(W), lat_rows(W), src(W), src(3 * D_MODEL), ctx_rows(D_MODEL),
                     lat_rows(D_MODEL)]
    else:
        acts = [oa, of_l, ret, gm, xs_in]
        act_specs = [dst(W), dst(W), src(W), src(3 * D_MODEL), src(D_MODEL)]
    return pl.pallas_call(
        functools.partial(_merge_body, layer0=layer0),
        grid=(nt,),
        in_specs=act_specs + [
                  pl.BlockSpec((None, 6, D_MODEL), lambda i: (mrow(i), 0, 0)),
                  full(gnorm), full(wba), full(wbf), full(wbr), full(wout), full(wrh), full(wrl), full(brb),
                  full(tri), full(ltri)],
        out_specs=[dst(D_MODEL), pl.BlockSpec((SLOTS, XS_COLS), lambda i: (i, 0)),
                   pl.BlockSpec((8, tm), lambda i: (0, i)),
                   pl.BlockSpec((None, N_EXPERTS, 128), lambda i: (i, 0, 0))],
        out_shape=[jax.ShapeDtypeStruct((rm, D_MODEL), F32),
                   jax.ShapeDtypeStruct((nt * SLOTS, XS_COLS), jnp.uint32),
                   jax.ShapeDtypeStruct((8, rm), F32),
                   jax.ShapeDtypeStruct((nt, N_EXPERTS, 128), jnp.int32)],
        compiler_params=_params(("arbitrary",)),
        name="merge_router",
    )(*acts, mod_l, gnorm, wba, wbf, wbr, wout, wrh, wrl, brb, tri, ltri)


def _moe_plan(nch, nb):
    nt = nch.shape[0]
    choff = jnp.cumsum(nch, axis=1) - nch
    used_ch = jnp.sum(nch, axis=1)
    cum_t = jnp.cumsum(nch, axis=0)
    tot = cum_t[-1]
    ptot = (tot + CPB - 1) // CPB * CPB
    pend = jnp.cumsum(ptot)
    pstart = pend - ptot
    n_used = pend[-1] // CPB
    blk = jnp.arange(nb, dtype=jnp.int32)
    lane = jnp.arange(CPB, dtype=jnp.int32)
    blk_e = jnp.minimum(jnp.sum((blk[:, None] * CPB >= pend[None, :]).astype(jnp.int32), axis=1), N_EXPERTS - 1)
    oe = (blk_e[:, None] == jnp.arange(N_EXPERTS, dtype=jnp.int32)[None, :]).astype(jnp.int32)
    sel = lambda tab: jnp.sum(oe[:, :, None] * tab.T[None, :, :], axis=1)
    pstart_b = jnp.sum(oe * pstart[None, :], axis=1)
    tot_b = jnp.sum(oe * tot[None, :], axis=1)
    cum_b, nch_b, choff_b = sel(cum_t), sel(nch), sel(choff)
    i = blk[:, None] * CPB + lane[None, :] - pstart_b[:, None]
    valid = (i < tot_b[:, None]) & (blk[:, None] < n_used)
    t = jnp.minimum(jnp.sum((i[:, :, None] >= cum_b[:, None, :]).astype(jnp.int32), axis=2), nt - 1)
    tiles = jnp.arange(nt, dtype=jnp.int32)[None, None, :]
    before = jnp.sum(jnp.where(tiles < t[:, :, None], nch_b[:, None, :], 0), axis=2)
    coff = jnp.sum(jnp.where(tiles == t[:, :, None], choff_b[:, None, :], 0), axis=2)
    row = t * SLOTS + CHUNK * (coff + i - before)
    src = jnp.where(valid, row, SLOTS - CHUNK)
    dummy = nt * SLOTS + CHUNK * ((blk[:, None] % 2) * CPB + lane[None, :])
    dst = jnp.where(valid, row, dummy)
    blk_start = jnp.concatenate([pstart, pend[-1:]]) // CPB
    return (blk_start.astype(jnp.int32), n_used.astype(jnp.int32).reshape(1), src.reshape(-1).astype(jnp.int32),
            dst.reshape(-1).astype(jnp.int32), used_ch.astype(jnp.int32))


def _ffn_body(bs_ref, nu_ref, src_ref, dst_ref, uc_ref, xs_ref, wg_ref, wu_ref, wd_ref, ys_ref,
              xbuf, ybuf, zbuf, wgb, wub, wdb, sem_in, sem_out, sem_z, *, nt):
    e = pl.program_id(0)
    nu = nu_ref[0]
    half = D_MODEL // 2

    def gather(blk, sl):
        for c in range(CPB):
            r = pl.multiple_of(src_ref[blk * CPB + c], CHUNK)
            pltpu.make_async_copy(xs_ref.at[pl.ds(r, CHUNK)], xbuf.at[sl, pl.ds(c * CHUNK, CHUNK)],
                                  sem_in.at[sl]).start()

    def scatter(blk, sl):
        for c in range(CPB):
            r = pl.multiple_of(dst_ref[blk * CPB + c], CHUNK)
            pltpu.make_async_copy(ybuf.at[sl, pl.ds(c * CHUNK, CHUNK)], ys_ref.at[pl.ds(r, CHUNK)],
                                  sem_out.at[sl]).start()

    def wait_gather(sl):
        pltpu.make_async_copy(xs_ref.at[pl.ds(0, MOE_BM)], xbuf.at[sl], sem_in.at[sl]).wait()

    def wait_scatter(sl):
        pltpu.make_async_copy(ybuf.at[sl], ys_ref.at[pl.ds(0, MOE_BM)], sem_out.at[sl]).wait()

    def zero_copy(r):
        return pltpu.make_async_copy(zbuf, ys_ref.at[pl.ds(pl.multiple_of(r, CHUNK), CHUNK)], sem_z)

    @pl.when(e == 0)
    def _():
        zbuf[...] = jnp.zeros_like(zbuf)

        def tails(fn):
            def per_tile(t, carry):
                def per_chunk(c, carry2):
                    fn(t * SLOTS + c * CHUNK)
                    return carry2
                lax.fori_loop(uc_ref[t], SLOTS // CHUNK, per_chunk, 0)
                return carry
            lax.fori_loop(0, nt, per_tile, 0)
            for c in range(2 * CPB):
                fn(nt * SLOTS + c * CHUNK)

        tails(lambda r: zero_copy(r).start())
        tails(lambda r: zero_copy(r).wait())
        gather(0, 0)

    b0, b1 = bs_ref[e], bs_ref[e + 1]

    @pl.when(b1 > b0)
    def _():
        wgb[...] = wg_ref[...].astype(BF16)
        wub[...] = wu_ref[...].astype(BF16)
        wdb[...] = wd_ref[...].astype(BF16)

        def block(b, carry):
            slot = b % 2

            @pl.when(b + 1 < nu)
            def _():
                gather(b + 1, 1 - slot)

            wait_gather(slot)

            @pl.when(b >= 2)
            def _():
                wait_scatter(slot)

            xw = xbuf[slot]
            xa, xb = _unpack_pair(xw[:, 0:half])
            x = jnp.concatenate([xa, xb], axis=1).astype(BF16)
            wt = lax.bitcast_convert_type(xw[:, half:XS_COLS], F32)
            g = _dot(x, wgb[...])
            u = _dot(x, wub[...])
            hmid = (g * _sigmoid(g) * u).astype(BF16)
            y = _dot(hmid, wdb[...]) * jnp.concatenate([wt] * (D_MODEL // 128), axis=1)
            yb = y.astype(BF16).astype(F32)
            ybuf[slot] = _pack_pair(yb[:, 0:half], yb[:, half:D_MODEL])
            scatter(b, slot)
            return carry

        lax.fori_loop(b0, b1, block, 0)

    @pl.when(e == N_EXPERTS - 1)
    def _():
        wait_scatter((nu - 1) % 2)

        @pl.when(nu >= 2)
        def _():
            wait_scatter(nu % 2)


def _ffn(plan, xs, w_g, w_u, w_d, layer, nt):
    wmap = lambda e, *_: (layer, e, 0, 0)
    half = D_MODEL // 2
    grid_spec = pltpu.PrefetchScalarGridSpec(
        num_scalar_prefetch=5,
        grid=(N_EXPERTS,),
        in_specs=[pl.BlockSpec(memory_space=pl.ANY),
                  pl.BlockSpec((None, None, D_MODEL, EXPERT_HIDDEN), wmap),
                  pl.BlockSpec((None, None, D_MODEL, EXPERT_HIDDEN), wmap),
                  pl.BlockSpec((None, None, EXPERT_HIDDEN, D_MODEL), wmap)],
        out_specs=pl.BlockSpec(memory_space=pl.ANY),
        scratch_shapes=[pltpu.VMEM((2, MOE_BM, XS_COLS), jnp.uint32), pltpu.VMEM((2, MOE_BM, half), jnp.uint32),
                        pltpu.VMEM((CHUNK, half), jnp.uint32),
                        pltpu.VMEM((D_MODEL, EXPERT_HIDDEN), BF16), pltpu.VMEM((D_MODEL, EXPERT_HIDDEN), BF16),
                        pltpu.VMEM((EXPERT_HIDDEN, D_MODEL), BF16),
                        pltpu.SemaphoreType.DMA((2,)), pltpu.SemaphoreType.DMA((2,)), pltpu.SemaphoreType.DMA(())],
    )
    return pl.pallas_call(
        functools.partial(_ffn_body, nt=nt),
        grid_spec=grid_spec,
        out_shape=jax.ShapeDtypeStruct((nt * SLOTS + 2 * CPB * CHUNK, half), jnp.uint32),
        compiler_params=_params(("arbitrary",)),
        name="moe_experts",
    )(*plan, xs, w_g, w_u, w_d)


def _combine_body(ys_ref, ro_ref, x_ref, mod_ref, gn_ref, o_ref, *, final):
    s = ro_ref[...]
    p0, p1 = _slot_onehot(s[0:1, :].astype(jnp.int32), s[1:2, :].astype(jnp.int32), SLOTS)
    ya, yb = _unpack_pair(ys_ref[...])
    y = jnp.concatenate([ya, yb], axis=1).astype(BF16)
    f = _dot_tn(p0 + p1, y)
    x = x_ref[...] + mod_ref[5:6, :] * f
    if final:
        ms = jnp.mean(x * x, axis=-1, keepdims=True)
        x = x * lax.rsqrt(ms + NORM_EPS) * gn_ref[...]
    o_ref[...] = x


def _combine(ys, route, x, mod_l, gnorm, row0, final):
    tm = TM_MERGE
    rm = x.shape[0]
    mrow = _mod_row(row0, tm)
    return pl.pallas_call(
        functools.partial(_combine_body, final=final),
        grid=(rm // tm,),
        in_specs=[pl.BlockSpec((SLOTS, D_MODEL // 2), lambda i: (i, 0)),
                  pl.BlockSpec((8, tm), lambda i: (0, i)),
                  pl.BlockSpec((tm, D_MODEL), lambda i: (i, 0)),
                  pl.BlockSpec((None, 6, D_MODEL), lambda i: (mrow(i), 0, 0)),
                  pl.BlockSpec((1, D_MODEL), lambda i: (0, 0))],
        out_specs=pl.BlockSpec((tm, D_MODEL), lambda i: (i, 0)),
        out_shape=jax.ShapeDtypeStruct((rm, D_MODEL), F32),
        compiler_params=_params(("arbitrary",)),
        name="moe_combine",
    )(ys, route, x, mod_l, gnorm)


def _moe(xs, route, nch3, x, mod_l, gnorm, w_g, w_u, w_d, row0, final, layer):
    nt = nch3.shape[0]
    max_chunks = nt * ((2 * TM_MERGE + N_EXPERTS * (CHUNK - 1)) // CHUNK)
    nb = -(-max_chunks // CPB) + N_EXPERTS
    plan = _moe_plan(nch3[:, :, 0], nb)
    ys = _ffn(plan, xs, w_g, w_u, w_d, layer, nt)
    return _combine(ys, route, x, mod_l, gnorm, row0, final)


def kernel(x, c, ctx, c_ctx, norm_mix, norm_ffn, w_ada, b_ada, w_in, attn_sink, ret_decay_fwd, ret_decay_bwd,
           w_branch_attn, w_branch_fourier, w_branch_ret, w_out, w_router_group, b_router_group,
           w_router_expert, b_router_expert, w_exp_gate, w_exp_up, w_exp_down, norm_final):
    tabs = [jnp.asarray(t) for t in _rope_tables()]
    dft = [jnp.asarray(t).astype(BF16) for t in _dft_tables()]

    cc = jnp.zeros((MOD_ROWS, D_MODEL), F32).at[0:BATCH].set(c).at[CTX_MOD_ROW].set(c_ctx)
    mod = _ada(cc, w_ada, b_ada).reshape(DEPTH, MOD_ROWS, 6, D_MODEL)

    xf = (ctx.reshape(RC, D_MODEL), x.reshape(RL, D_MODEL))
    w_in_bf = w_in.astype(BF16)
    for l in range(DEPTH):
        need_ctx = l < DEPTH - 1
        row0 = 0 if need_ctx else RC
        mod_l = mod[l]
        qa, ka, va, qr, kr, vr, gr, fu, gm = _proj(xf, mod_l, norm_mix[l][None, :], w_in_bf, tabs, l)
        oa = _attention(attn_sink[l], qa, ka, va, need_ctx)
        of_c, of_l = _fourier(fu, dft, need_ctx)
        ret = _retention(qr, kr, vr, gr, _retention_tables(ret_decay_fwd[l], ret_decay_bwd[l]))
        wr = jnp.zeros((ROUTER_ROWS, D_MODEL), F32)
        wr = wr.at[0:N_GROUPS].set(w_router_group[l].T).at[8:8 + N_EXPERTS].set(w_router_expert[l].T)
        br = jnp.full((ROUTER_ROWS,), NEG, F32)
        br = br.at[0:N_GROUPS].set(b_router_group[l]).at[8:8 + N_EXPERTS].set(b_router_expert[l])
        wrh, wrl = _split(wr)
        brb = jnp.broadcast_to(br[:, None], (ROUTER_ROWS, TM_MERGE))
        x_mid, xs, route, nch3 = _merge(oa, of_c, of_l, ret, gm, xf, mod_l, norm_ffn[l][None, :],
                                        w_branch_attn[l].astype(BF16), w_branch_fourier[l].astype(BF16),
                                        w_branch_ret[l].astype(BF16), w_out[l].astype(BF16), wrh, wrl, brb,
                                        need_ctx)
        final = l == DEPTH - 1
        xf = _moe(xs, route, nch3, x_mid, mod_l, norm_final[None, :], w_exp_gate, w_exp_up, w_exp_down,
                  row0, final, l)
    return xf.reshape(BATCH, SEQ, D_MODEL)
```

```python
import functools

import numpy as np
import jax
import jax.numpy as jnp
from jax import lax
from jax.experimental import pallas as pl
from jax.experimental.pallas import tpu as pltpu

F32 = jnp.float32
BF16 = jnp.bfloat16

D_MODEL = 1024
BATCH = 8
SEQ = 2048
DEPTH = 2
CTX_LEN = 256
GRID_W = 64
HEAD_DIM = 64
ATTN_HEADS = 8
ATTN_KV_HEADS = 2
ATTN_GROUP = ATTN_HEADS // ATTN_KV_HEADS
ATTN_BLOCK = 128
RET_HEADS = 8
RET_DK = 64
RET_CHUNK = 128
FOURIER_GROUPS = 4
FOURIER_DIM = 128
N_GROUPS = 4
EXPERTS_PER_GROUP = 8
N_EXPERTS = N_GROUPS * EXPERTS_PER_GROUP
EXPERT_HIDDEN = 512
ROPE_BASE = 10000.0
NORM_EPS = 1e-6
GN_EPS = 1e-5

W = 512
IN_COLS = 6400
RC = BATCH * CTX_LEN
RL = BATCH * SEQ
R = RC + RL
MOD_ROWS = 16
CTX_MOD_ROW = 8

VMEM_LIMIT = 52 * 1024 * 1024

TM_PROJ = 512
TM_MERGE = 512
TN_ADA = 1536
TR_FOURIER = 512
MOE_BM = 512
CHUNK = 8
CHUNK_SHIFT = 3
CPB = MOE_BM // CHUNK
SLOTS = 1280
XS_COLS = D_MODEL // 2 + 128
NEG = -1e30
LOG2E = 1.4426950408889634
ROUTER_ROWS = 40


def _dot(a, b):
    return jnp.dot(a, b, preferred_element_type=F32)


def _dot_nt(a, b):
    return lax.dot_general(a, b, (((1,), (1,)), ((), ())), preferred_element_type=F32)


def _dot_tn(a, b):
    return lax.dot_general(a, b, (((0,), (0,)), ((), ())), preferred_element_type=F32)


def _split(x):
    hi = x.astype(BF16)
    lo = (x - hi.astype(F32)).astype(BF16)
    return hi, lo


def _sigmoid(x):
    return 1.0 / (1.0 + jnp.exp(-x))


def _params(sem, vmem=VMEM_LIMIT):
    return pltpu.CompilerParams(dimension_semantics=sem, vmem_limit_bytes=vmem)


def _mod_row(row0, tm):
    def f(i):
        g0 = i * tm + row0
        return jnp.where(g0 < RC, CTX_MOD_ROW, (g0 - RC) // SEQ)
    return f


def _rope_tables():
    pos = np.arange(SEQ, dtype=np.float64)
    row = np.floor(pos / GRID_W)
    col = pos % GRID_W

    def cs(p, nf):
        inv = ROPE_BASE ** (-np.arange(nf, dtype=np.float64) / nf)
        ang = p[:, None] * inv[None, :]
        return np.cos(ang), np.sin(ang)

    rc, rs = cs(row, HEAD_DIM // 4)
    cc, cs_ = cs(col, HEAD_DIM // 4)
    cos_a = np.concatenate([rc, rc, cc, cc], axis=1)
    sin_a = np.concatenate([-rs, rs, -cs_, cs_], axis=1)
    tc, ts = cs(pos, RET_DK // 2)
    cos_r = np.concatenate([tc, tc], axis=1)
    sin_r = np.concatenate([-ts, ts], axis=1)

    def full(t, ident):
        t2 = np.concatenate([t, t], axis=1)
        return np.concatenate([np.full_like(t2, ident), t2], axis=0).astype(np.float32)

    return full(cos_a, 1.0), full(sin_a, 0.0), full(cos_r, 1.0), full(sin_r, 0.0)


def _dft_tables():
    def cs(n):
        k = np.arange(n, dtype=np.int64)
        m = (k[:, None] * k[None, :]) % n
        ang = 2.0 * np.pi * m.astype(np.float64) / n
        return np.cos(ang), np.sin(ang)

    c128, s128 = cs(FOURIER_DIM)
    eye = np.eye(FOURIER_GROUPS)
    bdc = np.kron(eye, c128).astype(np.float32)
    bds = np.kron(eye, s128).astype(np.float32)
    cn, sn = cs(SEQ)
    w2 = np.concatenate([cn, -sn], axis=1).astype(np.float32)
    cl, sl = cs(CTX_LEN)
    w2c = np.concatenate([cl, -sl], axis=1).astype(np.float32)
    return bdc, bds, w2, w2c


def _retention_tables(dec_f, dec_b):
    lg_f = jax.nn.log_sigmoid(dec_f.astype(F32))
    lg_b = jax.nn.log_sigmoid(dec_b.astype(F32))
    i = jnp.arange(RET_CHUNK)
    diff = (i[:, None] - i[None, :]).astype(F32)
    fwd = jnp.exp(jnp.maximum(diff, 0.0)[None] * lg_f[:, None, None])
    bwd = jnp.exp(jnp.maximum(-diff, 0.0)[None] * lg_b[:, None, None])
    dcomb = jnp.where((diff >= 0)[None], fwd, bwd).reshape(2, 4 * RET_CHUNK, RET_CHUNK)
    fi = i.astype(F32)
    lanes = lambda t: jnp.repeat(t, RET_DK, axis=1)
    xi_f = lanes(jnp.exp((fi + 1.0)[:, None] * lg_f[None, :]))
    zt_f = lanes(jnp.exp((RET_CHUNK - 1 - fi)[:, None] * lg_f[None, :]))
    xi_b = lanes(jnp.exp((RET_CHUNK - fi)[:, None] * lg_b[None, :]))
    zt_b = lanes(jnp.exp(fi[:, None] * lg_b[None, :]))
    g_f = jnp.repeat(jnp.exp(RET_CHUNK * lg_f), RET_DK).reshape(2, 256, 1)
    g_b = jnp.repeat(jnp.exp(RET_CHUNK * lg_b), RET_DK).reshape(2, 256, 1)
    g_f = jnp.broadcast_to(g_f, (2, 256, 256))
    g_b = jnp.broadcast_to(g_b, (2, 256, 256))
    return dcomb, xi_f, zt_f, xi_b, zt_b, g_f, g_b


def _ada_body(c_ref, w_ref, b_ref, o_ref):
    c = c_ref[...]
    s = c * _sigmoid(c)
    sh, sl = _split(s)
    wh, wl = _split(w_ref[...])
    o_ref[...] = _dot(sh, wh) + _dot(sl, wh) + _dot(sh, wl) + b_ref[...]


def _ada(cc, w_ada, b_ada):
    nt = 6 * D_MODEL // TN_ADA
    return pl.pallas_call(
        _ada_body,
        grid=(DEPTH, nt),
        in_specs=[
            pl.BlockSpec((MOD_ROWS, D_MODEL), lambda l, j: (0, 0)),
            pl.BlockSpec((None, D_MODEL, TN_ADA), lambda l, j: (l, 0, j)),
            pl.BlockSpec((None, 1, TN_ADA), lambda l, j: (l, 0, j)),
        ],
        out_specs=pl.BlockSpec((None, MOD_ROWS, TN_ADA), lambda l, j: (l, 0, j)),
        out_shape=jax.ShapeDtypeStruct((DEPTH, MOD_ROWS, 6 * D_MODEL), F32),
        compiler_params=_params(("arbitrary", "arbitrary")),
        name="ada_mod",
    )(cc, w_ada, b_ada.reshape(DEPTH, 1, 6 * D_MODEL))


def _rope(xc, cos, sin, half):
    fwd = pltpu.roll(xc, 128 - half, axis=1)
    bwd = pltpu.roll(xc, half, axis=1)
    lane = lax.broadcasted_iota(jnp.int32, xc.shape, 1)
    first = (lane & (2 * half - 1)) < half
    return xc * cos + jnp.where(first, fwd, bwd) * sin


def _proj_body(*refs, split):
    if split:
        x = _pick(pl.program_id(0) < RC // TM_PROJ, refs[0], refs[1])
        refs = refs[2:]
    else:
        x = refs[0][...]
        refs = refs[1:]
    (mod_ref, gn_ref, w_ref, ca_ref, sa_ref, cr_ref, sr_ref,
     qa_ref, ka_ref, va_ref, qr_ref, kr_ref, vr_ref, gr_ref, fu_ref, gm_ref) = refs
    ms = jnp.mean(x * x, axis=-1, keepdims=True)
    y = x * lax.rsqrt(ms + NORM_EPS) * gn_ref[...]
    h = y * (1.0 + mod_ref[1:2, :]) + mod_ref[0:1, :]
    hb = h.astype(BF16)

    def proj(c0, width):
        return _dot(hb, w_ref[:, c0:c0 + width])

    ca, sa, cr, sr = ca_ref[...], sa_ref[...], cr_ref[...], sr_ref[...]

    qa = proj(0, W) * (HEAD_DIM ** -0.5 * LOG2E)
    for c in range(W // 128):
        qa_ref[:, c * 128:(c + 1) * 128] = _rope(qa[:, c * 128:(c + 1) * 128], ca, sa, 16).astype(BF16)
    kv = proj(W, 256)
    ka = _rope(kv[:, 0:128], ca, sa, 16).astype(BF16)
    ka_ref[0] = ka[:, 0:64]
    ka_ref[1] = ka[:, 64:128]
    va = kv[:, 128:256].astype(BF16)
    ones_col = jnp.where(lax.broadcasted_iota(jnp.int32, (va.shape[0], 64), 1) == 0, 1.0, 0.0).astype(BF16)
    va_ref[0] = jnp.concatenate([va[:, 0:64], ones_col], axis=1)
    va_ref[1] = jnp.concatenate([va[:, 64:128], ones_col], axis=1)
    qr = proj(768, W)
    kr = proj(1280, W) * (RET_DK ** -0.5)
    for c in range(W // 128):
        sl = slice(c * 128, (c + 1) * 128)
        qr_ref[:, sl] = _rope(qr[:, sl], cr, sr, 32).astype(BF16)
        kr_ref[:, sl] = _rope(kr[:, sl], cr, sr, 32).astype(BF16)
    vr_ref[...] = proj(1792, W).astype(BF16)
    g = proj(2304, W)
    gr_ref[...] = (g * _sigmoid(g)).astype(BF16)
    fu_ref[...] = proj(2816, W).astype(BF16)
    for c in range(3):
        gm_ref[:, c * D_MODEL:(c + 1) * D_MODEL] = _sigmoid(proj(3328 + c * D_MODEL, D_MODEL)).astype(BF16)


def _proj(x, mod_l, gnorm, w_in_bf, tabs, layer):
    tm = TM_PROJ
    nt = R // tm
    nc = RC // tm
    split = isinstance(x, tuple)
    if split:
        xs = list(x)
        x_specs = [pl.BlockSpec((tm, D_MODEL), lambda i: (jnp.minimum(i, nc - 1), 0)),
                   pl.BlockSpec((tm, D_MODEL), lambda i: (jnp.maximum(i - nc, 0), 0))]
    else:
        xs = [x]
        x_specs = [pl.BlockSpec((tm, D_MODEL), lambda i: (i, 0))]

    def tab_map(i):
        return (jnp.where(i < nc, i, nc + (i - nc) % (SEQ // tm)), 0)

    row = lambda i: (i, 0)
    wide = lambda n: pl.BlockSpec((tm, n), row)
    kv_spec = lambda n: pl.BlockSpec((2, tm, n), lambda i: (0, i, 0))
    sds = lambda n: jax.ShapeDtypeStruct((R, n), BF16)
    kv_sds = lambda n: jax.ShapeDtypeStruct((2, R, n), BF16)
    mrow = _mod_row(0, tm)
    return pl.pallas_call(
        functools.partial(_proj_body, split=split),
        grid=(nt,),
        in_specs=x_specs + [
            pl.BlockSpec((None, 6, D_MODEL), lambda i: (mrow(i), 0, 0)),
            pl.BlockSpec((1, D_MODEL), lambda i: (0, 0)),
            pl.BlockSpec((None, D_MODEL, IN_COLS), lambda i: (layer, 0, 0), pipeline_mode=pl.Buffered(1)),
        ] + [pl.BlockSpec((tm, 128), tab_map)] * 4,
        out_specs=[wide(W), kv_spec(64), kv_spec(128), wide(W), wide(W), wide(W), wide(W), wide(W),
                   wide(3 * D_MODEL)],
        out_shape=[sds(W), kv_sds(64), kv_sds(128), sds(W), sds(W), sds(W), sds(W), sds(W), sds(3 * D_MODEL)],
        compiler_params=_params(("arbitrary",)),
        name="in_proj",
    )(*xs, mod_l, gnorm, w_in_bf, *tabs)


def _attn_body(sink_ref, q_ref, kc_ref, kp_ref, kk_ref, kn_ref, vc_ref, vp_ref, vk_ref, vn_ref, o_ref, *, q_lo):
    qi = pl.program_id(1) + q_lo
    m = qi - 2
    nr = ATTN_GROUP * ATTN_BLOCK
    groups = [slice(g * ATTN_BLOCK, (g + 1) * ATTN_BLOCK) for g in range(ATTN_GROUP)]

    def attend(pieces):
        outs = []
        for h in range(ATTN_KV_HEADS):
            q = q_ref[:, h * 256:(h + 1) * 256]
            q4 = jnp.concatenate([q[:, g * 64:(g + 1) * 64] for g in range(ATTN_GROUP)], axis=0)
            sinks = [sink_ref[h * ATTN_GROUP + g] * LOG2E for g in range(ATTN_GROUP)]
            ss = []
            for k_ref, _, ok in pieces:
                s = _dot_nt(q4, k_ref[h])
                ss.append(s if ok is None else jnp.where(ok, s, NEG))
            mxs, ps = [], [[] for _ in pieces]
            for g, r in enumerate(groups):
                mx = functools.reduce(jnp.maximum, [jnp.max(s[r], axis=-1, keepdims=True) for s in ss])
                mx = jnp.maximum(mx, sinks[g])
                mxs.append(mx)
                for i, s in enumerate(ss):
                    ps[i].append(jnp.exp2(s[r] - mx).astype(BF16))
            oa = sum(_dot(jnp.concatenate(p, axis=0), v_ref[h]) for p, (_, v_ref, _) in zip(ps, pieces))
            for g, r in enumerate(groups):
                den = oa[r, HEAD_DIM:HEAD_DIM + 1] + jnp.exp2(sinks[g] - mxs[g])
                outs.append(oa[r, 0:HEAD_DIM] / den)
        o_ref[...] = jnp.concatenate(outs, axis=1).astype(BF16)

    @pl.when(qi >= 2)
    def _():
        ri = lax.broadcasted_iota(jnp.int32, (nr, ATTN_BLOCK), 0) & (ATTN_BLOCK - 1)
        ci = lax.broadcasted_iota(jnp.int32, (nr, ATTN_BLOCK), 1)
        far = 4 * ATTN_BLOCK
        ok_p = ci >= ri + jnp.where(m >= 1, 0, far)
        ok_n = ci + jnp.where(m <= SEQ // ATTN_BLOCK - 2, 0, far) <= ri
        attend([(kc_ref, vc_ref, None), (kp_ref, vp_ref, ok_p), (kk_ref, vk_ref, None), (kn_ref, vn_ref, ok_n)])

    if q_lo == 0:
        @pl.when(qi < 2)
        def _():
            attend([(kc_ref, vc_ref, None)])


def _attention(sink, qa, ka, va, need_ctx):
    q_lo = 0 if need_ctx else 2
    nq = SEQ // ATTN_BLOCK + (2 if need_ctx else 0)
    nb = SEQ // ATTN_BLOCK
    lat0 = RC // ATTN_BLOCK
    out0 = 0 if need_ctx else lat0

    def qrow(b, qi):
        return jnp.where(qi < 2, b * 2 + qi, lat0 + b * nb + qi - 2)

    def loc(delta):
        def f(b, i):
            m = jnp.clip(i + q_lo - 2 + delta, 0, nb - 1)
            return (0, lat0 + b * nb + m, 0)
        return f

    def kv_specs(width):
        ctx_spec = pl.BlockSpec((ATTN_KV_HEADS, CTX_LEN, width), lambda b, i: (0, b, 0))
        loc_spec = lambda d: pl.BlockSpec((ATTN_KV_HEADS, ATTN_BLOCK, width), loc(d))
        return [ctx_spec, loc_spec(-1), loc_spec(0), loc_spec(1)]

    return pl.pallas_call(
        functools.partial(_attn_body, q_lo=q_lo),
        grid=(BATCH, nq),
        in_specs=[pl.BlockSpec(memory_space=pltpu.SMEM),
                  pl.BlockSpec((ATTN_BLOCK, W), lambda b, i: (qrow(b, i + q_lo), 0))]
                 + kv_specs(HEAD_DIM) + kv_specs(128),
        out_specs=pl.BlockSpec((ATTN_BLOCK, W), lambda b, i: (qrow(b, i + q_lo) - out0, 0)),
        out_shape=jax.ShapeDtypeStruct((R - out0 * ATTN_BLOCK, W), BF16),
        compiler_params=_params(("arbitrary", "arbitrary")),
        name="window_attn",
    )(sink, qa, ka, ka, ka, ka, va, va, va, va)


def _fourier_body(*refs, has_ctx):
    if has_ctx:
        uc_ref, ul_ref, bdc_ref, bds_ref, w2_ref, w2c_ref, oc_ref, ol_ref, as_ref = refs
    else:
        ul_ref, bdc_ref, bds_ref, w2_ref, ol_ref, as_ref = refs
    j = pl.program_id(1)
    first = 1 if has_ctx else 0

    if has_ctx:
        @pl.when(j == 0)
        def _():
            u = uc_ref[...]
            a = _dot(u, bdc_ref[...]).astype(BF16)
            s = _dot(u, bds_ref[...]).astype(BF16)
            z = _dot(w2c_ref[...], jnp.concatenate([a, s], axis=0))
            oc_ref[...] = (z * ((CTX_LEN * FOURIER_DIM) ** -0.5)).astype(BF16)

    @pl.when(j == first)
    def _():
        u = ul_ref[...]
        as_ref[0:SEQ, :] = _dot(u, bdc_ref[...]).astype(BF16)
        as_ref[SEQ:2 * SEQ, :] = _dot(u, bds_ref[...]).astype(BF16)

    @pl.when(j >= first)
    def _():
        ol_ref[...] = (_dot(w2_ref[...], as_ref[...]) * ((SEQ * FOURIER_DIM) ** -0.5)).astype(BF16)


def _fourier(fu, dft, need_ctx):
    bdc, bds, w2, w2c = dft
    tr = TR_FOURIER
    nj = SEQ // tr
    first = 1 if need_ctx else 0
    full = lambda a: pl.BlockSpec(a.shape, lambda *_: (0,) * a.ndim)
    lat_tile = lambda j: jnp.maximum(j - first, 0)
    ul_spec = pl.BlockSpec((SEQ, W), lambda b, j: (1 + b, 0))
    w2_spec = pl.BlockSpec((tr, 2 * SEQ), lambda b, j: (lat_tile(j), 0))
    ol_spec = pl.BlockSpec((tr, W), lambda b, j: (b * nj + lat_tile(j), 0))
    ol_shape = jax.ShapeDtypeStruct((RL, W), BF16)
    if need_ctx:
        in_specs = [pl.BlockSpec((CTX_LEN, W), lambda b, j: (b, 0)), ul_spec, full(bdc), full(bds), w2_spec,
                    full(w2c)]
        out_specs = [pl.BlockSpec((CTX_LEN, W), lambda b, j: (b, 0)), ol_spec]
        out_shape = [jax.ShapeDtypeStruct((RC, W), BF16), ol_shape]
        args = (fu, fu, bdc, bds, w2, w2c)
    else:
        in_specs = [ul_spec, full(bdc), full(bds), w2_spec]
        out_specs = [ol_spec]
        out_shape = [ol_shape]
        args = (fu, bdc, bds, w2)
    outs = pl.pallas_call(
        functools.partial(_fourier_body, has_ctx=need_ctx),
        grid=(BATCH, nj + first),
        in_specs=in_specs,
        out_specs=out_specs,
        out_shape=out_shape,
        scratch_shapes=[pltpu.VMEM((2 * SEQ, W), BF16)],
        compiler_params=_params(("arbitrary", "arbitrary")),
        name="fourier_mix",
    )(*args)
    return (outs[0], outs[1]) if need_ctx else (None, outs[0])


def _retention_body(qc_ref, kc_ref, vc_ref, ql_ref, kl_ref, vl_ref, g_ref,
                    dcomb_ref, xif_ref, ztf_ref, xib_ref, ztb_ref, gf_ref, gb_ref, mbd_ref, avg_ref,
                    o_ref, os_ref, st_ref):
    j = pl.program_id(1)
    C = RET_CHUNK
    nl = SEQ // C

    def group_parts(q, k, v, gi):
        sl = slice(gi * 256, (gi + 1) * 256)
        return q[:, sl], k[:, sl], v[:, sl], sl

    own = ((lax.broadcasted_iota(jnp.int32, (4 * C, 256), 0) >> 7)
           == (lax.broadcasted_iota(jnp.int32, (4 * C, 256), 1) >> 6))

    def chunk_fwd(q, k, v, r0):
        for gi in range(2):
            q4, k4, v4, sl = group_parts(q, k, v, gi)
            s_prev = st_ref[gi]
            q4f = q4.astype(F32)
            o4 = _dot((q4f * xif_ref[:, sl]).astype(BF16), s_prev.astype(BF16))
            qstack = jnp.where(own, jnp.concatenate([q4f] * 4, axis=0), 0.0).astype(BF16)
            p = (_dot_nt(qstack, k4) * dcomb_ref[gi]).astype(BF16)
            ov = jnp.where(own, _dot(p, v4), 0.0)
            intra = ov[0:C] + ov[C:2 * C] + ov[2 * C:3 * C] + ov[3 * C:4 * C]
            os_ref[pl.ds(r0, C), sl] = o4 + intra
            u = _dot_tn(k4, (v4.astype(F32) * ztf_ref[:, sl]).astype(BF16))
            st_ref[gi] = gf_ref[gi] * s_prev + mbd_ref[...] * u

    def chunk_bwd(q, k, v, r0):
        for gi in range(2):
            q4, k4, v4, sl = group_parts(q, k, v, gi)
            s_prev = st_ref[gi]
            cross = _dot((q4.astype(F32) * xib_ref[:, sl]).astype(BF16), s_prev.astype(BF16))
            os_ref[pl.ds(r0, C), sl] = os_ref[pl.ds(r0, C), sl] + cross
            u = _dot_tn(k4, (v4.astype(F32) * ztb_ref[:, sl]).astype(BF16))
            st_ref[gi] = gb_ref[gi] * s_prev + mbd_ref[...] * u

    @pl.when(j == 0)
    def _():
        st_ref[...] = jnp.zeros_like(st_ref)
        for c in range(CTX_LEN // C):
            rs = slice(c * C, (c + 1) * C)
            chunk_fwd(qc_ref[rs, :], kc_ref[rs, :], vc_ref[rs, :], c * C)

        def fbody(c, carry):
            r0 = pl.multiple_of(c * C, C)
            rs = pl.ds(r0, C)
            chunk_fwd(ql_ref[rs, :], kl_ref[rs, :], vl_ref[rs, :], CTX_LEN + r0)
            return carry

        lax.fori_loop(0, nl, fbody, 0)

        st_ref[...] = jnp.zeros_like(st_ref)
        for c in reversed(range(CTX_LEN // C)):
            rs = slice(c * C, (c + 1) * C)
            chunk_bwd(qc_ref[rs, :], kc_ref[rs, :], vc_ref[rs, :], c * C)

        def bbody(t, carry):
            r0 = pl.multiple_of((nl - 1 - t) * C, C)
            rs = pl.ds(r0, C)
            chunk_bwd(ql_ref[rs, :], kl_ref[rs, :], vl_ref[rs, :], CTX_LEN + r0)
            return carry

        lax.fori_loop(0, nl, bbody, 0)

    o = os_ref[pl.ds(pl.multiple_of(j * 256, 256), 256), :]
    avg = avg_ref[...]
    oh, ol = _split(o)
    mu = _dot(oh, avg) + _dot(ol, avg)
    d = o - mu
    var = _dot((d * d).astype(BF16), avg)
    o_ref[...] = (g_ref[...].astype(F32) * d * lax.rsqrt(var + GN_EPS)).astype(BF16)


def _retention(qr, kr, vr, gr, rtabs):
    nj = 1 + SEQ // 256

    def out_map(b, j):
        return (jnp.where(j == 0, b, RC // 256 + b * (SEQ // 256) + j - 1), 0)

    ctx = pl.BlockSpec((CTX_LEN, W), lambda b, j: (b, 0))
    lat = pl.BlockSpec((SEQ, W), lambda b, j: (1 + b, 0))
    full = lambda a: pl.BlockSpec(a.shape, lambda *_: (0,) * a.ndim)
    avg = jnp.asarray(np.kron(np.eye(RET_HEADS), np.full((RET_DK, RET_DK), 1.0 / RET_DK)).astype(np.float32)).astype(BF16)
    mbd = jnp.asarray(np.kron(np.eye(4), np.ones((RET_DK, RET_DK))).astype(np.float32))
    tabs = list(rtabs) + [mbd, avg]
    return pl.pallas_call(
        _retention_body,
        grid=(BATCH, nj),
        in_specs=[ctx, ctx, ctx, lat, lat, lat, pl.BlockSpec((256, W), out_map)] + [full(t) for t in tabs],
        out_specs=pl.BlockSpec((256, W), out_map),
        out_shape=jax.ShapeDtypeStruct((R, W), BF16),
        scratch_shapes=[pltpu.VMEM((CTX_LEN + SEQ, W), F32), pltpu.VMEM((2, 256, 256), F32)],
        compiler_params=_params(("arbitrary", "arbitrary")),
        name="retention",
    )(qr, kr, vr, qr, kr, vr, gr, *tabs)


def _pack_pair(a, b):
    ua = lax.bitcast_convert_type(a, jnp.uint32) >> 16
    ub = lax.bitcast_convert_type(b, jnp.uint32) & jnp.uint32(0xFFFF0000)
    return ua | ub


def _unpack_pair(w):
    a = lax.bitcast_convert_type(w << 16, F32)
    b = lax.bitcast_convert_type(w & jnp.uint32(0xFFFF0000), F32)
    return a, b


def _slot_onehot(s0, s1, n):
    srow = lax.broadcasted_iota(jnp.int32, (n, s0.shape[1]), 0)
    p0 = jnp.where(srow == s0, 1.0, 0.0).astype(BF16)
    p1 = jnp.where(srow == s1, 1.0, 0.0).astype(BF16)
    return p0, p1


def _pick(first, a_ref, b_ref):
    a = a_ref[...]
    flag = jnp.zeros(a.shape, jnp.int32) + first.astype(jnp.int32)
    return jnp.where(flag > 0, a, b_ref[...])


def _merge_body(*refs, layer0):
    if layer0:
        oa_ref, ofc_ref, ofl_ref, rt_ref, gm_ref, xc_ref, xl_ref = refs[:7]
        rest = refs[7:]
        is_ctx = pl.program_id(0) < RC // TM_MERGE
        of_in = _pick(is_ctx, ofc_ref, ofl_ref)
        x_in = _pick(is_ctx, xc_ref, xl_ref)
    else:
        oa_ref, ofl_ref, rt_ref, gm_ref, x_ref = refs[:5]
        rest = refs[5:]
        of_in = ofl_ref[...]
        x_in = x_ref[...]
    (mod_ref, gn_ref, wba_ref, wbf_ref, wbr_ref, wout_ref, wrh_ref, wrl_ref, br_ref, tri_ref, ltri_ref,
     xo_ref, xs_ref, ro_ref, nch_ref) = rest
    gm = gm_ref[...].astype(F32)
    z = (gm[:, 0:D_MODEL] * _dot(oa_ref[...], wba_ref[...])
         + gm[:, D_MODEL:2 * D_MODEL] * _dot(of_in, wbf_ref[...])
         + gm[:, 2 * D_MODEL:3 * D_MODEL] * _dot(rt_ref[...], wbr_ref[...]))
    y = _dot(z.astype(BF16), wout_ref[...])
    x = x_in + mod_ref[2:3, :] * y
    xo_ref[...] = x
    ms = jnp.mean(x * x, axis=-1, keepdims=True)
    hn = x * lax.rsqrt(ms + NORM_EPS) * gn_ref[...]
    h2 = hn * (1.0 + mod_ref[4:5, :]) + mod_ref[3:4, :]
    hh, hl = _split(h2)
    wh, wl = wrh_ref[...], wrl_ref[...]
    lg = _dot_nt(wh, hh) + _dot_nt(wh, hl) + _dot_nt(wl, hh) + br_ref[...]
    tm = lg.shape[1]
    row8 = lax.broadcasted_iota(jnp.int32, (8, tm), 0)
    lgg = lg[0:8, :]
    mg = jnp.max(lgg, axis=0, keepdims=True)
    grp = jnp.min(jnp.where(lgg == mg, row8, 8), axis=0, keepdims=True)
    pg = 1.0 / jnp.sum(jnp.exp(lgg - mg), axis=0, keepdims=True)
    lin = jnp.zeros((8, tm), F32)
    for g in range(N_GROUPS):
        lin = jnp.where(grp == g, lg[8 + 8 * g:16 + 8 * g, :], lin)
    v1 = jnp.max(lin, axis=0, keepdims=True)
    i1 = jnp.min(jnp.where(lin == v1, row8, 8), axis=0, keepdims=True)
    rest = jnp.where(row8 == i1, -jnp.inf, lin)
    v2 = jnp.max(rest, axis=0, keepdims=True)
    i2 = jnp.min(jnp.where(rest == v2, row8, 8), axis=0, keepdims=True)
    e2 = jnp.exp(v2 - v1)
    w1 = pg / (1.0 + e2)
    w2 = pg * e2 / (1.0 + e2)
    e_1 = grp * EXPERTS_PER_GROUP + i1
    e_2 = grp * EXPERTS_PER_GROUP + i2

    row32 = lax.broadcasted_iota(jnp.int32, (N_EXPERTS, tm), 0)
    oh0 = jnp.where(row32 == e_1, 1.0, 0.0)
    oh1 = jnp.where(row32 == e_2, 1.0, 0.0)
    tri = tri_ref[...]
    cum0 = _dot(oh0.astype(BF16), tri)
    cum1 = _dot(oh1.astype(BF16), tri)
    tot0 = jnp.sum(oh0, axis=1, keepdims=True)
    tot1 = jnp.sum(oh1, axis=1, keepdims=True)
    nch = ((tot0 + tot1).astype(jnp.int32) + (CHUNK - 1)) >> CHUNK_SHIFT
    nch_b = jnp.broadcast_to(nch.astype(F32), (N_EXPERTS, 128))
    nch_ref[...] = nch_b.astype(jnp.int32)
    base = CHUNK * _dot(ltri_ref[...], nch_b.astype(BF16))[:, 0:1]
    s0 = jnp.sum(oh0 * (base + cum0), axis=0, keepdims=True).astype(jnp.int32)
    s1 = jnp.sum(oh1 * (base + tot0 + cum1), axis=0, keepdims=True).astype(jnp.int32)
    p0, p1 = _slot_onehot(s0, s1, SLOTS)
    xs = _dot(p0 + p1, hh)
    xs_ref[:, 0:D_MODEL // 2] = _pack_pair(xs[:, 0:D_MODEL // 2], xs[:, D_MODEL // 2:D_MODEL])

    def wrows(w):
        hi, lo = _split(w)
        return jnp.where(row8 == 0, hi.astype(F32), jnp.where(row8 == 1, lo.astype(F32), 0.0)).astype(BF16)

    wc = _dot_nt(p0, wrows(w1)) + _dot_nt(p1, wrows(w2))
    wcol = jnp.broadcast_to(wc[:, 0:1] + wc[:, 1:2], (SLOTS, 128))
    xs_ref[:, D_MODEL // 2:XS_COLS] = lax.bitcast_convert_type(wcol, jnp.uint32)
    s0f, s1f = s0.astype(F32), s1.astype(F32)
    ro_ref[...] = jnp.where(row8 == 0, s0f, jnp.where(row8 == 1, s1f, 0.0))


def _merge(oa, of_c, of_l, ret, gm, xs_in, mod_l, gnorm, wba, wbf, wbr, wout, wrh, wrl, brb, layer0):
    tm = TM_MERGE
    row0 = 0 if layer0 else RC
    rm = R - row0
    nt = rm // tm
    off = row0 // tm
    nc = RC // tm
    src = lambda n: pl.BlockSpec((tm, n), lambda i: (i + off, 0))
    dst = lambda n: pl.BlockSpec((tm, n), lambda i: (i, 0))
    ctx_rows = lambda n: pl.BlockSpec((tm, n), lambda i: (jnp.minimum(i, nc - 1), 0))
    lat_rows = lambda n: pl.BlockSpec((tm, n), lambda i: (jnp.maximum(i - nc, 0), 0))
    full = lambda a: pl.BlockSpec(a.shape, lambda *_: (0,) * a.ndim, pipeline_mode=pl.Buffered(1))
    mrow = _mod_row(row0, tm)
    tri = jnp.asarray(np.triu(np.ones((tm, tm), np.float32), 1)).astype(BF16)
    ltri = jnp.asarray(np.tril(np.ones((N_EXPERTS, N_EXPERTS), np.float32), -1)).astype(BF16)
    if layer0:
        acts = [oa, of_c, of_l, ret, gm, xs_in[0], xs_in[1]]
        act_specs = [src(W), ctx_rows(W), lat_rows(W), src(W), src(3 * D_MODEL), ctx_rows(D_MODEL),
                     lat_rows(D_MODEL)]
    else:
        acts = [oa, of_l, ret, gm, xs_in]
        act_specs = [dst(W), dst(W), src(W), src(3 * D_MODEL), src(D_MODEL)]
    return pl.pallas_call(
        functools.partial(_merge_body, layer0=layer0),
        grid=(nt,),
        in_specs=act_specs + [
                  pl.BlockSpec((None, 6, D_MODEL), lambda i: (mrow(i), 0, 0)),
                  full(gnorm), full(wba), full(wbf), full(wbr), full(wout), full(wrh), full(wrl), full(brb),
                  full(tri), full(ltri)],
        out_specs=[dst(D_MODEL), pl.BlockSpec((SLOTS, XS_COLS), lambda i: (i, 0)),
                   pl.BlockSpec((8, tm), lambda i: (0, i)),
                   pl.BlockSpec((None, N_EXPERTS, 128), lambda i: (i, 0, 0))],
        out_shape=[jax.ShapeDtypeStruct((rm, D_MODEL), F32),
                   jax.ShapeDtypeStruct((nt * SLOTS, XS_COLS), jnp.uint32),
                   jax.ShapeDtypeStruct((8, rm), F32),
                   jax.ShapeDtypeStruct((nt, N_EXPERTS, 128), jnp.int32)],
        compiler_params=_params(("arbitrary",)),
        name="merge_router",
    )(*acts, mod_l, gnorm, wba, wbf, wbr, wout, wrh, wrl, brb, tri, ltri)


def _moe_plan(nch, nb):
    nt = nch.shape[0]
    choff = jnp.cumsum(nch, axis=1) - nch
    used_ch = jnp.sum(nch, axis=1)
    cum_t = jnp.cumsum(nch, axis=0)
    tot = cum_t[-1]
    ptot = (tot + CPB - 1) // CPB * CPB
    pend = jnp.cumsum(ptot)
    pstart = pend - ptot
    n_used = pend[-1] // CPB
    blk = jnp.arange(nb, dtype=jnp.int32)
    lane = jnp.arange(CPB, dtype=jnp.int32)
    blk_e = jnp.minimum(jnp.sum((blk[:, None] * CPB >= pend[None, :]).astype(jnp.int32), axis=1), N_EXPERTS - 1)
    oe = (blk_e[:, None] == jnp.arange(N_EXPERTS, dtype=jnp.int32)[None, :]).astype(jnp.int32)
    sel = lambda tab: jnp.sum(oe[:, :, None] * tab.T[None, :, :], axis=1)
    pstart_b = jnp.sum(oe * pstart[None, :], axis=1)
    tot_b = jnp.sum(oe * tot[None, :], axis=1)
    cum_b, nch_b, choff_b = sel(cum_t), sel(nch), sel(choff)
    i = blk[:, None] * CPB + lane[None, :] - pstart_b[:, None]
    valid = (i < tot_b[:, None]) & (blk[:, None] < n_used)
    t = jnp.minimum(jnp.sum((i[:, :, None] >= cum_b[:, None, :]).astype(jnp.int32), axis=2), nt - 1)
    tiles = jnp.arange(nt, dtype=jnp.int32)[None, None, :]
    before = jnp.sum(jnp.where(tiles < t[:, :, None], nch_b[:, None, :], 0), axis=2)
    coff = jnp.sum(jnp.where(tiles == t[:, :, None], choff_b[:, None, :], 0), axis=2)
    row = t * SLOTS + CHUNK * (coff + i - before)
    src = jnp.where(valid, row, SLOTS - CHUNK)
    dummy = nt * SLOTS + CHUNK * ((blk[:, None] % 2) * CPB + lane[None, :])
    dst = jnp.where(valid, row, dummy)
    blk_start = jnp.concatenate([pstart, pend[-1:]]) // CPB
    return (blk_start.astype(jnp.int32), n_used.astype(jnp.int32).reshape(1), src.reshape(-1).astype(jnp.int32),
            dst.reshape(-1).astype(jnp.int32), used_ch.astype(jnp.int32))


def _ffn_body(bs_ref, nu_ref, src_ref, dst_ref, uc_ref, xs_ref, wg_ref, wu_ref, wd_ref, ys_ref,
              xbuf, ybuf, zbuf, wgb, wub, wdb, sem_in, sem_out, sem_z, *, nt):
    e = pl.program_id(0)
    nu = nu_ref[0]
    half = D_MODEL // 2

    def gather(blk, sl):
        for c in range(CPB):
            r = pl.multiple_of(src_ref[blk * CPB + c], CHUNK)
            pltpu.make_async_copy(xs_ref.at[pl.ds(r, CHUNK)], xbuf.at[sl, pl.ds(c * CHUNK, CHUNK)],
                                  sem_in.at[sl]).start(priority=1)

    def scatter(blk, sl):
        for c in range(CPB):
            r = pl.multiple_of(dst_ref[blk * CPB + c], CHUNK)
            pltpu.make_async_copy(ybuf.at[sl, pl.ds(c * CHUNK, CHUNK)], ys_ref.at[pl.ds(r, CHUNK)],
                                  sem_out.at[sl]).start(priority=1)

    def wait_gather(sl):
        pltpu.make_async_copy(xs_ref.at[pl.ds(0, MOE_BM)], xbuf.at[sl], sem_in.at[sl]).wait()

    def wait_scatter(sl):
        pltpu.make_async_copy(ybuf.at[sl], ys_ref.at[pl.ds(0, MOE_BM)], sem_out.at[sl]).wait()

    def zero_copy(r):
        return pltpu.make_async_copy(zbuf, ys_ref.at[pl.ds(pl.multiple_of(r, CHUNK), CHUNK)], sem_z)

    @pl.when(e == 0)
    def _():
        zbuf[...] = jnp.zeros_like(zbuf)

        def tails(fn):
            def per_tile(t, carry):
                def per_chunk(c, carry2):
                    fn(t * SLOTS + c * CHUNK)
                    return carry2
                lax.fori_loop(uc_ref[t], SLOTS // CHUNK, per_chunk, 0)
                return carry
            lax.fori_loop(0, nt, per_tile, 0)
            for c in range(2 * CPB):
                fn(nt * SLOTS + c * CHUNK)

        tails(lambda r: zero_copy(r).start())
        tails(lambda r: zero_copy(r).wait())
        gather(0, 0)

    b0, b1 = bs_ref[e], bs_ref[e + 1]

    @pl.when(b1 > b0)
    def _():
        wgb[...] = wg_ref[...].astype(BF16)
        wub[...] = wu_ref[...].astype(BF16)
        wdb[...] = wd_ref[...].astype(BF16)

        def block(b, carry):
            slot = b % 2

            @pl.when(b + 1 < nu)
            def _():
                gather(b + 1, 1 - slot)

            wait_gather(slot)

            @pl.when(b >= 2)
            def _():
                wait_scatter(slot)

            xw = xbuf[slot]
            xa, xb = _unpack_pair(xw[:, 0:half])
            x = jnp.concatenate([xa, xb], axis=1).astype(BF16)
            wt = lax.bitcast_convert_type(xw[:, half:XS_COLS], F32)
            g = _dot(x, wgb[...])
            u = _dot(x, wub[...])
            hmid = (g * _sigmoid(g) * u).astype(BF16)
            y = _dot(hmid, wdb[...]) * jnp.concatenate([wt] * (D_MODEL // 128), axis=1)
            yb = y.astype(BF16).astype(F32)
            ybuf[slot] = _pack_pair(yb[:, 0:half], yb[:, half:D_MODEL])
            scatter(b, slot)
            return carry

        lax.fori_loop(b0, b1, block, 0)

    @pl.when(e == N_EXPERTS - 1)
    def _():
        wait_scatter((nu - 1) % 2)

        @pl.when(nu >= 2)
        def _():
            wait_scatter(nu % 2)


def _ffn(plan, xs, w_g, w_u, w_d, layer, nt):
    wmap = lambda e, *_: (layer, e, 0, 0)
    half = D_MODEL // 2
    grid_spec = pltpu.PrefetchScalarGridSpec(
        num_scalar_prefetch=5,
        grid=(N_EXPERTS,),
        in_specs=[pl.BlockSpec(memory_space=pl.ANY),
                  pl.BlockSpec((None, None, D_MODEL, EXPERT_HIDDEN), wmap),
                  pl.BlockSpec((None, None, D_MODEL, EXPERT_HIDDEN), wmap),
                  pl.BlockSpec((None, None, EXPERT_HIDDEN, D_MODEL), wmap)],
        out_specs=pl.BlockSpec(memory_space=pl.ANY),
        scratch_shapes=[pltpu.VMEM((2, MOE_BM, XS_COLS), jnp.uint32), pltpu.VMEM((2, MOE_BM, half), jnp.uint32),
                        pltpu.VMEM((CHUNK, half), jnp.uint32),
                        pltpu.VMEM((D_MODEL, EXPERT_HIDDEN), BF16), pltpu.VMEM((D_MODEL, EXPERT_HIDDEN), BF16),
                        pltpu.VMEM((EXPERT_HIDDEN, D_MODEL), BF16),
                        pltpu.SemaphoreType.DMA((2,)), pltpu.SemaphoreType.DMA((2,)), pltpu.SemaphoreType.DMA(())],
    )
    return pl.pallas_call(
        functools.partial(_ffn_body, nt=nt),
        grid_spec=grid_spec,
        out_shape=jax.ShapeDtypeStruct((nt * SLOTS + 2 * CPB * CHUNK, half), jnp.uint32),
        compiler_params=_params(("arbitrary",)),
        name="moe_experts",
    )(*plan, xs, w_g, w_u, w_d)


def _combine_body(ys_ref, ro_ref, x_ref, mod_ref, gn_ref, o_ref, *, final):
    s = ro_ref[...]
    p0, p1 = _slot_onehot(s[0:1, :].astype(jnp.int32), s[1:2, :].astype(jnp.int32), SLOTS)
    ya, yb = _unpack_pair(ys_ref[...])
    y = jnp.concatenate([ya, yb], axis=1).astype(BF16)
    f = _dot_tn(p0 + p1, y)
    x = x_ref[...] + mod_ref[5:6, :] * f
    if final:
        ms = jnp.mean(x * x, axis=-1, keepdims=True)
        x = x * lax.rsqrt(ms + NORM_EPS) * gn_ref[...]
    o_ref[...] = x


def _combine(ys, route, x, mod_l, gnorm, row0, final):
    tm = TM_MERGE
    rm = x.shape[0]
    mrow = _mod_row(row0, tm)
    return pl.pallas_call(
        functools.partial(_combine_body, final=final),
        grid=(rm // tm,),
        in_specs=[pl.BlockSpec((SLOTS, D_MODEL // 2), lambda i: (i, 0)),
                  pl.BlockSpec((8, tm), lambda i: (0, i)),
                  pl.BlockSpec((tm, D_MODEL), lambda i: (i, 0)),
                  pl.BlockSpec((None, 6, D_MODEL), lambda i: (mrow(i), 0, 0)),
                  pl.BlockSpec((1, D_MODEL), lambda i: (0, 0))],
        out_specs=pl.BlockSpec((tm, D_MODEL), lambda i: (i, 0)),
        out_shape=jax.ShapeDtypeStruct((rm, D_MODEL), F32),
        compiler_params=_params(("arbitrary",)),
        name="moe_combine",
    )(ys, route, x, mod_l, gnorm)


def _moe(xs, route, nch3, x, mod_l, gnorm, w_g, w_u, w_d, row0, final, layer):
    nt = nch3.shape[0]
    max_chunks = nt * ((2 * TM_MERGE + N_EXPERTS * (CHUNK - 1)) // CHUNK)
    nb = -(-max_chunks // CPB) + N_EXPERTS
    plan = _moe_plan(nch3[:, :, 0], nb)
    ys = _ffn(plan, xs, w_g, w_u, w_d, layer, nt)
    return _combine(ys, route, x, mod_l, gnorm, row0, final)


def kernel(x, c, ctx, c_ctx, norm_mix, norm_ffn, w_ada, b_ada, w_in, attn_sink, ret_decay_fwd, ret_decay_bwd,
           w_branch_attn, w_branch_fourier, w_branch_ret, w_out, w_router_group, b_router_group,
           w_router_expert, b_router_expert, w_exp_gate, w_exp_up, w_exp_down, norm_final):
    tabs = [jnp.asarray(t) for t in _rope_tables()]
    dft = [jnp.asarray(t).astype(BF16) for t in _dft_tables()]

    cc = jnp.zeros((MOD_ROWS, D_MODEL), F32).at[0:BATCH].set(c).at[CTX_MOD_ROW].set(c_ctx)
    mod = _ada(cc, w_ada, b_ada).reshape(DEPTH, MOD_ROWS, 6, D_MODEL)

    xf = (ctx.reshape(RC, D_MODEL), x.reshape(RL, D_MODEL))
    w_in_bf = w_in.astype(BF16)
    for l in range(DEPTH):
        need_ctx = l < DEPTH - 1
        row0 = 0 if need_ctx else RC
        mod_l = mod[l]
        qa, ka, va, qr, kr, vr, gr, fu, gm = _proj(xf, mod_l, norm_mix[l][None, :], w_in_bf, tabs, l)
        oa = _attention(attn_sink[l], qa, ka, va, need_ctx)
        of_c, of_l = _fourier(fu, dft, need_ctx)
        ret = _retention(qr, kr, vr, gr, _retention_tables(ret_decay_fwd[l], ret_decay_bwd[l]))
        wr = jnp.zeros((ROUTER_ROWS, D_MODEL), F32)
        wr = wr.at[0:N_GROUPS].set(w_router_group[l].T).at[8:8 + N_EXPERTS].set(w_router_expert[l].T)
        br = jnp.full((ROUTER_ROWS,), NEG, F32)
        br = br.at[0:N_GROUPS].set(b_router_group[l]).at[8:8 + N_EXPERTS].set(b_router_expert[l])
        wrh, wrl = _split(wr)
        brb = jnp.broadcast_to(br[:, None], (ROUTER_ROWS, TM_MERGE))
        x_mid, xs, route, nch3 = _merge(oa, of_c, of_l, ret, gm, xf, mod_l, norm_ffn[l][None, :],
                                        w_branch_attn[l].astype(BF16), w_branch_fourier[l].astype(BF16),
                                        w_branch_ret[l].astype(BF16), w_out[l].astype(BF16), wrh, wrl, brb,
                                        need_ctx)
        final = l == DEPTH - 1
        xf = _moe(xs, route, nch3, x_mid, mod_l, norm_final[None, :], w_exp_gate, w_exp_up, w_exp_down,
                  row0, final, l)
    return xf.reshape(BATCH, SEQ, D_MODEL)
```

```python
import functools

import numpy as np
import jax
import jax.numpy as jnp
from jax import lax
from jax.experimental import pallas as pl
from jax.experimental.pallas import tpu as pltpu

F32 = jnp.float32
BF16 = jnp.bfloat16

D_MODEL = 1024
BATCH = 8
SEQ = 2048
DEPTH = 2
CTX_LEN = 256
GRID_W = 64
HEAD_DIM = 64
ATTN_HEADS = 8
ATTN_KV_HEADS = 2
ATTN_GROUP = ATTN_HEADS // ATTN_KV_HEADS
ATTN_BLOCK = 128
RET_HEADS = 8
RET_DK = 64
RET_CHUNK = 128
FOURIER_GROUPS = 4
FOURIER_DIM = 128
N_GROUPS = 4
EXPERTS_PER_GROUP = 8
N_EXPERTS = N_GROUPS * EXPERTS_PER_GROUP
EXPERT_HIDDEN = 512
ROPE_BASE = 10000.0
NORM_EPS = 1e-6
GN_EPS = 1e-5

W = 512
IN_COLS = 6400
RC = BATCH * CTX_LEN
RL = BATCH * SEQ
R = RC + RL
MOD_ROWS = 16
CTX_MOD_ROW = 8

VMEM_LIMIT = 52 * 1024 * 1024

TM_PROJ = 512
TM_MERGE = 512
TN_ADA = 1536
TR_FOURIER = 512
MOE_BM = 512
CHUNK = 8
CHUNK_SHIFT = 3
CPB = MOE_BM // CHUNK
SLOTS = 1280
XS_COLS = D_MODEL // 2 + 128
NEG = -1e30
LOG2E = 1.4426950408889634
ROUTER_ROWS = 40


def _dot(a, b):
    return jnp.dot(a, b, preferred_element_type=F32)


def _dot_nt(a, b):
    return lax.dot_general(a, b, (((1,), (1,)), ((), ())), preferred_element_type=F32)


def _dot_tn(a, b):
    return lax.dot_general(a, b, (((0,), (0,)), ((), ())), preferred_element_type=F32)


def _split(x):
    hi = x.astype(BF16)
    lo = (x - hi.astype(F32)).astype(BF16)
    return hi, lo


def _sigmoid(x):
    return 1.0 / (1.0 + jnp.exp(-x))


def _params(sem, vmem=VMEM_LIMIT):
    return pltpu.CompilerParams(dimension_semantics=sem, vmem_limit_bytes=vmem)


def _mod_row(row0, tm):
    def f(i):
        g0 = i * tm + row0
        return jnp.where(g0 < RC, CTX_MOD_ROW, (g0 - RC) // SEQ)
    return f


def _rope_tables():
    pos = np.arange(SEQ, dtype=np.float64)
    row = np.floor(pos / GRID_W)
    col = pos % GRID_W

    def cs(p, nf):
        inv = ROPE_BASE ** (-np.arange(nf, dtype=np.float64) / nf)
        ang = p[:, None] * inv[None, :]
        return np.cos(ang), np.sin(ang)

    rc, rs = cs(row, HEAD_DIM // 4)
    cc, cs_ = cs(col, HEAD_DIM // 4)
    cos_a = np.concatenate([rc, rc, cc, cc], axis=1)
    sin_a = np.concatenate([-rs, rs, -cs_, cs_], axis=1)
    tc, ts = cs(pos, RET_DK // 2)
    cos_r = np.concatenate([tc, tc], axis=1)
    sin_r = np.concatenate([-ts, ts], axis=1)

    def full(t, ident):
        t2 = np.concatenate([t, t], axis=1)
        return np.concatenate([np.full_like(t2, ident), t2], axis=0).astype(np.float32)

    return full(cos_a, 1.0), full(sin_a, 0.0), full(cos_r, 1.0), full(sin_r, 0.0)


def _dft_tables():
    def cs(n):
        k = np.arange(n, dtype=np.int64)
        m = (k[:, None] * k[None, :]) % n
        ang = 2.0 * np.pi * m.astype(np.float64) / n
        return np.cos(ang), np.sin(ang)

    c128, s128 = cs(FOURIER_DIM)
    eye = np.eye(FOURIER_GROUPS)
    bdc = np.kron(eye, c128).astype(np.float32)
    bds = np.kron(eye, s128).astype(np.float32)
    cn, sn = cs(SEQ)
    w2 = np.concatenate([cn, -sn], axis=1).astype(np.float32)
    cl, sl = cs(CTX_LEN)
    w2c = np.concatenate([cl, -sl], axis=1).astype(np.float32)
    return bdc, bds, w2, w2c


def _retention_tables(dec_f, dec_b):
    lg_f = jax.nn.log_sigmoid(dec_f.astype(F32))
    lg_b = jax.nn.log_sigmoid(dec_b.astype(F32))
    i = jnp.arange(RET_CHUNK)
    diff = (i[:, None] - i[None, :]).astype(F32)
    fwd = jnp.exp(jnp.maximum(diff, 0.0)[None] * lg_f[:, None, None])
    bwd = jnp.exp(jnp.maximum(-diff, 0.0)[None] * lg_b[:, None, None])
    dcomb = jnp.where((diff >= 0)[None], fwd, bwd).reshape(2, 4 * RET_CHUNK, RET_CHUNK)
    fi = i.astype(F32)
    lanes = lambda t: jnp.repeat(t, RET_DK, axis=1)
    xi_f = lanes(jnp.exp((fi + 1.0)[:, None] * lg_f[None, :]))
    zt_f = lanes(jnp.exp((RET_CHUNK - 1 - fi)[:, None] * lg_f[None, :]))
    xi_b = lanes(jnp.exp((RET_CHUNK - fi)[:, None] * lg_b[None, :]))
    zt_b = lanes(jnp.exp(fi[:, None] * lg_b[None, :]))
    g_f = jnp.repeat(jnp.exp(RET_CHUNK * lg_f), RET_DK).reshape(2, 256, 1)
    g_b = jnp.repeat(jnp.exp(RET_CHUNK * lg_b), RET_DK).reshape(2, 256, 1)
    g_f = jnp.broadcast_to(g_f, (2, 256, 256))
    g_b = jnp.broadcast_to(g_b, (2, 256, 256))
    return dcomb, xi_f, zt_f, xi_b, zt_b, g_f, g_b


def _ada_body(c_ref, w_ref, b_ref, o_ref):
    c = c_ref[...]
    s = c * _sigmoid(c)
    sh, sl = _split(s)
    wh, wl = _split(w_ref[...])
    o_ref[...] = _dot(sh, wh) + _dot(sl, wh) + _dot(sh, wl) + b_ref[...]


def _ada(cc, w_ada, b_ada):
    nt = 6 * D_MODEL // TN_ADA
    return pl.pallas_call(
        _ada_body,
        grid=(DEPTH, nt),
        in_specs=[
            pl.BlockSpec((MOD_ROWS, D_MODEL), lambda l, j: (0, 0)),
            pl.BlockSpec((None, D_MODEL, TN_ADA), lambda l, j: (l, 0, j)),
            pl.BlockSpec((None, 1, TN_ADA), lambda l, j: (l, 0, j)),
        ],
        out_specs=pl.BlockSpec((None, MOD_ROWS, TN_ADA), lambda l, j: (l, 0, j)),
        out_shape=jax.ShapeDtypeStruct((DEPTH, MOD_ROWS, 6 * D_MODEL), F32),
        compiler_params=_params(("arbitrary", "arbitrary")),
        name="ada_mod",
    )(cc, w_ada, b_ada.reshape(DEPTH, 1, 6 * D_MODEL))


def _rope(xc, cos, sin, half):
    fwd = pltpu.roll(xc, 128 - half, axis=1)
    bwd = pltpu.roll(xc, half, axis=1)
    lane = lax.broadcasted_iota(jnp.int32, xc.shape, 1)
    first = (lane & (2 * half - 1)) < half
    return xc * cos + jnp.where(first, fwd, bwd) * sin


def _proj_body(*refs, split):
    if split:
        x = _pick(pl.program_id(0) < RC // TM_PROJ, refs[0], refs[1])
        refs = refs[2:]
    else:
        x = refs[0][...]
        refs = refs[1:]
    (mod_ref, gn_ref, w_ref, ca_ref, sa_ref, cr_ref, sr_ref,
     qa_ref, ka_ref, va_ref, qr_ref, kr_ref, vr_ref, gr_ref, fu_ref, gm_ref) = refs
    ms = jnp.mean(x * x, axis=-1, keepdims=True)
    y = x * lax.rsqrt(ms + NORM_EPS) * gn_ref[...]
    h = y * (1.0 + mod_ref[1:2, :]) + mod_ref[0:1, :]
    hb = h.astype(BF16)

    def proj(c0, width):
        return _dot(hb, w_ref[:, c0:c0 + width])

    ca, sa, cr, sr = ca_ref[...], sa_ref[...], cr_ref[...], sr_ref[...]

    qa = proj(0, W) * (HEAD_DIM ** -0.5 * LOG2E)
    for c in range(W // 128):
        qa_ref[:, c * 128:(c + 1) * 128] = _rope(qa[:, c * 128:(c + 1) * 128], ca, sa, 16).astype(BF16)
    kv = proj(W, 256)
    ka = _rope(kv[:, 0:128], ca, sa, 16).astype(BF16)
    ka_ref[0] = ka[:, 0:64]
    ka_ref[1] = ka[:, 64:128]
    va = kv[:, 128:256].astype(BF16)
    ones_col = jnp.where(lax.broadcasted_iota(jnp.int32, (va.shape[0], 64), 1) == 0, 1.0, 0.0).astype(BF16)
    va_ref[0] = jnp.concatenate([va[:, 0:64], ones_col], axis=1)
    va_ref[1] = jnp.concatenate([va[:, 64:128], ones_col], axis=1)
    qr = proj(768, W)
    kr = proj(1280, W) * (RET_DK ** -0.5)
    for c in range(W // 128):
        sl = slice(c * 128, (c + 1) * 128)
        qr_ref[:, sl] = _rope(qr[:, sl], cr, sr, 32).astype(BF16)
        kr_ref[:, sl] = _rope(kr[:, sl], cr, sr, 32).astype(BF16)
    vr_ref[...] = proj(1792, W).astype(BF16)
    g = proj(2304, W)
    gr_ref[...] = (g * _sigmoid(g)).astype(BF16)
    fu_ref[...] = proj(2816, W).astype(BF16)
    for c in range(3):
        gm_ref[:, c * D_MODEL:(c + 1) * D_MODEL] = _sigmoid(proj(3328 + c * D_MODEL, D_MODEL)).astype(BF16)


def _proj(x, mod_l, gnorm, w_in_bf, tabs, layer):
    tm = TM_PROJ
    nt = R // tm
    nc = RC // tm
    split = isinstance(x, tuple)
    if split:
        xs = list(x)
        x_specs = [pl.BlockSpec((tm, D_MODEL), lambda i: (jnp.minimum(i, nc - 1), 0)),
                   pl.BlockSpec((tm, D_MODEL), lambda i: (jnp.maximum(i - nc, 0), 0))]
    else:
        xs = [x]
        x_specs = [pl.BlockSpec((tm, D_MODEL), lambda i: (i, 0))]

    def tab_map(i):
        return (jnp.where(i < nc, i, nc + (i - nc) % (SEQ // tm)), 0)

    row = lambda i: (i, 0)
    wide = lambda n: pl.BlockSpec((tm, n), row)
    kv_spec = lambda n: pl.BlockSpec((2, tm, n), lambda i: (0, i, 0))
    sds = lambda n: jax.ShapeDtypeStruct((R, n), BF16)
    kv_sds = lambda n: jax.ShapeDtypeStruct((2, R, n), BF16)
    mrow = _mod_row(0, tm)
    return pl.pallas_call(
        functools.partial(_proj_body, split=split),
        grid=(nt,),
        in_specs=x_specs + [
            pl.BlockSpec((None, 6, D_MODEL), lambda i: (mrow(i), 0, 0)),
            pl.BlockSpec((1, D_MODEL), lambda i: (0, 0)),
            pl.BlockSpec((None, D_MODEL, IN_COLS), lambda i: (layer, 0, 0), pipeline_mode=pl.Buffered(1)),
        ] + [pl.BlockSpec((tm, 128), tab_map)] * 4,
        out_specs=[wide(W), kv_spec(64), kv_spec(128), wide(W), wide(W), wide(W), wide(W), wide(W),
                   wide(3 * D_MODEL)],
        out_shape=[sds(W), kv_sds(64), kv_sds(128), sds(W), sds(W), sds(W), sds(W), sds(W), sds(3 * D_MODEL)],
        compiler_params=_params(("arbitrary",)),
        name="in_proj",
    )(*xs, mod_l, gnorm, w_in_bf, *tabs)


def _attn_body(sink_ref, q_ref, kc_ref, k0_ref, k1_ref, k2_ref, k3_ref, vc_ref, v0_ref, v1_ref, v2_ref, v3_ref,
               o_ref, *, has_ctx):
    j = pl.program_id(1)
    p = j - 1 if has_ctx else j
    nr = ATTN_GROUP * ATTN_BLOCK
    groups = [slice(g * ATTN_BLOCK, (g + 1) * ATTN_BLOCK) for g in range(ATTN_GROUP)]

    def attend(sub, pieces):
        rows = slice(sub * ATTN_BLOCK, (sub + 1) * ATTN_BLOCK)
        outs = []
        for h in range(ATTN_KV_HEADS):
            q = q_ref[rows, h * 256:(h + 1) * 256]
            q4 = jnp.concatenate([q[:, g * 64:(g + 1) * 64] for g in range(ATTN_GROUP)], axis=0)
            sinks = [sink_ref[h * ATTN_GROUP + g] * LOG2E for g in range(ATTN_GROUP)]
            k_all = jnp.concatenate([k_ref[h] for k_ref, _, _ in pieces], axis=0)
            v_all = jnp.concatenate([v_ref[h] for _, v_ref, _ in pieces], axis=0)
            s = _dot_nt(q4, k_all)
            cols, c0 = [], 0
            for k_ref, _, ok in pieces:
                n = k_ref.shape[1]
                cols.append(s[:, c0:c0 + n] if ok is None else jnp.where(ok, s[:, c0:c0 + n], NEG))
                c0 += n
            s = jnp.concatenate(cols, axis=1)
            mxs, ps = [], []
            for g, r in enumerate(groups):
                mx = jnp.maximum(jnp.max(s[r], axis=-1, keepdims=True), sinks[g])
                mxs.append(mx)
                ps.append(jnp.exp2(s[r] - mx).astype(BF16))
            oa = _dot(jnp.concatenate(ps, axis=0), v_all)
            for g, r in enumerate(groups):
                den = oa[r, HEAD_DIM:HEAD_DIM + 1] + jnp.exp2(sinks[g] - mxs[g])
                outs.append(oa[r, 0:HEAD_DIM] / den)
        o_ref[rows, :] = jnp.concatenate(outs, axis=1).astype(BF16)

    @pl.when(p >= 0)
    def _():
        ri = lax.broadcasted_iota(jnp.int32, (nr, ATTN_BLOCK), 0) & (ATTN_BLOCK - 1)
        ci = lax.broadcasted_iota(jnp.int32, (nr, ATTN_BLOCK), 1)
        far = 4 * ATTN_BLOCK
        first_prev = ci >= ri + jnp.where(p >= 1, 0, far)
        last_next = ci + jnp.where(p <= SEQ // (2 * ATTN_BLOCK) - 2, 0, far) <= ri
        attend(0, [(kc_ref, vc_ref, None), (k0_ref, v0_ref, first_prev), (k1_ref, v1_ref, None),
                   (k2_ref, v2_ref, ci <= ri)])
        attend(1, [(kc_ref, vc_ref, None), (k1_ref, v1_ref, ci >= ri), (k2_ref, v2_ref, None),
                   (k3_ref, v3_ref, last_next)])

    if has_ctx:
        @pl.when(p < 0)
        def _():
            attend(0, [(kc_ref, vc_ref, None)])
            attend(1, [(kc_ref, vc_ref, None)])


def _attention(sink, qa, ka, va, need_ctx):
    nb = SEQ // ATTN_BLOCK
    npair = nb // 2
    first = 1 if need_ctx else 0
    tq = 2 * ATTN_BLOCK
    lat0 = RC // tq
    out0 = 0 if need_ctx else lat0

    def qrow(b, j):
        return jnp.where(j < first, b, lat0 + b * npair + j - first)

    def loc(delta):
        def f(b, j):
            m = jnp.clip(2 * (j - first) + delta, 0, nb - 1)
            return (0, RC // ATTN_BLOCK + b * nb + m, 0)
        return f

    def kv_specs(width):
        ctx_spec = pl.BlockSpec((ATTN_KV_HEADS, CTX_LEN, width), lambda b, j: (0, b, 0))
        return [ctx_spec] + [pl.BlockSpec((ATTN_KV_HEADS, ATTN_BLOCK, width), loc(d)) for d in (-1, 0, 1, 2)]

    return pl.pallas_call(
        functools.partial(_attn_body, has_ctx=need_ctx),
        grid=(BATCH, npair + first),
        in_specs=[pl.BlockSpec(memory_space=pltpu.SMEM),
                  pl.BlockSpec((tq, W), lambda b, j: (qrow(b, j), 0))]
                 + kv_specs(HEAD_DIM) + kv_specs(128),
        out_specs=pl.BlockSpec((tq, W), lambda b, j: (qrow(b, j) - out0, 0)),
        out_shape=jax.ShapeDtypeStruct((R - out0 * tq, W), BF16),
        compiler_params=_params(("arbitrary", "arbitrary")),
        name="window_attn",
    )(sink, qa, *([ka] * 5), *([va] * 5))


def _fourier_body(*refs, has_ctx):
    if has_ctx:
        uc_ref, ul_ref, bdc_ref, bds_ref, w2_ref, w2c_ref, oc_ref, ol_ref, as_ref = refs
    else:
        ul_ref, bdc_ref, bds_ref, w2_ref, ol_ref, as_ref = refs
    j = pl.program_id(1)
    first = 1 if has_ctx else 0

    if has_ctx:
        @pl.when(j == 0)
        def _():
            u = uc_ref[...]
            a = _dot(u, bdc_ref[...]).astype(BF16)
            s = _dot(u, bds_ref[...]).astype(BF16)
            z = _dot(w2c_ref[...], jnp.concatenate([a, s], axis=0))
            oc_ref[...] = (z * ((CTX_LEN * FOURIER_DIM) ** -0.5)).astype(BF16)

    @pl.when(j == first)
    def _():
        u = ul_ref[...]
        as_ref[0:SEQ, :] = _dot(u, bdc_ref[...]).astype(BF16)
        as_ref[SEQ:2 * SEQ, :] = _dot(u, bds_ref[...]).astype(BF16)

    @pl.when(j >= first)
    def _():
        ol_ref[...] = (_dot(w2_ref[...], as_ref[...]) * ((SEQ * FOURIER_DIM) ** -0.5)).astype(BF16)


def _fourier(fu, dft, need_ctx):
    bdc, bds, w2, w2c = dft
    tr = TR_FOURIER
    nj = SEQ // tr
    first = 1 if need_ctx else 0
    full = lambda a: pl.BlockSpec(a.shape, lambda *_: (0,) * a.ndim)
    lat_tile = lambda j: jnp.maximum(j - first, 0)
    ul_spec = pl.BlockSpec((SEQ, W), lambda b, j: (1 + b, 0))
    w2_spec = pl.BlockSpec((tr, 2 * SEQ), lambda b, j: (lat_tile(j), 0))
    ol_spec = pl.BlockSpec((tr, W), lambda b, j: (b * nj + lat_tile(j), 0))
    ol_shape = jax.ShapeDtypeStruct((RL, W), BF16)
    if need_ctx:
        in_specs = [pl.BlockSpec((CTX_LEN, W), lambda b, j: (b, 0)), ul_spec, full(bdc), full(bds), w2_spec,
                    full(w2c)]
        out_specs = [pl.BlockSpec((CTX_LEN, W), lambda b, j: (b, 0)), ol_spec]
        out_shape = [jax.ShapeDtypeStruct((RC, W), BF16), ol_shape]
        args = (fu, fu, bdc, bds, w2, w2c)
    else:
        in_specs = [ul_spec, full(bdc), full(bds), w2_spec]
        out_specs = [ol_spec]
        out_shape = [ol_shape]
        args = (fu, bdc, bds, w2)
    outs = pl.pallas_call(
        functools.partial(_fourier_body, has_ctx=need_ctx),
        grid=(BATCH, nj + first),
        in_specs=in_specs,
        out_specs=out_specs,
        out_shape=out_shape,
        scratch_shapes=[pltpu.VMEM((2 * SEQ, W), BF16)],
        compiler_params=_params(("arbitrary", "arbitrary")),
        name="fourier_mix",
    )(*args)
    return (outs[0], outs[1]) if need_ctx else (None, outs[0])


def _retention_body(qc_ref, kc_ref, vc_ref, ql_ref, kl_ref, vl_ref, g_ref,
                    dcomb_ref, xif_ref, ztf_ref, xib_ref, ztb_ref, gf_ref, gb_ref, mbd_ref, avg_ref,
                    o_ref, os_ref, st_ref):
    j = pl.program_id(1)
    C = RET_CHUNK
    nl = SEQ // C

    def group_parts(q, k, v, gi):
        sl = slice(gi * 256, (gi + 1) * 256)
        return q[:, sl], k[:, sl], v[:, sl], sl

    own = ((lax.broadcasted_iota(jnp.int32, (4 * C, 256), 0) >> 7)
           == (lax.broadcasted_iota(jnp.int32, (4 * C, 256), 1) >> 6))

    def chunk_fwd(q, k, v, r0):
        for gi in range(2):
            q4, k4, v4, sl = group_parts(q, k, v, gi)
            s_prev = st_ref[gi]
            q4f = q4.astype(F32)
            o4 = _dot((q4f * xif_ref[:, sl]).astype(BF16), s_prev.astype(BF16))
            qstack = jnp.where(own, jnp.concatenate([q4f] * 4, axis=0), 0.0).astype(BF16)
            p = (_dot_nt(qstack, k4) * dcomb_ref[gi]).astype(BF16)
            ov = jnp.where(own, _dot(p, v4), 0.0)
            intra = ov[0:C] + ov[C:2 * C] + ov[2 * C:3 * C] + ov[3 * C:4 * C]
            os_ref[pl.ds(r0, C), sl] = o4 + intra
            u = _dot_tn(k4, (v4.astype(F32) * ztf_ref[:, sl]).astype(BF16))
            st_ref[gi] = gf_ref[gi] * s_prev + mbd_ref[...] * u

    def chunk_bwd(q, k, v, r0):
        for gi in range(2):
            q4, k4, v4, sl = group_parts(q, k, v, gi)
            s_prev = st_ref[gi]
            cross = _dot((q4.astype(F32) * xib_ref[:, sl]).astype(BF16), s_prev.astype(BF16))
            os_ref[pl.ds(r0, C), sl] = os_ref[pl.ds(r0, C), sl] + cross
            u = _dot_tn(k4, (v4.astype(F32) * ztb_ref[:, sl]).astype(BF16))
            st_ref[gi] = gb_ref[gi] * s_prev + mbd_ref[...] * u

    @pl.when(j == 0)
    def _():
        st_ref[...] = jnp.zeros_like(st_ref)
        for c in range(CTX_LEN // C):
            rs = slice(c * C, (c + 1) * C)
            chunk_fwd(qc_ref[rs, :], kc_ref[rs, :], vc_ref[rs, :], c * C)

        def fbody(c, carry):
            r0 = pl.multiple_of(c * C, C)
            rs = pl.ds(r0, C)
            chunk_fwd(ql_ref[rs, :], kl_ref[rs, :], vl_ref[rs, :], CTX_LEN + r0)
            return carry

        lax.fori_loop(0, nl, fbody, 0)

        st_ref[...] = jnp.zeros_like(st_ref)
        for c in reversed(range(CTX_LEN // C)):
            rs = slice(c * C, (c + 1) * C)
            chunk_bwd(qc_ref[rs, :], kc_ref[rs, :], vc_ref[rs, :], c * C)

        def bbody(t, carry):
            r0 = pl.multiple_of((nl - 1 - t) * C, C)
            rs = pl.ds(r0, C)
            chunk_bwd(ql_ref[rs, :], kl_ref[rs, :], vl_ref[rs, :], CTX_LEN + r0)
            return carry

        lax.fori_loop(0, nl, bbody, 0)

    o = os_ref[pl.ds(pl.multiple_of(j * 256, 256), 256), :]
    avg = avg_ref[...]
    oh, ol = _split(o)
    mu = _dot(oh, avg) + _dot(ol, avg)
    d = o - mu
    var = _dot((d * d).astype(BF16), avg)
    o_ref[...] = (g_ref[...].astype(F32) * d * lax.rsqrt(var + GN_EPS)).astype(BF16)


def _retention(qr, kr, vr, gr, rtabs):
    nj = 1 + SEQ // 256

    def out_map(b, j):
        return (jnp.where(j == 0, b, RC // 256 + b * (SEQ // 256) + j - 1), 0)

    ctx = pl.BlockSpec((CTX_LEN, W), lambda b, j: (b, 0))
    lat = pl.BlockSpec((SEQ, W), lambda b, j: (1 + b, 0))
    full = lambda a: pl.BlockSpec(a.shape, lambda *_: (0,) * a.ndim)
    avg = jnp.asarray(np.kron(np.eye(RET_HEADS), np.full((RET_DK, RET_DK), 1.0 / RET_DK)).astype(np.float32)).astype(BF16)
    mbd = jnp.asarray(np.kron(np.eye(4), np.ones((RET_DK, RET_DK))).astype(np.float32))
    tabs = list(rtabs) + [mbd, avg]
    return pl.pallas_call(
        _retention_body,
        grid=(BATCH, nj),
        in_specs=[ctx, ctx, ctx, lat, lat, lat, pl.BlockSpec((256, W), out_map)] + [full(t) for t in tabs],
        out_specs=pl.BlockSpec((256, W), out_map),
        out_shape=jax.ShapeDtypeStruct((R, W), BF16),
        scratch_shapes=[pltpu.VMEM((CTX_LEN + SEQ, W), F32), pltpu.VMEM((2, 256, 256), F32)],
        compiler_params=_params(("arbitrary", "arbitrary")),
        name="retention",
    )(qr, kr, vr, qr, kr, vr, gr, *tabs)


def _pack_pair(a, b):
    ua = lax.bitcast_convert_type(a, jnp.uint32) >> 16
    ub = lax.bitcast_convert_type(b, jnp.uint32) & jnp.uint32(0xFFFF0000)
    return ua | ub


def _unpack_pair(w):
    a = lax.bitcast_convert_type(w << 16, F32)
    b = lax.bitcast_convert_type(w & jnp.uint32(0xFFFF0000), F32)
    return a, b


def _slot_onehot(s0, s1, n):
    srow = lax.broadcasted_iota(jnp.int32, (n, s0.shape[1]), 0)
    p0 = jnp.where(srow == s0, 1.0, 0.0).astype(BF16)
    p1 = jnp.where(srow == s1, 1.0, 0.0).astype(BF16)
    return p0, p1


def _pick(first, a_ref, b_ref):
    a = a_ref[...]
    flag = jnp.zeros(a.shape, jnp.int32) + first.astype(jnp.int32)
    return jnp.where(flag > 0, a, b_ref[...])


def _merge_body(*refs, layer0):
    if layer0:
        oa_ref, ofc_ref, ofl_ref, rt_ref, gm_ref, xc_ref, xl_ref = refs[:7]
        rest = refs[7:]
        is_ctx = pl.program_id(0) < RC // TM_MERGE
        of_in = _pick(is_ctx, ofc_ref, ofl_ref)
        x_in = _pick(is_ctx, xc_ref, xl_ref)
    else:
        oa_ref, ofl_ref, rt_ref, gm_ref, x_ref = refs[:5]
        rest = refs[5:]
        of_in = ofl_ref[...]
        x_in = x_ref[...]
    (mod_ref, gn_ref, wba_ref, wbf_ref, wbr_ref, wout_ref, wrh_ref, wrl_ref, br_ref, tri_ref, ltri_ref,
     xo_ref, xs_ref, ro_ref, nch_ref) = rest
    gm = gm_ref[...].astype(F32)
    z = (gm[:, 0:D_MODEL] * _dot(oa_ref[...], wba_ref[...])
         + gm[:, D_MODEL:2 * D_MODEL] * _dot(of_in, wbf_ref[...])
         + gm[:, 2 * D_MODEL:3 * D_MODEL] * _dot(rt_ref[...], wbr_ref[...]))
    y = _dot(z.astype(BF16), wout_ref[...])
    x = x_in + mod_ref[2:3, :] * y
    xo_ref[...] = x
    ms = jnp.mean(x * x, axis=-1, keepdims=True)
    hn = x * lax.rsqrt(ms + NORM_EPS) * gn_ref[...]
    h2 = hn * (1.0 + mod_ref[4:5, :]) + mod_ref[3:4, :]
    hh, hl = _split(h2)
    wh, wl = wrh_ref[...], wrl_ref[...]
    lg = _dot_nt(wh, hh) + _dot_nt(wh, hl) + _dot_nt(wl, hh) + br_ref[...]
    tm = lg.shape[1]
    row8 = lax.broadcasted_iota(jnp.int32, (8, tm), 0)
    lgg = lg[0:8, :]
    mg = jnp.max(lgg, axis=0, keepdims=True)
    grp = jnp.min(jnp.where(lgg == mg, row8, 8), axis=0, keepdims=True)
    pg = 1.0 / jnp.sum(jnp.exp(lgg - mg), axis=0, keepdims=True)
    lin = jnp.zeros((8, tm), F32)
    for g in range(N_GROUPS):
        lin = jnp.where(grp == g, lg[8 + 8 * g:16 + 8 * g, :], lin)
    v1 = jnp.max(lin, axis=0, keepdims=True)
    i1 = jnp.min(jnp.where(lin == v1, row8, 8), axis=0, keepdims=True)
    rest = jnp.where(row8 == i1, -jnp.inf, lin)
    v2 = jnp.max(rest, axis=0, keepdims=True)
    i2 = jnp.min(jnp.where(rest == v2, row8, 8), axis=0, keepdims=True)
    e2 = jnp.exp(v2 - v1)
    w1 = pg / (1.0 + e2)
    w2 = pg * e2 / (1.0 + e2)
    e_1 = grp * EXPERTS_PER_GROUP + i1
    e_2 = grp * EXPERTS_PER_GROUP + i2

    row32 = lax.broadcasted_iota(jnp.int32, (N_EXPERTS, tm), 0)
    oh0 = jnp.where(row32 == e_1, 1.0, 0.0)
    oh1 = jnp.where(row32 == e_2, 1.0, 0.0)
    tri = tri_ref[...]
    cum0 = _dot(oh0.astype(BF16), tri)
    cum1 = _dot(oh1.astype(BF16), tri)
    tot0 = jnp.sum(oh0, axis=1, keepdims=True)
    tot1 = jnp.sum(oh1, axis=1, keepdims=True)
    nch = ((tot0 + tot1).astype(jnp.int32) + (CHUNK - 1)) >> CHUNK_SHIFT
    nch_b = jnp.broadcast_to(nch.astype(F32), (N_EXPERTS, 128))
    nch_ref[...] = nch_b.astype(jnp.int32)
    base = CHUNK * _dot(ltri_ref[...], nch_b.astype(BF16))[:, 0:1]
    s0 = jnp.sum(oh0 * (base + cum0), axis=0, keepdims=True).astype(jnp.int32)
    s1 = jnp.sum(oh1 * (base + tot0 + cum1), axis=0, keepdims=True).astype(jnp.int32)
    p0, p1 = _slot_onehot(s0, s1, SLOTS)
    xs = _dot(p0 + p1, hh)
    xs_ref[:, 0:D_MODEL // 2] = _pack_pair(xs[:, 0:D_MODEL // 2], xs[:, D_MODEL // 2:D_MODEL])

    def wrows(w):
        hi, lo = _split(w)
        return jnp.where(row8 == 0, hi.astype(F32), jnp.where(row8 == 1, lo.astype(F32), 0.0)).astype(BF16)

    wc = _dot_nt(p0, wrows(w1)) + _dot_nt(p1, wrows(w2))
    wcol = jnp.broadcast_to(wc[:, 0:1] + wc[:, 1:2], (SLOTS, 128))
    xs_ref[:, D_MODEL // 2:XS_COLS] = lax.bitcast_convert_type(wcol, jnp.uint32)
    s0f, s1f = s0.astype(F32), s1.astype(F32)
    ro_ref[...] = jnp.where(row8 == 0, s0f, jnp.where(row8 == 1, s1f, 0.0))


def _merge(oa, of_c, of_l, ret, gm, xs_in, mod_l, gnorm, wba, wbf, wbr, wout, wrh, wrl, brb, layer0):
    tm = TM_MERGE
    row0 = 0 if layer0 else RC
    rm = R - row0
    nt = rm // tm
    off = row0 // tm
    nc = RC // tm
    src = lambda n: pl.BlockSpec((tm, n), lambda i: (i + off, 0))
    dst = lambda n: pl.BlockSpec((tm, n), lambda i: (i, 0))
    ctx_rows = lambda n: pl.BlockSpec((tm, n), lambda i: (jnp.minimum(i, nc - 1), 0))
    lat_rows = lambda n: pl.BlockSpec((tm, n), lambda i: (jnp.maximum(i - nc, 0), 0))
    full = lambda a: pl.BlockSpec(a.shape, lambda *_: (0,) * a.ndim, pipeline_mode=pl.Buffered(1))
    mrow = _mod_row(row0, tm)
    tri = jnp.asarray(np.triu(np.ones((tm, tm), np.float32), 1)).astype(BF16)
    ltri = jnp.asarray(np.tril(np.ones((N_EXPERTS, N_EXPERTS), np.float32), -1)).astype(BF16)
    if layer0:
        acts = [oa, of_c, of_l, ret, gm, xs_in[0], xs_in[1]]
        act_specs = [src(W), ctx_rows(W), lat_rows(W), src(W), src(3 * D_MODEL), ctx_rows(D_MODEL),
                     lat_rows(D_MODEL)]
    else:
        acts = [oa, of_l, ret, gm, xs_in]
        act_specs = [dst(W), dst(W), src(W), src(3 * D_MODEL), src(D_MODEL)]
    return pl.pallas_call(
        functools.partial(_merge_body, layer0=layer0),
        grid=(nt,),
        in_specs=act_specs + [
                  pl.BlockSpec((None, 6, D_MODEL), lambda i: (mrow(i), 0, 0)),
                  full(gnorm), full(wba), full(wbf), full(wbr), full(wout), full(wrh), full(wrl), full(brb),
                  full(tri), full(ltri)],
        out_specs=[dst(D_MODEL), pl.BlockSpec((SLOTS, XS_COLS), lambda i: (i, 0)),
                   pl.BlockSpec((8, tm), lambda i: (0, i)),
                   pl.BlockSpec((None, N_EXPERTS, 128), lambda i: (i, 0, 0))],
        out_shape=[jax.ShapeDtypeStruct((rm, D_MODEL), F32),
                   jax.ShapeDtypeStruct((nt * SLOTS, XS_COLS), jnp.uint32),
                   jax.ShapeDtypeStruct((8, rm), F32),
                   jax.ShapeDtypeStruct((nt, N_EXPERTS, 128), jnp.int32)],
        compiler_params=_params(("arbitrary",)),
        name="merge_router",
    )(*acts, mod_l, gnorm, wba, wbf, wbr, wout, wrh, wrl, brb, tri, ltri)


def _moe_plan(nch, nb):
    nt = nch.shape[0]
    choff = jnp.cumsum(nch, axis=1) - nch
    used_ch = jnp.sum(nch, axis=1)
    cum_t = jnp.cumsum(nch, axis=0)
    tot = cum_t[-1]
    ptot = (tot + CPB - 1) // CPB * CPB
    pend = jnp.cumsum(ptot)
    pstart = pend - ptot
    n_used = pend[-1] // CPB
    blk = jnp.arange(nb, dtype=jnp.int32)
    lane = jnp.arange(CPB, dtype=jnp.int32)
    blk_e = jnp.minimum(jnp.sum((blk[:, None] * CPB >= pend[None, :]).astype(jnp.int32), axis=1), N_EXPERTS - 1)
    oe = (blk_e[:, None] == jnp.arange(N_EXPERTS, dtype=jnp.int32)[None, :]).astype(jnp.int32)
    sel = lambda tab: jnp.sum(oe[:, :, None] * tab.T[None, :, :], axis=1)
    pstart_b = jnp.sum(oe * pstart[None, :], axis=1)
    tot_b = jnp.sum(oe * tot[None, :], axis=1)
    cum_b, nch_b, choff_b = sel(cum_t), sel(nch), sel(choff)
    i = blk[:, None] * CPB + lane[None, :] - pstart_b[:, None]
    valid = (i < tot_b[:, None]) & (blk[:, None] < n_used)
    t = jnp.minimum(jnp.sum((i[:, :, None] >= cum_b[:, None, :]).astype(jnp.int32), axis=2), nt - 1)
    tiles = jnp.arange(nt, dtype=jnp.int32)[None, None, :]
    before = jnp.sum(jnp.where(tiles < t[:, :, None], nch_b[:, None, :], 0), axis=2)
    coff = jnp.sum(jnp.where(tiles == t[:, :, None], choff_b[:, None, :], 0), axis=2)
    row = t * SLOTS + CHUNK * (coff + i - before)
    src = jnp.where(valid, row, SLOTS - CHUNK)
    dummy = nt * SLOTS + CHUNK * ((blk[:, None] % 2) * CPB + lane[None, :])
    dst = jnp.where(valid, row, dummy)
    blk_start = jnp.concatenate([pstart, pend[-1:]]) // CPB
    return (blk_start.astype(jnp.int32), n_used.astype(jnp.int32).reshape(1), src.reshape(-1).astype(jnp.int32),
            dst.reshape(-1).astype(jnp.int32), used_ch.astype(jnp.int32))


def _ffn_body(bs_ref, nu_ref, src_ref, dst_ref, uc_ref, xs_ref, wg_ref, wu_ref, wd_ref, ys_ref,
              xbuf, ybuf, zbuf, wgb, wub, wdb, sem_in, sem_out, sem_z, *, nt):
    e = pl.program_id(0)
    nu = nu_ref[0]
    half = D_MODEL // 2

    def gather(blk, sl):
        for c in range(CPB):
            r = pl.multiple_of(src_ref[blk * CPB + c], CHUNK)
            pltpu.make_async_copy(xs_ref.at[pl.ds(r, CHUNK)], xbuf.at[sl, pl.ds(c * CHUNK, CHUNK)],
                                  sem_in.at[sl]).start(priority=1)

    def scatter(blk, sl):
        for c in range(CPB):
            r = pl.multiple_of(dst_ref[blk * CPB + c], CHUNK)
            pltpu.make_async_copy(ybuf.at[sl, pl.ds(c * CHUNK, CHUNK)], ys_ref.at[pl.ds(r, CHUNK)],
                                  sem_out.at[sl]).start(priority=1)

    def wait_gather(sl):
        pltpu.make_async_copy(xs_ref.at[pl.ds(0, MOE_BM)], xbuf.at[sl], sem_in.at[sl]).wait()

    def wait_scatter(sl):
        pltpu.make_async_copy(ybuf.at[sl], ys_ref.at[pl.ds(0, MOE_BM)], sem_out.at[sl]).wait()

    def zero_copy(r):
        return pltpu.make_async_copy(zbuf, ys_ref.at[pl.ds(pl.multiple_of(r, CHUNK), CHUNK)], sem_z)

    @pl.when(e == 0)
    def _():
        zbuf[...] = jnp.zeros_like(zbuf)

        def tails(fn):
            def per_tile(t, carry):
                def per_chunk(c, carry2):
                    fn(t * SLOTS + c * CHUNK)
                    return carry2
                lax.fori_loop(uc_ref[t], SLOTS // CHUNK, per_chunk, 0)
                return carry
            lax.fori_loop(0, nt, per_tile, 0)
            for c in range(2 * CPB):
                fn(nt * SLOTS + c * CHUNK)

        tails(lambda r: zero_copy(r).start())
        tails(lambda r: zero_copy(r).wait())
        gather(0, 0)

    b0, b1 = bs_ref[e], bs_ref[e + 1]

    @pl.when(b1 > b0)
    def _():
        wgb[...] = wg_ref[...].astype(BF16)
        wub[...] = wu_ref[...].astype(BF16)
        wdb[...] = wd_ref[...].astype(BF16)

        def block(b, carry):
            slot = b % 2

            @pl.when(b + 1 < nu)
            def _():
                gather(b + 1, 1 - slot)

            wait_gather(slot)

            @pl.when(b >= 2)
            def _():
                wait_scatter(slot)

            xw = xbuf[slot]
            xa, xb = _unpack_pair(xw[:, 0:half])
            x = jnp.concatenate([xa, xb], axis=1).astype(BF16)
            wt = lax.bitcast_convert_type(xw[:, half:XS_COLS], F32)
            g = _dot(x, wgb[...])
            u = _dot(x, wub[...])
            hmid = (g * _sigmoid(g) * u).astype(BF16)
            y = _dot(hmid, wdb[...]) * jnp.concatenate([wt] * (D_MODEL // 128), axis=1)
            yb = y.astype(BF16).astype(F32)
            ybuf[slot] = _pack_pair(yb[:, 0:half], yb[:, half:D_MODEL])
            scatter(b, slot)
            return carry

        lax.fori_loop(b0, b1, block, 0)

    @pl.when(e == N_EXPERTS - 1)
    def _():
        wait_scatter((nu - 1) % 2)

        @pl.when(nu >= 2)
        def _():
            wait_scatter(nu % 2)


def _ffn(plan, xs, w_g, w_u, w_d, layer, nt):
    wmap = lambda e, *_: (layer, e, 0, 0)
    half = D_MODEL // 2
    grid_spec = pltpu.PrefetchScalarGridSpec(
        num_scalar_prefetch=5,
        grid=(N_EXPERTS,),
        in_specs=[pl.BlockSpec(memory_space=pl.ANY),
                  pl.BlockSpec((None, None, D_MODEL, EXPERT_HIDDEN), wmap),
                  pl.BlockSpec((None, None, D_MODEL, EXPERT_HIDDEN), wmap),
                  pl.BlockSpec((None, None, EXPERT_HIDDEN, D_MODEL), wmap)],
        out_specs=pl.BlockSpec(memory_space=pl.ANY),
        scratch_shapes=[pltpu.VMEM((2, MOE_BM, XS_COLS), jnp.uint32), pltpu.VMEM((2, MOE_BM, half), jnp.uint32),
                        pltpu.VMEM((CHUNK, half), jnp.uint32),
                        pltpu.VMEM((D_MODEL, EXPERT_HIDDEN), BF16), pltpu.VMEM((D_MODEL, EXPERT_HIDDEN), BF16),
                        pltpu.VMEM((EXPERT_HIDDEN, D_MODEL), BF16),
                        pltpu.SemaphoreType.DMA((2,)), pltpu.SemaphoreType.DMA((2,)), pltpu.SemaphoreType.DMA(())],
    )
    return pl.pallas_call(
        functools.partial(_ffn_body, nt=nt),
        grid_spec=grid_spec,
        out_shape=jax.ShapeDtypeStruct((nt * SLOTS + 2 * CPB * CHUNK, half), jnp.uint32),
        compiler_params=_params(("arbitrary",)),
        name="moe_experts",
    )(*plan, xs, w_g, w_u, w_d)


def _combine_body(ys_ref, ro_ref, x_ref, mod_ref, gn_ref, o_ref, *, final):
    s = ro_ref[...]
    p0, p1 = _slot_onehot(s[0:1, :].astype(jnp.int32), s[1:2, :].astype(jnp.int32), SLOTS)
    ya, yb = _unpack_pair(ys_ref[...])
    y = jnp.concatenate([ya, yb], axis=1).astype(BF16)
    f = _dot_tn(p0 + p1, y)
    x = x_ref[...] + mod_ref[5:6, :] * f
    if final:
        ms = jnp.mean(x * x, axis=-1, keepdims=True)
        x = x * lax.rsqrt(ms + NORM_EPS) * gn_ref[...]
    o_ref[...] = x


def _combine(ys, route, x, mod_l, gnorm, row0, final):
    tm = TM_MERGE
    rm = x.shape[0]
    mrow = _mod_row(row0, tm)
    return pl.pallas_call(
        functools.partial(_combine_body, final=final),
        grid=(rm // tm,),
        in_specs=[pl.BlockSpec((SLOTS, D_MODEL // 2), lambda i: (i, 0)),
                  pl.BlockSpec((8, tm), lambda i: (0, i)),
                  pl.BlockSpec((tm, D_MODEL), lambda i: (i, 0)),
                  pl.BlockSpec((None, 6, D_MODEL), lambda i: (mrow(i), 0, 0)),
                  pl.BlockSpec((1, D_MODEL), lambda i: (0, 0))],
        out_specs=pl.BlockSpec((tm, D_MODEL), lambda i: (i, 0)),
        out_shape=jax.ShapeDtypeStruct((rm, D_MODEL), F32),
        compiler_params=_params(("arbitrary",)),
        name="moe_combine",
    )(ys, route, x, mod_l, gnorm)


def _moe(xs, route, nch3, x, mod_l, gnorm, w_g, w_u, w_d, row0, final, layer):
    nt = nch3.shape[0]
    max_chunks = nt * ((2 * TM_MERGE + N_EXPERTS * (CHUNK - 1)) // CHUNK)
    nb = -(-max_chunks // CPB) + N_EXPERTS
    plan = _moe_plan(nch3[:, :, 0], nb)
    ys = _ffn(plan, xs, w_g, w_u, w_d, layer, nt)
    return _combine(ys, route, x, mod_l, gnorm, row0, final)


def kernel(x, c, ctx, c_ctx, norm_mix, norm_ffn, w_ada, b_ada, w_in, attn_sink, ret_decay_fwd, ret_decay_bwd,
           w_branch_attn, w_branch_fourier, w_branch_ret, w_out, w_router_group, b_router_group,
           w_router_expert, b_router_expert, w_exp_gate, w_exp_up, w_exp_down, norm_final):
    tabs = [jnp.asarray(t) for t in _rope_tables()]
    dft = [jnp.asarray(t).astype(BF16) for t in _dft_tables()]

    cc = jnp.zeros((MOD_ROWS, D_MODEL), F32).at[0:BATCH].set(c).at[CTX_MOD_ROW].set(c_ctx)
    mod = _ada(cc, w_ada, b_ada).reshape(DEPTH, MOD_ROWS, 6, D_MODEL)

    xf = (ctx.reshape(RC, D_MODEL), x.reshape(RL, D_MODEL))
    w_in_bf = w_in.astype(BF16)
    for l in range(DEPTH):
        need_ctx = l < DEPTH - 1
        row0 = 0 if need_ctx else RC
        mod_l = mod[l]
        qa, ka, va, qr, kr, vr, gr, fu, gm = _proj(xf, mod_l, norm_mix[l][None, :], w_in_bf, tabs, l)
        oa = _attention(attn_sink[l], qa, ka, va, need_ctx)
        of_c, of_l = _fourier(fu, dft, need_ctx)
        ret = _retention(qr, kr, vr, gr, _retention_tables(ret_decay_fwd[l], ret_decay_bwd[l]))
        wr = jnp.zeros((ROUTER_ROWS, D_MODEL), F32)
        wr = wr.at[0:N_GROUPS].set(w_router_group[l].T).at[8:8 + N_EXPERTS].set(w_router_expert[l].T)
        br = jnp.full((ROUTER_ROWS,), NEG, F32)
        br = br.at[0:N_GROUPS].set(b_router_group[l]).at[8:8 + N_EXPERTS].set(b_router_expert[l])
        wrh, wrl = _split(wr)
        brb = jnp.broadcast_to(br[:, None], (ROUTER_ROWS, TM_MERGE))
        x_mid, xs, route, nch3 = _merge(oa, of_c, of_l, ret, gm, xf, mod_l, norm_ffn[l][None, :],
                                        w_branch_attn[l].astype(BF16), w_branch_fourier[l].astype(BF16),
                                        w_branch_ret[l].astype(BF16), w_out[l].astype(BF16), wrh, wrl, brb,
                                        need_ctx)
        final = l == DEPTH - 1
        xf = _moe(xs, route, nch3, x_mid, mod_l, norm_final[None, :], w_exp_gate, w_exp_up, w_exp_down,
                  row0, final, l)
    return xf.reshape(BATCH, SEQ, D_MODEL)
```

```python
import functools

import numpy as np
import jax
import jax.numpy as jnp
from jax import lax
from jax.experimental import pallas as pl
from jax.experimental.pallas import tpu as pltpu

F32 = jnp.float32
BF16 = jnp.bfloat16

D_MODEL = 1024
BATCH = 8
SEQ = 2048
DEPTH = 2
CTX_LEN = 256
GRID_W = 64
HEAD_DIM = 64
ATTN_HEADS = 8
ATTN_KV_HEADS = 2
ATTN_GROUP = ATTN_HEADS // ATTN_KV_HEADS
ATTN_BLOCK = 128
RET_HEADS = 8
RET_DK = 64
RET_CHUNK = 128
FOURIER_GROUPS = 4
FOURIER_DIM = 128
N_GROUPS = 4
EXPERTS_PER_GROUP = 8
N_EXPERTS = N_GROUPS * EXPERTS_PER_GROUP
EXPERT_HIDDEN = 512
ROPE_BASE = 10000.0
NORM_EPS = 1e-6
GN_EPS = 1e-5

W = 512
IN_COLS = 6400
RC = BATCH * CTX_LEN
RL = BATCH * SEQ
R = RC + RL
MOD_ROWS = 16
CTX_MOD_ROW = 8

VMEM_LIMIT = 52 * 1024 * 1024

TM_PROJ = 512
TM_MERGE = 512
TN_ADA = 1536
TR_FOURIER = 512
RET_SLAB = 2 * CTX_LEN
MOE_BM = 512
CHUNK = 8
CHUNK_SHIFT = 3
CPB = MOE_BM // CHUNK
SLOTS = 1280
XS_COLS = D_MODEL // 2 + 128
NEG = -1e30
LOG2E = 1.4426950408889634
ROUTER_ROWS = 40


def _dot(a, b):
    return jnp.dot(a, b, preferred_element_type=F32)


def _dot_nt(a, b):
    return lax.dot_general(a, b, (((1,), (1,)), ((), ())), preferred_element_type=F32)


def _dot_tn(a, b):
    return lax.dot_general(a, b, (((0,), (0,)), ((), ())), preferred_element_type=F32)


def _split(x):
    hi = x.astype(BF16)
    lo = (x - hi.astype(F32)).astype(BF16)
    return hi, lo


def _sigmoid(x):
    return 1.0 / (1.0 + jnp.exp(-x))


def _params(sem, vmem=VMEM_LIMIT):
    return pltpu.CompilerParams(dimension_semantics=sem, vmem_limit_bytes=vmem)


def _mod_row(row0, tm):
    def f(i):
        g0 = i * tm + row0
        return jnp.where(g0 < RC, CTX_MOD_ROW, (g0 - RC) // SEQ)
    return f


def _rope_tables():
    pos = np.arange(SEQ, dtype=np.float64)
    row = np.floor(pos / GRID_W)
    col = pos % GRID_W

    def cs(p, nf):
        inv = ROPE_BASE ** (-np.arange(nf, dtype=np.float64) / nf)
        ang = p[:, None] * inv[None, :]
        return np.cos(ang), np.sin(ang)

    rc, rs = cs(row, HEAD_DIM // 4)
    cc, cs_ = cs(col, HEAD_DIM // 4)
    cos_a = np.concatenate([rc, rc, cc, cc], axis=1)
    sin_a = np.concatenate([-rs, rs, -cs_, cs_], axis=1)
    tc, ts = cs(pos, RET_DK // 2)
    cos_r = np.concatenate([tc, tc], axis=1)
    sin_r = np.concatenate([-ts, ts], axis=1)

    def full(t, ident):
        t2 = np.concatenate([t, t], axis=1)
        return np.concatenate([np.full_like(t2, ident), t2], axis=0).astype(np.float32)

    return full(cos_a, 1.0), full(sin_a, 0.0), full(cos_r, 1.0), full(sin_r, 0.0)


def _dft_tables():
    def cs(n):
        k = np.arange(n, dtype=np.int64)
        m = (k[:, None] * k[None, :]) % n
        ang = 2.0 * np.pi * m.astype(np.float64) / n
        return np.cos(ang), np.sin(ang)

    c128, s128 = cs(FOURIER_DIM)
    eye = np.eye(FOURIER_GROUPS)
    bdc = np.kron(eye, c128).astype(np.float32)
    bds = np.kron(eye, s128).astype(np.float32)
    cn, sn = cs(SEQ)
    w2 = np.concatenate([cn, -sn], axis=1).astype(np.float32)
    cl, sl = cs(CTX_LEN)
    w2c = np.concatenate([cl, -sl], axis=1).astype(np.float32)
    return bdc, bds, w2, w2c


def _retention_tables(dec_f, dec_b):
    lg_f = jax.nn.log_sigmoid(dec_f.astype(F32))
    lg_b = jax.nn.log_sigmoid(dec_b.astype(F32))
    i = jnp.arange(RET_CHUNK)
    diff = (i[:, None] - i[None, :]).astype(F32)
    fwd = jnp.exp(jnp.maximum(diff, 0.0)[None] * lg_f[:, None, None])
    bwd = jnp.exp(jnp.maximum(-diff, 0.0)[None] * lg_b[:, None, None])
    dcomb = jnp.where((diff >= 0)[None], fwd, bwd).reshape(2, 4 * RET_CHUNK, RET_CHUNK)
    fi = i.astype(F32)
    lanes = lambda t: jnp.repeat(t, RET_DK, axis=1)
    xi_f = lanes(jnp.exp((fi + 1.0)[:, None] * lg_f[None, :]))
    zt_f = lanes(jnp.exp((RET_CHUNK - 1 - fi)[:, None] * lg_f[None, :]))
    xi_b = lanes(jnp.exp((RET_CHUNK - fi)[:, None] * lg_b[None, :]))
    zt_b = lanes(jnp.exp(fi[:, None] * lg_b[None, :]))
    g_f = jnp.repeat(jnp.exp(RET_CHUNK * lg_f), RET_DK).reshape(2, 256, 1)
    g_b = jnp.repeat(jnp.exp(RET_CHUNK * lg_b), RET_DK).reshape(2, 256, 1)
    g_f = jnp.broadcast_to(g_f, (2, 256, 256))
    g_b = jnp.broadcast_to(g_b, (2, 256, 256))
    return dcomb, xi_f, zt_f, xi_b, zt_b, g_f, g_b


def _ada_body(c_ref, w_ref, b_ref, o_ref):
    c = c_ref[...]
    s = c * _sigmoid(c)
    sh, sl = _split(s)
    wh, wl = _split(w_ref[...])
    o_ref[...] = _dot(sh, wh) + _dot(sl, wh) + _dot(sh, wl) + b_ref[...]


def _ada(cc, w_ada, b_ada):
    nt = 6 * D_MODEL // TN_ADA
    return pl.pallas_call(
        _ada_body,
        grid=(DEPTH, nt),
        in_specs=[
            pl.BlockSpec((MOD_ROWS, D_MODEL), lambda l, j: (0, 0)),
            pl.BlockSpec((None, D_MODEL, TN_ADA), lambda l, j: (l, 0, j)),
            pl.BlockSpec((None, 1, TN_ADA), lambda l, j: (l, 0, j)),
        ],
        out_specs=pl.BlockSpec((None, MOD_ROWS, TN_ADA), lambda l, j: (l, 0, j)),
        out_shape=jax.ShapeDtypeStruct((DEPTH, MOD_ROWS, 6 * D_MODEL), F32),
        compiler_params=_params(("arbitrary", "arbitrary")),
        name="ada_mod",
    )(cc, w_ada, b_ada.reshape(DEPTH, 1, 6 * D_MODEL))


def _rope(xc, cos, sin, half):
    fwd = pltpu.roll(xc, 128 - half, axis=1)
    bwd = pltpu.roll(xc, half, axis=1)
    lane = lax.broadcasted_iota(jnp.int32, xc.shape, 1)
    first = (lane & (2 * half - 1)) < half
    return xc * cos + jnp.where(first, fwd, bwd) * sin


def _proj_body(*refs, split):
    if split:
        x = _pick(pl.program_id(0) < RC // TM_PROJ, refs[0], refs[1])
        refs = refs[2:]
    else:
        x = refs[0][...]
        refs = refs[1:]
    (mod_ref, gn_ref, w_ref, ca_ref, sa_ref, cr_ref, sr_ref,
     qa_ref, ka_ref, va_ref, qr_ref, kr_ref, vr_ref, gr_ref, fu_ref, gm_ref) = refs
    ms = jnp.mean(x * x, axis=-1, keepdims=True)
    y = x * lax.rsqrt(ms + NORM_EPS) * gn_ref[...]
    h = y * (1.0 + mod_ref[1:2, :]) + mod_ref[0:1, :]
    hb = h.astype(BF16)

    def proj(c0, width):
        return _dot(hb, w_ref[:, c0:c0 + width])

    ca, sa, cr, sr = ca_ref[...], sa_ref[...], cr_ref[...], sr_ref[...]

    qa = proj(0, W) * (HEAD_DIM ** -0.5 * LOG2E)
    for c in range(W // 128):
        qa_ref[:, c * 128:(c + 1) * 128] = _rope(qa[:, c * 128:(c + 1) * 128], ca, sa, 16).astype(BF16)
    kv = proj(W, 256)
    ka = _rope(kv[:, 0:128], ca, sa, 16).astype(BF16)
    ka_ref[0] = ka[:, 0:64]
    ka_ref[1] = ka[:, 64:128]
    va = kv[:, 128:256].astype(BF16)
    ones_col = jnp.where(lax.broadcasted_iota(jnp.int32, (va.shape[0], 64), 1) == 0, 1.0, 0.0).astype(BF16)
    va_ref[0] = jnp.concatenate([va[:, 0:64], ones_col], axis=1)
    va_ref[1] = jnp.concatenate([va[:, 64:128], ones_col], axis=1)
    qr = proj(768, W)
    kr = proj(1280, W) * (RET_DK ** -0.5)
    for c in range(W // 128):
        sl = slice(c * 128, (c + 1) * 128)
        qr_ref[:, sl] = _rope(qr[:, sl], cr, sr, 32).astype(BF16)
        kr_ref[:, sl] = _rope(kr[:, sl], cr, sr, 32).astype(BF16)
    vr_ref[...] = proj(1792, W).astype(BF16)
    g = proj(2304, W)
    gr_ref[...] = (g * _sigmoid(g)).astype(BF16)
    fu_ref[...] = proj(2816, W).astype(BF16)
    for c in range(3):
        gm_ref[:, c * D_MODEL:(c + 1) * D_MODEL] = _sigmoid(proj(3328 + c * D_MODEL, D_MODEL)).astype(BF16)


def _proj(x, mod_l, gnorm, w_in_bf, tabs, layer):
    tm = TM_PROJ
    nt = R // tm
    nc = RC // tm
    split = isinstance(x, tuple)
    if split:
        xs = list(x)
        x_specs = [pl.BlockSpec((tm, D_MODEL), lambda i: (jnp.minimum(i, nc - 1), 0)),
                   pl.BlockSpec((tm, D_MODEL), lambda i: (jnp.maximum(i - nc, 0), 0))]
    else:
        xs = [x]
        x_specs = [pl.BlockSpec((tm, D_MODEL), lambda i: (i, 0))]

    def tab_map(i):
        return (jnp.where(i < nc, i, nc + (i - nc) % (SEQ // tm)), 0)

    row = lambda i: (i, 0)
    wide = lambda n: pl.BlockSpec((tm, n), row)
    kv_spec = lambda n: pl.BlockSpec((2, tm, n), lambda i: (0, i, 0))
    sds = lambda n: jax.ShapeDtypeStruct((R, n), BF16)
    kv_sds = lambda n: jax.ShapeDtypeStruct((2, R, n), BF16)
    mrow = _mod_row(0, tm)
    return pl.pallas_call(
        functools.partial(_proj_body, split=split),
        grid=(nt,),
        in_specs=x_specs + [
            pl.BlockSpec((None, 6, D_MODEL), lambda i: (mrow(i), 0, 0)),
            pl.BlockSpec((1, D_MODEL), lambda i: (0, 0)),
            pl.BlockSpec((None, D_MODEL, IN_COLS), lambda i: (layer, 0, 0), pipeline_mode=pl.Buffered(1)),
        ] + [pl.BlockSpec((tm, 128), tab_map)] * 4,
        out_specs=[wide(W), kv_spec(64), kv_spec(128), wide(W), wide(W), wide(W), wide(W), wide(W),
                   wide(3 * D_MODEL)],
        out_shape=[sds(W), kv_sds(64), kv_sds(128), sds(W), sds(W), sds(W), sds(W), sds(W), sds(3 * D_MODEL)],
        compiler_params=_params(("arbitrary",)),
        name="in_proj",
    )(*xs, mod_l, gnorm, w_in_bf, *tabs)


def _attn_body(sink_ref, q_ref, kc_ref, k0_ref, k1_ref, k2_ref, k3_ref, vc_ref, v0_ref, v1_ref, v2_ref, v3_ref,
               o_ref, *, has_ctx):
    j = pl.program_id(1)
    p = j - 1 if has_ctx else j
    nr = ATTN_GROUP * ATTN_BLOCK
    groups = [slice(g * ATTN_BLOCK, (g + 1) * ATTN_BLOCK) for g in range(ATTN_GROUP)]

    def attend(sub, pieces):
        rows = slice(sub * ATTN_BLOCK, (sub + 1) * ATTN_BLOCK)
        outs = []
        for h in range(ATTN_KV_HEADS):
            q = q_ref[rows, h * 256:(h + 1) * 256]
            q4 = jnp.concatenate([q[:, g * 64:(g + 1) * 64] for g in range(ATTN_GROUP)], axis=0)
            sinks = [sink_ref[h * ATTN_GROUP + g] * LOG2E for g in range(ATTN_GROUP)]
            k_all = jnp.concatenate([k_ref[h] for k_ref, _, _ in pieces], axis=0)
            v_all = jnp.concatenate([v_ref[h] for _, v_ref, _ in pieces], axis=0)
            s = _dot_nt(q4, k_all)
            cols, c0 = [], 0
            for k_ref, _, ok in pieces:
                n = k_ref.shape[1]
                cols.append(s[:, c0:c0 + n] if ok is None else jnp.where(ok, s[:, c0:c0 + n], NEG))
                c0 += n
            s = jnp.concatenate(cols, axis=1)
            mxs, ps = [], []
            for g, r in enumerate(groups):
                mx = jnp.maximum(jnp.max(s[r], axis=-1, keepdims=True), sinks[g])
                mxs.append(mx)
                ps.append(jnp.exp2(s[r] - mx).astype(BF16))
            oa = _dot(jnp.concatenate(ps, axis=0), v_all)
            for g, r in enumerate(groups):
                den = oa[r, HEAD_DIM:HEAD_DIM + 1] + jnp.exp2(sinks[g] - mxs[g])
                outs.append(oa[r, 0:HEAD_DIM] / den)
        o_ref[rows, :] = jnp.concatenate(outs, axis=1).astype(BF16)

    @pl.when(p >= 0)
    def _():
        ri = lax.broadcasted_iota(jnp.int32, (nr, ATTN_BLOCK), 0) & (ATTN_BLOCK - 1)
        ci = lax.broadcasted_iota(jnp.int32, (nr, ATTN_BLOCK), 1)
        far = 4 * ATTN_BLOCK
        first_prev = ci >= ri + jnp.where(p >= 1, 0, far)
        last_next = ci + jnp.where(p <= SEQ // (2 * ATTN_BLOCK) - 2, 0, far) <= ri
        attend(0, [(kc_ref, vc_ref, None), (k0_ref, v0_ref, first_prev), (k1_ref, v1_ref, None),
                   (k2_ref, v2_ref, ci <= ri)])
        attend(1, [(kc_ref, vc_ref, None), (k1_ref, v1_ref, ci >= ri), (k2_ref, v2_ref, None),
                   (k3_ref, v3_ref, last_next)])

    if has_ctx:
        @pl.when(p < 0)
        def _():
            attend(0, [(kc_ref, vc_ref, None)])
            attend(1, [(kc_ref, vc_ref, None)])


def _attention(sink, qa, ka, va, need_ctx):
    nb = SEQ // ATTN_BLOCK
    npair = nb // 2
    first = 1 if need_ctx else 0
    tq = 2 * ATTN_BLOCK
    lat0 = RC // tq
    out0 = 0 if need_ctx else lat0

    def qrow(b, j):
        return jnp.where(j < first, b, lat0 + b * npair + j - first)

    def loc(delta):
        def f(b, j):
            m = jnp.clip(2 * (j - first) + delta, 0, nb - 1)
            return (0, RC // ATTN_BLOCK + b * nb + m, 0)
        return f

    def kv_specs(width):
        ctx_spec = pl.BlockSpec((ATTN_KV_HEADS, CTX_LEN, width), lambda b, j: (0, b, 0))
        return [ctx_spec] + [pl.BlockSpec((ATTN_KV_HEADS, ATTN_BLOCK, width), loc(d)) for d in (-1, 0, 1, 2)]

    return pl.pallas_call(
        functools.partial(_attn_body, has_ctx=need_ctx),
        grid=(BATCH, npair + first),
        in_specs=[pl.BlockSpec(memory_space=pltpu.SMEM),
                  pl.BlockSpec((tq, W), lambda b, j: (qrow(b, j), 0))]
                 + kv_specs(HEAD_DIM) + kv_specs(128),
        out_specs=pl.BlockSpec((tq, W), lambda b, j: (qrow(b, j) - out0, 0)),
        out_shape=jax.ShapeDtypeStruct((R - out0 * tq, W), BF16),
        compiler_params=_params(("arbitrary", "arbitrary")),
        name="window_attn",
    )(sink, qa, *([ka] * 5), *([va] * 5))


def _fourier_body(*refs, has_ctx):
    if has_ctx:
        uc_ref, ul_ref, bdc_ref, bds_ref, w2_ref, w2c_ref, oc_ref, ol_ref, as_ref = refs
    else:
        ul_ref, bdc_ref, bds_ref, w2_ref, ol_ref, as_ref = refs
    j = pl.program_id(1)
    first = 1 if has_ctx else 0

    if has_ctx:
        @pl.when(j == 0)
        def _():
            u = uc_ref[...]
            a = _dot(u, bdc_ref[...]).astype(BF16)
            s = _dot(u, bds_ref[...]).astype(BF16)
            z = _dot(w2c_ref[...], jnp.concatenate([a, s], axis=0))
            oc_ref[...] = (z * ((CTX_LEN * FOURIER_DIM) ** -0.5)).astype(BF16)

    @pl.when(j == first)
    def _():
        u = ul_ref[...]
        as_ref[0:SEQ, :] = _dot(u, bdc_ref[...]).astype(BF16)
        as_ref[SEQ:2 * SEQ, :] = _dot(u, bds_ref[...]).astype(BF16)

    @pl.when(j >= first)
    def _():
        ol_ref[...] = (_dot(w2_ref[...], as_ref[...]) * ((SEQ * FOURIER_DIM) ** -0.5)).astype(BF16)


def _fourier(fu, dft, need_ctx):
    bdc, bds, w2, w2c = dft
    tr = TR_FOURIER
    nj = SEQ // tr
    first = 1 if need_ctx else 0
    full = lambda a: pl.BlockSpec(a.shape, lambda *_: (0,) * a.ndim)
    lat_tile = lambda j: jnp.maximum(j - first, 0)
    ul_spec = pl.BlockSpec((SEQ, W), lambda b, j: (1 + b, 0))
    w2_spec = pl.BlockSpec((tr, 2 * SEQ), lambda b, j: (lat_tile(j), 0))
    ol_spec = pl.BlockSpec((tr, W), lambda b, j: (b * nj + lat_tile(j), 0))
    ol_shape = jax.ShapeDtypeStruct((RL, W), BF16)
    if need_ctx:
        in_specs = [pl.BlockSpec((CTX_LEN, W), lambda b, j: (b, 0)), ul_spec, full(bdc), full(bds), w2_spec,
                    full(w2c)]
        out_specs = [pl.BlockSpec((CTX_LEN, W), lambda b, j: (b, 0)), ol_spec]
        out_shape = [jax.ShapeDtypeStruct((RC, W), BF16), ol_shape]
        args = (fu, fu, bdc, bds, w2, w2c)
    else:
        in_specs = [ul_spec, full(bdc), full(bds), w2_spec]
        out_specs = [ol_spec]
        out_shape = [ol_shape]
        args = (fu, bdc, bds, w2)
    outs = pl.pallas_call(
        functools.partial(_fourier_body, has_ctx=need_ctx),
        grid=(BATCH, nj + first),
        in_specs=in_specs,
        out_specs=out_specs,
        out_shape=out_shape,
        scratch_shapes=[pltpu.VMEM((2 * SEQ, W), BF16)],
        compiler_params=_params(("arbitrary", "arbitrary")),
        name="fourier_mix",
    )(*args)
    return (outs[0], outs[1]) if need_ctx else (None, outs[0])


def _retention_body(qc_ref, kc_ref, vc_ref, ql0_ref, kl0_ref, vl0_ref, ql1_ref, kl1_ref, vl1_ref, g_ref,
                    dcomb_ref, xif_ref, ztf_ref, xib_ref, ztb_ref, gf_ref, gb_ref, mbd_ref, avg_ref,
                    o_ref, os_ref, st_ref):
    j = pl.program_id(1)
    C = RET_CHUNK
    nl = SEQ // C
    lat_refs = ((ql0_ref, kl0_ref, vl0_ref), (ql1_ref, kl1_ref, vl1_ref))

    own = ((lax.broadcasted_iota(jnp.int32, (4 * C, 256), 0) >> 7)
           == (lax.broadcasted_iota(jnp.int32, (4 * C, 256), 1) >> 6))

    def chunk_fwd(bb, q, k, v, r0):
        for gi in range(2):
            sl = slice(gi * 256, (gi + 1) * 256)
            q4, k4, v4 = q[:, sl], k[:, sl], v[:, sl]
            s_prev = st_ref[2 * bb + gi]
            q4f = q4.astype(F32)
            o4 = _dot((q4f * xif_ref[:, sl]).astype(BF16), s_prev.astype(BF16))
            qstack = jnp.where(own, jnp.concatenate([q4f] * 4, axis=0), 0.0).astype(BF16)
            p = (_dot_nt(qstack, k4) * dcomb_ref[gi]).astype(BF16)
            ov = jnp.where(own, _dot(p, v4), 0.0)
            intra = ov[0:C] + ov[C:2 * C] + ov[2 * C:3 * C] + ov[3 * C:4 * C]
            os_ref[pl.ds(r0, C), sl] = o4 + intra
            u = _dot_tn(k4, (v4.astype(F32) * ztf_ref[:, sl]).astype(BF16))
            st_ref[2 * bb + gi] = gf_ref[gi] * s_prev + mbd_ref[...] * u

    def chunk_bwd(bb, q, k, v, r0):
        for gi in range(2):
            sl = slice(gi * 256, (gi + 1) * 256)
            q4, k4, v4 = q[:, sl], k[:, sl], v[:, sl]
            s_prev = st_ref[2 * bb + gi]
            cross = _dot((q4.astype(F32) * xib_ref[:, sl]).astype(BF16), s_prev.astype(BF16))
            os_ref[pl.ds(r0, C), sl] = os_ref[pl.ds(r0, C), sl] + cross
            u = _dot_tn(k4, (v4.astype(F32) * ztb_ref[:, sl]).astype(BF16))
            st_ref[2 * bb + gi] = gb_ref[gi] * s_prev + mbd_ref[...] * u

    def scan(chunk, ctx_order, lat_index):
        st_ref[...] = jnp.zeros_like(st_ref)
        for c in ctx_order:
            for bb in range(2):
                rs = slice(bb * CTX_LEN + c * C, bb * CTX_LEN + (c + 1) * C)
                chunk(bb, qc_ref[rs, :], kc_ref[rs, :], vc_ref[rs, :], bb * CTX_LEN + c * C)

        def body(t, carry):
            r0 = pl.multiple_of(lat_index(t) * C, C)
            for bb, (q_ref, k_ref, v_ref) in enumerate(lat_refs):
                rs = pl.ds(r0, C)
                chunk(bb, q_ref[rs, :], k_ref[rs, :], v_ref[rs, :], 2 * CTX_LEN + bb * SEQ + r0)
            return carry

        lax.fori_loop(0, nl, body, 0)

    @pl.when(j == 0)
    def _():
        scan(chunk_fwd, range(CTX_LEN // C), lambda t: t)
        scan(chunk_bwd, reversed(range(CTX_LEN // C)), lambda t: nl - 1 - t)

    o = os_ref[pl.ds(pl.multiple_of(j * RET_SLAB, RET_SLAB), RET_SLAB), :]
    avg = avg_ref[...]
    oh, ol = _split(o)
    mu = _dot(oh, avg) + _dot(ol, avg)
    d = o - mu
    var = _dot((d * d).astype(BF16), avg)
    o_ref[...] = (g_ref[...].astype(F32) * d * lax.rsqrt(var + GN_EPS)).astype(BF16)


def _retention(qr, kr, vr, gr, rtabs):
    nslab = SEQ // RET_SLAB
    nj = 1 + 2 * nslab

    def out_map(b2, j):
        return (jnp.where(j == 0, b2, RC // RET_SLAB + 2 * b2 * nslab + j - 1), 0)

    ctx = pl.BlockSpec((2 * CTX_LEN, W), lambda b2, j: (b2, 0))
    lat = lambda k: pl.BlockSpec((SEQ, W), lambda b2, j: (1 + 2 * b2 + k, 0))
    full = lambda a: pl.BlockSpec(a.shape, lambda *_: (0,) * a.ndim)
    avg = jnp.asarray(np.kron(np.eye(RET_HEADS), np.full((RET_DK, RET_DK), 1.0 / RET_DK)).astype(np.float32)).astype(BF16)
    mbd = jnp.asarray(np.kron(np.eye(4), np.ones((RET_DK, RET_DK))).astype(np.float32))
    tabs = list(rtabs) + [mbd, avg]
    return pl.pallas_call(
        _retention_body,
        grid=(BATCH // 2, nj),
        in_specs=[ctx, ctx, ctx, lat(0), lat(0), lat(0), lat(1), lat(1), lat(1),
                  pl.BlockSpec((RET_SLAB, W), out_map)] + [full(t) for t in tabs],
        out_specs=pl.BlockSpec((RET_SLAB, W), out_map),
        out_shape=jax.ShapeDtypeStruct((R, W), BF16),
        scratch_shapes=[pltpu.VMEM((2 * (CTX_LEN + SEQ), W), F32), pltpu.VMEM((4, 256, 256), F32)],
        compiler_params=_params(("arbitrary", "arbitrary")),
        name="retention",
    )(qr, kr, vr, qr, kr, vr, qr, kr, vr, gr, *tabs)


def _pack_pair(a, b):
    ua = lax.bitcast_convert_type(a, jnp.uint32) >> 16
    ub = lax.bitcast_convert_type(b, jnp.uint32) & jnp.uint32(0xFFFF0000)
    return ua | ub


def _unpack_pair(w):
    a = lax.bitcast_convert_type(w << 16, F32)
    b = lax.bitcast_convert_type(w & jnp.uint32(0xFFFF0000), F32)
    return a, b


def _slot_onehot(s0, s1, n):
    srow = lax.broadcasted_iota(jnp.int32, (n, s0.shape[1]), 0)
    p0 = jnp.where(srow == s0, 1.0, 0.0).astype(BF16)
    p1 = jnp.where(srow == s1, 1.0, 0.0).astype(BF16)
    return p0, p1


def _pick(first, a_ref, b_ref):
    a = a_ref[...]
    flag = jnp.zeros(a.shape, jnp.int32) + first.astype(jnp.int32)
    return jnp.where(flag > 0, a, b_ref[...])


def _merge_body(*refs, layer0):
    if layer0:
        oa_ref, ofc_ref, ofl_ref, rt_ref, gm_ref, xc_ref, xl_ref = refs[:7]
        rest = refs[7:]
        is_ctx = pl.program_id(0) < RC // TM_MERGE
        of_in = _pick(is_ctx, ofc_ref, ofl_ref)
        x_in = _pick(is_ctx, xc_ref, xl_ref)
    else:
        oa_ref, ofl_ref, rt_ref, gm_ref, x_ref = refs[:5]
        rest = refs[5:]
        of_in = ofl_ref[...]
        x_in = x_ref[...]
    (mod_ref, gn_ref, wba_ref, wbf_ref, wbr_ref, wout_ref, wrh_ref, wrl_ref, br_ref, tri_ref, ltri_ref,
     xo_ref, xs_ref, ro_ref, nch_ref) = rest
    gm = gm_ref[...].astype(F32)
    z = (gm[:, 0:D_MODEL] * _dot(oa_ref[...], wba_ref[...])
         + gm[:, D_MODEL:2 * D_MODEL] * _dot(of_in, wbf_ref[...])
         + gm[:, 2 * D_MODEL:3 * D_MODEL] * _dot(rt_ref[...], wbr_ref[...]))
    y = _dot(z.astype(BF16), wout_ref[...])
    x = x_in + mod_ref[2:3, :] * y
    xo_ref[...] = x
    ms = jnp.mean(x * x, axis=-1, keepdims=True)
    hn = x * lax.rsqrt(ms + NORM_EPS) * gn_ref[...]
    h2 = hn * (1.0 + mod_ref[4:5, :]) + mod_ref[3:4, :]
    hh, hl = _split(h2)
    wh, wl = wrh_ref[...], wrl_ref[...]
    lg = _dot_nt(wh, hh) + _dot_nt(wh, hl) + _dot_nt(wl, hh) + br_ref[...]
    tm = lg.shape[1]
    row8 = lax.broadcasted_iota(jnp.int32, (8, tm), 0)
    lgg = lg[0:8, :]
    mg = jnp.max(lgg, axis=0, keepdims=True)
    grp = jnp.min(jnp.where(lgg == mg, row8, 8), axis=0, keepdims=True)
    pg = 1.0 / jnp.sum(jnp.exp(lgg - mg), axis=0, keepdims=True)
    lin = jnp.zeros((8, tm), F32)
    for g in range(N_GROUPS):
        lin = jnp.where(grp == g, lg[8 + 8 * g:16 + 8 * g, :], lin)
    v1 = jnp.max(lin, axis=0, keepdims=True)
    i1 = jnp.min(jnp.where(lin == v1, row8, 8), axis=0, keepdims=True)
    rest = jnp.where(row8 == i1, -jnp.inf, lin)
    v2 = jnp.max(rest, axis=0, keepdims=True)
    i2 = jnp.min(jnp.where(rest == v2, row8, 8), axis=0, keepdims=True)
    e2 = jnp.exp(v2 - v1)
    w1 = pg / (1.0 + e2)
    w2 = pg * e2 / (1.0 + e2)
    e_1 = grp * EXPERTS_PER_GROUP + i1
    e_2 = grp * EXPERTS_PER_GROUP + i2

    row32 = lax.broadcasted_iota(jnp.int32, (N_EXPERTS, tm), 0)
    oh0 = jnp.where(row32 == e_1, 1.0, 0.0)
    oh1 = jnp.where(row32 == e_2, 1.0, 0.0)
    tri = tri_ref[...]
    cum0 = _dot(oh0.astype(BF16), tri)
    cum1 = _dot(oh1.astype(BF16), tri)
    tot0 = jnp.sum(oh0, axis=1, keepdims=True)
    tot1 = jnp.sum(oh1, axis=1, keepdims=True)
    nch = ((tot0 + tot1).astype(jnp.int32) + (CHUNK - 1)) >> CHUNK_SHIFT
    nch_b = jnp.broadcast_to(nch.astype(F32), (N_EXPERTS, 128))
    nch_ref[...] = nch_b.astype(jnp.int32)
    base = CHUNK * _dot(ltri_ref[...], nch_b.astype(BF16))[:, 0:1]
    s0 = jnp.sum(oh0 * (base + cum0), axis=0, keepdims=True).astype(jnp.int32)
    s1 = jnp.sum(oh1 * (base + tot0 + cum1), axis=0, keepdims=True).astype(jnp.int32)
    p0, p1 = _slot_onehot(s0, s1, SLOTS)
    xs = _dot(p0 + p1, hh)
    xs_ref[:, 0:D_MODEL // 2] = _pack_pair(xs[:, 0:D_MODEL // 2], xs[:, D_MODEL // 2:D_MODEL])

    def wrows(w):
        hi, lo = _split(w)
        return jnp.where(row8 == 0, hi.astype(F32), jnp.where(row8 == 1, lo.astype(F32), 0.0)).astype(BF16)

    wc = _dot_nt(p0, wrows(w1)) + _dot_nt(p1, wrows(w2))
    wcol = jnp.broadcast_to(wc[:, 0:1] + wc[:, 1:2], (SLOTS, 128))
    xs_ref[:, D_MODEL // 2:XS_COLS] = lax.bitcast_convert_type(wcol, jnp.uint32)
    s0f, s1f = s0.astype(F32), s1.astype(F32)
    ro_ref[...] = jnp.where(row8 == 0, s0f, jnp.where(row8 == 1, s1f, 0.0))


def _merge(oa, of_c, of_l, ret, gm, xs_in, mod_l, gnorm, wba, wbf, wbr, wout, wrh, wrl, brb, layer0):
    tm = TM_MERGE
    row0 = 0 if layer0 else RC
    rm = R - row0
    nt = rm // tm
    off = row0 // tm
    nc = RC // tm
    src = lambda n: pl.BlockSpec((tm, n), lambda i: (i + off, 0))
    dst = lambda n: pl.BlockSpec((tm, n), lambda i: (i, 0))
    ctx_rows = lambda n: pl.BlockSpec((tm, n), lambda i: (jnp.minimum(i, nc - 1), 0))
    lat_rows = lambda n: pl.BlockSpec((tm, n), lambda i: (jnp.maximum(i - nc, 0), 0))
    full = lambda a: pl.BlockSpec(a.shape, lambda *_: (0,) * a.ndim, pipeline_mode=pl.Buffered(1))
    mrow = _mod_row(row0, tm)
    tri = jnp.asarray(np.triu(np.ones((tm, tm), np.float32), 1)).astype(BF16)
    ltri = jnp.asarray(np.tril(np.ones((N_EXPERTS, N_EXPERTS), np.float32), -1)).astype(BF16)
    if layer0:
        acts = [oa, of_c, of_l, ret, gm, xs_in[0], xs_in[1]]
        act_specs = [src(W), ctx_rows(W), lat_rows(W), src(W), src(3 * D_MODEL), ctx_rows(D_MODEL),
                     lat_rows(D_MODEL)]
    else:
        acts = [oa, of_l, ret, gm, xs_in]
        act_specs = [dst(W), dst(W), src(W), src(3 * D_MODEL), src(D_MODEL)]
    return pl.pallas_call(
        functools.partial(_merge_body, layer0=layer0),
        grid=(nt,),
        in_specs=act_specs + [
                  pl.BlockSpec((None, 6, D_MODEL), lambda i: (mrow(i), 0, 0)),
                  full(gnorm), full(wba), full(wbf), full(wbr), full(wout), full(wrh), full(wrl), full(brb),
                  full(tri), full(ltri)],
        out_specs=[dst(D_MODEL), pl.BlockSpec((SLOTS, XS_COLS), lambda i: (i, 0)),
                   pl.BlockSpec((8, tm), lambda i: (0, i)),
                   pl.BlockSpec((None, N_EXPERTS, 128), lambda i: (i, 0, 0))],
        out_shape=[jax.ShapeDtypeStruct((rm, D_MODEL), F32),
                   jax.ShapeDtypeStruct((nt * SLOTS, XS_COLS), jnp.uint32),
                   jax.ShapeDtypeStruct((8, rm), F32),
                   jax.ShapeDtypeStruct((nt, N_EXPERTS, 128), jnp.int32)],
        compiler_params=_params(("arbitrary",)),
        name="merge_router",
    )(*acts, mod_l, gnorm, wba, wbf, wbr, wout, wrh, wrl, brb, tri, ltri)


def _moe_plan(nch, nb):
    nt = nch.shape[0]
    choff = jnp.cumsum(nch, axis=1) - nch
    used_ch = jnp.sum(nch, axis=1)
    cum_t = jnp.cumsum(nch, axis=0)
    tot = cum_t[-1]
    ptot = (tot + CPB - 1) // CPB * CPB
    pend = jnp.cumsum(ptot)
    pstart = pend - ptot
    n_used = pend[-1] // CPB
    blk = jnp.arange(nb, dtype=jnp.int32)
    lane = jnp.arange(CPB, dtype=jnp.int32)
    blk_e = jnp.minimum(jnp.sum((blk[:, None] * CPB >= pend[None, :]).astype(jnp.int32), axis=1), N_EXPERTS - 1)
    oe = (blk_e[:, None] == jnp.arange(N_EXPERTS, dtype=jnp.int32)[None, :]).astype(jnp.int32)
    sel = lambda tab: jnp.sum(oe[:, :, None] * tab.T[None, :, :], axis=1)
    pstart_b = jnp.sum(oe * pstart[None, :], axis=1)
    tot_b = jnp.sum(oe * tot[None, :], axis=1)
    cum_b, nch_b, choff_b = sel(cum_t), sel(nch), sel(choff)
    i = blk[:, None] * CPB + lane[None, :] - pstart_b[:, None]
    valid = (i < tot_b[:, None]) & (blk[:, None] < n_used)
    t = jnp.minimum(jnp.sum((i[:, :, None] >= cum_b[:, None, :]).astype(jnp.int32), axis=2), nt - 1)
    tiles = jnp.arange(nt, dtype=jnp.int32)[None, None, :]
    before = jnp.sum(jnp.where(tiles < t[:, :, None], nch_b[:, None, :], 0), axis=2)
    coff = jnp.sum(jnp.where(tiles == t[:, :, None], choff_b[:, None, :], 0), axis=2)
    row = t * SLOTS + CHUNK * (coff + i - before)
    src = jnp.where(valid, row, SLOTS - CHUNK)
    dummy = nt * SLOTS + CHUNK * ((blk[:, None] % 2) * CPB + lane[None, :])
    dst = jnp.where(valid, row, dummy)
    blk_start = jnp.concatenate([pstart, pend[-1:]]) // CPB
    return (blk_start.astype(jnp.int32), n_used.astype(jnp.int32).reshape(1), src.reshape(-1).astype(jnp.int32),
            dst.reshape(-1).astype(jnp.int32), used_ch.astype(jnp.int32))


def _ffn_body(bs_ref, nu_ref, src_ref, dst_ref, uc_ref, xs_ref, wg_ref, wu_ref, wd_ref, ys_ref,
              xbuf, ybuf, zbuf, wgb, wub, wdb, sem_in, sem_out, sem_z, *, nt):
    e = pl.program_id(0)
    nu = nu_ref[0]
    half = D_MODEL // 2

    def gather(blk, sl):
        for c in range(CPB):
            r = pl.multiple_of(src_ref[blk * CPB + c], CHUNK)
            pltpu.make_async_copy(xs_ref.at[pl.ds(r, CHUNK)], xbuf.at[sl, pl.ds(c * CHUNK, CHUNK)],
                                  sem_in.at[sl]).start(priority=1)

    def scatter(blk, sl):
        for c in range(CPB):
            r = pl.multiple_of(dst_ref[blk * CPB + c], CHUNK)
            pltpu.make_async_copy(ybuf.at[sl, pl.ds(c * CHUNK, CHUNK)], ys_ref.at[pl.ds(r, CHUNK)],
                                  sem_out.at[sl]).start(priority=1)

    def wait_gather(sl):
        pltpu.make_async_copy(xs_ref.at[pl.ds(0, MOE_BM)], xbuf.at[sl], sem_in.at[sl]).wait()

    def wait_scatter(sl):
        pltpu.make_async_copy(ybuf.at[sl], ys_ref.at[pl.ds(0, MOE_BM)], sem_out.at[sl]).wait()

    def zero_copy(r):
        return pltpu.make_async_copy(zbuf, ys_ref.at[pl.ds(pl.multiple_of(r, CHUNK), CHUNK)], sem_z)

    @pl.when(e == 0)
    def _():
        zbuf[...] = jnp.zeros_like(zbuf)

        def tails(fn):
            def per_tile(t, carry):
                def per_chunk(c, carry2):
                    fn(t * SLOTS + c * CHUNK)
                    return carry2
                lax.fori_loop(uc_ref[t], SLOTS // CHUNK, per_chunk, 0)
                return carry
            lax.fori_loop(0, nt, per_tile, 0)
            for c in range(2 * CPB):
                fn(nt * SLOTS + c * CHUNK)

        tails(lambda r: zero_copy(r).start())
        tails(lambda r: zero_copy(r).wait())
        gather(0, 0)

    b0, b1 = bs_ref[e], bs_ref[e + 1]

    @pl.when(b1 > b0)
    def _():
        wgb[...] = wg_ref[...].astype(BF16)
        wub[...] = wu_ref[...].astype(BF16)
        wdb[...] = wd_ref[...].astype(BF16)

        def block(b, carry):
            slot = b % 2

            @pl.when(b + 1 < nu)
            def _():
                gather(b + 1, 1 - slot)

            wait_gather(slot)

            @pl.when(b >= 2)
            def _():
                wait_scatter(slot)

            xw = xbuf[slot]
            xa, xb = _unpack_pair(xw[:, 0:half])
            x = jnp.concatenate([xa, xb], axis=1).astype(BF16)
            wt = lax.bitcast_convert_type(xw[:, half:XS_COLS], F32)
            g = _dot(x, wgb[...])
            u = _dot(x, wub[...])
            hmid = (g * _sigmoid(g) * u).astype(BF16)
            y = _dot(hmid, wdb[...]) * jnp.concatenate([wt] * (D_MODEL // 128), axis=1)
            yb = y.astype(BF16).astype(F32)
            ybuf[slot] = _pack_pair(yb[:, 0:half], yb[:, half:D_MODEL])
            scatter(b, slot)
            return carry

        lax.fori_loop(b0, b1, block, 0)

    @pl.when(e == N_EXPERTS - 1)
    def _():
        wait_scatter((nu - 1) % 2)

        @pl.when(nu >= 2)
        def _():
            wait_scatter(nu % 2)


def _ffn(plan, xs, w_g, w_u, w_d, layer, nt):
    wmap = lambda e, *_: (layer, e, 0, 0)
    half = D_MODEL // 2
    grid_spec = pltpu.PrefetchScalarGridSpec(
        num_scalar_prefetch=5,
        grid=(N_EXPERTS,),
        in_specs=[pl.BlockSpec(memory_space=pl.ANY),
                  pl.BlockSpec((None, None, D_MODEL, EXPERT_HIDDEN), wmap),
                  pl.BlockSpec((None, None, D_MODEL, EXPERT_HIDDEN), wmap),
                  pl.BlockSpec((None, None, EXPERT_HIDDEN, D_MODEL), wmap)],
        out_specs=pl.BlockSpec(memory_space=pl.ANY),
        scratch_shapes=[pltpu.VMEM((2, MOE_BM, XS_COLS), jnp.uint32), pltpu.VMEM((2, MOE_BM, half), jnp.uint32),
                        pltpu.VMEM((CHUNK, half), jnp.uint32),
                        pltpu.VMEM((D_MODEL, EXPERT_HIDDEN), BF16), pltpu.VMEM((D_MODEL, EXPERT_HIDDEN), BF16),
                        pltpu.VMEM((EXPERT_HIDDEN, D_MODEL), BF16),
                        pltpu.SemaphoreType.DMA((2,)), pltpu.SemaphoreType.DMA((2,)), pltpu.SemaphoreType.DMA(())],
    )
    return pl.pallas_call(
        functools.partial(_ffn_body, nt=nt),
        grid_spec=grid_spec,
        out_shape=jax.ShapeDtypeStruct((nt * SLOTS + 2 * CPB * CHUNK, half), jnp.uint32),
        compiler_params=_params(("arbitrary",)),
        name="moe_experts",
    )(*plan, xs, w_g, w_u, w_d)


def _combine_body(ys_ref, ro_ref, x_ref, mod_ref, gn_ref, o_ref, *, final):
    s = ro_ref[...]
    p0, p1 = _slot_onehot(s[0:1, :].astype(jnp.int32), s[1:2, :].astype(jnp.int32), SLOTS)
    ya, yb = _unpack_pair(ys_ref[...])
    y = jnp.concatenate([ya, yb], axis=1).astype(BF16)
    f = _dot_tn(p0 + p1, y)
    x = x_ref[...] + mod_ref[5:6, :] * f
    if final:
        ms = jnp.mean(x * x, axis=-1, keepdims=True)
        x = x * lax.rsqrt(ms + NORM_EPS) * gn_ref[...]
    o_ref[...] = x


def _combine(ys, route, x, mod_l, gnorm, row0, final):
    tm = TM_MERGE
    rm = x.shape[0]
    mrow = _mod_row(row0, tm)
    return pl.pallas_call(
        functools.partial(_combine_body, final=final),
        grid=(rm // tm,),
        in_specs=[pl.BlockSpec((SLOTS, D_MODEL // 2), lambda i: (i, 0)),
                  pl.BlockSpec((8, tm), lambda i: (0, i)),
                  pl.BlockSpec((tm, D_MODEL), lambda i: (i, 0)),
                  pl.BlockSpec((None, 6, D_MODEL), lambda i: (mrow(i), 0, 0)),
                  pl.BlockSpec((1, D_MODEL), lambda i: (0, 0))],
        out_specs=pl.BlockSpec((tm, D_MODEL), lambda i: (i, 0)),
        out_shape=jax.ShapeDtypeStruct((rm, D_MODEL), F32),
        compiler_params=_params(("arbitrary",)),
        name="moe_combine",
    )(ys, route, x, mod_l, gnorm)


def _moe(xs, route, nch3, x, mod_l, gnorm, w_g, w_u, w_d, row0, final, layer):
    nt = nch3.shape[0]
    max_chunks = nt * ((2 * TM_MERGE + N_EXPERTS * (CHUNK - 1)) // CHUNK)
    nb = -(-max_chunks // CPB) + N_EXPERTS
    plan = _moe_plan(nch3[:, :, 0], nb)
    ys = _ffn(plan, xs, w_g, w_u, w_d, layer, nt)
    return _combine(ys, route, x, mod_l, gnorm, row0, final)


def kernel(x, c, ctx, c_ctx, norm_mix, norm_ffn, w_ada, b_ada, w_in, attn_sink, ret_decay_fwd, ret_decay_bwd,
           w_branch_attn, w_branch_fourier, w_branch_ret, w_out, w_router_group, b_router_group,
           w_router_expert, b_router_expert, w_exp_gate, w_exp_up, w_exp_down, norm_final):
    tabs = [jnp.asarray(t) for t in _rope_tables()]
    dft = [jnp.asarray(t).astype(BF16) for t in _dft_tables()]

    cc = jnp.zeros((MOD_ROWS, D_MODEL), F32).at[0:BATCH].set(c).at[CTX_MOD_ROW].set(c_ctx)
    mod = _ada(cc, w_ada, b_ada).reshape(DEPTH, MOD_ROWS, 6, D_MODEL)

    xf = (ctx.reshape(RC, D_MODEL), x.reshape(RL, D_MODEL))
    w_in_bf = w_in.astype(BF16)
    for l in range(DEPTH):
        need_ctx = l < DEPTH - 1
        row0 = 0 if need_ctx else RC
        mod_l = mod[l]
        qa, ka, va, qr, kr, vr, gr, fu, gm = _proj(xf, mod_l, norm_mix[l][None, :], w_in_bf, tabs, l)
        oa = _attention(attn_sink[l], qa, ka, va, need_ctx)
        of_c, of_l = _fourier(fu, dft, need_ctx)
        ret = _retention(qr, kr, vr, gr, _retention_tables(ret_decay_fwd[l], ret_decay_bwd[l]))
        wr = jnp.zeros((ROUTER_ROWS, D_MODEL), F32)
        wr = wr.at[0:N_GROUPS].set(w_router_group[l].T).at[8:8 + N_EXPERTS].set(w_router_expert[l].T)
        br = jnp.full((ROUTER_ROWS,), NEG, F32)
        br = br.at[0:N_GROUPS].set(b_router_group[l]).at[8:8 + N_EXPERTS].set(b_router_expert[l])
        wrh, wrl = _split(wr)
        brb = jnp.broadcast_to(br[:, None], (ROUTER_ROWS, TM_MERGE))
        x_mid, xs, route, nch3 = _merge(oa, of_c, of_l, ret, gm, xf, mod_l, norm_ffn[l][None, :],
                                        w_branch_attn[l].astype(BF16), w_branch_fourier[l].astype(BF16),
                                        w_branch_ret[l].astype(BF16), w_out[l].astype(BF16), wrh, wrl, brb,
                                        need_ctx)
        final = l == DEPTH - 1
        xf = _moe(xs, route, nch3, x_mid, mod_l, norm_final[None, :], w_exp_gate, w_exp_up, w_exp_down,
                  row0, final, l)
    return xf.reshape(BATCH, SEQ, D_MODEL)
```

```python
import functools

import numpy as np
import jax
import jax.numpy as jnp
from jax import lax
from jax.experimental import pallas as pl
from jax.experimental.pallas import tpu as pltpu

F32 = jnp.float32
BF16 = jnp.bfloat16

D_MODEL = 1024
BATCH = 8
SEQ = 2048
DEPTH = 2
CTX_LEN = 256
GRID_W = 64
HEAD_DIM = 64
ATTN_HEADS = 8
ATTN_KV_HEADS = 2
ATTN_GROUP = ATTN_HEADS // ATTN_KV_HEADS
ATTN_BLOCK = 128
RET_HEADS = 8
RET_DK = 64
RET_CHUNK = 128
FOURIER_GROUPS = 4
FOURIER_DIM = 128
N_GROUPS = 4
EXPERTS_PER_GROUP = 8
N_EXPERTS = N_GROUPS * EXPERTS_PER_GROUP
EXPERT_HIDDEN = 512
ROPE_BASE = 10000.0
NORM_EPS = 1e-6
GN_EPS = 1e-5

W = 512
IN_COLS = 6400
RC = BATCH * CTX_LEN
RL = BATCH * SEQ
R = RC + RL
MOD_ROWS = 16
CTX_MOD_ROW = 8

VMEM_LIMIT = 52 * 1024 * 1024

TM_PROJ = 512
TM_MERGE = 512
TN_ADA = 1536
TR_FOURIER = 512
RET_SLAB = 2 * CTX_LEN
MOE_BM = 512
CHUNK = 8
CHUNK_SHIFT = 3
CPB = MOE_BM // CHUNK
GATHER_DEPTH = 3
SLOTS = 1280
XS_COLS = D_MODEL // 2 + 128
NEG = -1e30
LOG2E = 1.4426950408889634
ROUTER_ROWS = 40


def _dot(a, b):
    return jnp.dot(a, b, preferred_element_type=F32)


def _dot_nt(a, b):
    return lax.dot_general(a, b, (((1,), (1,)), ((), ())), preferred_element_type=F32)


def _dot_tn(a, b):
    return lax.dot_general(a, b, (((0,), (0,)), ((), ())), preferred_element_type=F32)


def _split(x):
    hi = x.astype(BF16)
    lo = (x - hi.astype(F32)).astype(BF16)
    return hi, lo


def _sigmoid(x):
    return 1.0 / (1.0 + jnp.exp(-x))


def _params(sem, vmem=VMEM_LIMIT):
    return pltpu.CompilerParams(dimension_semantics=sem, vmem_limit_bytes=vmem)


def _mod_row(row0, tm):
    def f(i):
        g0 = i * tm + row0
        return jnp.where(g0 < RC, CTX_MOD_ROW, (g0 - RC) // SEQ)
    return f


def _rope_tables():
    pos = np.arange(SEQ, dtype=np.float64)
    row = np.floor(pos / GRID_W)
    col = pos % GRID_W

    def cs(p, nf):
        inv = ROPE_BASE ** (-np.arange(nf, dtype=np.float64) / nf)
        ang = p[:, None] * inv[None, :]
        return np.cos(ang), np.sin(ang)

    rc, rs = cs(row, HEAD_DIM // 4)
    cc, cs_ = cs(col, HEAD_DIM // 4)
    cos_a = np.concatenate([rc, rc, cc, cc], axis=1)
    sin_a = np.concatenate([-rs, rs, -cs_, cs_], axis=1)
    tc, ts = cs(pos, RET_DK // 2)
    cos_r = np.concatenate([tc, tc], axis=1)
    sin_r = np.concatenate([-ts, ts], axis=1)

    def full(t, ident):
        t2 = np.concatenate([t, t], axis=1)
        return np.concatenate([np.full_like(t2, ident), t2], axis=0).astype(np.float32)

    return full(cos_a, 1.0), full(sin_a, 0.0), full(cos_r, 1.0), full(sin_r, 0.0)


def _dft_tables():
    def cs(n):
        k = np.arange(n, dtype=np.int64)
        m = (k[:, None] * k[None, :]) % n
        ang = 2.0 * np.pi * m.astype(np.float64) / n
        return np.cos(ang), np.sin(ang)

    c128, s128 = cs(FOURIER_DIM)
    eye = np.eye(FOURIER_GROUPS)
    bdc = np.kron(eye, c128).astype(np.float32)
    bds = np.kron(eye, s128).astype(np.float32)
    cn, sn = cs(SEQ)
    w2 = np.concatenate([cn, -sn], axis=1).astype(np.float32)
    cl, sl = cs(CTX_LEN)
    w2c = np.concatenate([cl, -sl], axis=1).astype(np.float32)
    return bdc, bds, w2, w2c


def _retention_tables(dec_f, dec_b):
    lg_f = jax.nn.log_sigmoid(dec_f.astype(F32))
    lg_b = jax.nn.log_sigmoid(dec_b.astype(F32))
    i = jnp.arange(RET_CHUNK)
    diff = (i[:, None] - i[None, :]).astype(F32)
    fwd = jnp.exp(jnp.maximum(diff, 0.0)[None] * lg_f[:, None, None])
    bwd = jnp.exp(jnp.maximum(-diff, 0.0)[None] * lg_b[:, None, None])
    dcomb = jnp.where((diff >= 0)[None], fwd, bwd).reshape(2, 4 * RET_CHUNK, RET_CHUNK)
    fi = i.astype(F32)
    lanes = lambda t: jnp.repeat(t, RET_DK, axis=1)
    xi_f = lanes(jnp.exp((fi + 1.0)[:, None] * lg_f[None, :]))
    zt_f = lanes(jnp.exp((RET_CHUNK - 1 - fi)[:, None] * lg_f[None, :]))
    xi_b = lanes(jnp.exp((RET_CHUNK - fi)[:, None] * lg_b[None, :]))
    zt_b = lanes(jnp.exp(fi[:, None] * lg_b[None, :]))
    g_f = jnp.repeat(jnp.exp(RET_CHUNK * lg_f), RET_DK).reshape(2, 256, 1)
    g_b = jnp.repeat(jnp.exp(RET_CHUNK * lg_b), RET_DK).reshape(2, 256, 1)
    g_f = jnp.broadcast_to(g_f, (2, 256, 256))
    g_b = jnp.broadcast_to(g_b, (2, 256, 256))
    return dcomb, xi_f, zt_f, xi_b, zt_b, g_f, g_b


def _ada_body(c_ref, w_ref, b_ref, o_ref):
    c = c_ref[...]
    s = c * _sigmoid(c)
    sh, sl = _split(s)
    wh, wl = _split(w_ref[...])
    o_ref[...] = _dot(sh, wh) + _dot(sl, wh) + _dot(sh, wl) + b_ref[...]


def _ada(cc, w_ada, b_ada):
    nt = 6 * D_MODEL // TN_ADA
    return pl.pallas_call(
        _ada_body,
        grid=(DEPTH, nt),
        in_specs=[
            pl.BlockSpec((MOD_ROWS, D_MODEL), lambda l, j: (0, 0)),
            pl.BlockSpec((None, D_MODEL, TN_ADA), lambda l, j: (l, 0, j)),
            pl.BlockSpec((None, 1, TN_ADA), lambda l, j: (l, 0, j)),
        ],
        out_specs=pl.BlockSpec((None, MOD_ROWS, TN_ADA), lambda l, j: (l, 0, j)),
        out_shape=jax.ShapeDtypeStruct((DEPTH, MOD_ROWS, 6 * D_MODEL), F32),
        compiler_params=_params(("arbitrary", "arbitrary")),
        name="ada_mod",
    )(cc, w_ada, b_ada.reshape(DEPTH, 1, 6 * D_MODEL))


def _rope(xc, cos, sin, half):
    fwd = pltpu.roll(xc, 128 - half, axis=1)
    bwd = pltpu.roll(xc, half, axis=1)
    lane = lax.broadcasted_iota(jnp.int32, xc.shape, 1)
    first = (lane & (2 * half - 1)) < half
    return xc * cos + jnp.where(first, fwd, bwd) * sin


def _proj_body(*refs, split):
    if split:
        x = _pick(pl.program_id(0) < RC // TM_PROJ, refs[0], refs[1])
        refs = refs[2:]
    else:
        x = refs[0][...]
        refs = refs[1:]
    (mod_ref, gn_ref, w_ref, ca_ref, sa_ref, cr_ref, sr_ref,
     qa_ref, ka_ref, va_ref, qr_ref, kr_ref, vr_ref, gr_ref, fu_ref, gm_ref) = refs
    ms = jnp.mean(x * x, axis=-1, keepdims=True)
    y = x * lax.rsqrt(ms + NORM_EPS) * gn_ref[...]
    h = y * (1.0 + mod_ref[1:2, :]) + mod_ref[0:1, :]
    hb = h.astype(BF16)

    def proj(c0, width):
        return _dot(hb, w_ref[:, c0:c0 + width])

    ca, sa, cr, sr = ca_ref[...], sa_ref[...], cr_ref[...], sr_ref[...]

    for c in range(3):
        gm_ref[:, c * D_MODEL:(c + 1) * D_MODEL] = _sigmoid(proj(3328 + c * D_MODEL, D_MODEL)).astype(BF16)
    g = proj(2304, W)
    gr_ref[...] = (g * _sigmoid(g)).astype(BF16)
    qa = proj(0, W) * (HEAD_DIM ** -0.5 * LOG2E)
    for c in range(W // 128):
        qa_ref[:, c * 128:(c + 1) * 128] = _rope(qa[:, c * 128:(c + 1) * 128], ca, sa, 16).astype(BF16)
    kv = proj(W, 256)
    ka = _rope(kv[:, 0:128], ca, sa, 16).astype(BF16)
    ka_ref[0] = ka[:, 0:64]
    ka_ref[1] = ka[:, 64:128]
    va = kv[:, 128:256].astype(BF16)
    ones_col = jnp.where(lax.broadcasted_iota(jnp.int32, (va.shape[0], 64), 1) == 0, 1.0, 0.0).astype(BF16)
    va_ref[0] = jnp.concatenate([va[:, 0:64], ones_col], axis=1)
    va_ref[1] = jnp.concatenate([va[:, 64:128], ones_col], axis=1)
    qr = proj(768, W)
    kr = proj(1280, W) * (RET_DK ** -0.5)
    for c in range(W // 128):
        sl = slice(c * 128, (c + 1) * 128)
        qr_ref[:, sl] = _rope(qr[:, sl], cr, sr, 32).astype(BF16)
        kr_ref[:, sl] = _rope(kr[:, sl], cr, sr, 32).astype(BF16)
    vr_ref[...] = proj(1792, W).astype(BF16)
    fu_ref[...] = proj(2816, W).astype(BF16)


def _proj(x, mod_l, gnorm, w_in_bf, tabs, layer):
    tm = TM_PROJ
    nt = R // tm
    nc = RC // tm
    split = isinstance(x, tuple)
    if split:
        xs = list(x)
        x_specs = [pl.BlockSpec((tm, D_MODEL), lambda i: (jnp.minimum(i, nc - 1), 0)),
                   pl.BlockSpec((tm, D_MODEL), lambda i: (jnp.maximum(i - nc, 0), 0))]
    else:
        xs = [x]
        x_specs = [pl.BlockSpec((tm, D_MODEL), lambda i: (i, 0))]

    def tab_map(i):
        return (jnp.where(i < nc, i, nc + (i - nc) % (SEQ // tm)), 0)

    row = lambda i: (i, 0)
    wide = lambda n: pl.BlockSpec((tm, n), row)
    kv_spec = lambda n: pl.BlockSpec((2, tm, n), lambda i: (0, i, 0))
    sds = lambda n: jax.ShapeDtypeStruct((R, n), BF16)
    kv_sds = lambda n: jax.ShapeDtypeStruct((2, R, n), BF16)
    mrow = _mod_row(0, tm)
    return pl.pallas_call(
        functools.partial(_proj_body, split=split),
        grid=(nt,),
        in_specs=x_specs + [
            pl.BlockSpec((None, 6, D_MODEL), lambda i: (mrow(i), 0, 0)),
            pl.BlockSpec((1, D_MODEL), lambda i: (0, 0)),
            pl.BlockSpec((None, D_MODEL, IN_COLS), lambda i: (layer, 0, 0), pipeline_mode=pl.Buffered(1)),
        ] + [pl.BlockSpec((tm, 128), tab_map)] * 4,
        out_specs=[wide(W), kv_spec(64), kv_spec(128), wide(W), wide(W), wide(W), wide(W), wide(W),
                   wide(3 * D_MODEL)],
        out_shape=[sds(W), kv_sds(64), kv_sds(128), sds(W), sds(W), sds(W), sds(W), sds(W), sds(3 * D_MODEL)],
        compiler_params=_params(("arbitrary",)),
        name="in_proj",
    )(*xs, mod_l, gnorm, w_in_bf, *tabs)


def _attn_body(sink_ref, q_ref, kc_ref, k0_ref, k1_ref, k2_ref, k3_ref, vc_ref, v0_ref, v1_ref, v2_ref, v3_ref,
               o_ref, *, has_ctx):
    j = pl.program_id(1)
    p = j - 1 if has_ctx else j
    nr = ATTN_GROUP * ATTN_BLOCK
    groups = [slice(g * ATTN_BLOCK, (g + 1) * ATTN_BLOCK) for g in range(ATTN_GROUP)]

    def attend(sub, pieces):
        rows = slice(sub * ATTN_BLOCK, (sub + 1) * ATTN_BLOCK)
        outs = []
        for h in range(ATTN_KV_HEADS):
            q = q_ref[rows, h * 256:(h + 1) * 256]
            q4 = jnp.concatenate([q[:, g * 64:(g + 1) * 64] for g in range(ATTN_GROUP)], axis=0)
            sinks = [sink_ref[h * ATTN_GROUP + g] * LOG2E for g in range(ATTN_GROUP)]
            k_all = jnp.concatenate([k_ref[h] for k_ref, _, _ in pieces], axis=0)
            v_all = jnp.concatenate([v_ref[h] for _, v_ref, _ in pieces], axis=0)
            s = _dot_nt(q4, k_all)
            cols, c0 = [], 0
            for k_ref, _, ok in pieces:
                n = k_ref.shape[1]
                cols.append(s[:, c0:c0 + n] if ok is None else jnp.where(ok, s[:, c0:c0 + n], NEG))
                c0 += n
            s = jnp.concatenate(cols, axis=1)
            mxs, ps = [], []
            for g, r in enumerate(groups):
                mx = jnp.maximum(jnp.max(s[r], axis=-1, keepdims=True), sinks[g])
                mxs.append(mx)
                ps.append(jnp.exp2(s[r] - mx).astype(BF16))
            oa = _dot(jnp.concatenate(ps, axis=0), v_all)
            for g, r in enumerate(groups):
                den = oa[r, HEAD_DIM:HEAD_DIM + 1] + jnp.exp2(sinks[g] - mxs[g])
                outs.append(oa[r, 0:HEAD_DIM] / den)
        o_ref[rows, :] = jnp.concatenate(outs, axis=1).astype(BF16)

    @pl.when(p >= 0)
    def _():
        ri = lax.broadcasted_iota(jnp.int32, (nr, ATTN_BLOCK), 0) & (ATTN_BLOCK - 1)
        ci = lax.broadcasted_iota(jnp.int32, (nr, ATTN_BLOCK), 1)
        far = 4 * ATTN_BLOCK
        first_prev = ci >= ri + jnp.where(p >= 1, 0, far)
        last_next = ci + jnp.where(p <= SEQ // (2 * ATTN_BLOCK) - 2, 0, far) <= ri
        attend(0, [(kc_ref, vc_ref, None), (k0_ref, v0_ref, first_prev), (k1_ref, v1_ref, None),
                   (k2_ref, v2_ref, ci <= ri)])
        attend(1, [(kc_ref, vc_ref, None), (k1_ref, v1_ref, ci >= ri), (k2_ref, v2_ref, None),
                   (k3_ref, v3_ref, last_next)])

    if has_ctx:
        @pl.when(p < 0)
        def _():
            attend(0, [(kc_ref, vc_ref, None)])
            attend(1, [(kc_ref, vc_ref, None)])


def _attention(sink, qa, ka, va, need_ctx):
    nb = SEQ // ATTN_BLOCK
    npair = nb // 2
    first = 1 if need_ctx else 0
    tq = 2 * ATTN_BLOCK
    lat0 = RC // tq
    out0 = 0 if need_ctx else lat0

    def qrow(b, j):
        return jnp.where(j < first, b, lat0 + b * npair + j - first)

    def loc(delta):
        def f(b, j):
            m = jnp.clip(2 * (j - first) + delta, 0, nb - 1)
            return (0, RC // ATTN_BLOCK + b * nb + m, 0)
        return f

    def kv_specs(width):
        ctx_spec = pl.BlockSpec((ATTN_KV_HEADS, CTX_LEN, width), lambda b, j: (0, b, 0))
        return [ctx_spec] + [pl.BlockSpec((ATTN_KV_HEADS, ATTN_BLOCK, width), loc(d)) for d in (-1, 0, 1, 2)]

    return pl.pallas_call(
        functools.partial(_attn_body, has_ctx=need_ctx),
        grid=(BATCH, npair + first),
        in_specs=[pl.BlockSpec(memory_space=pltpu.SMEM),
                  pl.BlockSpec((tq, W), lambda b, j: (qrow(b, j), 0))]
                 + kv_specs(HEAD_DIM) + kv_specs(128),
        out_specs=pl.BlockSpec((tq, W), lambda b, j: (qrow(b, j) - out0, 0)),
        out_shape=jax.ShapeDtypeStruct((R - out0 * tq, W), BF16),
        compiler_params=_params(("arbitrary", "arbitrary")),
        name="window_attn",
    )(sink, qa, *([ka] * 5), *([va] * 5))


def _fourier_body(*refs, has_ctx):
    if has_ctx:
        uc_ref, ul_ref, bdc_ref, bds_ref, w2_ref, w2c_ref, oc_ref, ol_ref, as_ref = refs
    else:
        ul_ref, bdc_ref, bds_ref, w2_ref, ol_ref, as_ref = refs
    j = pl.program_id(1)
    first = 1 if has_ctx else 0

    if has_ctx:
        @pl.when(j == 0)
        def _():
            u = uc_ref[...]
            a = _dot(u, bdc_ref[...]).astype(BF16)
            s = _dot(u, bds_ref[...]).astype(BF16)
            z = _dot(w2c_ref[...], jnp.concatenate([a, s], axis=0))
            oc_ref[...] = (z * ((CTX_LEN * FOURIER_DIM) ** -0.5)).astype(BF16)

    @pl.when(j == first)
    def _():
        u = ul_ref[...]
        as_ref[0:SEQ, :] = _dot(u, bdc_ref[...]).astype(BF16)
        as_ref[SEQ:2 * SEQ, :] = _dot(u, bds_ref[...]).astype(BF16)

    @pl.when(j >= first)
    def _():
        ol_ref[...] = (_dot(w2_ref[...], as_ref[...]) * ((SEQ * FOURIER_DIM) ** -0.5)).astype(BF16)


def _fourier(fu, dft, need_ctx):
    bdc, bds, w2, w2c = dft
    tr = TR_FOURIER
    nj = SEQ // tr
    first = 1 if need_ctx else 0
    full = lambda a: pl.BlockSpec(a.shape, lambda *_: (0,) * a.ndim)
    lat_tile = lambda j: jnp.maximum(j - first, 0)
    ul_spec = pl.BlockSpec((SEQ, W), lambda b, j: (1 + b, 0))
    w2_spec = pl.BlockSpec((tr, 2 * SEQ), lambda b, j: (lat_tile(j), 0))
    ol_spec = pl.BlockSpec((tr, W), lambda b, j: (b * nj + lat_tile(j), 0))
    ol_shape = jax.ShapeDtypeStruct((RL, W), BF16)
    if need_ctx:
        in_specs = [pl.BlockSpec((CTX_LEN, W), lambda b, j: (b, 0)), ul_spec, full(bdc), full(bds), w2_spec,
                    full(w2c)]
        out_specs = [pl.BlockSpec((CTX_LEN, W), lambda b, j: (b, 0)), ol_spec]
        out_shape = [jax.ShapeDtypeStruct((RC, W), BF16), ol_shape]
        args = (fu, fu, bdc, bds, w2, w2c)
    else:
        in_specs = [ul_spec, full(bdc), full(bds), w2_spec]
        out_specs = [ol_spec]
        out_shape = [ol_shape]
        args = (fu, bdc, bds, w2)
    outs = pl.pallas_call(
        functools.partial(_fourier_body, has_ctx=need_ctx),
        grid=(BATCH, nj + first),
        in_specs=in_specs,
        out_specs=out_specs,
        out_shape=out_shape,
        scratch_shapes=[pltpu.VMEM((2 * SEQ, W), BF16)],
        compiler_params=_params(("arbitrary", "arbitrary")),
        name="fourier_mix",
    )(*args)
    return (outs[0], outs[1]) if need_ctx else (None, outs[0])


def _retention_body(qc_ref, kc_ref, vc_ref, ql0_ref, kl0_ref, vl0_ref, ql1_ref, kl1_ref, vl1_ref, g_ref,
                    dcomb_ref, xif_ref, ztf_ref, xib_ref, ztb_ref, gf_ref, gb_ref, mbd_ref, avg_ref,
                    o_ref, os_ref, st_ref):
    j = pl.program_id(1)
    C = RET_CHUNK
    nl = SEQ // C
    lat_refs = ((ql0_ref, kl0_ref, vl0_ref), (ql1_ref, kl1_ref, vl1_ref))

    own = ((lax.broadcasted_iota(jnp.int32, (4 * C, 256), 0) >> 7)
           == (lax.broadcasted_iota(jnp.int32, (4 * C, 256), 1) >> 6))

    def chunk_fwd(bb, q, k, v, r0):
        for gi in range(2):
            sl = slice(gi * 256, (gi + 1) * 256)
            q4, k4, v4 = q[:, sl], k[:, sl], v[:, sl]
            s_prev = st_ref[2 * bb + gi]
            q4f = q4.astype(F32)
            o4 = _dot((q4f * xif_ref[:, sl]).astype(BF16), s_prev.astype(BF16))
            qstack = jnp.where(own, jnp.concatenate([q4f] * 4, axis=0), 0.0).astype(BF16)
            p = (_dot_nt(qstack, k4) * dcomb_ref[gi]).astype(BF16)
            ov = jnp.where(own, _dot(p, v4), 0.0)
            intra = ov[0:C] + ov[C:2 * C] + ov[2 * C:3 * C] + ov[3 * C:4 * C]
            os_ref[pl.ds(r0, C), sl] = o4 + intra
            u = _dot_tn(k4, (v4.astype(F32) * ztf_ref[:, sl]).astype(BF16))
            st_ref[2 * bb + gi] = gf_ref[gi] * s_prev + mbd_ref[...] * u

    def chunk_bwd(bb, q, k, v, r0):
        for gi in range(2):
            sl = slice(gi * 256, (gi + 1) * 256)
            q4, k4, v4 = q[:, sl], k[:, sl], v[:, sl]
            s_prev = st_ref[2 * bb + gi]
            cross = _dot((q4.astype(F32) * xib_ref[:, sl]).astype(BF16), s_prev.astype(BF16))
            os_ref[pl.ds(r0, C), sl] = os_ref[pl.ds(r0, C), sl] + cross
            u = _dot_tn(k4, (v4.astype(F32) * ztb_ref[:, sl]).astype(BF16))
            st_ref[2 * bb + gi] = gb_ref[gi] * s_prev + mbd_ref[...] * u

    def scan(chunk, ctx_order, lat_index):
        st_ref[...] = jnp.zeros_like(st_ref)
        for c in ctx_order:
            for bb in range(2):
                rs = slice(bb * CTX_LEN + c * C, bb * CTX_LEN + (c + 1) * C)
                chunk(bb, qc_ref[rs, :], kc_ref[rs, :], vc_ref[rs, :], bb * CTX_LEN + c * C)

        def body(t, carry):
            r0 = pl.multiple_of(lat_index(t) * C, C)
            for bb, (q_ref, k_ref, v_ref) in enumerate(lat_refs):
                rs = pl.ds(r0, C)
                chunk(bb, q_ref[rs, :], k_ref[rs, :], v_ref[rs, :], 2 * CTX_LEN + bb * SEQ + r0)
            return carry

        lax.fori_loop(0, nl, body, 0)

    @pl.when(j == 0)
    def _():
        scan(chunk_fwd, range(CTX_LEN // C), lambda t: t)
        scan(chunk_bwd, reversed(range(CTX_LEN // C)), lambda t: nl - 1 - t)

    o = os_ref[pl.ds(pl.multiple_of(j * RET_SLAB, RET_SLAB), RET_SLAB), :]
    avg = avg_ref[...]
    oh, ol = _split(o)
    mu = _dot(oh, avg) + _dot(ol, avg)
    d = o - mu
    var = _dot((d * d).astype(BF16), avg)
    o_ref[...] = (g_ref[...].astype(F32) * d * lax.rsqrt(var + GN_EPS)).astype(BF16)


def _retention(qr, kr, vr, gr, rtabs):
    nslab = SEQ // RET_SLAB
    nj = 1 + 2 * nslab

    def out_map(b2, j):
        return (jnp.where(j == 0, b2, RC // RET_SLAB + 2 * b2 * nslab + j - 1), 0)

    ctx = pl.BlockSpec((2 * CTX_LEN, W), lambda b2, j: (b2, 0))
    lat = lambda k: pl.BlockSpec((SEQ, W), lambda b2, j: (1 + 2 * b2 + k, 0))
    full = lambda a: pl.BlockSpec(a.shape, lambda *_: (0,) * a.ndim)
    avg = jnp.asarray(np.kron(np.eye(RET_HEADS), np.full((RET_DK, RET_DK), 1.0 / RET_DK)).astype(np.float32)).astype(BF16)
    mbd = jnp.asarray(np.kron(np.eye(4), np.ones((RET_DK, RET_DK))).astype(np.float32))
    tabs = list(rtabs) + [mbd, avg]
    return pl.pallas_call(
        _retention_body,
        grid=(BATCH // 2, nj),
        in_specs=[ctx, ctx, ctx, lat(0), lat(0), lat(0), lat(1), lat(1), lat(1),
                  pl.BlockSpec((RET_SLAB, W), out_map)] + [full(t) for t in tabs],
        out_specs=pl.BlockSpec((RET_SLAB, W), out_map),
        out_shape=jax.ShapeDtypeStruct((R, W), BF16),
        scratch_shapes=[pltpu.VMEM((2 * (CTX_LEN + SEQ), W), F32), pltpu.VMEM((4, 256, 256), F32)],
        compiler_params=_params(("arbitrary", "arbitrary")),
        name="retention",
    )(qr, kr, vr, qr, kr, vr, qr, kr, vr, gr, *tabs)


def _pack_pair(a, b):
    ua = lax.bitcast_convert_type(a, jnp.uint32) >> 16
    ub = lax.bitcast_convert_type(b, jnp.uint32) & jnp.uint32(0xFFFF0000)
    return ua | ub


def _unpack_pair(w):
    a = lax.bitcast_convert_type(w << 16, F32)
    b = lax.bitcast_convert_type(w & jnp.uint32(0xFFFF0000), F32)
    return a, b


def _slot_onehot(s0, s1, n):
    srow = lax.broadcasted_iota(jnp.int32, (n, s0.shape[1]), 0)
    p0 = jnp.where(srow == s0, 1.0, 0.0).astype(BF16)
    p1 = jnp.where(srow == s1, 1.0, 0.0).astype(BF16)
    return p0, p1


def _pick(first, a_ref, b_ref):
    a = a_ref[...]
    flag = jnp.zeros(a.shape, jnp.int32) + first.astype(jnp.int32)
    return jnp.where(flag > 0, a, b_ref[...])


def _merge_body(*refs, layer0):
    if layer0:
        oa_ref, ofc_ref, ofl_ref, rt_ref, gm_ref, xc_ref, xl_ref = refs[:7]
        rest = refs[7:]
        is_ctx = pl.program_id(0) < RC // TM_MERGE
        of_in = _pick(is_ctx, ofc_ref, ofl_ref)
        x_in = _pick(is_ctx, xc_ref, xl_ref)
    else:
        oa_ref, ofl_ref, rt_ref, gm_ref, x_ref = refs[:5]
        rest = refs[5:]
        of_in = ofl_ref[...]
        x_in = x_ref[...]
    (mod_ref, gn_ref, wba_ref, wbf_ref, wbr_ref, wout_ref, wrh_ref, wrl_ref, br_ref, tri_ref, ltri_ref,
     xo_ref, xs_ref, ro_ref, nch_ref) = rest
    gm = gm_ref[...].astype(F32)
    z = (gm[:, 0:D_MODEL] * _dot(oa_ref[...], wba_ref[...])
         + gm[:, D_MODEL:2 * D_MODEL] * _dot(of_in, wbf_ref[...])
         + gm[:, 2 * D_MODEL:3 * D_MODEL] * _dot(rt_ref[...], wbr_ref[...]))
    y = _dot(z.astype(BF16), wout_ref[...])
    x = x_in + mod_ref[2:3, :] * y
    xo_ref[...] = x
    ms = jnp.mean(x * x, axis=-1, keepdims=True)
    hn = x * lax.rsqrt(ms + NORM_EPS) * gn_ref[...]
    h2 = hn * (1.0 + mod_ref[4:5, :]) + mod_ref[3:4, :]
    hh, hl = _split(h2)
    wh, wl = wrh_ref[...], wrl_ref[...]
    lg = _dot_nt(wh, hh) + _dot_nt(wh, hl) + _dot_nt(wl, hh) + br_ref[...]
    tm = lg.shape[1]
    row8 = lax.broadcasted_iota(jnp.int32, (8, tm), 0)
    lgg = lg[0:8, :]
    mg = jnp.max(lgg, axis=0, keepdims=True)
    grp = jnp.min(jnp.where(lgg == mg, row8, 8), axis=0, keepdims=True)
    pg = 1.0 / jnp.sum(jnp.exp(lgg - mg), axis=0, keepdims=True)
    lin = jnp.zeros((8, tm), F32)
    for g in range(N_GROUPS):
        lin = jnp.where(grp == g, lg[8 + 8 * g:16 + 8 * g, :], lin)
    v1 = jnp.max(lin, axis=0, keepdims=True)
    i1 = jnp.min(jnp.where(lin == v1, row8, 8), axis=0, keepdims=True)
    rest = jnp.where(row8 == i1, -jnp.inf, lin)
    v2 = jnp.max(rest, axis=0, keepdims=True)
    i2 = jnp.min(jnp.where(rest == v2, row8, 8), axis=0, keepdims=True)
    e2 = jnp.exp(v2 - v1)
    w1 = pg / (1.0 + e2)
    w2 = pg * e2 / (1.0 + e2)
    e_1 = grp * EXPERTS_PER_GROUP + i1
    e_2 = grp * EXPERTS_PER_GROUP + i2

    row32 = lax.broadcasted_iota(jnp.int32, (N_EXPERTS, tm), 0)
    oh0 = jnp.where(row32 == e_1, 1.0, 0.0)
    oh1 = jnp.where(row32 == e_2, 1.0, 0.0)
    tri = tri_ref[...]
    cum0 = _dot(oh0.astype(BF16), tri)
    cum1 = _dot(oh1.astype(BF16), tri)
    tot0 = jnp.sum(oh0, axis=1, keepdims=True)
    tot1 = jnp.sum(oh1, axis=1, keepdims=True)
    nch = ((tot0 + tot1).astype(jnp.int32) + (CHUNK - 1)) >> CHUNK_SHIFT
    nch_b = jnp.broadcast_to(nch.astype(F32), (N_EXPERTS, 128))
    nch_ref[...] = nch_b.astype(jnp.int32)
    base = CHUNK * _dot(ltri_ref[...], nch_b.astype(BF16))[:, 0:1]
    s0 = jnp.sum(oh0 * (base + cum0), axis=0, keepdims=True).astype(jnp.int32)
    s1 = jnp.sum(oh1 * (base + tot0 + cum1), axis=0, keepdims=True).astype(jnp.int32)
    p0, p1 = _slot_onehot(s0, s1, SLOTS)
    xs = _dot(p0 + p1, hh)
    xs_ref[:, 0:D_MODEL // 2] = _pack_pair(xs[:, 0:D_MODEL // 2], xs[:, D_MODEL // 2:D_MODEL])

    def wrows(w):
        hi, lo = _split(w)
        return jnp.where(row8 == 0, hi.astype(F32), jnp.where(row8 == 1, lo.astype(F32), 0.0)).astype(BF16)

    wc = _dot_nt(p0, wrows(w1)) + _dot_nt(p1, wrows(w2))
    wcol = jnp.broadcast_to(wc[:, 0:1] + wc[:, 1:2], (SLOTS, 128))
    xs_ref[:, D_MODEL // 2:XS_COLS] = lax.bitcast_convert_type(wcol, jnp.uint32)
    s0f, s1f = s0.astype(F32), s1.astype(F32)
    ro_ref[...] = jnp.where(row8 == 0, s0f, jnp.where(row8 == 1, s1f, 0.0))


def _merge(oa, of_c, of_l, ret, gm, xs_in, mod_l, gnorm, wba, wbf, wbr, wout, wrh, wrl, brb, layer0):
    tm = TM_MERGE
    row0 = 0 if layer0 else RC
    rm = R - row0
    nt = rm // tm
    off = row0 // tm
    nc = RC // tm
    src = lambda n: pl.BlockSpec((tm, n), lambda i: (i + off, 0))
    dst = lambda n: pl.BlockSpec((tm, n), lambda i: (i, 0))
    ctx_rows = lambda n: pl.BlockSpec((tm, n), lambda i: (jnp.minimum(i, nc - 1), 0))
    lat_rows = lambda n: pl.BlockSpec((tm, n), lambda i: (jnp.maximum(i - nc, 0), 0))
    full = lambda a: pl.BlockSpec(a.shape, lambda *_: (0,) * a.ndim, pipeline_mode=pl.Buffered(1))
    mrow = _mod_row(row0, tm)
    tri = jnp.asarray(np.triu(np.ones((tm, tm), np.float32), 1)).astype(BF16)
    ltri = jnp.asarray(np.tril(np.ones((N_EXPERTS, N_EXPERTS), np.float32), -1)).astype(BF16)
    if layer0:
        acts = [oa, of_c, of_l, ret, gm, xs_in[0], xs_in[1]]
        act_specs = [src(W), ctx_rows(W), lat_rows(W), src(W), src(3 * D_MODEL), ctx_rows(D_MODEL),
                     lat_rows(D_MODEL)]
    else:
        acts = [oa, of_l, ret, gm, xs_in]
        act_specs = [dst(W), dst(W), src(W), src(3 * D_MODEL), src(D_MODEL)]
    return pl.pallas_call(
        functools.partial(_merge_body, layer0=layer0),
        grid=(nt,),
        in_specs=act_specs + [
                  pl.BlockSpec((None, 6, D_MODEL), lambda i: (mrow(i), 0, 0)),
                  full(gnorm), full(wba), full(wbf), full(wbr), full(wout), full(wrh), full(wrl), full(brb),
                  full(tri), full(ltri)],
        out_specs=[dst(D_MODEL), pl.BlockSpec((SLOTS, XS_COLS), lambda i: (i, 0)),
                   pl.BlockSpec((8, tm), lambda i: (0, i)),
                   pl.BlockSpec((None, N_EXPERTS, 128), lambda i: (i, 0, 0))],
        out_shape=[jax.ShapeDtypeStruct((rm, D_MODEL), F32),
                   jax.ShapeDtypeStruct((nt * SLOTS, XS_COLS), jnp.uint32),
                   jax.ShapeDtypeStruct((8, rm), F32),
                   jax.ShapeDtypeStruct((nt, N_EXPERTS, 128), jnp.int32)],
        compiler_params=_params(("arbitrary",)),
        name="merge_router",
    )(*acts, mod_l, gnorm, wba, wbf, wbr, wout, wrh, wrl, brb, tri, ltri)


def _moe_plan(nch, nb):
    nt = nch.shape[0]
    choff = jnp.cumsum(nch, axis=1) - nch
    used_ch = jnp.sum(nch, axis=1)
    cum_t = jnp.cumsum(nch, axis=0)
    tot = cum_t[-1]
    ptot = (tot + CPB - 1) // CPB * CPB
    pend = jnp.cumsum(ptot)
    pstart = pend - ptot
    n_used = pend[-1] // CPB
    blk = jnp.arange(nb, dtype=jnp.int32)
    lane = jnp.arange(CPB, dtype=jnp.int32)
    blk_e = jnp.minimum(jnp.sum((blk[:, None] * CPB >= pend[None, :]).astype(jnp.int32), axis=1), N_EXPERTS - 1)
    oe = (blk_e[:, None] == jnp.arange(N_EXPERTS, dtype=jnp.int32)[None, :]).astype(jnp.int32)
    sel = lambda tab: jnp.sum(oe[:, :, None] * tab.T[None, :, :], axis=1)
    pstart_b = jnp.sum(oe * pstart[None, :], axis=1)
    tot_b = jnp.sum(oe * tot[None, :], axis=1)
    cum_b, nch_b, choff_b = sel(cum_t), sel(nch), sel(choff)
    i = blk[:, None] * CPB + lane[None, :] - pstart_b[:, None]
    valid = (i < tot_b[:, None]) & (blk[:, None] < n_used)
    t = jnp.minimum(jnp.sum((i[:, :, None] >= cum_b[:, None, :]).astype(jnp.int32), axis=2), nt - 1)
    tiles = jnp.arange(nt, dtype=jnp.int32)[None, None, :]
    before = jnp.sum(jnp.where(tiles < t[:, :, None], nch_b[:, None, :], 0), axis=2)
    coff = jnp.sum(jnp.where(tiles == t[:, :, None], choff_b[:, None, :], 0), axis=2)
    row = t * SLOTS + CHUNK * (coff + i - before)
    src = jnp.where(valid, row, SLOTS - CHUNK)
    dummy = nt * SLOTS + CHUNK * ((blk[:, None] % 2) * CPB + lane[None, :])
    dst = jnp.where(valid, row, dummy)
    blk_start = jnp.concatenate([pstart, pend[-1:]]) // CPB
    return (blk_start.astype(jnp.int32), n_used.astype(jnp.int32).reshape(1), src.reshape(-1).astype(jnp.int32),
            dst.reshape(-1).astype(jnp.int32), used_ch.astype(jnp.int32))


def _ffn_body(bs_ref, nu_ref, src_ref, dst_ref, uc_ref, xs_ref, wg_ref, wu_ref, wd_ref, ys_ref,
              xbuf, ybuf, zbuf, wgb, wub, wdb, sem_in, sem_out, sem_z, *, nt):
    e = pl.program_id(0)
    nu = nu_ref[0]
    half = D_MODEL // 2

    def gather(blk, sl):
        for c in range(CPB):
            r = pl.multiple_of(src_ref[blk * CPB + c], CHUNK)
            pltpu.make_async_copy(xs_ref.at[pl.ds(r, CHUNK)], xbuf.at[sl, pl.ds(c * CHUNK, CHUNK)],
                                  sem_in.at[sl]).start(priority=1)

    def scatter(blk, sl):
        for c in range(CPB):
            r = pl.multiple_of(dst_ref[blk * CPB + c], CHUNK)
            pltpu.make_async_copy(ybuf.at[sl, pl.ds(c * CHUNK, CHUNK)], ys_ref.at[pl.ds(r, CHUNK)],
                                  sem_out.at[sl]).start(priority=1)

    def wait_gather(sl):
        pltpu.make_async_copy(xs_ref.at[pl.ds(0, MOE_BM)], xbuf.at[sl], sem_in.at[sl]).wait()

    def wait_scatter(sl):
        pltpu.make_async_copy(ybuf.at[sl], ys_ref.at[pl.ds(0, MOE_BM)], sem_out.at[sl]).wait()

    def zero_copy(r):
        return pltpu.make_async_copy(zbuf, ys_ref.at[pl.ds(pl.multiple_of(r, CHUNK), CHUNK)], sem_z)

    @pl.when(e == 0)
    def _():
        zbuf[...] = jnp.zeros_like(zbuf)

        def tails(fn):
            def per_tile(t, carry):
                def per_chunk(c, carry2):
                    fn(t * SLOTS + c * CHUNK)
                    return carry2
                lax.fori_loop(uc_ref[t], SLOTS // CHUNK, per_chunk, 0)
                return carry
            lax.fori_loop(0, nt, per_tile, 0)
            for c in range(2 * CPB):
                fn(nt * SLOTS + c * CHUNK)

        tails(lambda r: zero_copy(r).start())
        tails(lambda r: zero_copy(r).wait())
        gather(0, 0)
        for d in range(1, GATHER_DEPTH - 1):
            @pl.when(nu > d)
            def _():
                gather(d, d)

    b0, b1 = bs_ref[e], bs_ref[e + 1]

    @pl.when(b1 > b0)
    def _():
        wgb[...] = wg_ref[...].astype(BF16)
        wub[...] = wu_ref[...].astype(BF16)
        wdb[...] = wd_ref[...].astype(BF16)

        def block(b, carry):
            slot = b % 2
            xslot = lax.rem(b, GATHER_DEPTH)

            @pl.when(b + GATHER_DEPTH - 1 < nu)
            def _():
                gather(b + GATHER_DEPTH - 1, lax.rem(b + GATHER_DEPTH - 1, GATHER_DEPTH))

            wait_gather(xslot)

            @pl.when(b >= 2)
            def _():
                wait_scatter(slot)

            xw = xbuf[xslot]
            xa, xb = _unpack_pair(xw[:, 0:half])
            x = jnp.concatenate([xa, xb], axis=1).astype(BF16)
            wt = lax.bitcast_convert_type(xw[:, half:XS_COLS], F32)
            g = _dot(x, wgb[...])
            u = _dot(x, wub[...])
            hmid = (g * _sigmoid(g) * u).astype(BF16)
            y = _dot(hmid, wdb[...]) * jnp.concatenate([wt] * (D_MODEL // 128), axis=1)
            yb = y.astype(BF16).astype(F32)
            ybuf[slot] = _pack_pair(yb[:, 0:half], yb[:, half:D_MODEL])
            scatter(b, slot)
            return carry

        lax.fori_loop(b0, b1, block, 0)

    @pl.when(e == N_EXPERTS - 1)
    def _():
        wait_scatter((nu - 1) % 2)

        @pl.when(nu >= 2)
        def _():
            wait_scatter(nu % 2)


def _ffn(plan, xs, w_g, w_u, w_d, layer, nt):
    wmap = lambda e, *_: (layer, e, 0, 0)
    half = D_MODEL // 2
    grid_spec = pltpu.PrefetchScalarGridSpec(
        num_scalar_prefetch=5,
        grid=(N_EXPERTS,),
        in_specs=[pl.BlockSpec(memory_space=pl.ANY),
                  pl.BlockSpec((None, None, D_MODEL, EXPERT_HIDDEN), wmap),
                  pl.BlockSpec((None, None, D_MODEL, EXPERT_HIDDEN), wmap),
                  pl.BlockSpec((None, None, EXPERT_HIDDEN, D_MODEL), wmap)],
        out_specs=pl.BlockSpec(memory_space=pl.ANY),
        scratch_shapes=[pltpu.VMEM((GATHER_DEPTH, MOE_BM, XS_COLS), jnp.uint32),
                        pltpu.VMEM((2, MOE_BM, half), jnp.uint32),
                        pltpu.VMEM((CHUNK, half), jnp.uint32),
                        pltpu.VMEM((D_MODEL, EXPERT_HIDDEN), BF16), pltpu.VMEM((D_MODEL, EXPERT_HIDDEN), BF16),
                        pltpu.VMEM((EXPERT_HIDDEN, D_MODEL), BF16),
                        pltpu.SemaphoreType.DMA((GATHER_DEPTH,)), pltpu.SemaphoreType.DMA((2,)),
                        pltpu.SemaphoreType.DMA(())],
    )
    return pl.pallas_call(
        functools.partial(_ffn_body, nt=nt),
        grid_spec=grid_spec,
        out_shape=jax.ShapeDtypeStruct((nt * SLOTS + 2 * CPB * CHUNK, half), jnp.uint32),
        compiler_params=_params(("arbitrary",)),
        name="moe_experts",
    )(*plan, xs, w_g, w_u, w_d)


def _combine_body(ys_ref, ro_ref, x_ref, mod_ref, gn_ref, o_ref, *, final):
    s = ro_ref[...]
    p0, p1 = _slot_onehot(s[0:1, :].astype(jnp.int32), s[1:2, :].astype(jnp.int32), SLOTS)
    ya, yb = _unpack_pair(ys_ref[...])
    y = jnp.concatenate([ya, yb], axis=1).astype(BF16)
    f = _dot_tn(p0 + p1, y)
    x = x_ref[...] + mod_ref[5:6, :] * f
    if final:
        ms = jnp.mean(x * x, axis=-1, keepdims=True)
        x = x * lax.rsqrt(ms + NORM_EPS) * gn_ref[...]
    o_ref[...] = x


def _combine(ys, route, x, mod_l, gnorm, row0, final):
    tm = TM_MERGE
    rm = x.shape[0]
    mrow = _mod_row(row0, tm)
    return pl.pallas_call(
        functools.partial(_combine_body, final=final),
        grid=(rm // tm,),
        in_specs=[pl.BlockSpec((SLOTS, D_MODEL // 2), lambda i: (i, 0)),
                  pl.BlockSpec((8, tm), lambda i: (0, i)),
                  pl.BlockSpec((tm, D_MODEL), lambda i: (i, 0)),
                  pl.BlockSpec((None, 6, D_MODEL), lambda i: (mrow(i), 0, 0)),
                  pl.BlockSpec((1, D_MODEL), lambda i: (0, 0))],
        out_specs=pl.BlockSpec((tm, D_MODEL), lambda i: (i, 0)),
        out_shape=jax.ShapeDtypeStruct((rm, D_MODEL), F32),
        compiler_params=_params(("arbitrary",)),
        name="moe_combine",
    )(ys, route, x, mod_l, gnorm)


def _moe(xs, route, nch3, x, mod_l, gnorm, w_g, w_u, w_d, row0, final, layer):
    nt = nch3.shape[0]
    max_chunks = nt * ((2 * TM_MERGE + N_EXPERTS * (CHUNK - 1)) // CHUNK)
    nb = -(-max_chunks // CPB) + N_EXPERTS
    plan = _moe_plan(nch3[:, :, 0], nb)
    ys = _ffn(plan, xs, w_g, w_u, w_d, layer, nt)
    return _combine(ys, route, x, mod_l, gnorm, row0, final)


def kernel(x, c, ctx, c_ctx, norm_mix, norm_ffn, w_ada, b_ada, w_in, attn_sink, ret_decay_fwd, ret_decay_bwd,
           w_branch_attn, w_branch_fourier, w_branch_ret, w_out, w_router_group, b_router_group,
           w_router_expert, b_router_expert, w_exp_gate, w_exp_up, w_exp_down, norm_final):
    tabs = [jnp.asarray(t) for t in _rope_tables()]
    dft = [jnp.asarray(t).astype(BF16) for t in _dft_tables()]

    cc = jnp.zeros((MOD_ROWS, D_MODEL), F32).at[0:BATCH].set(c).at[CTX_MOD_ROW].set(c_ctx)
    mod = _ada(cc, w_ada, b_ada).reshape(DEPTH, MOD_ROWS, 6, D_MODEL)

    xf = (ctx.reshape(RC, D_MODEL), x.reshape(RL, D_MODEL))
    w_in_bf = w_in.astype(BF16)
    for l in range(DEPTH):
        need_ctx = l < DEPTH - 1
        row0 = 0 if need_ctx else RC
        mod_l = mod[l]
        qa, ka, va, qr, kr, vr, gr, fu, gm = _proj(xf, mod_l, norm_mix[l][None, :], w_in_bf, tabs, l)
        oa = _attention(attn_sink[l], qa, ka, va, need_ctx)
        of_c, of_l = _fourier(fu, dft, need_ctx)
        ret = _retention(qr, kr, vr, gr, _retention_tables(ret_decay_fwd[l], ret_decay_bwd[l]))
        wr = jnp.zeros((ROUTER_ROWS, D_MODEL), F32)
        wr = wr.at[0:N_GROUPS].set(w_router_group[l].T).at[8:8 + N_EXPERTS].set(w_router_expert[l].T)
        br = jnp.full((ROUTER_ROWS,), NEG, F32)
        br = br.at[0:N_GROUPS].set(b_router_group[l]).at[8:8 + N_EXPERTS].set(b_router_expert[l])
        wrh, wrl = _split(wr)
        brb = jnp.broadcast_to(br[:, None], (ROUTER_ROWS, TM_MERGE))
        x_mid, xs, route, nch3 = _merge(oa, of_c, of_l, ret, gm, xf, mod_l, norm_ffn[l][None, :],
                                        w_branch_attn[l].astype(BF16), w_branch_fourier[l].astype(BF16),
                                        w_branch_ret[l].astype(BF16), w_out[l].astype(BF16), wrh, wrl, brb,
                                        need_ctx)
        final = l == DEPTH - 1
        xf = _moe(xs, route, nch3, x_mid, mod_l, norm_final[None, :], w_exp_gate, w_exp_up, w_exp_down,
                  row0, final, l)
    return xf.reshape(BATCH, SEQ, D_MODEL)
```

```python
import functools

import numpy as np
import jax
import jax.numpy as jnp
from jax import lax
from jax.experimental import pallas as pl
from jax.experimental.pallas import tpu as pltpu

F32 = jnp.float32
BF16 = jnp.bfloat16

D_MODEL = 1024
BATCH = 8
SEQ = 2048
DEPTH = 2
CTX_LEN = 256
GRID_W = 64
HEAD_DIM = 64
ATTN_HEADS = 8
ATTN_KV_HEADS = 2
ATTN_GROUP = ATTN_HEADS // ATTN_KV_HEADS
ATTN_BLOCK = 128
RET_HEADS = 8
RET_DK = 64
RET_CHUNK = 128
FOURIER_GROUPS = 4
FOURIER_DIM = 128
N_GROUPS = 4
EXPERTS_PER_GROUP = 8
N_EXPERTS = N_GROUPS * EXPERTS_PER_GROUP
EXPERT_HIDDEN = 512
ROPE_BASE = 10000.0
NORM_EPS = 1e-6
GN_EPS = 1e-5

W = 512
IN_COLS = 6400
RC = BATCH * CTX_LEN
RL = BATCH * SEQ
R = RC + RL
MOD_ROWS = 16
CTX_MOD_ROW = 8

VMEM_LIMIT = 52 * 1024 * 1024

TM_PROJ = 512
TM_MERGE = 512
TN_ADA = 1536
TR_FOURIER = 512
RET_SLAB = 2 * CTX_LEN
MOE_BM = 512
CHUNK = 8
CHUNK_SHIFT = 3
CPB = MOE_BM // CHUNK
GATHER_DEPTH = 3
SLOTS = 1280
XS_COLS = D_MODEL // 2 + 128
NEG = -1e30
LOG2E = 1.4426950408889634
ROUTER_ROWS = 40


def _dot(a, b):
    return jnp.dot(a, b, preferred_element_type=F32)


def _dot_nt(a, b):
    return lax.dot_general(a, b, (((1,), (1,)), ((), ())), preferred_element_type=F32)


def _dot_tn(a, b):
    return lax.dot_general(a, b, (((0,), (0,)), ((), ())), preferred_element_type=F32)


def _split(x):
    hi = x.astype(BF16)
    lo = (x - hi.astype(F32)).astype(BF16)
    return hi, lo


def _sigmoid(x):
    return 1.0 / (1.0 + jnp.exp(-x))


def _params(sem, vmem=VMEM_LIMIT):
    return pltpu.CompilerParams(dimension_semantics=sem, vmem_limit_bytes=vmem)


def _mod_row(row0, tm):
    def f(i):
        g0 = i * tm + row0
        return jnp.where(g0 < RC, CTX_MOD_ROW, (g0 - RC) // SEQ)
    return f


def _rope_tables():
    pos = np.arange(SEQ, dtype=np.float64)
    row = np.floor(pos / GRID_W)
    col = pos % GRID_W

    def cs(p, nf):
        inv = ROPE_BASE ** (-np.arange(nf, dtype=np.float64) / nf)
        ang = p[:, None] * inv[None, :]
        return np.cos(ang), np.sin(ang)

    rc, rs = cs(row, HEAD_DIM // 4)
    cc, cs_ = cs(col, HEAD_DIM // 4)
    cos_a = np.concatenate([rc, rc, cc, cc], axis=1)
    sin_a = np.concatenate([-rs, rs, -cs_, cs_], axis=1)
    tc, ts = cs(pos, RET_DK // 2)
    cos_r = np.concatenate([tc, tc], axis=1)
    sin_r = np.concatenate([-ts, ts], axis=1)

    def full(t, ident):
        t2 = np.concatenate([t, t], axis=1)
        return np.concatenate([np.full_like(t2, ident), t2], axis=0).astype(np.float32)

    return full(cos_a, 1.0), full(sin_a, 0.0), full(cos_r, 1.0), full(sin_r, 0.0)


def _dft_tables():
    def cs(n):
        k = np.arange(n, dtype=np.int64)
        m = (k[:, None] * k[None, :]) % n
        ang = 2.0 * np.pi * m.astype(np.float64) / n
        return np.cos(ang), np.sin(ang)

    c128, s128 = cs(FOURIER_DIM)
    eye = np.eye(FOURIER_GROUPS)
    bdc = np.kron(eye, c128).astype(np.float32)
    bds = np.kron(eye, s128).astype(np.float32)
    cn, sn = cs(SEQ)
    hn = SEQ // 2
    w2 = np.concatenate([cn[:, :hn], -sn[:, :hn]], axis=1).astype(np.float32)
    cl, sl = cs(CTX_LEN)
    w2c = np.concatenate([cl, -sl], axis=1).astype(np.float32)
    return bdc, bds, w2, w2c


def _retention_tables(dec_f, dec_b):
    lg_f = jax.nn.log_sigmoid(dec_f.astype(F32))
    lg_b = jax.nn.log_sigmoid(dec_b.astype(F32))
    i = jnp.arange(RET_CHUNK)
    diff = (i[:, None] - i[None, :]).astype(F32)
    fwd = jnp.exp(jnp.maximum(diff, 0.0)[None] * lg_f[:, None, None])
    bwd = jnp.exp(jnp.maximum(-diff, 0.0)[None] * lg_b[:, None, None])
    dcomb = jnp.where((diff >= 0)[None], fwd, bwd).reshape(2, 4 * RET_CHUNK, RET_CHUNK)
    fi = i.astype(F32)
    lanes = lambda t: jnp.repeat(t, RET_DK, axis=1)
    xi_f = lanes(jnp.exp((fi + 1.0)[:, None] * lg_f[None, :]))
    zt_f = lanes(jnp.exp((RET_CHUNK - 1 - fi)[:, None] * lg_f[None, :]))
    xi_b = lanes(jnp.exp((RET_CHUNK - fi)[:, None] * lg_b[None, :]))
    zt_b = lanes(jnp.exp(fi[:, None] * lg_b[None, :]))
    g_f = jnp.repeat(jnp.exp(RET_CHUNK * lg_f), RET_DK).reshape(2, 256, 1)
    g_b = jnp.repeat(jnp.exp(RET_CHUNK * lg_b), RET_DK).reshape(2, 256, 1)
    g_f = jnp.broadcast_to(g_f, (2, 256, 256))
    g_b = jnp.broadcast_to(g_b, (2, 256, 256))
    return dcomb, xi_f, zt_f, xi_b, zt_b, g_f, g_b


def _ada_body(c_ref, w_ref, b_ref, o_ref):
    c = c_ref[...]
    s = c * _sigmoid(c)
    sh, sl = _split(s)
    wh, wl = _split(w_ref[...])
    o_ref[...] = _dot(sh, wh) + _dot(sl, wh) + _dot(sh, wl) + b_ref[...]


def _ada(cc, w_ada, b_ada):
    nt = 6 * D_MODEL // TN_ADA
    return pl.pallas_call(
        _ada_body,
        grid=(DEPTH, nt),
        in_specs=[
            pl.BlockSpec((MOD_ROWS, D_MODEL), lambda l, j: (0, 0)),
            pl.BlockSpec((None, D_MODEL, TN_ADA), lambda l, j: (l, 0, j)),
            pl.BlockSpec((None, 1, TN_ADA), lambda l, j: (l, 0, j)),
        ],
        out_specs=pl.BlockSpec((None, MOD_ROWS, TN_ADA), lambda l, j: (l, 0, j)),
        out_shape=jax.ShapeDtypeStruct((DEPTH, MOD_ROWS, 6 * D_MODEL), F32),
        compiler_params=_params(("arbitrary", "arbitrary")),
        name="ada_mod",
    )(cc, w_ada, b_ada.reshape(DEPTH, 1, 6 * D_MODEL))


def _rope(xc, cos, sin, half):
    fwd = pltpu.roll(xc, 128 - half, axis=1)
    bwd = pltpu.roll(xc, half, axis=1)
    lane = lax.broadcasted_iota(jnp.int32, xc.shape, 1)
    first = (lane & (2 * half - 1)) < half
    return xc * cos + jnp.where(first, fwd, bwd) * sin


def _proj_body(*refs, split):
    if split:
        x = _pick(pl.program_id(0) < RC // TM_PROJ, refs[0], refs[1])
        refs = refs[2:]
    else:
        x = refs[0][...]
        refs = refs[1:]
    (mod_ref, gn_ref, w_ref, ca_ref, sa_ref, cr_ref, sr_ref,
     qa_ref, ka_ref, va_ref, qr_ref, kr_ref, vr_ref, gr_ref, fu_ref, gm_ref) = refs
    ms = jnp.mean(x * x, axis=-1, keepdims=True)
    y = x * lax.rsqrt(ms + NORM_EPS) * gn_ref[...]
    h = y * (1.0 + mod_ref[1:2, :]) + mod_ref[0:1, :]
    hb = h.astype(BF16)

    def proj(c0, width):
        return _dot(hb, w_ref[:, c0:c0 + width])

    ca, sa, cr, sr = ca_ref[...], sa_ref[...], cr_ref[...], sr_ref[...]

    for c in range(3):
        gm_ref[:, c * D_MODEL:(c + 1) * D_MODEL] = _sigmoid(proj(3328 + c * D_MODEL, D_MODEL)).astype(BF16)
    g = proj(2304, W)
    gr_ref[...] = (g * _sigmoid(g)).astype(BF16)
    qa = proj(0, W) * (HEAD_DIM ** -0.5 * LOG2E)
    for c in range(W // 128):
        qa_ref[:, c * 128:(c + 1) * 128] = _rope(qa[:, c * 128:(c + 1) * 128], ca, sa, 16).astype(BF16)
    kv = proj(W, 256)
    ka = _rope(kv[:, 0:128], ca, sa, 16).astype(BF16)
    ka_ref[0] = ka[:, 0:64]
    ka_ref[1] = ka[:, 64:128]
    va = kv[:, 128:256].astype(BF16)
    ones_col = jnp.where(lax.broadcasted_iota(jnp.int32, (va.shape[0], 64), 1) == 0, 1.0, 0.0).astype(BF16)
    va_ref[0] = jnp.concatenate([va[:, 0:64], ones_col], axis=1)
    va_ref[1] = jnp.concatenate([va[:, 64:128], ones_col], axis=1)
    qr = proj(768, W)
    kr = proj(1280, W) * (RET_DK ** -0.5)
    for c in range(W // 128):
        sl = slice(c * 128, (c + 1) * 128)
        qr_ref[:, sl] = _rope(qr[:, sl], cr, sr, 32).astype(BF16)
        kr_ref[:, sl] = _rope(kr[:, sl], cr, sr, 32).astype(BF16)
    vr_ref[...] = proj(1792, W).astype(BF16)
    fu_ref[...] = proj(2816, W).astype(BF16)


def _proj(x, mod_l, gnorm, w_in_bf, tabs, layer):
    tm = TM_PROJ
    nt = R // tm
    nc = RC // tm
    split = isinstance(x, tuple)
    if split:
        xs = list(x)
        x_specs = [pl.BlockSpec((tm, D_MODEL), lambda i: (jnp.minimum(i, nc - 1), 0)),
                   pl.BlockSpec((tm, D_MODEL), lambda i: (jnp.maximum(i - nc, 0), 0))]
    else:
        xs = [x]
        x_specs = [pl.BlockSpec((tm, D_MODEL), lambda i: (i, 0))]

    def tab_map(i):
        return (jnp.where(i < nc, i, nc + (i - nc) % (SEQ // tm)), 0)

    row = lambda i: (i, 0)
    wide = lambda n: pl.BlockSpec((tm, n), row)
    kv_spec = lambda n: pl.BlockSpec((2, tm, n), lambda i: (0, i, 0))
    sds = lambda n: jax.ShapeDtypeStruct((R, n), BF16)
    kv_sds = lambda n: jax.ShapeDtypeStruct((2, R, n), BF16)
    mrow = _mod_row(0, tm)
    return pl.pallas_call(
        functools.partial(_proj_body, split=split),
        grid=(nt,),
        in_specs=x_specs + [
            pl.BlockSpec((None, 6, D_MODEL), lambda i: (mrow(i), 0, 0)),
            pl.BlockSpec((1, D_MODEL), lambda i: (0, 0)),
            pl.BlockSpec((None, D_MODEL, IN_COLS), lambda i: (layer, 0, 0), pipeline_mode=pl.Buffered(1)),
        ] + [pl.BlockSpec((tm, 128), tab_map)] * 4,
        out_specs=[wide(W), kv_spec(64), kv_spec(128), wide(W), wide(W), wide(W), wide(W), wide(W),
                   wide(3 * D_MODEL)],
        out_shape=[sds(W), kv_sds(64), kv_sds(128), sds(W), sds(W), sds(W), sds(W), sds(W), sds(3 * D_MODEL)],
        compiler_params=_params(("arbitrary",)),
        name="in_proj",
    )(*xs, mod_l, gnorm, w_in_bf, *tabs)


def _attn_body(sink_ref, q_ref, kc_ref, k0_ref, k1_ref, k2_ref, k3_ref, vc_ref, v0_ref, v1_ref, v2_ref, v3_ref,
               o_ref, *, has_ctx):
    j = pl.program_id(1)
    p = j - 1 if has_ctx else j
    nr = ATTN_GROUP * ATTN_BLOCK
    groups = [slice(g * ATTN_BLOCK, (g + 1) * ATTN_BLOCK) for g in range(ATTN_GROUP)]

    def attend(sub, pieces):
        rows = slice(sub * ATTN_BLOCK, (sub + 1) * ATTN_BLOCK)
        outs = []
        for h in range(ATTN_KV_HEADS):
            q = q_ref[rows, h * 256:(h + 1) * 256]
            q4 = jnp.concatenate([q[:, g * 64:(g + 1) * 64] for g in range(ATTN_GROUP)], axis=0)
            sinks = [sink_ref[h * ATTN_GROUP + g] * LOG2E for g in range(ATTN_GROUP)]
            k_all = jnp.concatenate([k_ref[h] for k_ref, _, _ in pieces], axis=0)
            v_all = jnp.concatenate([v_ref[h] for _, v_ref, _ in pieces], axis=0)
            s = _dot_nt(q4, k_all)
            cols, c0 = [], 0
            for k_ref, _, ok in pieces:
                n = k_ref.shape[1]
                cols.append(s[:, c0:c0 + n] if ok is None else jnp.where(ok, s[:, c0:c0 + n], NEG))
                c0 += n
            s = jnp.concatenate(cols, axis=1)
            mxs, ps = [], []
            for g, r in enumerate(groups):
                mx = jnp.maximum(jnp.max(s[r], axis=-1, keepdims=True), sinks[g])
                mxs.append(mx)
                ps.append(jnp.exp2(s[r] - mx).astype(BF16))
            oa = _dot(jnp.concatenate(ps, axis=0), v_all)
            for g, r in enumerate(groups):
                den = oa[r, HEAD_DIM:HEAD_DIM + 1] + jnp.exp2(sinks[g] - mxs[g])
                outs.append(oa[r, 0:HEAD_DIM] / den)
        o_ref[rows, :] = jnp.concatenate(outs, axis=1).astype(BF16)

    @pl.when(p >= 0)
    def _():
        ri = lax.broadcasted_iota(jnp.int32, (nr, ATTN_BLOCK), 0) & (ATTN_BLOCK - 1)
        ci = lax.broadcasted_iota(jnp.int32, (nr, ATTN_BLOCK), 1)
        far = 4 * ATTN_BLOCK
        first_prev = ci >= ri + jnp.where(p >= 1, 0, far)
        last_next = ci + jnp.where(p <= SEQ // (2 * ATTN_BLOCK) - 2, 0, far) <= ri
        attend(0, [(kc_ref, vc_ref, None), (k0_ref, v0_ref, first_prev), (k1_ref, v1_ref, None),
                   (k2_ref, v2_ref, ci <= ri)])
        attend(1, [(kc_ref, vc_ref, None), (k1_ref, v1_ref, ci >= ri), (k2_ref, v2_ref, None),
                   (k3_ref, v3_ref, last_next)])

    if has_ctx:
        @pl.when(p < 0)
        def _():
            attend(0, [(kc_ref, vc_ref, None)])
            attend(1, [(kc_ref, vc_ref, None)])


def _attention(sink, qa, ka, va, need_ctx):
    nb = SEQ // ATTN_BLOCK
    npair = nb // 2
    first = 1 if need_ctx else 0
    tq = 2 * ATTN_BLOCK
    lat0 = RC // tq
    out0 = 0 if need_ctx else lat0

    def qrow(b, j):
        return jnp.where(j < first, b, lat0 + b * npair + j - first)

    def loc(delta):
        def f(b, j):
            m = jnp.clip(2 * (j - first) + delta, 0, nb - 1)
            return (0, RC // ATTN_BLOCK + b * nb + m, 0)
        return f

    def kv_specs(width):
        ctx_spec = pl.BlockSpec((ATTN_KV_HEADS, CTX_LEN, width), lambda b, j: (0, b, 0))
        return [ctx_spec] + [pl.BlockSpec((ATTN_KV_HEADS, ATTN_BLOCK, width), loc(d)) for d in (-1, 0, 1, 2)]

    return pl.pallas_call(
        functools.partial(_attn_body, has_ctx=need_ctx),
        grid=(BATCH, npair + first),
        in_specs=[pl.BlockSpec(memory_space=pltpu.SMEM),
                  pl.BlockSpec((tq, W), lambda b, j: (qrow(b, j), 0))]
                 + kv_specs(HEAD_DIM) + kv_specs(128),
        out_specs=pl.BlockSpec((tq, W), lambda b, j: (qrow(b, j) - out0, 0)),
        out_shape=jax.ShapeDtypeStruct((R - out0 * tq, W), BF16),
        compiler_params=_params(("arbitrary", "arbitrary")),
        name="window_attn",
    )(sink, qa, *([ka] * 5), *([va] * 5))


def _fourier_body(*refs, has_ctx):
    if has_ctx:
        uc_ref, ul_ref, jrev_ref, bdc_ref, bds_ref, w2_ref, w2c_ref, oc_ref, ol_ref, as_ref, mid_ref = refs
    else:
        ul_ref, jrev_ref, bdc_ref, bds_ref, w2_ref, ol_ref, as_ref, mid_ref = refs
    j = pl.program_id(1)
    first = 1 if has_ctx else 0
    hn = SEQ // 2

    if has_ctx:
        @pl.when(j == 0)
        def _():
            u = uc_ref[...]
            a = _dot(u, bdc_ref[...]).astype(BF16)
            s = _dot(u, bds_ref[...]).astype(BF16)
            z = _dot(w2c_ref[...], jnp.concatenate([a, s], axis=0))
            oc_ref[...] = (z * ((CTX_LEN * FOURIER_DIM) ** -0.5)).astype(BF16)

    @pl.when(j == first)
    def _():
        uh = ul_ref[0:hn, :].astype(F32)
        ur = _dot(jrev_ref[...], ul_ref[hn:SEQ, :])
        row = lax.broadcasted_iota(jnp.int32, (hn, W), 0)
        vm = jnp.where(row == 0, 0.0, uh - ur)
        as_ref[0:hn, :] = _dot((uh + ur).astype(BF16), bdc_ref[...]).astype(BF16)
        as_ref[hn:SEQ, :] = _dot(vm.astype(BF16), bds_ref[...]).astype(BF16)
        mid_ref[...] = _dot(ul_ref[hn:hn + 8, :], bdc_ref[...])

    @pl.when(j >= first)
    def _():
        z = _dot(w2_ref[...], as_ref[...])
        k = lax.broadcasted_iota(jnp.int32, (z.shape[0], 1), 0)
        sign = (1 - 2 * (k & 1)).astype(F32)
        ol_ref[...] = ((z + sign * mid_ref[0:1, :]) * ((SEQ * FOURIER_DIM) ** -0.5)).astype(BF16)


def _fourier(fu, dft, need_ctx):
    bdc, bds, w2, w2c = dft
    tr = TR_FOURIER
    nj = SEQ // tr
    hn = SEQ // 2
    first = 1 if need_ctx else 0
    jrev_np = np.zeros((hn, hn), np.float32)
    jrev_np[np.arange(1, hn), hn - np.arange(1, hn)] = 1.0
    jrev = jnp.asarray(jrev_np).astype(BF16)
    full = lambda a: pl.BlockSpec(a.shape, lambda *_: (0,) * a.ndim)
    lat_tile = lambda j: jnp.maximum(j - first, 0)
    ul_spec = pl.BlockSpec((SEQ, W), lambda b, j: (1 + b, 0))
    w2_spec = pl.BlockSpec((tr, SEQ), lambda b, j: (lat_tile(j), 0))
    ol_spec = pl.BlockSpec((tr, W), lambda b, j: (b * nj + lat_tile(j), 0))
    ol_shape = jax.ShapeDtypeStruct((RL, W), BF16)
    if need_ctx:
        in_specs = [pl.BlockSpec((CTX_LEN, W), lambda b, j: (b, 0)), ul_spec, full(jrev), full(bdc), full(bds),
                    w2_spec, full(w2c)]
        out_specs = [pl.BlockSpec((CTX_LEN, W), lambda b, j: (b, 0)), ol_spec]
        out_shape = [jax.ShapeDtypeStruct((RC, W), BF16), ol_shape]
        args = (fu, fu, jrev, bdc, bds, w2, w2c)
    else:
        in_specs = [ul_spec, full(jrev), full(bdc), full(bds), w2_spec]
        out_specs = [ol_spec]
        out_shape = [ol_shape]
        args = (fu, jrev, bdc, bds, w2)
    outs = pl.pallas_call(
        functools.partial(_fourier_body, has_ctx=need_ctx),
        grid=(BATCH, nj + first),
        in_specs=in_specs,
        out_specs=out_specs,
        out_shape=out_shape,
        scratch_shapes=[pltpu.VMEM((SEQ, W), BF16), pltpu.VMEM((8, W), F32)],
        compiler_params=_params(("arbitrary", "arbitrary")),
        name="fourier_mix",
    )(*args)
    return (outs[0], outs[1]) if need_ctx else (None, outs[0])


def _retention_body(qc_ref, kc_ref, vc_ref, ql0_ref, kl0_ref, vl0_ref, ql1_ref, kl1_ref, vl1_ref, g_ref,
                    dcomb_ref, xif_ref, ztf_ref, xib_ref, ztb_ref, gf_ref, gb_ref, mbd_ref, avg_ref,
                    o_ref, os_ref, st_ref):
    j = pl.program_id(1)
    C = RET_CHUNK
    nl = SEQ // C
    lat_refs = ((ql0_ref, kl0_ref, vl0_ref), (ql1_ref, kl1_ref, vl1_ref))

    own = ((lax.broadcasted_iota(jnp.int32, (4 * C, 256), 0) >> 7)
           == (lax.broadcasted_iota(jnp.int32, (4 * C, 256), 1) >> 6))

    def chunk_fwd(bb, q, k, v, r0):
        for gi in range(2):
            sl = slice(gi * 256, (gi + 1) * 256)
            q4, k4, v4 = q[:, sl], k[:, sl], v[:, sl]
            s_prev = st_ref[2 * bb + gi]
            q4f = q4.astype(F32)
            o4 = _dot((q4f * xif_ref[:, sl]).astype(BF16), s_prev.astype(BF16))
            qstack = jnp.where(own, jnp.concatenate([q4f] * 4, axis=0), 0.0).astype(BF16)
            p = (_dot_nt(qstack, k4) * dcomb_ref[gi]).astype(BF16)
            ov = jnp.where(own, _dot(p, v4), 0.0)
            intra = ov[0:C] + ov[C:2 * C] + ov[2 * C:3 * C] + ov[3 * C:4 * C]
            os_ref[pl.ds(r0, C), sl] = o4 + intra
            u = _dot_tn(k4, (v4.astype(F32) * ztf_ref[:, sl]).astype(BF16))
            st_ref[2 * bb + gi] = gf_ref[gi] * s_prev + mbd_ref[...] * u

    def chunk_bwd(bb, q, k, v, r0):
        for gi in range(2):
            sl = slice(gi * 256, (gi + 1) * 256)
            q4, k4, v4 = q[:, sl], k[:, sl], v[:, sl]
            s_prev = st_ref[2 * bb + gi]
            cross = _dot((q4.astype(F32) * xib_ref[:, sl]).astype(BF16), s_prev.astype(BF16))
            os_ref[pl.ds(r0, C), sl] = os_ref[pl.ds(r0, C), sl] + cross
            u = _dot_tn(k4, (v4.astype(F32) * ztb_ref[:, sl]).astype(BF16))
            st_ref[2 * bb + gi] = gb_ref[gi] * s_prev + mbd_ref[...] * u

    def scan(chunk, ctx_order, lat_index):
        st_ref[...] = jnp.zeros_like(st_ref)
        for c in ctx_order:
            for bb in range(2):
                rs = slice(bb * CTX_LEN + c * C, bb * CTX_LEN + (c + 1) * C)
                chunk(bb, qc_ref[rs, :], kc_ref[rs, :], vc_ref[rs, :], bb * CTX_LEN + c * C)

        def body(t, carry):
            r0 = pl.multiple_of(lat_index(t) * C, C)
            for bb, (q_ref, k_ref, v_ref) in enumerate(lat_refs):
                rs = pl.ds(r0, C)
                chunk(bb, q_ref[rs, :], k_ref[rs, :], v_ref[rs, :], 2 * CTX_LEN + bb * SEQ + r0)
            return carry

        lax.fori_loop(0, nl, body, 0)

    @pl.when(j == 0)
    def _():
        scan(chunk_fwd, range(CTX_LEN // C), lambda t: t)
        scan(chunk_bwd, reversed(range(CTX_LEN // C)), lambda t: nl - 1 - t)

    o = os_ref[pl.ds(pl.multiple_of(j * RET_SLAB, RET_SLAB), RET_SLAB), :]
    avg = avg_ref[...]
    oh, ol = _split(o)
    mu = _dot(oh, avg) + _dot(ol, avg)
    d = o - mu
    var = _dot((d * d).astype(BF16), avg)
    o_ref[...] = (g_ref[...].astype(F32) * d * lax.rsqrt(var + GN_EPS)).astype(BF16)


def _retention(qr, kr, vr, gr, rtabs):
    nslab = SEQ // RET_SLAB
    nj = 1 + 2 * nslab

    def out_map(b2, j):
        return (jnp.where(j == 0, b2, RC // RET_SLAB + 2 * b2 * nslab + j - 1), 0)

    ctx = pl.BlockSpec((2 * CTX_LEN, W), lambda b2, j: (b2, 0))
    lat = lambda k: pl.BlockSpec((SEQ, W), lambda b2, j: (1 + 2 * b2 + k, 0))
    full = lambda a: pl.BlockSpec(a.shape, lambda *_: (0,) * a.ndim)
    avg = jnp.asarray(np.kron(np.eye(RET_HEADS), np.full((RET_DK, RET_DK), 1.0 / RET_DK)).astype(np.float32)).astype(BF16)
    mbd = jnp.asarray(np.kron(np.eye(4), np.ones((RET_DK, RET_DK))).astype(np.float32))
    tabs = list(rtabs) + [mbd, avg]
    return pl.pallas_call(
        _retention_body,
        grid=(BATCH // 2, nj),
        in_specs=[ctx, ctx, ctx, lat(0), lat(0), lat(0), lat(1), lat(1), lat(1),
                  pl.BlockSpec((RET_SLAB, W), out_map)] + [full(t) for t in tabs],
        out_specs=pl.BlockSpec((RET_SLAB, W), out_map),
        out_shape=jax.ShapeDtypeStruct((R, W), BF16),
        scratch_shapes=[pltpu.VMEM((2 * (CTX_LEN + SEQ), W), F32), pltpu.VMEM((4, 256, 256), F32)],
        compiler_params=_params(("arbitrary", "arbitrary")),
        name="retention",
    )(qr, kr, vr, qr, kr, vr, qr, kr, vr, gr, *tabs)


def _pack_pair(a, b):
    ua = lax.bitcast_convert_type(a, jnp.uint32) >> 16
    ub = lax.bitcast_convert_type(b, jnp.uint32) & jnp.uint32(0xFFFF0000)
    return ua | ub


def _unpack_pair(w):
    a = lax.bitcast_convert_type(w << 16, F32)
    b = lax.bitcast_convert_type(w & jnp.uint32(0xFFFF0000), F32)
    return a, b


def _slot_onehot(s0, s1, n):
    srow = lax.broadcasted_iota(jnp.int32, (n, s0.shape[1]), 0)
    p0 = jnp.where(srow == s0, 1.0, 0.0).astype(BF16)
    p1 = jnp.where(srow == s1, 1.0, 0.0).astype(BF16)
    return p0, p1


def _pick(first, a_ref, b_ref):
    a = a_ref[...]
    flag = jnp.zeros(a.shape, jnp.int32) + first.astype(jnp.int32)
    return jnp.where(flag > 0, a, b_ref[...])


def _merge_body(*refs, layer0):
    if layer0:
        oa_ref, ofc_ref, ofl_ref, rt_ref, gm_ref, xc_ref, xl_ref = refs[:7]
        rest = refs[7:]
        is_ctx = pl.program_id(0) < RC // TM_MERGE
        of_in = _pick(is_ctx, ofc_ref, ofl_ref)
        x_in = _pick(is_ctx, xc_ref, xl_ref)
    else:
        oa_ref, ofl_ref, rt_ref, gm_ref, x_ref = refs[:5]
        rest = refs[5:]
        of_in = ofl_ref[...]
        x_in = x_ref[...]
    (mod_ref, gn_ref, wba_ref, wbf_ref, wbr_ref, wout_ref, wrh_ref, wrl_ref, br_ref, tri_ref, ltri_ref,
     xo_ref, xs_ref, ro_ref, nch_ref) = rest
    gm = gm_ref[...].astype(F32)
    z = (gm[:, 0:D_MODEL] * _dot(oa_ref[...], wba_ref[...])
         + gm[:, D_MODEL:2 * D_MODEL] * _dot(of_in, wbf_ref[...])
         + gm[:, 2 * D_MODEL:3 * D_MODEL] * _dot(rt_ref[...], wbr_ref[...]))
    y = _dot(z.astype(BF16), wout_ref[...])
    x = x_in + mod_ref[2:3, :] * y
    xo_ref[...] = x
    ms = jnp.mean(x * x, axis=-1, keepdims=True)
    hn = x * lax.rsqrt(ms + NORM_EPS) * gn_ref[...]
    h2 = hn * (1.0 + mod_ref[4:5, :]) + mod_ref[3:4, :]
    hh, hl = _split(h2)
    wh, wl = wrh_ref[...], wrl_ref[...]
    lg = _dot_nt(wh, hh) + _dot_nt(wh, hl) + _dot_nt(wl, hh) + br_ref[...]
    tm = lg.shape[1]
    row8 = lax.broadcasted_iota(jnp.int32, (8, tm), 0)
    lgg = lg[0:8, :]
    mg = jnp.max(lgg, axis=0, keepdims=True)
    grp = jnp.min(jnp.where(lgg == mg, row8, 8), axis=0, keepdims=True)
    pg = 1.0 / jnp.sum(jnp.exp(lgg - mg), axis=0, keepdims=True)
    lin = jnp.zeros((8, tm), F32)
    for g in range(N_GROUPS):
        lin = jnp.where(grp == g, lg[8 + 8 * g:16 + 8 * g, :], lin)
    v1 = jnp.max(lin, axis=0, keepdims=True)
    i1 = jnp.min(jnp.where(lin == v1, row8, 8), axis=0, keepdims=True)
    rest = jnp.where(row8 == i1, -jnp.inf, lin)
    v2 = jnp.max(rest, axis=0, keepdims=True)
    i2 = jnp.min(jnp.where(rest == v2, row8, 8), axis=0, keepdims=True)
    e2 = jnp.exp(v2 - v1)
    w1 = pg / (1.0 + e2)
    w2 = pg * e2 / (1.0 + e2)
    e_1 = grp * EXPERTS_PER_GROUP + i1
    e_2 = grp * EXPERTS_PER_GROUP + i2

    row32 = lax.broadcasted_iota(jnp.int32, (N_EXPERTS, tm), 0)
    oh0 = jnp.where(row32 == e_1, 1.0, 0.0)
    oh1 = jnp.where(row32 == e_2, 1.0, 0.0)
    tri = tri_ref[...]
    cum0 = _dot(oh0.astype(BF16), tri)
    cum1 = _dot(oh1.astype(BF16), tri)
    tot0 = jnp.sum(oh0, axis=1, keepdims=True)
    tot1 = jnp.sum(oh1, axis=1, keepdims=True)
    nch = ((tot0 + tot1).astype(jnp.int32) + (CHUNK - 1)) >> CHUNK_SHIFT
    nch_b = jnp.broadcast_to(nch.astype(F32), (N_EXPERTS, 128))
    nch_ref[...] = nch_b.astype(jnp.int32)
    base = CHUNK * _dot(ltri_ref[...], nch_b.astype(BF16))[:, 0:1]
    s0 = jnp.sum(oh0 * (base + cum0), axis=0, keepdims=True).astype(jnp.int32)
    s1 = jnp.sum(oh1 * (base + tot0 + cum1), axis=0, keepdims=True).astype(jnp.int32)
    p0, p1 = _slot_onehot(s0, s1, SLOTS)
    xs = _dot(p0 + p1, hh)
    xs_ref[:, 0:D_MODEL // 2] = _pack_pair(xs[:, 0:D_MODEL // 2], xs[:, D_MODEL // 2:D_MODEL])

    def wrows(w):
        hi, lo = _split(w)
        return jnp.where(row8 == 0, hi.astype(F32), jnp.where(row8 == 1, lo.astype(F32), 0.0)).astype(BF16)

    wc = _dot_nt(p0, wrows(w1)) + _dot_nt(p1, wrows(w2))
    wcol = jnp.broadcast_to(wc[:, 0:1] + wc[:, 1:2], (SLOTS, 128))
    xs_ref[:, D_MODEL // 2:XS_COLS] = lax.bitcast_convert_type(wcol, jnp.uint32)
    s0f, s1f = s0.astype(F32), s1.astype(F32)
    ro_ref[...] = jnp.where(row8 == 0, s0f, jnp.where(row8 == 1, s1f, 0.0))


def _merge(oa, of_c, of_l, ret, gm, xs_in, mod_l, gnorm, wba, wbf, wbr, wout, wrh, wrl, brb, layer0):
    tm = TM_MERGE
    row0 = 0 if layer0 else RC
    rm = R - row0
    nt = rm // tm
    off = row0 // tm
    nc = RC // tm
    src = lambda n: pl.BlockSpec((tm, n), lambda i: (i + off, 0))
    dst = lambda n: pl.BlockSpec((tm, n), lambda i: (i, 0))
    ctx_rows = lambda n: pl.BlockSpec((tm, n), lambda i: (jnp.minimum(i, nc - 1), 0))
    lat_rows = lambda n: pl.BlockSpec((tm, n), lambda i: (jnp.maximum(i - nc, 0), 0))
    full = lambda a: pl.BlockSpec(a.shape, lambda *_: (0,) * a.ndim, pipeline_mode=pl.Buffered(1))
    mrow = _mod_row(row0, tm)
    tri = jnp.asarray(np.triu(np.ones((tm, tm), np.float32), 1)).astype(BF16)
    ltri = jnp.asarray(np.tril(np.ones((N_EXPERTS, N_EXPERTS), np.float32), -1)).astype(BF16)
    if layer0:
        acts = [oa, of_c, of_l, ret, gm, xs_in[0], xs_in[1]]
        act_specs = [src(W), ctx_rows(W), lat_rows(W), src(W), src(3 * D_MODEL), ctx_rows(D_MODEL),
                     lat_rows(D_MODEL)]
    else:
        acts = [oa, of_l, ret, gm, xs_in]
        act_specs = [dst(W), dst(W), src(W), src(3 * D_MODEL), src(D_MODEL)]
    return pl.pallas_call(
        functools.partial(_merge_body, layer0=layer0),
        grid=(nt,),
        in_specs=act_specs + [
                  pl.BlockSpec((None, 6, D_MODEL), lambda i: (mrow(i), 0, 0)),
                  full(gnorm), full(wba), full(wbf), full(wbr), full(wout), full(wrh), full(wrl), full(brb),
                  full(tri), full(ltri)],
        out_specs=[dst(D_MODEL), pl.BlockSpec((SLOTS, XS_COLS), lambda i: (i, 0)),
                   pl.BlockSpec((8, tm), lambda i: (0, i)),
                   pl.BlockSpec((None, N_EXPERTS, 128), lambda i: (i, 0, 0))],
        out_shape=[jax.ShapeDtypeStruct((rm, D_MODEL), F32),
                   jax.ShapeDtypeStruct((nt * SLOTS, XS_COLS), jnp.uint32),
                   jax.ShapeDtypeStruct((8, rm), F32),
                   jax.ShapeDtypeStruct((nt, N_EXPERTS, 128), jnp.int32)],
        compiler_params=_params(("arbitrary",)),
        name="merge_router",
    )(*acts, mod_l, gnorm, wba, wbf, wbr, wout, wrh, wrl, brb, tri, ltri)


def _moe_plan(nch, nb):
    nt = nch.shape[0]
    choff = jnp.cumsum(nch, axis=1) - nch
    used_ch = jnp.sum(nch, axis=1)
    cum_t = jnp.cumsum(nch, axis=0)
    tot = cum_t[-1]
    ptot = (tot + CPB - 1) // CPB * CPB
    pend = jnp.cumsum(ptot)
    pstart = pend - ptot
    n_used = pend[-1] // CPB
    blk = jnp.arange(nb, dtype=jnp.int32)
    lane = jnp.arange(CPB, dtype=jnp.int32)
    blk_e = jnp.minimum(jnp.sum((blk[:, None] * CPB >= pend[None, :]).astype(jnp.int32), axis=1), N_EXPERTS - 1)
    oe = (blk_e[:, None] == jnp.arange(N_EXPERTS, dtype=jnp.int32)[None, :]).astype(jnp.int32)
    sel = lambda tab: jnp.sum(oe[:, :, None] * tab.T[None, :, :], axis=1)
    pstart_b = jnp.sum(oe * pstart[None, :], axis=1)
    tot_b = jnp.sum(oe * tot[None, :], axis=1)
    cum_b, nch_b, choff_b = sel(cum_t), sel(nch), sel(choff)
    i = blk[:, None] * CPB + lane[None, :] - pstart_b[:, None]
    valid = (i < tot_b[:, None]) & (blk[:, None] < n_used)
    t = jnp.minimum(jnp.sum((i[:, :, None] >= cum_b[:, None, :]).astype(jnp.int32), axis=2), nt - 1)
    tiles = jnp.arange(nt, dtype=jnp.int32)[None, None, :]
    before = jnp.sum(jnp.where(tiles < t[:, :, None], nch_b[:, None, :], 0), axis=2)
    coff = jnp.sum(jnp.where(tiles == t[:, :, None], choff_b[:, None, :], 0), axis=2)
    row = t * SLOTS + CHUNK * (coff + i - before)
    src = jnp.where(valid, row, SLOTS - CHUNK)
    dummy = nt * SLOTS + CHUNK * ((blk[:, None] % 2) * CPB + lane[None, :])
    dst = jnp.where(valid, row, dummy)
    blk_start = jnp.concatenate([pstart, pend[-1:]]) // CPB
    return (blk_start.astype(jnp.int32), n_used.astype(jnp.int32).reshape(1), src.reshape(-1).astype(jnp.int32),
            dst.reshape(-1).astype(jnp.int32), used_ch.astype(jnp.int32))


def _ffn_body(bs_ref, nu_ref, src_ref, dst_ref, uc_ref, xs_ref, wg_ref, wu_ref, wd_ref, ys_ref,
              xbuf, ybuf, zbuf, wgb, wub, wdb, sem_in, sem_out, sem_z, *, nt):
    e = pl.program_id(0)
    nu = nu_ref[0]
    half = D_MODEL // 2

    def gather(blk, sl):
        for c in range(CPB):
            r = pl.multiple_of(src_ref[blk * CPB + c], CHUNK)
            pltpu.make_async_copy(xs_ref.at[pl.ds(r, CHUNK)], xbuf.at[sl, pl.ds(c * CHUNK, CHUNK)],
                                  sem_in.at[sl]).start(priority=1)

    def scatter(blk, sl):
        for c in range(CPB):
            r = pl.multiple_of(dst_ref[blk * CPB + c], CHUNK)
            pltpu.make_async_copy(ybuf.at[sl, pl.ds(c * CHUNK, CHUNK)], ys_ref.at[pl.ds(r, CHUNK)],
                                  sem_out.at[sl]).start(priority=1)

    def wait_gather(sl):
        pltpu.make_async_copy(xs_ref.at[pl.ds(0, MOE_BM)], xbuf.at[sl], sem_in.at[sl]).wait()

    def wait_scatter(sl):
        pltpu.make_async_copy(ybuf.at[sl], ys_ref.at[pl.ds(0, MOE_BM)], sem_out.at[sl]).wait()

    def zero_copy(r):
        return pltpu.make_async_copy(zbuf, ys_ref.at[pl.ds(pl.multiple_of(r, CHUNK), CHUNK)], sem_z)

    @pl.when(e == 0)
    def _():
        zbuf[...] = jnp.zeros_like(zbuf)

        def tails(fn):
            def per_tile(t, carry):
                def per_chunk(c, carry2):
                    fn(t * SLOTS + c * CHUNK)
                    return carry2
                lax.fori_loop(uc_ref[t], SLOTS // CHUNK, per_chunk, 0)
                return carry
            lax.fori_loop(0, nt, per_tile, 0)
            for c in range(2 * CPB):
                fn(nt * SLOTS + c * CHUNK)

        tails(lambda r: zero_copy(r).start())
        tails(lambda r: zero_copy(r).wait())
        gather(0, 0)
        for d in range(1, GATHER_DEPTH - 1):
            @pl.when(nu > d)
            def _():
                gather(d, d)

    b0, b1 = bs_ref[e], bs_ref[e + 1]

    @pl.when(b1 > b0)
    def _():
        wgb[...] = wg_ref[...].astype(BF16)
        wub[...] = wu_ref[...].astype(BF16)
        wdb[...] = wd_ref[...].astype(BF16)

        def block(b, carry):
            slot = b % 2
            xslot = lax.rem(b, GATHER_DEPTH)

            @pl.when(b + GATHER_DEPTH - 1 < nu)
            def _():
                gather(b + GATHER_DEPTH - 1, lax.rem(b + GATHER_DEPTH - 1, GATHER_DEPTH))

            wait_gather(xslot)

            @pl.when(b >= 2)
            def _():
                wait_scatter(slot)

            xw = xbuf[xslot]
            xa, xb = _unpack_pair(xw[:, 0:half])
            x = jnp.concatenate([xa, xb], axis=1).astype(BF16)
            wt = lax.bitcast_convert_type(xw[:, half:XS_COLS], F32)
            g = _dot(x, wgb[...])
            u = _dot(x, wub[...])
            hmid = (g * _sigmoid(g) * u).astype(BF16)
            y = _dot(hmid, wdb[...]) * jnp.concatenate([wt] * (D_MODEL // 128), axis=1)
            yb = y.astype(BF16).astype(F32)
            ybuf[slot] = _pack_pair(yb[:, 0:half], yb[:, half:D_MODEL])
            scatter(b, slot)
            return carry

        lax.fori_loop(b0, b1, block, 0)

    @pl.when(e == N_EXPERTS - 1)
    def _():
        wait_scatter((nu - 1) % 2)

        @pl.when(nu >= 2)
        def _():
            wait_scatter(nu % 2)


def _ffn(plan, xs, w_g, w_u, w_d, layer, nt):
    wmap = lambda e, *_: (layer, e, 0, 0)
    half = D_MODEL // 2
    grid_spec = pltpu.PrefetchScalarGridSpec(
        num_scalar_prefetch=5,
        grid=(N_EXPERTS,),
        in_specs=[pl.BlockSpec(memory_space=pl.ANY),
                  pl.BlockSpec((None, None, D_MODEL, EXPERT_HIDDEN), wmap),
                  pl.BlockSpec((None, None, D_MODEL, EXPERT_HIDDEN), wmap),
                  pl.BlockSpec((None, None, EXPERT_HIDDEN, D_MODEL), wmap)],
        out_specs=pl.BlockSpec(memory_space=pl.ANY),
        scratch_shapes=[pltpu.VMEM((GATHER_DEPTH, MOE_BM, XS_COLS), jnp.uint32),
                        pltpu.VMEM((2, MOE_BM, half), jnp.uint32),
                        pltpu.VMEM((CHUNK, half), jnp.uint32),
                        pltpu.VMEM((D_MODEL, EXPERT_HIDDEN), BF16), pltpu.VMEM((D_MODEL, EXPERT_HIDDEN), BF16),
                        pltpu.VMEM((EXPERT_HIDDEN, D_MODEL), BF16),
                        pltpu.SemaphoreType.DMA((GATHER_DEPTH,)), pltpu.SemaphoreType.DMA((2,)),
                        pltpu.SemaphoreType.DMA(())],
    )
    return pl.pallas_call(
        functools.partial(_ffn_body, nt=nt),
        grid_spec=grid_spec,
        out_shape=jax.ShapeDtypeStruct((nt * SLOTS + 2 * CPB * CHUNK, half), jnp.uint32),
        compiler_params=_params(("arbitrary",)),
        name="moe_experts",
    )(*plan, xs, w_g, w_u, w_d)


def _combine_body(ys_ref, ro_ref, x_ref, mod_ref, gn_ref, o_ref, *, final):
    s = ro_ref[...]
    p0, p1 = _slot_onehot(s[0:1, :].astype(jnp.int32), s[1:2, :].astype(jnp.int32), SLOTS)
    ya, yb = _unpack_pair(ys_ref[...])
    y = jnp.concatenate([ya, yb], axis=1).astype(BF16)
    f = _dot_tn(p0 + p1, y)
    x = x_ref[...] + mod_ref[5:6, :] * f
    if final:
        ms = jnp.mean(x * x, axis=-1, keepdims=True)
        x = x * lax.rsqrt(ms + NORM_EPS) * gn_ref[...]
    o_ref[...] = x


def _combine(ys, route, x, mod_l, gnorm, row0, final):
    tm = TM_MERGE
    rm = x.shape[0]
    mrow = _mod_row(row0, tm)
    return pl.pallas_call(
        functools.partial(_combine_body, final=final),
        grid=(rm // tm,),
        in_specs=[pl.BlockSpec((SLOTS, D_MODEL // 2), lambda i: (i, 0)),
                  pl.BlockSpec((8, tm), lambda i: (0, i)),
                  pl.BlockSpec((tm, D_MODEL), lambda i: (i, 0)),
                  pl.BlockSpec((None, 6, D_MODEL), lambda i: (mrow(i), 0, 0)),
                  pl.BlockSpec((1, D_MODEL), lambda i: (0, 0))],
        out_specs=pl.BlockSpec((tm, D_MODEL), lambda i: (i, 0)),
        out_shape=jax.ShapeDtypeStruct((rm, D_MODEL), F32),
        compiler_params=_params(("arbitrary",)),
        name="moe_combine",
    )(ys, route, x, mod_l, gnorm)


def _moe(xs, route, nch3, x, mod_l, gnorm, w_g, w_u, w_d, row0, final, layer):
    nt = nch3.shape[0]
    max_chunks = nt * ((2 * TM_MERGE + N_EXPERTS * (CHUNK - 1)) // CHUNK)
    nb = -(-max_chunks // CPB) + N_EXPERTS
    plan = _moe_plan(nch3[:, :, 0], nb)
    ys = _ffn(plan, xs, w_g, w_u, w_d, layer, nt)
    return _combine(ys, route, x, mod_l, gnorm, row0, final)


def kernel(x, c, ctx, c_ctx, norm_mix, norm_ffn, w_ada, b_ada, w_in, attn_sink, ret_decay_fwd, ret_decay_bwd,
           w_branch_attn, w_branch_fourier, w_branch_ret, w_out, w_router_group, b_router_group,
           w_router_expert, b_router_expert, w_exp_gate, w_exp_up, w_exp_down, norm_final):
    tabs = [jnp.asarray(t) for t in _rope_tables()]
    dft = [jnp.asarray(t).astype(BF16) for t in _dft_tables()]

    cc = jnp.zeros((MOD_ROWS, D_MODEL), F32).at[0:BATCH].set(c).at[CTX_MOD_ROW].set(c_ctx)
    mod = _ada(cc, w_ada, b_ada).reshape(DEPTH, MOD_ROWS, 6, D_MODEL)

    xf = (ctx.reshape(RC, D_MODEL), x.reshape(RL, D_MODEL))
    w_in_bf = w_in.astype(BF16)
    for l in range(DEPTH):
        need_ctx = l < DEPTH - 1
        row0 = 0 if need_ctx else RC
        mod_l = mod[l]
        qa, ka, va, qr, kr, vr, gr, fu, gm = _proj(xf, mod_l, norm_mix[l][None, :], w_in_bf, tabs, l)
        oa = _attention(attn_sink[l], qa, ka, va, need_ctx)
        of_c, of_l = _fourier(fu, dft, need_ctx)
        ret = _retention(qr, kr, vr, gr, _retention_tables(ret_decay_fwd[l], ret_decay_bwd[l]))
        wr = jnp.zeros((ROUTER_ROWS, D_MODEL), F32)
        wr = wr.at[0:N_GROUPS].set(w_router_group[l].T).at[8:8 + N_EXPERTS].set(w_router_expert[l].T)
        br = jnp.full((ROUTER_ROWS,), NEG, F32)
        br = br.at[0:N_GROUPS].set(b_router_group[l]).at[8:8 + N_EXPERTS].set(b_router_expert[l])
        wrh, wrl = _split(wr)
        brb = jnp.broadcast_to(br[:, None], (ROUTER_ROWS, TM_MERGE))
        x_mid, xs, route, nch3 = _merge(oa, of_c, of_l, ret, gm, xf, mod_l, norm_ffn[l][None, :],
                                        w_branch_attn[l].astype(BF16), w_branch_fourier[l].astype(BF16),
                                        w_branch_ret[l].astype(BF16), w_out[l].astype(BF16), wrh, wrl, brb,
                                        need_ctx)
        final = l == DEPTH - 1
        xf = _moe(xs, route, nch3, x_mid, mod_l, norm_final[None, :], w_exp_gate, w_exp_up, w_exp_down,
                  row0, final, l)
    return xf.reshape(BATCH, SEQ, D_MODEL)
```

```python
import functools

import numpy as np
import jax
import jax.numpy as jnp
from jax import lax
from jax.experimental import pallas as pl
from jax.experimental.pallas import tpu as pltpu

F32 = jnp.float32
BF16 = jnp.bfloat16

D_MODEL = 1024
BATCH = 8
SEQ = 2048
DEPTH = 2
CTX_LEN = 256
GRID_W = 64
HEAD_DIM = 64
ATTN_HEADS = 8
ATTN_KV_HEADS = 2
ATTN_GROUP = ATTN_HEADS // ATTN_KV_HEADS
ATTN_BLOCK = 128
ATTN_QB = 4
RET_HEADS = 8
RET_DK = 64
RET_CHUNK = 128
FOURIER_GROUPS = 4
FOURIER_DIM = 128
N_GROUPS = 4
EXPERTS_PER_GROUP = 8
N_EXPERTS = N_GROUPS * EXPERTS_PER_GROUP
EXPERT_HIDDEN = 512
ROPE_BASE = 10000.0
NORM_EPS = 1e-6
GN_EPS = 1e-5

W = 512
IN_COLS = 6400
RC = BATCH * CTX_LEN
RL = BATCH * SEQ
R = RC + RL
MOD_ROWS = 16
CTX_MOD_ROW = 8

VMEM_LIMIT = 52 * 1024 * 1024

TM_PROJ = 512
TM_MERGE = 512
TN_ADA = 1536
TR_FOURIER = 512
RET_SLAB = 2 * CTX_LEN
MOE_BM = 512
CHUNK = 8
CHUNK_SHIFT = 3
CPB = MOE_BM // CHUNK
GATHER_DEPTH = 3
SLOTS = 1280
XS_COLS = D_MODEL // 2 + 128
NEG = -1e30
LOG2E = 1.4426950408889634
ROUTER_ROWS = 40


def _dot(a, b):
    return jnp.dot(a, b, preferred_element_type=F32)


def _dot_nt(a, b):
    return lax.dot_general(a, b, (((1,), (1,)), ((), ())), preferred_element_type=F32)


def _dot_tn(a, b):
    return lax.dot_general(a, b, (((0,), (0,)), ((), ())), preferred_element_type=F32)


def _split(x):
    hi = x.astype(BF16)
    lo = (x - hi.astype(F32)).astype(BF16)
    return hi, lo


def _sigmoid(x):
    return 1.0 / (1.0 + jnp.exp(-x))


def _params(sem, vmem=VMEM_LIMIT):
    return pltpu.CompilerParams(dimension_semantics=sem, vmem_limit_bytes=vmem)


def _mod_row(row0, tm):
    def f(i):
        g0 = i * tm + row0
        return jnp.where(g0 < RC, CTX_MOD_ROW, (g0 - RC) // SEQ)
    return f


def _rope_tables():
    pos = np.arange(SEQ, dtype=np.float64)
    row = np.floor(pos / GRID_W)
    col = pos % GRID_W

    def cs(p, nf):
        inv = ROPE_BASE ** (-np.arange(nf, dtype=np.float64) / nf)
        ang = p[:, None] * inv[None, :]
        return np.cos(ang), np.sin(ang)

    rc, rs = cs(row, HEAD_DIM // 4)
    cc, cs_ = cs(col, HEAD_DIM // 4)
    cos_a = np.concatenate([rc, rc, cc, cc], axis=1)
    sin_a = np.concatenate([-rs, rs, -cs_, cs_], axis=1)
    tc, ts = cs(pos, RET_DK // 2)
    cos_r = np.concatenate([tc, tc], axis=1)
    sin_r = np.concatenate([-ts, ts], axis=1)

    def full(t, ident):
        t2 = np.concatenate([t, t], axis=1)
        return np.concatenate([np.full_like(t2, ident), t2], axis=0).astype(np.float32)

    return full(cos_a, 1.0), full(sin_a, 0.0), full(cos_r, 1.0), full(sin_r, 0.0)


def _dft_tables():
    def cs(n):
        k = np.arange(n, dtype=np.int64)
        m = (k[:, None] * k[None, :]) % n
        ang = 2.0 * np.pi * m.astype(np.float64) / n
        return np.cos(ang), np.sin(ang)

    c128, s128 = cs(FOURIER_DIM)
    eye = np.eye(FOURIER_GROUPS)
    bdc = np.kron(eye, c128).astype(np.float32)
    bds = np.kron(eye, s128).astype(np.float32)
    cn, sn = cs(SEQ)
    hn = SEQ // 2
    w2 = np.concatenate([cn[:, :hn], -sn[:, :hn]], axis=1).astype(np.float32)
    cl, sl = cs(CTX_LEN)
    w2c = np.concatenate([cl, -sl], axis=1).astype(np.float32)
    return bdc, bds, w2, w2c


def _retention_tables(dec_f, dec_b):
    lg_f = jax.nn.log_sigmoid(dec_f.astype(F32))
    lg_b = jax.nn.log_sigmoid(dec_b.astype(F32))
    i = jnp.arange(RET_CHUNK)
    diff = (i[:, None] - i[None, :]).astype(F32)
    fwd = jnp.exp(jnp.maximum(diff, 0.0)[None] * lg_f[:, None, None])
    bwd = jnp.exp(jnp.maximum(-diff, 0.0)[None] * lg_b[:, None, None])
    dcomb = jnp.where((diff >= 0)[None], fwd, bwd).reshape(2, 4 * RET_CHUNK, RET_CHUNK)
    fi = i.astype(F32)
    lanes = lambda t: jnp.repeat(t, RET_DK, axis=1)
    xi_f = lanes(jnp.exp((fi + 1.0)[:, None] * lg_f[None, :]))
    zt_f = lanes(jnp.exp((RET_CHUNK - 1 - fi)[:, None] * lg_f[None, :]))
    xi_b = lanes(jnp.exp((RET_CHUNK - fi)[:, None] * lg_b[None, :]))
    zt_b = lanes(jnp.exp(fi[:, None] * lg_b[None, :]))
    g_f = jnp.repeat(jnp.exp(RET_CHUNK * lg_f), RET_DK).reshape(2, 256, 1)
    g_b = jnp.repeat(jnp.exp(RET_CHUNK * lg_b), RET_DK).reshape(2, 256, 1)
    g_f = jnp.broadcast_to(g_f, (2, 256, 256))
    g_b = jnp.broadcast_to(g_b, (2, 256, 256))
    return dcomb, xi_f, zt_f, xi_b, zt_b, g_f, g_b


def _ada_body(c_ref, w_ref, b_ref, o_ref):
    c = c_ref[...]
    s = c * _sigmoid(c)
    sh, sl = _split(s)
    wh, wl = _split(w_ref[...])
    o_ref[...] = _dot(sh, wh) + _dot(sl, wh) + _dot(sh, wl) + b_ref[...]


def _ada(cc, w_ada, b_ada):
    nt = 6 * D_MODEL // TN_ADA
    return pl.pallas_call(
        _ada_body,
        grid=(DEPTH, nt),
        in_specs=[
            pl.BlockSpec((MOD_ROWS, D_MODEL), lambda l, j: (0, 0)),
            pl.BlockSpec((None, D_MODEL, TN_ADA), lambda l, j: (l, 0, j)),
            pl.BlockSpec((None, 1, TN_ADA), lambda l, j: (l, 0, j)),
        ],
        out_specs=pl.BlockSpec((None, MOD_ROWS, TN_ADA), lambda l, j: (l, 0, j)),
        out_shape=jax.ShapeDtypeStruct((DEPTH, MOD_ROWS, 6 * D_MODEL), F32),
        compiler_params=_params(("arbitrary", "arbitrary")),
        name="ada_mod",
    )(cc, w_ada, b_ada.reshape(DEPTH, 1, 6 * D_MODEL))


def _rope(xc, cos, sin, half):
    fwd = pltpu.roll(xc, 128 - half, axis=1)
    bwd = pltpu.roll(xc, half, axis=1)
    lane = lax.broadcasted_iota(jnp.int32, xc.shape, 1)
    first = (lane & (2 * half - 1)) < half
    return xc * cos + jnp.where(first, fwd, bwd) * sin


def _proj_body(*refs, split):
    if split:
        x = _pick(pl.program_id(0) < RC // TM_PROJ, refs[0], refs[1])
        refs = refs[2:]
    else:
        x = refs[0][...]
        refs = refs[1:]
    (mod_ref, gn_ref, w_ref, ca_ref, sa_ref, cr_ref, sr_ref,
     qa_ref, ka_ref, va_ref, qr_ref, kr_ref, vr_ref, gr_ref, fu_ref, gm_ref) = refs
    ms = jnp.mean(x * x, axis=-1, keepdims=True)
    y = x * lax.rsqrt(ms + NORM_EPS) * gn_ref[...]
    h = y * (1.0 + mod_ref[1:2, :]) + mod_ref[0:1, :]
    hb = h.astype(BF16)

    def proj(c0, width):
        return _dot(hb, w_ref[:, c0:c0 + width])

    ca, sa, cr, sr = ca_ref[...], sa_ref[...], cr_ref[...], sr_ref[...]

    for c in range(3):
        gm_ref[:, c * D_MODEL:(c + 1) * D_MODEL] = _sigmoid(proj(3328 + c * D_MODEL, D_MODEL)).astype(BF16)
    g = proj(2304, W)
    gr_ref[...] = (g * _sigmoid(g)).astype(BF16)
    qa = proj(0, W) * (HEAD_DIM ** -0.5 * LOG2E)
    for c in range(W // 128):
        qa_ref[:, c * 128:(c + 1) * 128] = _rope(qa[:, c * 128:(c + 1) * 128], ca, sa, 16).astype(BF16)
    kv = proj(W, 256)
    ka = _rope(kv[:, 0:128], ca, sa, 16).astype(BF16)
    ka_ref[0] = ka[:, 0:64]
    ka_ref[1] = ka[:, 64:128]
    va = kv[:, 128:256].astype(BF16)
    ones_col = jnp.where(lax.broadcasted_iota(jnp.int32, (va.shape[0], 64), 1) == 0, 1.0, 0.0).astype(BF16)
    va_ref[0] = jnp.concatenate([va[:, 0:64], ones_col], axis=1)
    va_ref[1] = jnp.concatenate([va[:, 64:128], ones_col], axis=1)
    qr = proj(768, W)
    kr = proj(1280, W) * (RET_DK ** -0.5)
    for c in range(W // 128):
        sl = slice(c * 128, (c + 1) * 128)
        qr_ref[:, sl] = _rope(qr[:, sl], cr, sr, 32).astype(BF16)
        kr_ref[:, sl] = _rope(kr[:, sl], cr, sr, 32).astype(BF16)
    vr_ref[...] = proj(1792, W).astype(BF16)
    fu_ref[...] = proj(2816, W).astype(BF16)


def _proj(x, mod_l, gnorm, w_in_bf, tabs, layer):
    tm = TM_PROJ
    nt = R // tm
    nc = RC // tm
    split = isinstance(x, tuple)
    if split:
        xs = list(x)
        x_specs = [pl.BlockSpec((tm, D_MODEL), lambda i: (jnp.minimum(i, nc - 1), 0)),
                   pl.BlockSpec((tm, D_MODEL), lambda i: (jnp.maximum(i - nc, 0), 0))]
    else:
        xs = [x]
        x_specs = [pl.BlockSpec((tm, D_MODEL), lambda i: (i, 0))]

    def tab_map(i):
        return (jnp.where(i < nc, i, nc + (i - nc) % (SEQ // tm)), 0)

    row = lambda i: (i, 0)
    wide = lambda n: pl.BlockSpec((tm, n), row)
    kv_spec = lambda n: pl.BlockSpec((2, tm, n), lambda i: (0, i, 0))
    sds = lambda n: jax.ShapeDtypeStruct((R, n), BF16)
    kv_sds = lambda n: jax.ShapeDtypeStruct((2, R, n), BF16)
    mrow = _mod_row(0, tm)
    return pl.pallas_call(
        functools.partial(_proj_body, split=split),
        grid=(nt,),
        in_specs=x_specs + [
            pl.BlockSpec((None, 6, D_MODEL), lambda i: (mrow(i), 0, 0)),
            pl.BlockSpec((1, D_MODEL), lambda i: (0, 0)),
            pl.BlockSpec((None, D_MODEL, IN_COLS), lambda i: (layer, 0, 0), pipeline_mode=pl.Buffered(1)),
        ] + [pl.BlockSpec((tm, 128), tab_map)] * 4,
        out_specs=[wide(W), kv_spec(64), kv_spec(128), wide(W), wide(W), wide(W), wide(W), wide(W),
                   wide(3 * D_MODEL)],
        out_shape=[sds(W), kv_sds(64), kv_sds(128), sds(W), sds(W), sds(W), sds(W), sds(W), sds(3 * D_MODEL)],
        compiler_params=_params(("arbitrary",)),
        name="in_proj",
    )(*xs, mod_l, gnorm, w_in_bf, *tabs)


def _attend(sink_ref, q_ref, o_ref, sub, pieces):
    rows = slice(sub * ATTN_BLOCK, (sub + 1) * ATTN_BLOCK)
    groups = [slice(g * ATTN_BLOCK, (g + 1) * ATTN_BLOCK) for g in range(ATTN_GROUP)]
    outs = []
    for h in range(ATTN_KV_HEADS):
        q = q_ref[rows, h * 256:(h + 1) * 256]
        q4 = jnp.concatenate([q[:, g * 64:(g + 1) * 64] for g in range(ATTN_GROUP)], axis=0)
        sinks = [sink_ref[h * ATTN_GROUP + g] * LOG2E for g in range(ATTN_GROUP)]
        k_all = jnp.concatenate([k_ref[h] for k_ref, _, _ in pieces], axis=0)
        v_all = jnp.concatenate([v_ref[h] for _, v_ref, _ in pieces], axis=0)
        s = _dot_nt(q4, k_all)
        cols, c0 = [], 0
        for k_ref, _, ok in pieces:
            n = k_ref.shape[1]
            cols.append(s[:, c0:c0 + n] if ok is None else jnp.where(ok, s[:, c0:c0 + n], NEG))
            c0 += n
        s = jnp.concatenate(cols, axis=1)
        mxs, ps = [], []
        for g, r in enumerate(groups):
            mx = jnp.maximum(jnp.max(s[r], axis=-1, keepdims=True), sinks[g])
            mxs.append(mx)
            ps.append(jnp.exp2(s[r] - mx).astype(BF16))
        oa = _dot(jnp.concatenate(ps, axis=0), v_all)
        for g, r in enumerate(groups):
            den = oa[r, HEAD_DIM:HEAD_DIM + 1] + jnp.exp2(sinks[g] - mxs[g])
            outs.append(oa[r, 0:HEAD_DIM] / den)
    o_ref[rows, :] = jnp.concatenate(outs, axis=1).astype(BF16)


def _attn_lat_body(sink_ref, q_ref, kc_ref, *refs):
    nk = ATTN_QB + 2
    k_refs, vc_ref, v_refs, o_ref = refs[:nk], refs[nk], refs[nk + 1:2 * nk + 1], refs[2 * nk + 1]
    p = pl.program_id(1)
    nr = ATTN_GROUP * ATTN_BLOCK
    ri = lax.broadcasted_iota(jnp.int32, (nr, ATTN_BLOCK), 0) & (ATTN_BLOCK - 1)
    ci = lax.broadcasted_iota(jnp.int32, (nr, ATTN_BLOCK), 1)
    far = 4 * ATTN_BLOCK
    first_prev = ci >= ri + jnp.where(p >= 1, 0, far)
    last_next = ci + jnp.where(p <= SEQ // (ATTN_QB * ATTN_BLOCK) - 2, 0, far) <= ri
    for sub in range(ATTN_QB):
        prev_ok = first_prev if sub == 0 else ci >= ri
        next_ok = last_next if sub == ATTN_QB - 1 else ci <= ri
        _attend(sink_ref, q_ref, o_ref, sub,
                [(kc_ref, vc_ref, None), (k_refs[sub], v_refs[sub], prev_ok), (k_refs[sub + 1], v_refs[sub + 1], None),
                 (k_refs[sub + 2], v_refs[sub + 2], next_ok)])


def _attn_ctx_body(sink_ref, q_ref, kc_ref, vc_ref, o_ref):
    for sub in range(CTX_LEN // ATTN_BLOCK):
        _attend(sink_ref, q_ref, o_ref, sub, [(kc_ref, vc_ref, None)])


def _attention(sink, qa, ka, va, need_ctx):
    nb = SEQ // ATTN_BLOCK
    nstep = nb // ATTN_QB
    tq = ATTN_QB * ATTN_BLOCK
    smem = pl.BlockSpec(memory_space=pltpu.SMEM)

    def loc(delta):
        def f(b, p):
            m = jnp.clip(ATTN_QB * p + delta, 0, nb - 1)
            return (0, RC // ATTN_BLOCK + b * nb + m, 0)
        return f

    def kv_specs(width):
        ctx_spec = pl.BlockSpec((ATTN_KV_HEADS, CTX_LEN, width), lambda b, p: (0, b, 0))
        return [ctx_spec] + [pl.BlockSpec((ATTN_KV_HEADS, ATTN_BLOCK, width), loc(d)) for d in range(-1, ATTN_QB + 1)]

    nloc = ATTN_QB + 3
    oa_l = pl.pallas_call(
        _attn_lat_body,
        grid=(BATCH, nstep),
        in_specs=[smem, pl.BlockSpec((tq, W), lambda b, p: (RC // tq + b * nstep + p, 0))]
                 + kv_specs(HEAD_DIM) + kv_specs(128),
        out_specs=pl.BlockSpec((tq, W), lambda b, p: (b * nstep + p, 0)),
        out_shape=jax.ShapeDtypeStruct((RL, W), BF16),
        compiler_params=_params(("arbitrary", "arbitrary")),
        name="window_attn",
    )(sink, qa, *([ka] * nloc), *([va] * nloc))
    if not need_ctx:
        return None, oa_l
    oa_c = pl.pallas_call(
        _attn_ctx_body,
        grid=(BATCH,),
        in_specs=[smem, pl.BlockSpec((CTX_LEN, W), lambda b: (b, 0)),
                  pl.BlockSpec((ATTN_KV_HEADS, CTX_LEN, HEAD_DIM), lambda b: (0, b, 0)),
                  pl.BlockSpec((ATTN_KV_HEADS, CTX_LEN, 128), lambda b: (0, b, 0))],
        out_specs=pl.BlockSpec((CTX_LEN, W), lambda b: (b, 0)),
        out_shape=jax.ShapeDtypeStruct((RC, W), BF16),
        compiler_params=_params(("arbitrary",)),
        name="ctx_attn",
    )(sink, qa, ka, va)
    return oa_c, oa_l


def _fourier_body(*refs, has_ctx):
    if has_ctx:
        uc_ref, ul_ref, jrev_ref, bdc_ref, bds_ref, w2_ref, w2c_ref, oc_ref, ol_ref, as_ref, mid_ref = refs
    else:
        ul_ref, jrev_ref, bdc_ref, bds_ref, w2_ref, ol_ref, as_ref, mid_ref = refs
    j = pl.program_id(1)
    first = 1 if has_ctx else 0
    hn = SEQ // 2

    if has_ctx:
        @pl.when(j == 0)
        def _():
            u = uc_ref[...]
            a = _dot(u, bdc_ref[...]).astype(BF16)
            s = _dot(u, bds_ref[...]).astype(BF16)
            z = _dot(w2c_ref[...], jnp.concatenate([a, s], axis=0))
            oc_ref[...] = (z * ((CTX_LEN * FOURIER_DIM) ** -0.5)).astype(BF16)

    @pl.when(j == first)
    def _():
        uh = ul_ref[0:hn, :].astype(F32)
        ur = _dot(jrev_ref[...], ul_ref[hn:SEQ, :])
        row = lax.broadcasted_iota(jnp.int32, (hn, W), 0)
        vm = jnp.where(row == 0, 0.0, uh - ur)
        as_ref[0:hn, :] = _dot((uh + ur).astype(BF16), bdc_ref[...]).astype(BF16)
        as_ref[hn:SEQ, :] = _dot(vm.astype(BF16), bds_ref[...]).astype(BF16)
        mid_ref[...] = _dot(ul_ref[hn:hn + 8, :], bdc_ref[...])

    @pl.when(j >= first)
    def _():
        z = _dot(w2_ref[...], as_ref[...])
        k = lax.broadcasted_iota(jnp.int32, (z.shape[0], 1), 0)
        sign = (1 - 2 * (k & 1)).astype(F32)
        ol_ref[...] = ((z + sign * mid_ref[0:1, :]) * ((SEQ * FOURIER_DIM) ** -0.5)).astype(BF16)


def _fourier(fu, dft, need_ctx):
    bdc, bds, w2, w2c = dft
    tr = TR_FOURIER
    nj = SEQ // tr
    hn = SEQ // 2
    first = 1 if need_ctx else 0
    jrev_np = np.zeros((hn, hn), np.float32)
    jrev_np[np.arange(1, hn), hn - np.arange(1, hn)] = 1.0
    jrev = jnp.asarray(jrev_np).astype(BF16)
    full = lambda a: pl.BlockSpec(a.shape, lambda *_: (0,) * a.ndim)
    lat_tile = lambda j: jnp.maximum(j - first, 0)
    ul_spec = pl.BlockSpec((SEQ, W), lambda b, j: (1 + b, 0))
    w2_spec = pl.BlockSpec((tr, SEQ), lambda b, j: (lat_tile(j), 0))
    ol_spec = pl.BlockSpec((tr, W), lambda b, j: (b * nj + lat_tile(j), 0))
    ol_shape = jax.ShapeDtypeStruct((RL, W), BF16)
    if need_ctx:
        in_specs = [pl.BlockSpec((CTX_LEN, W), lambda b, j: (b, 0)), ul_spec, full(jrev), full(bdc), full(bds),
                    w2_spec, full(w2c)]
        out_specs = [pl.BlockSpec((CTX_LEN, W), lambda b, j: (b, 0)), ol_spec]
        out_shape = [jax.ShapeDtypeStruct((RC, W), BF16), ol_shape]
        args = (fu, fu, jrev, bdc, bds, w2, w2c)
    else:
        in_specs = [ul_spec, full(jrev), full(bdc), full(bds), w2_spec]
        out_specs = [ol_spec]
        out_shape = [ol_shape]
        args = (fu, jrev, bdc, bds, w2)
    outs = pl.pallas_call(
        functools.partial(_fourier_body, has_ctx=need_ctx),
        grid=(BATCH, nj + first),
        in_specs=in_specs,
        out_specs=out_specs,
        out_shape=out_shape,
        scratch_shapes=[pltpu.VMEM((SEQ, W), BF16), pltpu.VMEM((8, W), F32)],
        compiler_params=_params(("arbitrary", "arbitrary")),
        name="fourier_mix",
    )(*args)
    return (outs[0], outs[1]) if need_ctx else (None, outs[0])


def _retention_body(qc_ref, kc_ref, vc_ref, ql0_ref, kl0_ref, vl0_ref, ql1_ref, kl1_ref, vl1_ref, g_ref,
                    dcomb_ref, xif_ref, ztf_ref, xib_ref, ztb_ref, gf_ref, gb_ref, mbd_ref, avg_ref,
                    o_ref, os_ref, st_ref):
    j = pl.program_id(1)
    C = RET_CHUNK
    nl = SEQ // C
    lat_refs = ((ql0_ref, kl0_ref, vl0_ref), (ql1_ref, kl1_ref, vl1_ref))

    own = ((lax.broadcasted_iota(jnp.int32, (4 * C, 256), 0) >> 7)
           == (lax.broadcasted_iota(jnp.int32, (4 * C, 256), 1) >> 6))

    def chunk_fwd(bb, q, k, v, r0):
        for gi in range(2):
            sl = slice(gi * 256, (gi + 1) * 256)
            q4, k4, v4 = q[:, sl], k[:, sl], v[:, sl]
            s_prev = st_ref[2 * bb + gi]
            q4f = q4.astype(F32)
            o4 = _dot((q4f * xif_ref[:, sl]).astype(BF16), s_prev.astype(BF16))
            qstack = jnp.where(own, jnp.concatenate([q4f] * 4, axis=0), 0.0).astype(BF16)
            p = (_dot_nt(qstack, k4) * dcomb_ref[gi]).astype(BF16)
            ov = jnp.where(own, _dot(p, v4), 0.0)
            intra = ov[0:C] + ov[C:2 * C] + ov[2 * C:3 * C] + ov[3 * C:4 * C]
            os_ref[pl.ds(r0, C), sl] = o4 + intra
            u = _dot_tn(k4, (v4.astype(F32) * ztf_ref[:, sl]).astype(BF16))
            st_ref[2 * bb + gi] = gf_ref[gi] * s_prev + mbd_ref[...] * u

    def chunk_bwd(bb, q, k, v, r0):
        for gi in range(2):
            sl = slice(gi * 256, (gi + 1) * 256)
            q4, k4, v4 = q[:, sl], k[:, sl], v[:, sl]
            s_prev = st_ref[2 * bb + gi]
            cross = _dot((q4.astype(F32) * xib_ref[:, sl]).astype(BF16), s_prev.astype(BF16))
            os_ref[pl.ds(r0, C), sl] = os_ref[pl.ds(r0, C), sl] + cross
            u = _dot_tn(k4, (v4.astype(F32) * ztb_ref[:, sl]).astype(BF16))
            st_ref[2 * bb + gi] = gb_ref[gi] * s_prev + mbd_ref[...] * u

    def scan(chunk, ctx_order, lat_index):
        st_ref[...] = jnp.zeros_like(st_ref)
        for c in ctx_order:
            for bb in range(2):
                rs = slice(bb * CTX_LEN + c * C, bb * CTX_LEN + (c + 1) * C)
                chunk(bb, qc_ref[rs, :], kc_ref[rs, :], vc_ref[rs, :], bb * CTX_LEN + c * C)

        def body(t, carry):
            r0 = pl.multiple_of(lat_index(t) * C, C)
            for bb, (q_ref, k_ref, v_ref) in enumerate(lat_refs):
                rs = pl.ds(r0, C)
                chunk(bb, q_ref[rs, :], k_ref[rs, :], v_ref[rs, :], 2 * CTX_LEN + bb * SEQ + r0)
            return carry

        lax.fori_loop(0, nl, body, 0)

    @pl.when(j == 0)
    def _():
        scan(chunk_fwd, range(CTX_LEN // C), lambda t: t)
        scan(chunk_bwd, reversed(range(CTX_LEN // C)), lambda t: nl - 1 - t)

    o = os_ref[pl.ds(pl.multiple_of(j * RET_SLAB, RET_SLAB), RET_SLAB), :]
    avg = avg_ref[...]
    oh, ol = _split(o)
    mu = _dot(oh, avg) + _dot(ol, avg)
    d = o - mu
    var = _dot((d * d).astype(BF16), avg)
    o_ref[...] = (g_ref[...].astype(F32) * d * lax.rsqrt(var + GN_EPS)).astype(BF16)


def _retention(qr, kr, vr, gr, rtabs):
    nslab = SEQ // RET_SLAB
    nj = 1 + 2 * nslab

    def out_map(b2, j):
        return (jnp.where(j == 0, b2, RC // RET_SLAB + 2 * b2 * nslab + j - 1), 0)

    ctx = pl.BlockSpec((2 * CTX_LEN, W), lambda b2, j: (b2, 0))
    lat = lambda k: pl.BlockSpec((SEQ, W), lambda b2, j: (1 + 2 * b2 + k, 0))
    full = lambda a: pl.BlockSpec(a.shape, lambda *_: (0,) * a.ndim)
    avg = jnp.asarray(np.kron(np.eye(RET_HEADS), np.full((RET_DK, RET_DK), 1.0 / RET_DK)).astype(np.float32)).astype(BF16)
    mbd = jnp.asarray(np.kron(np.eye(4), np.ones((RET_DK, RET_DK))).astype(np.float32))
    tabs = list(rtabs) + [mbd, avg]
    return pl.pallas_call(
        _retention_body,
        grid=(BATCH // 2, nj),
        in_specs=[ctx, ctx, ctx, lat(0), lat(0), lat(0), lat(1), lat(1), lat(1),
                  pl.BlockSpec((RET_SLAB, W), out_map)] + [full(t) for t in tabs],
        out_specs=pl.BlockSpec((RET_SLAB, W), out_map),
        out_shape=jax.ShapeDtypeStruct((R, W), BF16),
        scratch_shapes=[pltpu.VMEM((2 * (CTX_LEN + SEQ), W), F32), pltpu.VMEM((4, 256, 256), F32)],
        compiler_params=_params(("arbitrary", "arbitrary")),
        name="retention",
    )(qr, kr, vr, qr, kr, vr, qr, kr, vr, gr, *tabs)


def _pack_pair(a, b):
    ua = lax.bitcast_convert_type(a, jnp.uint32) >> 16
    ub = lax.bitcast_convert_type(b, jnp.uint32) & jnp.uint32(0xFFFF0000)
    return ua | ub


def _unpack_pair(w):
    a = lax.bitcast_convert_type(w << 16, F32)
    b = lax.bitcast_convert_type(w & jnp.uint32(0xFFFF0000), F32)
    return a, b


def _slot_onehot(s0, s1, n):
    srow = lax.broadcasted_iota(jnp.int32, (n, s0.shape[1]), 0)
    p0 = jnp.where(srow == s0, 1.0, 0.0).astype(BF16)
    p1 = jnp.where(srow == s1, 1.0, 0.0).astype(BF16)
    return p0, p1


def _pick(first, a_ref, b_ref):
    a = a_ref[...]
    flag = jnp.zeros(a.shape, jnp.int32) + first.astype(jnp.int32)
    return jnp.where(flag > 0, a, b_ref[...])


def _merge_body(*refs, layer0):
    if layer0:
        oac_ref, oal_ref, ofc_ref, ofl_ref, rt_ref, gm_ref, xc_ref, xl_ref = refs[:8]
        rest = refs[8:]
        is_ctx = pl.program_id(0) < RC // TM_MERGE
        oa_in = _pick(is_ctx, oac_ref, oal_ref)
        of_in = _pick(is_ctx, ofc_ref, ofl_ref)
        x_in = _pick(is_ctx, xc_ref, xl_ref)
    else:
        oal_ref, ofl_ref, rt_ref, gm_ref, x_ref = refs[:5]
        rest = refs[5:]
        oa_in = oal_ref[...]
        of_in = ofl_ref[...]
        x_in = x_ref[...]
    (mod_ref, gn_ref, wba_ref, wbf_ref, wbr_ref, wout_ref, wrh_ref, wrl_ref, br_ref, tri_ref, ltri_ref,
     xo_ref, xs_ref, ro_ref, nch_ref) = rest
    gm = gm_ref[...].astype(F32)
    z = (gm[:, 0:D_MODEL] * _dot(oa_in, wba_ref[...])
         + gm[:, D_MODEL:2 * D_MODEL] * _dot(of_in, wbf_ref[...])
         + gm[:, 2 * D_MODEL:3 * D_MODEL] * _dot(rt_ref[...], wbr_ref[...]))
    y = _dot(z.astype(BF16), wout_ref[...])
    x = x_in + mod_ref[2:3, :] * y
    xo_ref[...] = x
    ms = jnp.mean(x * x, axis=-1, keepdims=True)
    hn = x * lax.rsqrt(ms + NORM_EPS) * gn_ref[...]
    h2 = hn * (1.0 + mod_ref[4:5, :]) + mod_ref[3:4, :]
    hh, hl = _split(h2)
    wh, wl = wrh_ref[...], wrl_ref[...]
    lg = _dot_nt(wh, hh) + _dot_nt(wh, hl) + _dot_nt(wl, hh) + br_ref[...]
    tm = lg.shape[1]
    row8 = lax.broadcasted_iota(jnp.int32, (8, tm), 0)
    lgg = lg[0:8, :]
    mg = jnp.max(lgg, axis=0, keepdims=True)
    grp = jnp.min(jnp.where(lgg == mg, row8, 8), axis=0, keepdims=True)
    pg = 1.0 / jnp.sum(jnp.exp(lgg - mg), axis=0, keepdims=True)
    lin = jnp.zeros((8, tm), F32)
    for g in range(N_GROUPS):
        lin = jnp.where(grp == g, lg[8 + 8 * g:16 + 8 * g, :], lin)
    v1 = jnp.max(lin, axis=0, keepdims=True)
    i1 = jnp.min(jnp.where(lin == v1, row8, 8), axis=0, keepdims=True)
    rest = jnp.where(row8 == i1, -jnp.inf, lin)
    v2 = jnp.max(rest, axis=0, keepdims=True)
    i2 = jnp.min(jnp.where(rest == v2, row8, 8), axis=0, keepdims=True)
    e2 = jnp.exp(v2 - v1)
    w1 = pg / (1.0 + e2)
    w2 = pg * e2 / (1.0 + e2)
    e_1 = grp * EXPERTS_PER_GROUP + i1
    e_2 = grp * EXPERTS_PER_GROUP + i2

    row32 = lax.broadcasted_iota(jnp.int32, (N_EXPERTS, tm), 0)
    oh0 = jnp.where(row32 == e_1, 1.0, 0.0)
    oh1 = jnp.where(row32 == e_2, 1.0, 0.0)
    tri = tri_ref[...]
    cum0 = _dot(oh0.astype(BF16), tri)
    cum1 = _dot(oh1.astype(BF16), tri)
    tot0 = jnp.sum(oh0, axis=1, keepdims=True)
    tot1 = jnp.sum(oh1, axis=1, keepdims=True)
    nch = ((tot0 + tot1).astype(jnp.int32) + (CHUNK - 1)) >> CHUNK_SHIFT
    nch_b = jnp.broadcast_to(nch.astype(F32), (N_EXPERTS, 128))
    nch_ref[...] = nch_b.astype(jnp.int32)
    base = CHUNK * _dot(ltri_ref[...], nch_b.astype(BF16))[:, 0:1]
    s0 = jnp.sum(oh0 * (base + cum0), axis=0, keepdims=True).astype(jnp.int32)
    s1 = jnp.sum(oh1 * (base + tot0 + cum1), axis=0, keepdims=True).astype(jnp.int32)
    p0, p1 = _slot_onehot(s0, s1, SLOTS)
    xs = _dot(p0 + p1, hh)
    xs_ref[:, 0:D_MODEL // 2] = _pack_pair(xs[:, 0:D_MODEL // 2], xs[:, D_MODEL // 2:D_MODEL])

    def wrows(w):
        hi, lo = _split(w)
        return jnp.where(row8 == 0, hi.astype(F32), jnp.where(row8 == 1, lo.astype(F32), 0.0)).astype(BF16)

    wc = _dot_nt(p0, wrows(w1)) + _dot_nt(p1, wrows(w2))
    wcol = jnp.broadcast_to(wc[:, 0:1] + wc[:, 1:2], (SLOTS, 128))
    xs_ref[:, D_MODEL // 2:XS_COLS] = lax.bitcast_convert_type(wcol, jnp.uint32)
    s0f, s1f = s0.astype(F32), s1.astype(F32)
    ro_ref[...] = jnp.where(row8 == 0, s0f, jnp.where(row8 == 1, s1f, 0.0))


def _merge(oa_c, oa_l, of_c, of_l, ret, gm, xs_in, mod_l, gnorm, wba, wbf, wbr, wout, wrh, wrl, brb, layer0):
    tm = TM_MERGE
    row0 = 0 if layer0 else RC
    rm = R - row0
    nt = rm // tm
    off = row0 // tm
    nc = RC // tm
    src = lambda n: pl.BlockSpec((tm, n), lambda i: (i + off, 0))
    dst = lambda n: pl.BlockSpec((tm, n), lambda i: (i, 0))
    ctx_rows = lambda n: pl.BlockSpec((tm, n), lambda i: (jnp.minimum(i, nc - 1), 0))
    lat_rows = lambda n: pl.BlockSpec((tm, n), lambda i: (jnp.maximum(i - nc, 0), 0))
    full = lambda a: pl.BlockSpec(a.shape, lambda *_: (0,) * a.ndim, pipeline_mode=pl.Buffered(1))
    mrow = _mod_row(row0, tm)
    tri = jnp.asarray(np.triu(np.ones((tm, tm), np.float32), 1)).astype(BF16)
    ltri = jnp.asarray(np.tril(np.ones((N_EXPERTS, N_EXPERTS), np.float32), -1)).astype(BF16)
    if layer0:
        acts = [oa_c, oa_l, of_c, of_l, ret, gm, xs_in[0], xs_in[1]]
        act_specs = [ctx_rows(W), lat_rows(W), ctx_rows(W), lat_rows(W), src(W), src(3 * D_MODEL),
                     ctx_rows(D_MODEL), lat_rows(D_MODEL)]
    else:
        acts = [oa_l, of_l, ret, gm, xs_in]
        act_specs = [dst(W), dst(W), src(W), src(3 * D_MODEL), src(D_MODEL)]
    return pl.pallas_call(
        functools.partial(_merge_body, layer0=layer0),
        grid=(nt,),
        in_specs=act_specs + [
                  pl.BlockSpec((None, 6, D_MODEL), lambda i: (mrow(i), 0, 0)),
                  full(gnorm), full(wba), full(wbf), full(wbr), full(wout), full(wrh), full(wrl), full(brb),
                  full(tri), full(ltri)],
        out_specs=[dst(D_MODEL), pl.BlockSpec((SLOTS, XS_COLS), lambda i: (i, 0)),
                   pl.BlockSpec((8, tm), lambda i: (0, i)),
                   pl.BlockSpec((None, N_EXPERTS, 128), lambda i: (i, 0, 0))],
        out_shape=[jax.ShapeDtypeStruct((rm, D_MODEL), F32),
                   jax.ShapeDtypeStruct((nt * SLOTS, XS_COLS), jnp.uint32),
                   jax.ShapeDtypeStruct((8, rm), F32),
                   jax.ShapeDtypeStruct((nt, N_EXPERTS, 128), jnp.int32)],
        compiler_params=_params(("arbitrary",)),
        name="merge_router",
    )(*acts, mod_l, gnorm, wba, wbf, wbr, wout, wrh, wrl, brb, tri, ltri)


def _moe_plan(nch, nb):
    nt = nch.shape[0]
    choff = jnp.cumsum(nch, axis=1) - nch
    used_ch = jnp.sum(nch, axis=1)
    cum_t = jnp.cumsum(nch, axis=0)
    tot = cum_t[-1]
    ptot = (tot + CPB - 1) // CPB * CPB
    pend = jnp.cumsum(ptot)
    pstart = pend - ptot
    n_used = pend[-1] // CPB
    blk = jnp.arange(nb, dtype=jnp.int32)
    lane = jnp.arange(CPB, dtype=jnp.int32)
    blk_e = jnp.minimum(jnp.sum((blk[:, None] * CPB >= pend[None, :]).astype(jnp.int32), axis=1), N_EXPERTS - 1)
    oe = (blk_e[:, None] == jnp.arange(N_EXPERTS, dtype=jnp.int32)[None, :]).astype(jnp.int32)
    sel = lambda tab: jnp.sum(oe[:, :, None] * tab.T[None, :, :], axis=1)
    pstart_b = jnp.sum(oe * pstart[None, :], axis=1)
    tot_b = jnp.sum(oe * tot[None, :], axis=1)
    cum_b, nch_b, choff_b = sel(cum_t), sel(nch), sel(choff)
    i = blk[:, None] * CPB + lane[None, :] - pstart_b[:, None]
    valid = (i < tot_b[:, None]) & (blk[:, None] < n_used)
    t = jnp.minimum(jnp.sum((i[:, :, None] >= cum_b[:, None, :]).astype(jnp.int32), axis=2), nt - 1)
    tiles = jnp.arange(nt, dtype=jnp.int32)[None, None, :]
    before = jnp.sum(jnp.where(tiles < t[:, :, None], nch_b[:, None, :], 0), axis=2)
    coff = jnp.sum(jnp.where(tiles == t[:, :, None], choff_b[:, None, :], 0), axis=2)
    row = t * SLOTS + CHUNK * (coff + i - before)
    src = jnp.where(valid, row, SLOTS - CHUNK)
    dummy = nt * SLOTS + CHUNK * ((blk[:, None] % 2) * CPB + lane[None, :])
    dst = jnp.where(valid, row, dummy)
    blk_start = jnp.concatenate([pstart, pend[-1:]]) // CPB
    return (blk_start.astype(jnp.int32), n_used.astype(jnp.int32).reshape(1), src.reshape(-1).astype(jnp.int32),
            dst.reshape(-1).astype(jnp.int32), used_ch.astype(jnp.int32))


def _ffn_body(bs_ref, nu_ref, src_ref, dst_ref, uc_ref, xs_ref, wg_ref, wu_ref, wd_ref, ys_ref,
              xbuf, ybuf, zbuf, wgb, wub, wdb, sem_in, sem_out, sem_z, *, nt):
    e = pl.program_id(0)
    nu = nu_ref[0]
    half = D_MODEL // 2

    def gather(blk, sl):
        for c in range(CPB):
            r = pl.multiple_of(src_ref[blk * CPB + c], CHUNK)
            pltpu.make_async_copy(xs_ref.at[pl.ds(r, CHUNK)], xbuf.at[sl, pl.ds(c * CHUNK, CHUNK)],
                                  sem_in.at[sl]).start(priority=1)

    def scatter(blk, sl):
        for c in range(CPB):
            r = pl.multiple_of(dst_ref[blk * CPB + c], CHUNK)
            pltpu.make_async_copy(ybuf.at[sl, pl.ds(c * CHUNK, CHUNK)], ys_ref.at[pl.ds(r, CHUNK)],
                                  sem_out.at[sl]).start(priority=1)

    def wait_gather(sl):
        pltpu.make_async_copy(xs_ref.at[pl.ds(0, MOE_BM)], xbuf.at[sl], sem_in.at[sl]).wait()

    def wait_scatter(sl):
        pltpu.make_async_copy(ybuf.at[sl], ys_ref.at[pl.ds(0, MOE_BM)], sem_out.at[sl]).wait()

    def zero_copy(r):
        return pltpu.make_async_copy(zbuf, ys_ref.at[pl.ds(pl.multiple_of(r, CHUNK), CHUNK)], sem_z)

    @pl.when(e == 0)
    def _():
        zbuf[...] = jnp.zeros_like(zbuf)

        def tails(fn):
            def per_tile(t, carry):
                def per_chunk(c, carry2):
                    fn(t * SLOTS + c * CHUNK)
                    return carry2
                lax.fori_loop(uc_ref[t], SLOTS // CHUNK, per_chunk, 0)
                return carry
            lax.fori_loop(0, nt, per_tile, 0)
            for c in range(2 * CPB):
                fn(nt * SLOTS + c * CHUNK)

        tails(lambda r: zero_copy(r).start())
        tails(lambda r: zero_copy(r).wait())
        gather(0, 0)
        for d in range(1, GATHER_DEPTH - 1):
            @pl.when(nu > d)
            def _():
                gather(d, d)

    b0, b1 = bs_ref[e], bs_ref[e + 1]

    @pl.when(b1 > b0)
    def _():
        wgb[...] = wg_ref[...].astype(BF16)
        wub[...] = wu_ref[...].astype(BF16)
        wdb[...] = wd_ref[...].astype(BF16)

        def block(b, carry):
            slot = b % 2
            xslot = lax.rem(b, GATHER_DEPTH)

            @pl.when(b + GATHER_DEPTH - 1 < nu)
            def _():
                gather(b + GATHER_DEPTH - 1, lax.rem(b + GATHER_DEPTH - 1, GATHER_DEPTH))

            wait_gather(xslot)

            @pl.when(b >= 2)
            def _():
                wait_scatter(slot)

            xw = xbuf[xslot]
            xa, xb = _unpack_pair(xw[:, 0:half])
            x = jnp.concatenate([xa, xb], axis=1).astype(BF16)
            wt = lax.bitcast_convert_type(xw[:, half:XS_COLS], F32)
            g = _dot(x, wgb[...])
            u = _dot(x, wub[...])
            hmid = (g * _sigmoid(g) * u).astype(BF16)
            y = _dot(hmid, wdb[...]) * jnp.concatenate([wt] * (D_MODEL // 128), axis=1)
            yb = y.astype(BF16).astype(F32)
            ybuf[slot] = _pack_pair(yb[:, 0:half], yb[:, half:D_MODEL])
            scatter(b, slot)
            return carry

        lax.fori_loop(b0, b1, block, 0)

    @pl.when(e == N_EXPERTS - 1)
    def _():
        wait_scatter((nu - 1) % 2)

        @pl.when(nu >= 2)
        def _():
            wait_scatter(nu % 2)


def _ffn(plan, xs, w_g, w_u, w_d, layer, nt):
    wmap = lambda e, *_: (layer, e, 0, 0)
    half = D_MODEL // 2
    grid_spec = pltpu.PrefetchScalarGridSpec(
        num_scalar_prefetch=5,
        grid=(N_EXPERTS,),
        in_specs=[pl.BlockSpec(memory_space=pl.ANY),
                  pl.BlockSpec((None, None, D_MODEL, EXPERT_HIDDEN), wmap),
                  pl.BlockSpec((None, None, D_MODEL, EXPERT_HIDDEN), wmap),
                  pl.BlockSpec((None, None, EXPERT_HIDDEN, D_MODEL), wmap)],
        out_specs=pl.BlockSpec(memory_space=pl.ANY),
        scratch_shapes=[pltpu.VMEM((GATHER_DEPTH, MOE_BM, XS_COLS), jnp.uint32),
                        pltpu.VMEM((2, MOE_BM, half), jnp.uint32),
                        pltpu.VMEM((CHUNK, half), jnp.uint32),
                        pltpu.VMEM((D_MODEL, EXPERT_HIDDEN), BF16), pltpu.VMEM((D_MODEL, EXPERT_HIDDEN), BF16),
                        pltpu.VMEM((EXPERT_HIDDEN, D_MODEL), BF16),
                        pltpu.SemaphoreType.DMA((GATHER_DEPTH,)), pltpu.SemaphoreType.DMA((2,)),
                        pltpu.SemaphoreType.DMA(())],
    )
    return pl.pallas_call(
        functools.partial(_ffn_body, nt=nt),
        grid_spec=grid_spec,
        out_shape=jax.ShapeDtypeStruct((nt * SLOTS + 2 * CPB * CHUNK, half), jnp.uint32),
        compiler_params=_params(("arbitrary",)),
        name="moe_experts",
    )(*plan, xs, w_g, w_u, w_d)


def _combine_body(ys_ref, ro_ref, x_ref, mod_ref, gn_ref, o_ref, *, final):
    s = ro_ref[...]
    p0, p1 = _slot_onehot(s[0:1, :].astype(jnp.int32), s[1:2, :].astype(jnp.int32), SLOTS)
    ya, yb = _unpack_pair(ys_ref[...])
    y = jnp.concatenate([ya, yb], axis=1).astype(BF16)
    f = _dot_tn(p0 + p1, y)
    x = x_ref[...] + mod_ref[5:6, :] * f
    if final:
        ms = jnp.mean(x * x, axis=-1, keepdims=True)
        x = x * lax.rsqrt(ms + NORM_EPS) * gn_ref[...]
    o_ref[...] = x


def _combine(ys, route, x, mod_l, gnorm, row0, final):
    tm = TM_MERGE
    rm = x.shape[0]
    mrow = _mod_row(row0, tm)
    return pl.pallas_call(
        functools.partial(_combine_body, final=final),
        grid=(rm // tm,),
        in_specs=[pl.BlockSpec((SLOTS, D_MODEL // 2), lambda i: (i, 0)),
                  pl.BlockSpec((8, tm), lambda i: (0, i)),
                  pl.BlockSpec((tm, D_MODEL), lambda i: (i, 0)),
                  pl.BlockSpec((None, 6, D_MODEL), lambda i: (mrow(i), 0, 0)),
                  pl.BlockSpec((1, D_MODEL), lambda i: (0, 0))],
        out_specs=pl.BlockSpec((tm, D_MODEL), lambda i: (i, 0)),
        out_shape=jax.ShapeDtypeStruct((rm, D_MODEL), F32),
        compiler_params=_params(("arbitrary",)),
        name="moe_combine",
    )(ys, route, x, mod_l, gnorm)


def _moe(xs, route, nch3, x, mod_l, gnorm, w_g, w_u, w_d, row0, final, layer):
    nt = nch3.shape[0]
    max_chunks = nt * ((2 * TM_MERGE + N_EXPERTS * (CHUNK - 1)) // CHUNK)
    nb = -(-max_chunks // CPB) + N_EXPERTS
    plan = _moe_plan(nch3[:, :, 0], nb)
    ys = _ffn(plan, xs, w_g, w_u, w_d, layer, nt)
    return _combine(ys, route, x, mod_l, gnorm, row0, final)


def kernel(x, c, ctx, c_ctx, norm_mix, norm_ffn, w_ada, b_ada, w_in, attn_sink, ret_decay_fwd, ret_decay_bwd,
           w_branch_attn, w_branch_fourier, w_branch_ret, w_out, w_router_group, b_router_group,
           w_router_expert, b_router_expert, w_exp_gate, w_exp_up, w_exp_down, norm_final):
    tabs = [jnp.asarray(t) for t in _rope_tables()]
    dft = [jnp.asarray(t).astype(BF16) for t in _dft_tables()]

    cc = jnp.zeros((MOD_ROWS, D_MODEL), F32).at[0:BATCH].set(c).at[CTX_MOD_ROW].set(c_ctx)
    mod = _ada(cc, w_ada, b_ada).reshape(DEPTH, MOD_ROWS, 6, D_MODEL)

    xf = (ctx.reshape(RC, D_MODEL), x.reshape(RL, D_MODEL))
    w_in_bf = w_in.astype(BF16)
    for l in range(DEPTH):
        need_ctx = l < DEPTH - 1
        row0 = 0 if need_ctx else RC
        mod_l = mod[l]
        qa, ka, va, qr, kr, vr, gr, fu, gm = _proj(xf, mod_l, norm_mix[l][None, :], w_in_bf, tabs, l)
        oa_c, oa_l = _attention(attn_sink[l], qa, ka, va, need_ctx)
        of_c, of_l = _fourier(fu, dft, need_ctx)
        ret = _retention(qr, kr, vr, gr, _retention_tables(ret_decay_fwd[l], ret_decay_bwd[l]))
        wr = jnp.zeros((ROUTER_ROWS, D_MODEL), F32)
        wr = wr.at[0:N_GROUPS].set(w_router_group[l].T).at[8:8 + N_EXPERTS].set(w_router_expert[l].T)
        br = jnp.full((ROUTER_ROWS,), NEG, F32)
        br = br.at[0:N_GROUPS].set(b_router_group[l]).at[8:8 + N_EXPERTS].set(b_router_expert[l])
        wrh, wrl = _split(wr)
        brb = jnp.broadcast_to(br[:, None], (ROUTER_ROWS, TM_MERGE))
        x_mid, xs, route, nch3 = _merge(oa_c, oa_l, of_c, of_l, ret, gm, xf, mod_l, norm_ffn[l][None, :],
                                        w_branch_attn[l].astype(BF16), w_branch_fourier[l].astype(BF16),
                                        w_branch_ret[l].astype(BF16), w_out[l].astype(BF16), wrh, wrl, brb,
                                        need_ctx)
        final = l == DEPTH - 1
        xf = _moe(xs, route, nch3, x_mid, mod_l, norm_final[None, :], w_exp_gate, w_exp_up, w_exp_down,
                  row0, final, l)
    return xf.reshape(BATCH, SEQ, D_MODEL)
```

```python
import functools

import numpy as np
import jax
import jax.numpy as jnp
from jax import lax
from jax.experimental import pallas as pl
from jax.experimental.pallas import tpu as pltpu

F32 = jnp.float32
BF16 = jnp.bfloat16

D_MODEL = 1024
BATCH = 8
SEQ = 2048
DEPTH = 2
CTX_LEN = 256
GRID_W = 64
HEAD_DIM = 64
ATTN_HEADS = 8
ATTN_KV_HEADS = 2
ATTN_GROUP = ATTN_HEADS // ATTN_KV_HEADS
ATTN_BLOCK = 128
ATTN_QB = 16
RET_HEADS = 8
RET_DK = 64
RET_CHUNK = 128
FOURIER_GROUPS = 4
FOURIER_DIM = 128
N_GROUPS = 4
EXPERTS_PER_GROUP = 8
N_EXPERTS = N_GROUPS * EXPERTS_PER_GROUP
EXPERT_HIDDEN = 512
ROPE_BASE = 10000.0
NORM_EPS = 1e-6
GN_EPS = 1e-5

W = 512
IN_COLS = 6400
RC = BATCH * CTX_LEN
RL = BATCH * SEQ
R = RC + RL
MOD_ROWS = 16
CTX_MOD_ROW = 8

VMEM_LIMIT = 52 * 1024 * 1024

TM_PROJ = 512
TM_MERGE = 512
TN_ADA = 1536
TR_FOURIER = 1024
RET_SLAB = 2 * CTX_LEN
MOE_BM = 384
CHUNK = 8
CHUNK_SHIFT = 3
CPB = MOE_BM // CHUNK
GATHER_DEPTH = 3
SLOTS = 1280
XS_COLS = D_MODEL // 2 + 128
NEG = -1e30
LOG2E = 1.4426950408889634
ROUTER_ROWS = 40


def _dot(a, b):
    return jnp.dot(a, b, preferred_element_type=F32)


def _dot_nt(a, b):
    return lax.dot_general(a, b, (((1,), (1,)), ((), ())), preferred_element_type=F32)


def _dot_tn(a, b):
    return lax.dot_general(a, b, (((0,), (0,)), ((), ())), preferred_element_type=F32)


def _split(x):
    hi = x.astype(BF16)
    lo = (x - hi.astype(F32)).astype(BF16)
    return hi, lo


def _sigmoid(x):
    return 1.0 / (1.0 + jnp.exp(-x))


def _params(sem, vmem=VMEM_LIMIT):
    return pltpu.CompilerParams(dimension_semantics=sem, vmem_limit_bytes=vmem)


def _mod_row(row0, tm):
    def f(i):
        g0 = i * tm + row0
        return jnp.where(g0 < RC, CTX_MOD_ROW, (g0 - RC) // SEQ)
    return f


def _rope_tables():
    pos = np.arange(SEQ, dtype=np.float64)
    row = np.floor(pos / GRID_W)
    col = pos % GRID_W

    def cs(p, nf):
        inv = ROPE_BASE ** (-np.arange(nf, dtype=np.float64) / nf)
        ang = p[:, None] * inv[None, :]
        return np.cos(ang), np.sin(ang)

    rc, rs = cs(row, HEAD_DIM // 4)
    cc, cs_ = cs(col, HEAD_DIM // 4)
    cos_a = np.concatenate([rc, rc, cc, cc], axis=1)
    sin_a = np.concatenate([-rs, rs, -cs_, cs_], axis=1)
    tc, ts = cs(pos, RET_DK // 2)
    cos_r = np.concatenate([tc, tc], axis=1)
    sin_r = np.concatenate([-ts, ts], axis=1)

    def full(t, ident):
        t2 = np.concatenate([t, t], axis=1)
        return np.concatenate([np.full_like(t2, ident), t2], axis=0).astype(np.float32)

    return full(cos_a, 1.0), full(sin_a, 0.0), full(cos_r, 1.0), full(sin_r, 0.0)


def _dft_tables():
    def cs(n):
        k = np.arange(n, dtype=np.int64)
        m = (k[:, None] * k[None, :]) % n
        ang = 2.0 * np.pi * m.astype(np.float64) / n
        return np.cos(ang), np.sin(ang)

    c128, s128 = cs(FOURIER_DIM)
    eye = np.eye(FOURIER_GROUPS)
    bdc = np.kron(eye, c128).astype(np.float32)
    bds = np.kron(eye, s128).astype(np.float32)
    cn, sn = cs(SEQ)
    hn = SEQ // 2
    w2 = np.concatenate([cn[:, :hn], -sn[:, :hn]], axis=1).astype(np.float32)
    cl, sl = cs(CTX_LEN)
    w2c = np.concatenate([cl, -sl], axis=1).astype(np.float32)
    return bdc, bds, w2, w2c


def _retention_tables(dec_f, dec_b):
    lg_f = jax.nn.log_sigmoid(dec_f.astype(F32))
    lg_b = jax.nn.log_sigmoid(dec_b.astype(F32))
    i = jnp.arange(RET_CHUNK)
    diff = (i[:, None] - i[None, :]).astype(F32)
    fwd = jnp.exp(jnp.maximum(diff, 0.0)[None] * lg_f[:, None, None])
    bwd = jnp.exp(jnp.maximum(-diff, 0.0)[None] * lg_b[:, None, None])
    dcomb = jnp.where((diff >= 0)[None], fwd, bwd).reshape(2, 4 * RET_CHUNK, RET_CHUNK)
    fi = i.astype(F32)
    lanes = lambda t: jnp.repeat(t, RET_DK, axis=1)
    xi_f = lanes(jnp.exp((fi + 1.0)[:, None] * lg_f[None, :]))
    zt_f = lanes(jnp.exp((RET_CHUNK - 1 - fi)[:, None] * lg_f[None, :]))
    xi_b = lanes(jnp.exp((RET_CHUNK - fi)[:, None] * lg_b[None, :]))
    zt_b = lanes(jnp.exp(fi[:, None] * lg_b[None, :]))
    g_f = jnp.repeat(jnp.exp(RET_CHUNK * lg_f), RET_DK).reshape(2, 256, 1)
    g_b = jnp.repeat(jnp.exp(RET_CHUNK * lg_b), RET_DK).reshape(2, 256, 1)
    g_f = jnp.broadcast_to(g_f, (2, 256, 256))
    g_b = jnp.broadcast_to(g_b, (2, 256, 256))
    return dcomb, xi_f, zt_f, xi_b, zt_b, g_f, g_b


def _ada_body(c_ref, w_ref, b_ref, o_ref):
    c = c_ref[...]
    s = c * _sigmoid(c)
    sh, sl = _split(s)
    wh, wl = _split(w_ref[...])
    o_ref[...] = _dot(sh, wh) + _dot(sl, wh) + _dot(sh, wl) + b_ref[...]


def _ada(cc, w_ada, b_ada):
    nt = 6 * D_MODEL // TN_ADA
    return pl.pallas_call(
        _ada_body,
        grid=(DEPTH, nt),
        in_specs=[
            pl.BlockSpec((MOD_ROWS, D_MODEL), lambda l, j: (0, 0)),
            pl.BlockSpec((None, D_MODEL, TN_ADA), lambda l, j: (l, 0, j)),
            pl.BlockSpec((None, 1, TN_ADA), lambda l, j: (l, 0, j)),
        ],
        out_specs=pl.BlockSpec((None, MOD_ROWS, TN_ADA), lambda l, j: (l, 0, j)),
        out_shape=jax.ShapeDtypeStruct((DEPTH, MOD_ROWS, 6 * D_MODEL), F32),
        compiler_params=_params(("arbitrary", "arbitrary")),
        name="ada_mod",
    )(cc, w_ada, b_ada.reshape(DEPTH, 1, 6 * D_MODEL))


def _rope(xc, cos, sin, half):
    fwd = pltpu.roll(xc, 128 - half, axis=1)
    bwd = pltpu.roll(xc, half, axis=1)
    lane = lax.broadcasted_iota(jnp.int32, xc.shape, 1)
    first = (lane & (2 * half - 1)) < half
    return xc * cos + jnp.where(first, fwd, bwd) * sin


def _proj_body(*refs, split):
    if split:
        x = _pick(pl.program_id(0) < RC // TM_PROJ, refs[0], refs[1])
        refs = refs[2:]
    else:
        x = refs[0][...]
        refs = refs[1:]
    (mod_ref, gn_ref, w_ref, ca_ref, sa_ref, cr_ref, sr_ref,
     qa_ref, ka_ref, va_ref, qr_ref, kr_ref, vr_ref, gr_ref, fu_ref, gm_ref) = refs
    ms = jnp.mean(x * x, axis=-1, keepdims=True)
    y = x * lax.rsqrt(ms + NORM_EPS) * gn_ref[...]
    h = y * (1.0 + mod_ref[1:2, :]) + mod_ref[0:1, :]
    hb = h.astype(BF16)

    def proj(c0, width):
        return _dot(hb, w_ref[:, c0:c0 + width])

    ca, sa, cr, sr = ca_ref[...], sa_ref[...], cr_ref[...], sr_ref[...]

    for c in range(3):
        gm_ref[:, c * D_MODEL:(c + 1) * D_MODEL] = _sigmoid(proj(3328 + c * D_MODEL, D_MODEL)).astype(BF16)
    g = proj(2304, W)
    gr_ref[...] = (g * _sigmoid(g)).astype(BF16)
    qa = proj(0, W) * (HEAD_DIM ** -0.5 * LOG2E)
    for c in range(W // 128):
        qa_ref[:, c * 128:(c + 1) * 128] = _rope(qa[:, c * 128:(c + 1) * 128], ca, sa, 16).astype(BF16)
    kv = proj(W, 256)
    ka = _rope(kv[:, 0:128], ca, sa, 16).astype(BF16)
    ka_ref[0] = ka[:, 0:64]
    ka_ref[1] = ka[:, 64:128]
    va = kv[:, 128:256].astype(BF16)
    ones_col = jnp.where(lax.broadcasted_iota(jnp.int32, (va.shape[0], 64), 1) == 0, 1.0, 0.0).astype(BF16)
    va_ref[0] = jnp.concatenate([va[:, 0:64], ones_col], axis=1)
    va_ref[1] = jnp.concatenate([va[:, 64:128], ones_col], axis=1)
    qr = proj(768, W)
    kr = proj(1280, W) * (RET_DK ** -0.5)
    for c in range(W // 128):
        sl = slice(c * 128, (c + 1) * 128)
        qr_ref[:, sl] = _rope(qr[:, sl], cr, sr, 32).astype(BF16)
        kr_ref[:, sl] = _rope(kr[:, sl], cr, sr, 32).astype(BF16)
    vr_ref[...] = proj(1792, W).astype(BF16)
    fu_ref[...] = proj(2816, W).astype(BF16)


def _proj(x, mod_l, gnorm, w_in_bf, tabs, layer):
    tm = TM_PROJ
    nt = R // tm
    nc = RC // tm
    split = isinstance(x, tuple)
    if split:
        xs = list(x)
        x_specs = [pl.BlockSpec((tm, D_MODEL), lambda i: (jnp.minimum(i, nc - 1), 0)),
                   pl.BlockSpec((tm, D_MODEL), lambda i: (jnp.maximum(i - nc, 0), 0))]
    else:
        xs = [x]
        x_specs = [pl.BlockSpec((tm, D_MODEL), lambda i: (i, 0))]

    def tab_map(i):
        return (jnp.where(i < nc, i, nc + (i - nc) % (SEQ // tm)), 0)

    row = lambda i: (i, 0)
    wide = lambda n: pl.BlockSpec((tm, n), row)
    kv_spec = lambda n: pl.BlockSpec((2, tm, n), lambda i: (0, i, 0))
    sds = lambda n: jax.ShapeDtypeStruct((R, n), BF16)
    kv_sds = lambda n: jax.ShapeDtypeStruct((2, R, n), BF16)
    mrow = _mod_row(0, tm)
    return pl.pallas_call(
        functools.partial(_proj_body, split=split),
        grid=(nt,),
        in_specs=x_specs + [
            pl.BlockSpec((None, 6, D_MODEL), lambda i: (mrow(i), 0, 0)),
            pl.BlockSpec((1, D_MODEL), lambda i: (0, 0)),
            pl.BlockSpec((None, D_MODEL, IN_COLS), lambda i: (layer, 0, 0), pipeline_mode=pl.Buffered(1)),
        ] + [pl.BlockSpec((tm, 128), tab_map)] * 4,
        out_specs=[wide(W), kv_spec(64), kv_spec(128), wide(W), wide(W), wide(W), wide(W), wide(W),
                   wide(3 * D_MODEL)],
        out_shape=[sds(W), kv_sds(64), kv_sds(128), sds(W), sds(W), sds(W), sds(W), sds(W), sds(3 * D_MODEL)],
        compiler_params=_params(("arbitrary",)),
        name="in_proj",
    )(*xs, mod_l, gnorm, w_in_bf, *tabs)


def _attend(sink_ref, q_ref, o_ref, sub, pieces):
    rows = slice(sub * ATTN_BLOCK, (sub + 1) * ATTN_BLOCK)
    groups = [slice(g * ATTN_BLOCK, (g + 1) * ATTN_BLOCK) for g in range(ATTN_GROUP)]
    outs = []
    for h in range(ATTN_KV_HEADS):
        q = q_ref[rows, h * 256:(h + 1) * 256]
        q4 = jnp.concatenate([q[:, g * 64:(g + 1) * 64] for g in range(ATTN_GROUP)], axis=0)
        sinks = [sink_ref[h * ATTN_GROUP + g] * LOG2E for g in range(ATTN_GROUP)]
        k_all = jnp.concatenate([k_ref[h] for k_ref, _, _ in pieces], axis=0)
        v_all = jnp.concatenate([v_ref[h] for _, v_ref, _ in pieces], axis=0)
        s = _dot_nt(q4, k_all)
        cols, c0 = [], 0
        for k_ref, _, ok in pieces:
            n = k_ref.shape[1]
            cols.append(s[:, c0:c0 + n] if ok is None else jnp.where(ok, s[:, c0:c0 + n], NEG))
            c0 += n
        s = jnp.concatenate(cols, axis=1)
        mxs, ps = [], []
        for g, r in enumerate(groups):
            mx = jnp.maximum(jnp.max(s[r], axis=-1, keepdims=True), sinks[g])
            mxs.append(mx)
            ps.append(jnp.exp2(s[r] - mx).astype(BF16))
        oa = _dot(jnp.concatenate(ps, axis=0), v_all)
        for g, r in enumerate(groups):
            den = oa[r, HEAD_DIM:HEAD_DIM + 1] + jnp.exp2(sinks[g] - mxs[g])
            outs.append(oa[r, 0:HEAD_DIM] / den)
    o_ref[rows, :] = jnp.concatenate(outs, axis=1).astype(BF16)


def _attn_lat_body(sink_ref, q_ref, kc_ref, *refs):
    nk = ATTN_QB + 2
    k_refs, vc_ref, v_refs, o_ref = refs[:nk], refs[nk], refs[nk + 1:2 * nk + 1], refs[2 * nk + 1]
    p = pl.program_id(1)
    nr = ATTN_GROUP * ATTN_BLOCK
    ri = lax.broadcasted_iota(jnp.int32, (nr, ATTN_BLOCK), 0) & (ATTN_BLOCK - 1)
    ci = lax.broadcasted_iota(jnp.int32, (nr, ATTN_BLOCK), 1)
    far = 4 * ATTN_BLOCK
    first_prev = ci >= ri + jnp.where(p >= 1, 0, far)
    last_next = ci + jnp.where(p <= SEQ // (ATTN_QB * ATTN_BLOCK) - 2, 0, far) <= ri
    for sub in range(ATTN_QB):
        prev_ok = first_prev if sub == 0 else ci >= ri
        next_ok = last_next if sub == ATTN_QB - 1 else ci <= ri
        _attend(sink_ref, q_ref, o_ref, sub,
                [(kc_ref, vc_ref, None), (k_refs[sub], v_refs[sub], prev_ok), (k_refs[sub + 1], v_refs[sub + 1], None),
                 (k_refs[sub + 2], v_refs[sub + 2], next_ok)])


def _attn_ctx_body(sink_ref, q_ref, kc_ref, vc_ref, o_ref):
    for sub in range(CTX_LEN // ATTN_BLOCK):
        _attend(sink_ref, q_ref, o_ref, sub, [(kc_ref, vc_ref, None)])


def _attention(sink, qa, ka, va, need_ctx):
    nb = SEQ // ATTN_BLOCK
    nstep = nb // ATTN_QB
    tq = ATTN_QB * ATTN_BLOCK
    smem = pl.BlockSpec(memory_space=pltpu.SMEM)

    def loc(delta):
        def f(b, p):
            m = jnp.clip(ATTN_QB * p + delta, 0, nb - 1)
            return (0, RC // ATTN_BLOCK + b * nb + m, 0)
        return f

    def kv_specs(width):
        ctx_spec = pl.BlockSpec((ATTN_KV_HEADS, CTX_LEN, width), lambda b, p: (0, b, 0))
        return [ctx_spec] + [pl.BlockSpec((ATTN_KV_HEADS, ATTN_BLOCK, width), loc(d)) for d in range(-1, ATTN_QB + 1)]

    nloc = ATTN_QB + 3
    oa_l = pl.pallas_call(
        _attn_lat_body,
        grid=(BATCH, nstep),
        in_specs=[smem, pl.BlockSpec((tq, W), lambda b, p: (RC // tq + b * nstep + p, 0))]
                 + kv_specs(HEAD_DIM) + kv_specs(128),
        out_specs=pl.BlockSpec((tq, W), lambda b, p: (b * nstep + p, 0)),
        out_shape=jax.ShapeDtypeStruct((RL, W), BF16),
        compiler_params=_params(("arbitrary", "arbitrary")),
        name="window_attn",
    )(sink, qa, *([ka] * nloc), *([va] * nloc))
    if not need_ctx:
        return None, oa_l
    oa_c = pl.pallas_call(
        _attn_ctx_body,
        grid=(BATCH,),
        in_specs=[smem, pl.BlockSpec((CTX_LEN, W), lambda b: (b, 0)),
                  pl.BlockSpec((ATTN_KV_HEADS, CTX_LEN, HEAD_DIM), lambda b: (0, b, 0)),
                  pl.BlockSpec((ATTN_KV_HEADS, CTX_LEN, 128), lambda b: (0, b, 0))],
        out_specs=pl.BlockSpec((CTX_LEN, W), lambda b: (b, 0)),
        out_shape=jax.ShapeDtypeStruct((RC, W), BF16),
        compiler_params=_params(("arbitrary",)),
        name="ctx_attn",
    )(sink, qa, ka, va)
    return oa_c, oa_l


def _fourier_body(*refs, has_ctx):
    if has_ctx:
        uc_ref, ul_ref, jrev_ref, bdc_ref, bds_ref, w2_ref, w2c_ref, oc_ref, ol_ref, as_ref, mid_ref = refs
    else:
        ul_ref, jrev_ref, bdc_ref, bds_ref, w2_ref, ol_ref, as_ref, mid_ref = refs
    j = pl.program_id(1)
    first = 1 if has_ctx else 0
    hn = SEQ // 2

    if has_ctx:
        @pl.when(j == 0)
        def _():
            u = uc_ref[...]
            a = _dot(u, bdc_ref[...]).astype(BF16)
            s = _dot(u, bds_ref[...]).astype(BF16)
            z = _dot(w2c_ref[...], jnp.concatenate([a, s], axis=0))
            oc_ref[...] = (z * ((CTX_LEN * FOURIER_DIM) ** -0.5)).astype(BF16)

    @pl.when(j == first)
    def _():
        uh = ul_ref[0:hn, :].astype(F32)
        ur = _dot(jrev_ref[...], ul_ref[hn:SEQ, :])
        row = lax.broadcasted_iota(jnp.int32, (hn, W), 0)
        vm = jnp.where(row == 0, 0.0, uh - ur)
        as_ref[0:hn, :] = _dot((uh + ur).astype(BF16), bdc_ref[...]).astype(BF16)
        as_ref[hn:SEQ, :] = _dot(vm.astype(BF16), bds_ref[...]).astype(BF16)
        mid_ref[...] = _dot(ul_ref[hn:hn + 8, :], bdc_ref[...])

    @pl.when(j >= first)
    def _():
        z = _dot(w2_ref[...], as_ref[...])
        k = lax.broadcasted_iota(jnp.int32, (z.shape[0], 1), 0)
        sign = (1 - 2 * (k & 1)).astype(F32)
        ol_ref[...] = ((z + sign * mid_ref[0:1, :]) * ((SEQ * FOURIER_DIM) ** -0.5)).astype(BF16)


def _fourier(fu, dft, need_ctx):
    bdc, bds, w2, w2c = dft
    tr = TR_FOURIER
    nj = SEQ // tr
    hn = SEQ // 2
    first = 1 if need_ctx else 0
    jrev_np = np.zeros((hn, hn), np.float32)
    jrev_np[np.arange(1, hn), hn - np.arange(1, hn)] = 1.0
    jrev = jnp.asarray(jrev_np).astype(BF16)
    full = lambda a: pl.BlockSpec(a.shape, lambda *_: (0,) * a.ndim)
    lat_tile = lambda j: jnp.maximum(j - first, 0)
    ul_spec = pl.BlockSpec((SEQ, W), lambda b, j: (1 + b, 0))
    w2_spec = pl.BlockSpec((tr, SEQ), lambda b, j: (lat_tile(j), 0))
    ol_spec = pl.BlockSpec((tr, W), lambda b, j: (b * nj + lat_tile(j), 0))
    ol_shape = jax.ShapeDtypeStruct((RL, W), BF16)
    if need_ctx:
        in_specs = [pl.BlockSpec((CTX_LEN, W), lambda b, j: (b, 0)), ul_spec, full(jrev), full(bdc), full(bds),
                    w2_spec, full(w2c)]
        out_specs = [pl.BlockSpec((CTX_LEN, W), lambda b, j: (b, 0)), ol_spec]
        out_shape = [jax.ShapeDtypeStruct((RC, W), BF16), ol_shape]
        args = (fu, fu, jrev, bdc, bds, w2, w2c)
    else:
        in_specs = [ul_spec, full(jrev), full(bdc), full(bds), w2_spec]
        out_specs = [ol_spec]
        out_shape = [ol_shape]
        args = (fu, jrev, bdc, bds, w2)
    outs = pl.pallas_call(
        functools.partial(_fourier_body, has_ctx=need_ctx),
        grid=(BATCH, nj + first),
        in_specs=in_specs,
        out_specs=out_specs,
        out_shape=out_shape,
        scratch_shapes=[pltpu.VMEM((SEQ, W), BF16), pltpu.VMEM((8, W), F32)],
        compiler_params=_params(("arbitrary", "arbitrary")),
        name="fourier_mix",
    )(*args)
    return (outs[0], outs[1]) if need_ctx else (None, outs[0])


def _retention_body(qc_ref, kc_ref, vc_ref, ql0_ref, kl0_ref, vl0_ref, ql1_ref, kl1_ref, vl1_ref, g_ref,
                    dcomb_ref, xif_ref, ztf_ref, xib_ref, ztb_ref, gf_ref, gb_ref, mbd_ref, avg_ref,
                    o_ref, os_ref, st_ref):
    j = pl.program_id(1)
    C = RET_CHUNK
    nl = SEQ // C
    lat_refs = ((ql0_ref, kl0_ref, vl0_ref), (ql1_ref, kl1_ref, vl1_ref))

    own = ((lax.broadcasted_iota(jnp.int32, (4 * C, 256), 0) >> 7)
           == (lax.broadcasted_iota(jnp.int32, (4 * C, 256), 1) >> 6))

    def chunk_fwd(bb, q, k, v, r0):
        for gi in range(2):
            sl = slice(gi * 256, (gi + 1) * 256)
            q4, k4, v4 = q[:, sl], k[:, sl], v[:, sl]
            s_prev = st_ref[2 * bb + gi]
            q4f = q4.astype(F32)
            o4 = _dot((q4f * xif_ref[:, sl]).astype(BF16), s_prev.astype(BF16))
            qstack = jnp.where(own, jnp.concatenate([q4f] * 4, axis=0), 0.0).astype(BF16)
            p = (_dot_nt(qstack, k4) * dcomb_ref[gi]).astype(BF16)
            ov = jnp.where(own, _dot(p, v4), 0.0)
            intra = ov[0:C] + ov[C:2 * C] + ov[2 * C:3 * C] + ov[3 * C:4 * C]
            os_ref[pl.ds(r0, C), sl] = o4 + intra
            u = _dot_tn(k4, (v4.astype(F32) * ztf_ref[:, sl]).astype(BF16))
            st_ref[2 * bb + gi] = gf_ref[gi] * s_prev + mbd_ref[...] * u

    def chunk_bwd(bb, q, k, v, r0):
        for gi in range(2):
            sl = slice(gi * 256, (gi + 1) * 256)
            q4, k4, v4 = q[:, sl], k[:, sl], v[:, sl]
            s_prev = st_ref[2 * bb + gi]
            cross = _dot((q4.astype(F32) * xib_ref[:, sl]).astype(BF16), s_prev.astype(BF16))
            os_ref[pl.ds(r0, C), sl] = os_ref[pl.ds(r0, C), sl] + cross
            u = _dot_tn(k4, (v4.astype(F32) * ztb_ref[:, sl]).astype(BF16))
            st_ref[2 * bb + gi] = gb_ref[gi] * s_prev + mbd_ref[...] * u

    def scan(chunk, ctx_order, lat_index):
        st_ref[...] = jnp.zeros_like(st_ref)
        for c in ctx_order:
            for bb in range(2):
                rs = slice(bb * CTX_LEN + c * C, bb * CTX_LEN + (c + 1) * C)
                chunk(bb, qc_ref[rs, :], kc_ref[rs, :], vc_ref[rs, :], bb * CTX_LEN + c * C)

        def body(t, carry):
            r0 = pl.multiple_of(lat_index(t) * C, C)
            for bb, (q_ref, k_ref, v_ref) in enumerate(lat_refs):
                rs = pl.ds(r0, C)
                chunk(bb, q_ref[rs, :], k_ref[rs, :], v_ref[rs, :], 2 * CTX_LEN + bb * SEQ + r0)
            return carry

        lax.fori_loop(0, nl, body, 0)

    @pl.when(j == 0)
    def _():
        scan(chunk_fwd, range(CTX_LEN // C), lambda t: t)
        scan(chunk_bwd, reversed(range(CTX_LEN // C)), lambda t: nl - 1 - t)

    o = os_ref[pl.ds(pl.multiple_of(j * RET_SLAB, RET_SLAB), RET_SLAB), :]
    avg = avg_ref[...]
    oh, ol = _split(o)
    mu = _dot(oh, avg) + _dot(ol, avg)
    d = o - mu
    var = _dot((d * d).astype(BF16), avg)
    o_ref[...] = (g_ref[...].astype(F32) * d * lax.rsqrt(var + GN_EPS)).astype(BF16)


def _retention(qr, kr, vr, gr, rtabs):
    nslab = SEQ // RET_SLAB
    nj = 1 + 2 * nslab

    def out_map(b2, j):
        return (jnp.where(j == 0, b2, RC // RET_SLAB + 2 * b2 * nslab + j - 1), 0)

    ctx = pl.BlockSpec((2 * CTX_LEN, W), lambda b2, j: (b2, 0))
    lat = lambda k: pl.BlockSpec((SEQ, W), lambda b2, j: (1 + 2 * b2 + k, 0))
    full = lambda a: pl.BlockSpec(a.shape, lambda *_: (0,) * a.ndim)
    avg = jnp.asarray(np.kron(np.eye(RET_HEADS), np.full((RET_DK, RET_DK), 1.0 / RET_DK)).astype(np.float32)).astype(BF16)
    mbd = jnp.asarray(np.kron(np.eye(4), np.ones((RET_DK, RET_DK))).astype(np.float32))
    tabs = list(rtabs) + [mbd, avg]
    return pl.pallas_call(
        _retention_body,
        grid=(BATCH // 2, nj),
        in_specs=[ctx, ctx, ctx, lat(0), lat(0), lat(0), lat(1), lat(1), lat(1),
                  pl.BlockSpec((RET_SLAB, W), out_map)] + [full(t) for t in tabs],
        out_specs=pl.BlockSpec((RET_SLAB, W), out_map),
        out_shape=jax.ShapeDtypeStruct((R, W), BF16),
        scratch_shapes=[pltpu.VMEM((2 * (CTX_LEN + SEQ), W), F32), pltpu.VMEM((4, 256, 256), F32)],
        compiler_params=_params(("arbitrary", "arbitrary")),
        name="retention",
    )(qr, kr, vr, qr, kr, vr, qr, kr, vr, gr, *tabs)


def _pack_pair(a, b):
    ua = lax.bitcast_convert_type(a, jnp.uint32) >> 16
    ub = lax.bitcast_convert_type(b, jnp.uint32) & jnp.uint32(0xFFFF0000)
    return ua | ub


def _unpack_pair(w):
    a = lax.bitcast_convert_type(w << 16, F32)
    b = lax.bitcast_convert_type(w & jnp.uint32(0xFFFF0000), F32)
    return a, b


def _slot_onehot(s0, s1, n):
    srow = lax.broadcasted_iota(jnp.int32, (n, s0.shape[1]), 0)
    p0 = jnp.where(srow == s0, 1.0, 0.0).astype(BF16)
    p1 = jnp.where(srow == s1, 1.0, 0.0).astype(BF16)
    return p0, p1


def _pick(first, a_ref, b_ref):
    a = a_ref[...]
    flag = jnp.zeros(a.shape, jnp.int32) + first.astype(jnp.int32)
    return jnp.where(flag > 0, a, b_ref[...])


def _merge_body(*refs, layer0):
    if layer0:
        oac_ref, oal_ref, ofc_ref, ofl_ref, rt_ref, gm_ref, xc_ref, xl_ref = refs[:8]
        rest = refs[8:]
        is_ctx = pl.program_id(0) < RC // TM_MERGE
        oa_in = _pick(is_ctx, oac_ref, oal_ref)
        of_in = _pick(is_ctx, ofc_ref, ofl_ref)
        x_in = _pick(is_ctx, xc_ref, xl_ref)
    else:
        oal_ref, ofl_ref, rt_ref, gm_ref, x_ref = refs[:5]
        rest = refs[5:]
        oa_in = oal_ref[...]
        of_in = ofl_ref[...]
        x_in = x_ref[...]
    (mod_ref, gn_ref, wba_ref, wbf_ref, wbr_ref, wout_ref, wrh_ref, wrl_ref, br_ref, tri_ref, ltri_ref,
     xo_ref, xs_ref, ro_ref, nch_ref) = rest
    gm = gm_ref[...].astype(F32)
    z = (gm[:, 0:D_MODEL] * _dot(oa_in, wba_ref[...])
         + gm[:, D_MODEL:2 * D_MODEL] * _dot(of_in, wbf_ref[...])
         + gm[:, 2 * D_MODEL:3 * D_MODEL] * _dot(rt_ref[...], wbr_ref[...]))
    y = _dot(z.astype(BF16), wout_ref[...])
    x = x_in + mod_ref[2:3, :] * y
    xo_ref[...] = x
    ms = jnp.mean(x * x, axis=-1, keepdims=True)
    hn = x * lax.rsqrt(ms + NORM_EPS) * gn_ref[...]
    h2 = hn * (1.0 + mod_ref[4:5, :]) + mod_ref[3:4, :]
    hh, hl = _split(h2)
    wh, wl = wrh_ref[...], wrl_ref[...]
    lg = _dot_nt(wh, hh) + _dot_nt(wh, hl) + _dot_nt(wl, hh) + br_ref[...]
    tm = lg.shape[1]
    row8 = lax.broadcasted_iota(jnp.int32, (8, tm), 0)
    lgg = lg[0:8, :]
    mg = jnp.max(lgg, axis=0, keepdims=True)
    grp = jnp.min(jnp.where(lgg == mg, row8, 8), axis=0, keepdims=True)
    pg = 1.0 / jnp.sum(jnp.exp(lgg - mg), axis=0, keepdims=True)
    lin = jnp.zeros((8, tm), F32)
    for g in range(N_GROUPS):
        lin = jnp.where(grp == g, lg[8 + 8 * g:16 + 8 * g, :], lin)
    v1 = jnp.max(lin, axis=0, keepdims=True)
    i1 = jnp.min(jnp.where(lin == v1, row8, 8), axis=0, keepdims=True)
    rest = jnp.where(row8 == i1, -jnp.inf, lin)
    v2 = jnp.max(rest, axis=0, keepdims=True)
    i2 = jnp.min(jnp.where(rest == v2, row8, 8), axis=0, keepdims=True)
    e2 = jnp.exp(v2 - v1)
    w1 = pg / (1.0 + e2)
    w2 = pg * e2 / (1.0 + e2)
    e_1 = grp * EXPERTS_PER_GROUP + i1
    e_2 = grp * EXPERTS_PER_GROUP + i2

    row32 = lax.broadcasted_iota(jnp.int32, (N_EXPERTS, tm), 0)
    oh0 = jnp.where(row32 == e_1, 1.0, 0.0)
    oh1 = jnp.where(row32 == e_2, 1.0, 0.0)
    tri = tri_ref[...]
    cum0 = _dot(oh0.astype(BF16), tri)
    cum1 = _dot(oh1.astype(BF16), tri)
    tot0 = jnp.sum(oh0, axis=1, keepdims=True)
    tot1 = jnp.sum(oh1, axis=1, keepdims=True)
    nch = ((tot0 + tot1).astype(jnp.int32) + (CHUNK - 1)) >> CHUNK_SHIFT
    nch_b = jnp.broadcast_to(nch.astype(F32), (N_EXPERTS, 128))
    nch_ref[...] = nch_b.astype(jnp.int32)
    base = CHUNK * _dot(ltri_ref[...], nch_b.astype(BF16))[:, 0:1]
    s0 = jnp.sum(oh0 * (base + cum0), axis=0, keepdims=True).astype(jnp.int32)
    s1 = jnp.sum(oh1 * (base + tot0 + cum1), axis=0, keepdims=True).astype(jnp.int32)
    p0, p1 = _slot_onehot(s0, s1, SLOTS)
    xs = _dot(p0 + p1, hh)
    xs_ref[:, 0:D_MODEL // 2] = _pack_pair(xs[:, 0:D_MODEL // 2], xs[:, D_MODEL // 2:D_MODEL])

    def wrows(w):
        hi, lo = _split(w)
        return jnp.where(row8 == 0, hi.astype(F32), jnp.where(row8 == 1, lo.astype(F32), 0.0)).astype(BF16)

    wc = _dot_nt(p0, wrows(w1)) + _dot_nt(p1, wrows(w2))
    wcol = jnp.broadcast_to(wc[:, 0:1] + wc[:, 1:2], (SLOTS, 128))
    xs_ref[:, D_MODEL // 2:XS_COLS] = lax.bitcast_convert_type(wcol, jnp.uint32)
    s0f, s1f = s0.astype(F32), s1.astype(F32)
    ro_ref[...] = jnp.where(row8 == 0, s0f, jnp.where(row8 == 1, s1f, 0.0))


def _merge(oa_c, oa_l, of_c, of_l, ret, gm, xs_in, mod_l, gnorm, wba, wbf, wbr, wout, wrh, wrl, brb, layer0):
    tm = TM_MERGE
    row0 = 0 if layer0 else RC
    rm = R - row0
    nt = rm // tm
    off = row0 // tm
    nc = RC // tm
    src = lambda n: pl.BlockSpec((tm, n), lambda i: (i + off, 0))
    dst = lambda n: pl.BlockSpec((tm, n), lambda i: (i, 0))
    ctx_rows = lambda n: pl.BlockSpec((tm, n), lambda i: (jnp.minimum(i, nc - 1), 0))
    lat_rows = lambda n: pl.BlockSpec((tm, n), lambda i: (jnp.maximum(i - nc, 0), 0))
    full = lambda a: pl.BlockSpec(a.shape, lambda *_: (0,) * a.ndim, pipeline_mode=pl.Buffered(1))
    mrow = _mod_row(row0, tm)
    tri = jnp.asarray(np.triu(np.ones((tm, tm), np.float32), 1)).astype(BF16)
    ltri = jnp.asarray(np.tril(np.ones((N_EXPERTS, N_EXPERTS), np.float32), -1)).astype(BF16)
    if layer0:
        acts = [oa_c, oa_l, of_c, of_l, ret, gm, xs_in[0], xs_in[1]]
        act_specs = [ctx_rows(W), lat_rows(W), ctx_rows(W), lat_rows(W), src(W), src(3 * D_MODEL),
                     ctx_rows(D_MODEL), lat_rows(D_MODEL)]
    else:
        acts = [oa_l, of_l, ret, gm, xs_in]
        act_specs = [dst(W), dst(W), src(W), src(3 * D_MODEL), src(D_MODEL)]
    return pl.pallas_call(
        functools.partial(_merge_body, layer0=layer0),
        grid=(nt,),
        in_specs=act_specs + [
                  pl.BlockSpec((None, 6, D_MODEL), lambda i: (mrow(i), 0, 0)),
                  full(gnorm), full(wba), full(wbf), full(wbr), full(wout), full(wrh), full(wrl), full(brb),
                  full(tri), full(ltri)],
        out_specs=[dst(D_MODEL), pl.BlockSpec((SLOTS, XS_COLS), lambda i: (i, 0)),
                   pl.BlockSpec((8, tm), lambda i: (0, i)),
                   pl.BlockSpec((None, N_EXPERTS, 128), lambda i: (i, 0, 0))],
        out_shape=[jax.ShapeDtypeStruct((rm, D_MODEL), F32),
                   jax.ShapeDtypeStruct((nt * SLOTS, XS_COLS), jnp.uint32),
                   jax.ShapeDtypeStruct((8, rm), F32),
                   jax.ShapeDtypeStruct((nt, N_EXPERTS, 128), jnp.int32)],
        compiler_params=_params(("arbitrary",)),
        name="merge_router",
    )(*acts, mod_l, gnorm, wba, wbf, wbr, wout, wrh, wrl, brb, tri, ltri)


def _moe_plan(nch, nb):
    nt = nch.shape[0]
    choff = jnp.cumsum(nch, axis=1) - nch
    used_ch = jnp.sum(nch, axis=1)
    cum_t = jnp.cumsum(nch, axis=0)
    tot = cum_t[-1]
    ptot = (tot + CPB - 1) // CPB * CPB
    pend = jnp.cumsum(ptot)
    pstart = pend - ptot
    n_used = pend[-1] // CPB
    blk = jnp.arange(nb, dtype=jnp.int32)
    lane = jnp.arange(CPB, dtype=jnp.int32)
    blk_e = jnp.minimum(jnp.sum((blk[:, None] * CPB >= pend[None, :]).astype(jnp.int32), axis=1), N_EXPERTS - 1)
    oe = (blk_e[:, None] == jnp.arange(N_EXPERTS, dtype=jnp.int32)[None, :]).astype(jnp.int32)
    sel = lambda tab: jnp.sum(oe[:, :, None] * tab.T[None, :, :], axis=1)
    pstart_b = jnp.sum(oe * pstart[None, :], axis=1)
    tot_b = jnp.sum(oe * tot[None, :], axis=1)
    cum_b, nch_b, choff_b = sel(cum_t), sel(nch), sel(choff)
    i = blk[:, None] * CPB + lane[None, :] - pstart_b[:, None]
    valid = (i < tot_b[:, None]) & (blk[:, None] < n_used)
    t = jnp.minimum(jnp.sum((i[:, :, None] >= cum_b[:, None, :]).astype(jnp.int32), axis=2), nt - 1)
    tiles = jnp.arange(nt, dtype=jnp.int32)[None, None, :]
    before = jnp.sum(jnp.where(tiles < t[:, :, None], nch_b[:, None, :], 0), axis=2)
    coff = jnp.sum(jnp.where(tiles == t[:, :, None], choff_b[:, None, :], 0), axis=2)
    row = t * SLOTS + CHUNK * (coff + i - before)
    src = jnp.where(valid, row, SLOTS - CHUNK)
    dummy = nt * SLOTS + CHUNK * ((blk[:, None] % 2) * CPB + lane[None, :])
    dst = jnp.where(valid, row, dummy)
    blk_start = jnp.concatenate([pstart, pend[-1:]]) // CPB
    return (blk_start.astype(jnp.int32), n_used.astype(jnp.int32).reshape(1), src.reshape(-1).astype(jnp.int32),
            dst.reshape(-1).astype(jnp.int32), used_ch.astype(jnp.int32))


def _ffn_body(bs_ref, nu_ref, src_ref, dst_ref, uc_ref, xs_ref, wg_ref, wu_ref, wd_ref, ys_ref,
              xbuf, ybuf, zbuf, wgb, wub, wdb, sem_in, sem_out, sem_z, *, nt):
    e = pl.program_id(0)
    nu = nu_ref[0]
    half = D_MODEL // 2

    def gather(blk, sl):
        for c in range(CPB):
            r = pl.multiple_of(src_ref[blk * CPB + c], CHUNK)
            pltpu.make_async_copy(xs_ref.at[pl.ds(r, CHUNK)], xbuf.at[sl, pl.ds(c * CHUNK, CHUNK)],
                                  sem_in.at[sl]).start(priority=1)

    def scatter(blk, sl):
        for c in range(CPB):
            r = pl.multiple_of(dst_ref[blk * CPB + c], CHUNK)
            pltpu.make_async_copy(ybuf.at[sl, pl.ds(c * CHUNK, CHUNK)], ys_ref.at[pl.ds(r, CHUNK)],
                                  sem_out.at[sl]).start(priority=1)

    def wait_gather(sl):
        pltpu.make_async_copy(xs_ref.at[pl.ds(0, MOE_BM)], xbuf.at[sl], sem_in.at[sl]).wait()

    def wait_scatter(sl):
        pltpu.make_async_copy(ybuf.at[sl], ys_ref.at[pl.ds(0, MOE_BM)], sem_out.at[sl]).wait()

    def zero_copy(r):
        return pltpu.make_async_copy(zbuf, ys_ref.at[pl.ds(pl.multiple_of(r, CHUNK), CHUNK)], sem_z)

    @pl.when(e == 0)
    def _():
        zbuf[...] = jnp.zeros_like(zbuf)

        def tails(fn):
            def per_tile(t, carry):
                def per_chunk(c, carry2):
                    fn(t * SLOTS + c * CHUNK)
                    return carry2
                lax.fori_loop(uc_ref[t], SLOTS // CHUNK, per_chunk, 0)
                return carry
            lax.fori_loop(0, nt, per_tile, 0)
            for c in range(2 * CPB):
                fn(nt * SLOTS + c * CHUNK)

        tails(lambda r: zero_copy(r).start())
        tails(lambda r: zero_copy(r).wait())
        gather(0, 0)
        for d in range(1, GATHER_DEPTH - 1):
            @pl.when(nu > d)
            def _():
                gather(d, d)

    b0, b1 = bs_ref[e], bs_ref[e + 1]

    @pl.when(b1 > b0)
    def _():
        wgb[...] = wg_ref[...].astype(BF16)
        wub[...] = wu_ref[...].astype(BF16)
        wdb[...] = wd_ref[...].astype(BF16)

        def block(b, carry):
            slot = b % 2
            xslot = lax.rem(b, GATHER_DEPTH)

            @pl.when(b + GATHER_DEPTH - 1 < nu)
            def _():
                gather(b + GATHER_DEPTH - 1, lax.rem(b + GATHER_DEPTH - 1, GATHER_DEPTH))

            wait_gather(xslot)

            @pl.when(b >= 2)
            def _():
                wait_scatter(slot)

            xw = xbuf[xslot]
            xa, xb = _unpack_pair(xw[:, 0:half])
            x = jnp.concatenate([xa, xb], axis=1).astype(BF16)
            wt = lax.bitcast_convert_type(xw[:, half:XS_COLS], F32)
            g = _dot(x, wgb[...])
            u = _dot(x, wub[...])
            hmid = (g * _sigmoid(g) * u).astype(BF16)
            y = _dot(hmid, wdb[...]) * jnp.concatenate([wt] * (D_MODEL // 128), axis=1)
            yb = y.astype(BF16).astype(F32)
            ybuf[slot] = _pack_pair(yb[:, 0:half], yb[:, half:D_MODEL])
            scatter(b, slot)
            return carry

        lax.fori_loop(b0, b1, block, 0)

    @pl.when(e == N_EXPERTS - 1)
    def _():
        wait_scatter((nu - 1) % 2)

        @pl.when(nu >= 2)
        def _():
            wait_scatter(nu % 2)


def _ffn(plan, xs, w_g, w_u, w_d, layer, nt):
    wmap = lambda e, *_: (layer, e, 0, 0)
    half = D_MODEL // 2
    grid_spec = pltpu.PrefetchScalarGridSpec(
        num_scalar_prefetch=5,
        grid=(N_EXPERTS,),
        in_specs=[pl.BlockSpec(memory_space=pl.ANY),
                  pl.BlockSpec((None, None, D_MODEL, EXPERT_HIDDEN), wmap),
                  pl.BlockSpec((None, None, D_MODEL, EXPERT_HIDDEN), wmap),
                  pl.BlockSpec((None, None, EXPERT_HIDDEN, D_MODEL), wmap)],
        out_specs=pl.BlockSpec(memory_space=pl.ANY),
        scratch_shapes=[pltpu.VMEM((GATHER_DEPTH, MOE_BM, XS_COLS), jnp.uint32),
                        pltpu.VMEM((2, MOE_BM, half), jnp.uint32),
                        pltpu.VMEM((CHUNK, half), jnp.uint32),
                        pltpu.VMEM((D_MODEL, EXPERT_HIDDEN), BF16), pltpu.VMEM((D_MODEL, EXPERT_HIDDEN), BF16),
                        pltpu.VMEM((EXPERT_HIDDEN, D_MODEL), BF16),
                        pltpu.SemaphoreType.DMA((GATHER_DEPTH,)), pltpu.SemaphoreType.DMA((2,)),
                        pltpu.SemaphoreType.DMA(())],
    )
    return pl.pallas_call(
        functools.partial(_ffn_body, nt=nt),
        grid_spec=grid_spec,
        out_shape=jax.ShapeDtypeStruct((nt * SLOTS + 2 * CPB * CHUNK, half), jnp.uint32),
        compiler_params=_params(("arbitrary",)),
        name="moe_experts",
    )(*plan, xs, w_g, w_u, w_d)


def _combine_body(ys_ref, ro_ref, x_ref, mod_ref, gn_ref, o_ref, *, final):
    s = ro_ref[...]
    p0, p1 = _slot_onehot(s[0:1, :].astype(jnp.int32), s[1:2, :].astype(jnp.int32), SLOTS)
    ya, yb = _unpack_pair(ys_ref[...])
    y = jnp.concatenate([ya, yb], axis=1).astype(BF16)
    f = _dot_tn(p0 + p1, y)
    x = x_ref[...] + mod_ref[5:6, :] * f
    if final:
        ms = jnp.mean(x * x, axis=-1, keepdims=True)
        x = x * lax.rsqrt(ms + NORM_EPS) * gn_ref[...]
    o_ref[...] = x


def _combine(ys, route, x, mod_l, gnorm, row0, final):
    tm = TM_MERGE
    rm = x.shape[0]
    mrow = _mod_row(row0, tm)
    return pl.pallas_call(
        functools.partial(_combine_body, final=final),
        grid=(rm // tm,),
        in_specs=[pl.BlockSpec((SLOTS, D_MODEL // 2), lambda i: (i, 0)),
                  pl.BlockSpec((8, tm), lambda i: (0, i)),
                  pl.BlockSpec((tm, D_MODEL), lambda i: (i, 0)),
                  pl.BlockSpec((None, 6, D_MODEL), lambda i: (mrow(i), 0, 0)),
                  pl.BlockSpec((1, D_MODEL), lambda i: (0, 0))],
        out_specs=pl.BlockSpec((tm, D_MODEL), lambda i: (i, 0)),
        out_shape=jax.ShapeDtypeStruct((rm, D_MODEL), F32),
        compiler_params=_params(("arbitrary",)),
        name="moe_combine",
    )(ys, route, x, mod_l, gnorm)


def _moe(xs, route, nch3, x, mod_l, gnorm, w_g, w_u, w_d, row0, final, layer):
    nt = nch3.shape[0]
    max_chunks = nt * ((2 * TM_MERGE + N_EXPERTS * (CHUNK - 1)) // CHUNK)
    nb = -(-max_chunks // CPB) + N_EXPERTS
    plan = _moe_plan(nch3[:, :, 0], nb)
    ys = _ffn(plan, xs, w_g, w_u, w_d, layer, nt)
    return _combine(ys, route, x, mod_l, gnorm, row0, final)


def kernel(x, c, ctx, c_ctx, norm_mix, norm_ffn, w_ada, b_ada, w_in, attn_sink, ret_decay_fwd, ret_decay_bwd,
           w_branch_attn, w_branch_fourier, w_branch_ret, w_out, w_router_group, b_router_group,
           w_router_expert, b_router_expert, w_exp_gate, w_exp_up, w_exp_down, norm_final):
    tabs = [jnp.asarray(t) for t in _rope_tables()]
    dft = [jnp.asarray(t).astype(BF16) for t in _dft_tables()]

    cc = jnp.zeros((MOD_ROWS, D_MODEL), F32).at[0:BATCH].set(c).at[CTX_MOD_ROW].set(c_ctx)
    mod = _ada(cc, w_ada, b_ada).reshape(DEPTH, MOD_ROWS, 6, D_MODEL)

    xf = (ctx.reshape(RC, D_MODEL), x.reshape(RL, D_MODEL))
    w_in_bf = w_in.astype(BF16)
    for l in range(DEPTH):
        need_ctx = l < DEPTH - 1
        row0 = 0 if need_ctx else RC
        mod_l = mod[l]
        qa, ka, va, qr, kr, vr, gr, fu, gm = _proj(xf, mod_l, norm_mix[l][None, :], w_in_bf, tabs, l)
        oa_c, oa_l = _attention(attn_sink[l], qa, ka, va, need_ctx)
        of_c, of_l = _fourier(fu, dft, need_ctx)
        ret = _retention(qr, kr, vr, gr, _retention_tables(ret_decay_fwd[l], ret_decay_bwd[l]))
        wr = jnp.zeros((ROUTER_ROWS, D_MODEL), F32)
        wr = wr.at[0:N_GROUPS].set(w_router_group[l].T).at[8:8 + N_EXPERTS].set(w_router_expert[l].T)
        br = jnp.full((ROUTER_ROWS,), NEG, F32)
        br = br.at[0:N_GROUPS].set(b_router_group[l]).at[8:8 + N_EXPERTS].set(b_router_expert[l])
        wrh, wrl = _split(wr)
        brb = jnp.broadcast_to(br[:, None], (ROUTER_ROWS, TM_MERGE))
        x_mid, xs, route, nch3 = _merge(oa_c, oa_l, of_c, of_l, ret, gm, xf, mod_l, norm_ffn[l][None, :],
                                        w_branch_attn[l].astype(BF16), w_branch_fourier[l].astype(BF16),
                                        w_branch_ret[l].astype(BF16), w_out[l].astype(BF16), wrh, wrl, brb,
                                        need_ctx)
        final = l == DEPTH - 1
        xf = _moe(xs, route, nch3, x_mid, mod_l, norm_final[None, :], w_exp_gate, w_exp_up, w_exp_down,
                  row0, final, l)
    return xf.reshape(BATCH, SEQ, D_MODEL)
```

```python
import functools

import numpy as np
import jax
import jax.numpy as jnp
from jax import lax
from jax.experimental import pallas as pl
from jax.experimental.pallas import tpu as pltpu

F32 = jnp.float32
BF16 = jnp.bfloat16

D_MODEL = 1024
BATCH = 8
SEQ = 2048
DEPTH = 2
CTX_LEN = 256
GRID_W = 64
HEAD_DIM = 64
ATTN_HEADS = 8
ATTN_KV_HEADS = 2
ATTN_GROUP = ATTN_HEADS // ATTN_KV_HEADS
ATTN_BLOCK = 128
ATTN_QB = 16
RET_HEADS = 8
RET_DK = 64
RET_CHUNK = 256
FOURIER_GROUPS = 4
FOURIER_DIM = 128
N_GROUPS = 4
EXPERTS_PER_GROUP = 8
N_EXPERTS = N_GROUPS * EXPERTS_PER_GROUP
EXPERT_HIDDEN = 512
ROPE_BASE = 10000.0
NORM_EPS = 1e-6
GN_EPS = 1e-5

W = 512
IN_COLS = 6400
RC = BATCH * CTX_LEN
RL = BATCH * SEQ
R = RC + RL
MOD_ROWS = 16
CTX_MOD_ROW = 8

VMEM_LIMIT = 52 * 1024 * 1024

TM_PROJ = 512
TM_MERGE = 512
TN_ADA = 1536
TR_FOURIER = 1024
RET_SLAB = 2 * CTX_LEN
MOE_BM = 512
CHUNK = 8
CHUNK_SHIFT = 3
CPB = MOE_BM // CHUNK
GATHER_DEPTH = 3
SLOTS = 1280
XS_COLS = D_MODEL // 2 + 128
NEG = -1e30
LOG2E = 1.4426950408889634
ROUTER_ROWS = 40


def _dot(a, b):
    return jnp.dot(a, b, preferred_element_type=F32)


def _dot_nt(a, b):
    return lax.dot_general(a, b, (((1,), (1,)), ((), ())), preferred_element_type=F32)


def _dot_tn(a, b):
    return lax.dot_general(a, b, (((0,), (0,)), ((), ())), preferred_element_type=F32)


def _split(x):
    hi = x.astype(BF16)
    lo = (x - hi.astype(F32)).astype(BF16)
    return hi, lo


def _sigmoid(x):
    return 1.0 / (1.0 + jnp.exp(-x))


def _params(sem, vmem=VMEM_LIMIT):
    return pltpu.CompilerParams(dimension_semantics=sem, vmem_limit_bytes=vmem)


def _mod_row(row0, tm):
    def f(i):
        g0 = i * tm + row0
        return jnp.where(g0 < RC, CTX_MOD_ROW, (g0 - RC) // SEQ)
    return f


def _rope_tables():
    pos = np.arange(SEQ, dtype=np.float64)
    row = np.floor(pos / GRID_W)
    col = pos % GRID_W

    def cs(p, nf):
        inv = ROPE_BASE ** (-np.arange(nf, dtype=np.float64) / nf)
        ang = p[:, None] * inv[None, :]
        return np.cos(ang), np.sin(ang)

    rc, rs = cs(row, HEAD_DIM // 4)
    cc, cs_ = cs(col, HEAD_DIM // 4)
    cos_a = np.concatenate([rc, rc, cc, cc], axis=1)
    sin_a = np.concatenate([-rs, rs, -cs_, cs_], axis=1)
    tc, ts = cs(pos, RET_DK // 2)
    cos_r = np.concatenate([tc, tc], axis=1)
    sin_r = np.concatenate([-ts, ts], axis=1)

    def full(t, ident):
        t2 = np.concatenate([t, t], axis=1)
        return np.concatenate([np.full_like(t2, ident), t2], axis=0).astype(np.float32)

    return full(cos_a, 1.0), full(sin_a, 0.0), full(cos_r, 1.0), full(sin_r, 0.0)


def _dft_tables():
    def cs(n):
        k = np.arange(n, dtype=np.int64)
        m = (k[:, None] * k[None, :]) % n
        ang = 2.0 * np.pi * m.astype(np.float64) / n
        return np.cos(ang), np.sin(ang)

    c128, s128 = cs(FOURIER_DIM)
    eye = np.eye(FOURIER_GROUPS)
    bdc = np.kron(eye, c128).astype(np.float32)
    bds = np.kron(eye, s128).astype(np.float32)
    cn, sn = cs(SEQ)
    hn = SEQ // 2
    w2 = np.concatenate([cn[:, :hn], -sn[:, :hn]], axis=1).astype(np.float32)
    cl, sl = cs(CTX_LEN)
    w2c = np.concatenate([cl, -sl], axis=1).astype(np.float32)
    return bdc, bds, w2, w2c


def _retention_tables(dec_f, dec_b):
    lg_f = jax.nn.log_sigmoid(dec_f.astype(F32))
    lg_b = jax.nn.log_sigmoid(dec_b.astype(F32))
    i = jnp.arange(RET_CHUNK)
    diff = (i[:, None] - i[None, :]).astype(F32)
    fwd = jnp.exp(jnp.maximum(diff, 0.0)[None] * lg_f[:, None, None])
    bwd = jnp.exp(jnp.maximum(-diff, 0.0)[None] * lg_b[:, None, None])
    dcomb = jnp.where((diff >= 0)[None], fwd, bwd).reshape(2, 4 * RET_CHUNK, RET_CHUNK)
    fi = i.astype(F32)
    lanes = lambda t: jnp.repeat(t, RET_DK, axis=1)
    xi_f = lanes(jnp.exp((fi + 1.0)[:, None] * lg_f[None, :]))
    zt_f = lanes(jnp.exp((RET_CHUNK - 1 - fi)[:, None] * lg_f[None, :]))
    xi_b = lanes(jnp.exp((RET_CHUNK - fi)[:, None] * lg_b[None, :]))
    zt_b = lanes(jnp.exp(fi[:, None] * lg_b[None, :]))
    g_f = jnp.repeat(jnp.exp(RET_CHUNK * lg_f), RET_DK).reshape(2, 256, 1)
    g_b = jnp.repeat(jnp.exp(RET_CHUNK * lg_b), RET_DK).reshape(2, 256, 1)
    g_f = jnp.broadcast_to(g_f, (2, 256, 256))
    g_b = jnp.broadcast_to(g_b, (2, 256, 256))
    return dcomb, xi_f, zt_f, xi_b, zt_b, g_f, g_b


def _ada_body(c_ref, w_ref, b_ref, o_ref):
    c = c_ref[...]
    s = c * _sigmoid(c)
    sh, sl = _split(s)
    wh, wl = _split(w_ref[...])
    o_ref[...] = _dot(sh, wh) + _dot(sl, wh) + _dot(sh, wl) + b_ref[...]


def _ada(cc, w_ada, b_ada):
    nt = 6 * D_MODEL // TN_ADA
    return pl.pallas_call(
        _ada_body,
        grid=(DEPTH, nt),
        in_specs=[
            pl.BlockSpec((MOD_ROWS, D_MODEL), lambda l, j: (0, 0)),
            pl.BlockSpec((None, D_MODEL, TN_ADA), lambda l, j: (l, 0, j)),
            pl.BlockSpec((None, 1, TN_ADA), lambda l, j: (l, 0, j)),
        ],
        out_specs=pl.BlockSpec((None, MOD_ROWS, TN_ADA), lambda l, j: (l, 0, j)),
        out_shape=jax.ShapeDtypeStruct((DEPTH, MOD_ROWS, 6 * D_MODEL), F32),
        compiler_params=_params(("arbitrary", "arbitrary")),
        name="ada_mod",
    )(cc, w_ada, b_ada.reshape(DEPTH, 1, 6 * D_MODEL))


def _rope(xc, cos, sin, half):
    fwd = pltpu.roll(xc, 128 - half, axis=1)
    bwd = pltpu.roll(xc, half, axis=1)
    lane = lax.broadcasted_iota(jnp.int32, xc.shape, 1)
    first = (lane & (2 * half - 1)) < half
    return xc * cos + jnp.where(first, fwd, bwd) * sin


def _proj_body(*refs, split):
    if split:
        x = _pick(pl.program_id(0) < RC // TM_PROJ, refs[0], refs[1])
        refs = refs[2:]
    else:
        x = refs[0][...]
        refs = refs[1:]
    (mod_ref, gn_ref, w_ref, ca_ref, sa_ref, cr_ref, sr_ref,
     qa_ref, ka_ref, va_ref, qr_ref, kr_ref, vr_ref, gr_ref, fu_ref, gm_ref) = refs
    ms = jnp.mean(x * x, axis=-1, keepdims=True)
    y = x * lax.rsqrt(ms + NORM_EPS) * gn_ref[...]
    h = y * (1.0 + mod_ref[1:2, :]) + mod_ref[0:1, :]
    hb = h.astype(BF16)

    def proj(c0, width):
        return _dot(hb, w_ref[:, c0:c0 + width])

    ca, sa, cr, sr = ca_ref[...], sa_ref[...], cr_ref[...], sr_ref[...]

    for c in range(3):
        gm_ref[:, c * D_MODEL:(c + 1) * D_MODEL] = _sigmoid(proj(3328 + c * D_MODEL, D_MODEL)).astype(BF16)
    g = proj(2304, W)
    gr_ref[...] = (g * _sigmoid(g)).astype(BF16)
    qa = proj(0, W) * (HEAD_DIM ** -0.5 * LOG2E)
    for c in range(W // 128):
        qa_ref[:, c * 128:(c + 1) * 128] = _rope(qa[:, c * 128:(c + 1) * 128], ca, sa, 16).astype(BF16)
    kv = proj(W, 256)
    ka = _rope(kv[:, 0:128], ca, sa, 16).astype(BF16)
    ka_ref[0] = ka[:, 0:64]
    ka_ref[1] = ka[:, 64:128]
    va = kv[:, 128:256].astype(BF16)
    ones_col = jnp.where(lax.broadcasted_iota(jnp.int32, (va.shape[0], 64), 1) == 0, 1.0, 0.0).astype(BF16)
    va_ref[0] = jnp.concatenate([va[:, 0:64], ones_col], axis=1)
    va_ref[1] = jnp.concatenate([va[:, 64:128], ones_col], axis=1)
    qr = proj(768, W)
    kr = proj(1280, W) * (RET_DK ** -0.5)
    for c in range(W // 128):
        sl = slice(c * 128, (c + 1) * 128)
        qr_ref[:, sl] = _rope(qr[:, sl], cr, sr, 32).astype(BF16)
        kr_ref[:, sl] = _rope(kr[:, sl], cr, sr, 32).astype(BF16)
    vr_ref[...] = proj(1792, W).astype(BF16)
    fu_ref[...] = proj(2816, W).astype(BF16)


def _proj(x, mod_l, gnorm, w_in_bf, tabs, layer):
    tm = TM_PROJ
    nt = R // tm
    nc = RC // tm
    split = isinstance(x, tuple)
    if split:
        xs = list(x)
        x_specs = [pl.BlockSpec((tm, D_MODEL), lambda i: (jnp.minimum(i, nc - 1), 0)),
                   pl.BlockSpec((tm, D_MODEL), lambda i: (jnp.maximum(i - nc, 0), 0))]
    else:
        xs = [x]
        x_specs = [pl.BlockSpec((tm, D_MODEL), lambda i: (i, 0))]

    def tab_map(i):
        return (jnp.where(i < nc, i, nc + (i - nc) % (SEQ // tm)), 0)

    row = lambda i: (i, 0)
    wide = lambda n: pl.BlockSpec((tm, n), row)
    kv_spec = lambda n: pl.BlockSpec((2, tm, n), lambda i: (0, i, 0))
    sds = lambda n: jax.ShapeDtypeStruct((R, n), BF16)
    kv_sds = lambda n: jax.ShapeDtypeStruct((2, R, n), BF16)
    mrow = _mod_row(0, tm)
    return pl.pallas_call(
        functools.partial(_proj_body, split=split),
        grid=(nt,),
        in_specs=x_specs + [
            pl.BlockSpec((None, 6, D_MODEL), lambda i: (mrow(i), 0, 0)),
            pl.BlockSpec((1, D_MODEL), lambda i: (0, 0)),
            pl.BlockSpec((None, D_MODEL, IN_COLS), lambda i: (layer, 0, 0), pipeline_mode=pl.Buffered(1)),
        ] + [pl.BlockSpec((tm, 128), tab_map)] * 4,
        out_specs=[wide(W), kv_spec(64), kv_spec(128), wide(W), wide(W), wide(W), wide(W), wide(W),
                   wide(3 * D_MODEL)],
        out_shape=[sds(W), kv_sds(64), kv_sds(128), sds(W), sds(W), sds(W), sds(W), sds(W), sds(3 * D_MODEL)],
        compiler_params=_params(("arbitrary",)),
        name="in_proj",
    )(*xs, mod_l, gnorm, w_in_bf, *tabs)


def _attend(sink_ref, q_ref, o_ref, sub, pieces):
    rows = slice(sub * ATTN_BLOCK, (sub + 1) * ATTN_BLOCK)
    groups = [slice(g * ATTN_BLOCK, (g + 1) * ATTN_BLOCK) for g in range(ATTN_GROUP)]
    outs = []
    for h in range(ATTN_KV_HEADS):
        q = q_ref[rows, h * 256:(h + 1) * 256]
        q4 = jnp.concatenate([q[:, g * 64:(g + 1) * 64] for g in range(ATTN_GROUP)], axis=0)
        sinks = [sink_ref[h * ATTN_GROUP + g] * LOG2E for g in range(ATTN_GROUP)]
        k_all = jnp.concatenate([k_ref[h] for k_ref, _, _ in pieces], axis=0)
        v_all = jnp.concatenate([v_ref[h] for _, v_ref, _ in pieces], axis=0)
        s = _dot_nt(q4, k_all)
        cols, c0 = [], 0
        for k_ref, _, ok in pieces:
            n = k_ref.shape[1]
            cols.append(s[:, c0:c0 + n] if ok is None else jnp.where(ok, s[:, c0:c0 + n], NEG))
            c0 += n
        s = jnp.concatenate(cols, axis=1)
        mxs, ps = [], []
        for g, r in enumerate(groups):
            mx = jnp.maximum(jnp.max(s[r], axis=-1, keepdims=True), sinks[g])
            mxs.append(mx)
            ps.append(jnp.exp2(s[r] - mx).astype(BF16))
        oa = _dot(jnp.concatenate(ps, axis=0), v_all)
        for g, r in enumerate(groups):
            den = oa[r, HEAD_DIM:HEAD_DIM + 1] + jnp.exp2(sinks[g] - mxs[g])
            outs.append(oa[r, 0:HEAD_DIM] / den)
    o_ref[rows, :] = jnp.concatenate(outs, axis=1).astype(BF16)


def _attn_lat_body(sink_ref, q_ref, kc_ref, *refs):
    nk = ATTN_QB + 2
    k_refs, vc_ref, v_refs, o_ref = refs[:nk], refs[nk], refs[nk + 1:2 * nk + 1], refs[2 * nk + 1]
    p = pl.program_id(1)
    nr = ATTN_GROUP * ATTN_BLOCK
    ri = lax.broadcasted_iota(jnp.int32, (nr, ATTN_BLOCK), 0) & (ATTN_BLOCK - 1)
    ci = lax.broadcasted_iota(jnp.int32, (nr, ATTN_BLOCK), 1)
    far = 4 * ATTN_BLOCK
    first_prev = ci >= ri + jnp.where(p >= 1, 0, far)
    last_next = ci + jnp.where(p <= SEQ // (ATTN_QB * ATTN_BLOCK) - 2, 0, far) <= ri
    for sub in range(ATTN_QB):
        prev_ok = first_prev if sub == 0 else ci >= ri
        next_ok = last_next if sub == ATTN_QB - 1 else ci <= ri
        _attend(sink_ref, q_ref, o_ref, sub,
                [(kc_ref, vc_ref, None), (k_refs[sub], v_refs[sub], prev_ok), (k_refs[sub + 1], v_refs[sub + 1], None),
                 (k_refs[sub + 2], v_refs[sub + 2], next_ok)])


def _attn_ctx_body(sink_ref, q_ref, kc_ref, vc_ref, o_ref):
    for sub in range(CTX_LEN // ATTN_BLOCK):
        _attend(sink_ref, q_ref, o_ref, sub, [(kc_ref, vc_ref, None)])


def _attention(sink, qa, ka, va, need_ctx):
    nb = SEQ // ATTN_BLOCK
    nstep = nb // ATTN_QB
    tq = ATTN_QB * ATTN_BLOCK
    smem = pl.BlockSpec(memory_space=pltpu.SMEM)

    def loc(delta):
        def f(b, p):
            m = jnp.clip(ATTN_QB * p + delta, 0, nb - 1)
            return (0, RC // ATTN_BLOCK + b * nb + m, 0)
        return f

    def kv_specs(width):
        ctx_spec = pl.BlockSpec((ATTN_KV_HEADS, CTX_LEN, width), lambda b, p: (0, b, 0))
        return [ctx_spec] + [pl.BlockSpec((ATTN_KV_HEADS, ATTN_BLOCK, width), loc(d)) for d in range(-1, ATTN_QB + 1)]

    nloc = ATTN_QB + 3
    oa_l = pl.pallas_call(
        _attn_lat_body,
        grid=(BATCH, nstep),
        in_specs=[smem, pl.BlockSpec((tq, W), lambda b, p: (RC // tq + b * nstep + p, 0))]
                 + kv_specs(HEAD_DIM) + kv_specs(128),
        out_specs=pl.BlockSpec((tq, W), lambda b, p: (b * nstep + p, 0)),
        out_shape=jax.ShapeDtypeStruct((RL, W), BF16),
        compiler_params=_params(("arbitrary", "arbitrary")),
        name="window_attn",
    )(sink, qa, *([ka] * nloc), *([va] * nloc))
    if not need_ctx:
        return None, oa_l
    oa_c = pl.pallas_call(
        _attn_ctx_body,
        grid=(BATCH,),
        in_specs=[smem, pl.BlockSpec((CTX_LEN, W), lambda b: (b, 0)),
                  pl.BlockSpec((ATTN_KV_HEADS, CTX_LEN, HEAD_DIM), lambda b: (0, b, 0)),
                  pl.BlockSpec((ATTN_KV_HEADS, CTX_LEN, 128), lambda b: (0, b, 0))],
        out_specs=pl.BlockSpec((CTX_LEN, W), lambda b: (b, 0)),
        out_shape=jax.ShapeDtypeStruct((RC, W), BF16),
        compiler_params=_params(("arbitrary",)),
        name="ctx_attn",
    )(sink, qa, ka, va)
    return oa_c, oa_l


def _fourier_body(*refs, has_ctx):
    if has_ctx:
        uc_ref, ul_ref, jrev_ref, bdc_ref, bds_ref, w2_ref, w2c_ref, oc_ref, ol_ref, as_ref, mid_ref = refs
    else:
        ul_ref, jrev_ref, bdc_ref, bds_ref, w2_ref, ol_ref, as_ref, mid_ref = refs
    j = pl.program_id(1)
    first = 1 if has_ctx else 0
    hn = SEQ // 2

    if has_ctx:
        @pl.when(j == 0)
        def _():
            u = uc_ref[...]
            a = _dot(u, bdc_ref[...]).astype(BF16)
            s = _dot(u, bds_ref[...]).astype(BF16)
            z = _dot(w2c_ref[...], jnp.concatenate([a, s], axis=0))
            oc_ref[...] = (z * ((CTX_LEN * FOURIER_DIM) ** -0.5)).astype(BF16)

    @pl.when(j == first)
    def _():
        uh = ul_ref[0:hn, :].astype(F32)
        ur = _dot(jrev_ref[...], ul_ref[hn:SEQ, :])
        row = lax.broadcasted_iota(jnp.int32, (hn, W), 0)
        vm = jnp.where(row == 0, 0.0, uh - ur)
        as_ref[0:hn, :] = _dot((uh + ur).astype(BF16), bdc_ref[...]).astype(BF16)
        as_ref[hn:SEQ, :] = _dot(vm.astype(BF16), bds_ref[...]).astype(BF16)
        mid_ref[...] = _dot(ul_ref[hn:hn + 8, :], bdc_ref[...])

    @pl.when(j >= first)
    def _():
        z = _dot(w2_ref[...], as_ref[...])
        k = lax.broadcasted_iota(jnp.int32, (z.shape[0], 1), 0)
        sign = (1 - 2 * (k & 1)).astype(F32)
        ol_ref[...] = ((z + sign * mid_ref[0:1, :]) * ((SEQ * FOURIER_DIM) ** -0.5)).astype(BF16)


def _fourier(fu, dft, need_ctx):
    bdc, bds, w2, w2c = dft
    tr = TR_FOURIER
    nj = SEQ // tr
    hn = SEQ // 2
    first = 1 if need_ctx else 0
    jrev_np = np.zeros((hn, hn), np.float32)
    jrev_np[np.arange(1, hn), hn - np.arange(1, hn)] = 1.0
    jrev = jnp.asarray(jrev_np).astype(BF16)
    full = lambda a: pl.BlockSpec(a.shape, lambda *_: (0,) * a.ndim)
    lat_tile = lambda j: jnp.maximum(j - first, 0)
    ul_spec = pl.BlockSpec((SEQ, W), lambda b, j: (1 + b, 0))
    w2_spec = pl.BlockSpec((tr, SEQ), lambda b, j: (lat_tile(j), 0))
    ol_spec = pl.BlockSpec((tr, W), lambda b, j: (b * nj + lat_tile(j), 0))
    ol_shape = jax.ShapeDtypeStruct((RL, W), BF16)
    if need_ctx:
        in_specs = [pl.BlockSpec((CTX_LEN, W), lambda b, j: (b, 0)), ul_spec, full(jrev), full(bdc), full(bds),
                    w2_spec, full(w2c)]
        out_specs = [pl.BlockSpec((CTX_LEN, W), lambda b, j: (b, 0)), ol_spec]
        out_shape = [jax.ShapeDtypeStruct((RC, W), BF16), ol_shape]
        args = (fu, fu, jrev, bdc, bds, w2, w2c)
    else:
        in_specs = [ul_spec, full(jrev), full(bdc), full(bds), w2_spec]
        out_specs = [ol_spec]
        out_shape = [ol_shape]
        args = (fu, jrev, bdc, bds, w2)
    outs = pl.pallas_call(
        functools.partial(_fourier_body, has_ctx=need_ctx),
        grid=(BATCH, nj + first),
        in_specs=in_specs,
        out_specs=out_specs,
        out_shape=out_shape,
        scratch_shapes=[pltpu.VMEM((SEQ, W), BF16), pltpu.VMEM((8, W), F32)],
        compiler_params=_params(("arbitrary", "arbitrary")),
        name="fourier_mix",
    )(*args)
    return (outs[0], outs[1]) if need_ctx else (None, outs[0])


def _retention_body(qc_ref, kc_ref, vc_ref, ql0_ref, kl0_ref, vl0_ref, ql1_ref, kl1_ref, vl1_ref, g_ref,
                    dcomb_ref, xif_ref, ztf_ref, xib_ref, ztb_ref, gf_ref, gb_ref, mbd_ref, avg_ref,
                    o_ref, os_ref, st_ref):
    j = pl.program_id(1)
    C = RET_CHUNK
    nl = SEQ // C
    lat_refs = ((ql0_ref, kl0_ref, vl0_ref), (ql1_ref, kl1_ref, vl1_ref))

    own = ((lax.broadcasted_iota(jnp.int32, (4 * C, 256), 0) >> (C.bit_length() - 1))
           == (lax.broadcasted_iota(jnp.int32, (4 * C, 256), 1) >> 6))

    def chunk_fwd(bb, q, k, v, r0):
        for gi in range(2):
            sl = slice(gi * 256, (gi + 1) * 256)
            q4, k4, v4 = q[:, sl], k[:, sl], v[:, sl]
            s_prev = st_ref[2 * bb + gi]
            q4f = q4.astype(F32)
            o4 = _dot((q4f * xif_ref[:, sl]).astype(BF16), s_prev.astype(BF16))
            qstack = jnp.where(own, jnp.concatenate([q4f] * 4, axis=0), 0.0).astype(BF16)
            p = (_dot_nt(qstack, k4) * dcomb_ref[gi]).astype(BF16)
            ov = jnp.where(own, _dot(p, v4), 0.0)
            intra = ov[0:C] + ov[C:2 * C] + ov[2 * C:3 * C] + ov[3 * C:4 * C]
            os_ref[pl.ds(r0, C), sl] = o4 + intra
            u = _dot_tn(k4, (v4.astype(F32) * ztf_ref[:, sl]).astype(BF16))
            st_ref[2 * bb + gi] = gf_ref[gi] * s_prev + mbd_ref[...] * u

    def chunk_bwd(bb, q, k, v, r0):
        for gi in range(2):
            sl = slice(gi * 256, (gi + 1) * 256)
            q4, k4, v4 = q[:, sl], k[:, sl], v[:, sl]
            s_prev = st_ref[2 * bb + gi]
            cross = _dot((q4.astype(F32) * xib_ref[:, sl]).astype(BF16), s_prev.astype(BF16))
            os_ref[pl.ds(r0, C), sl] = os_ref[pl.ds(r0, C), sl] + cross
            u = _dot_tn(k4, (v4.astype(F32) * ztb_ref[:, sl]).astype(BF16))
            st_ref[2 * bb + gi] = gb_ref[gi] * s_prev + mbd_ref[...] * u

    def scan(chunk, ctx_order, lat_index):
        st_ref[...] = jnp.zeros_like(st_ref)
        for c in ctx_order:
            for bb in range(2):
                rs = slice(bb * CTX_LEN + c * C, bb * CTX_LEN + (c + 1) * C)
                chunk(bb, qc_ref[rs, :], kc_ref[rs, :], vc_ref[rs, :], bb * CTX_LEN + c * C)

        def body(t, carry):
            r0 = pl.multiple_of(lat_index(t) * C, C)
            for bb, (q_ref, k_ref, v_ref) in enumerate(lat_refs):
                rs = pl.ds(r0, C)
                chunk(bb, q_ref[rs, :], k_ref[rs, :], v_ref[rs, :], 2 * CTX_LEN + bb * SEQ + r0)
            return carry

        lax.fori_loop(0, nl, body, 0)

    @pl.when(j == 0)
    def _():
        scan(chunk_fwd, range(CTX_LEN // C), lambda t: t)
        scan(chunk_bwd, reversed(range(CTX_LEN // C)), lambda t: nl - 1 - t)

    o = os_ref[pl.ds(pl.multiple_of(j * RET_SLAB, RET_SLAB), RET_SLAB), :]
    avg = avg_ref[...]
    oh, ol = _split(o)
    mu = _dot(oh, avg) + _dot(ol, avg)
    d = o - mu
    var = _dot((d * d).astype(BF16), avg)
    o_ref[...] = (g_ref[...].astype(F32) * d * lax.rsqrt(var + GN_EPS)).astype(BF16)


def _retention(qr, kr, vr, gr, rtabs):
    nslab = SEQ // RET_SLAB
    nj = 1 + 2 * nslab

    def out_map(b2, j):
        return (jnp.where(j == 0, b2, RC // RET_SLAB + 2 * b2 * nslab + j - 1), 0)

    ctx = pl.BlockSpec((2 * CTX_LEN, W), lambda b2, j: (b2, 0))
    lat = lambda k: pl.BlockSpec((SEQ, W), lambda b2, j: (1 + 2 * b2 + k, 0))
    full = lambda a: pl.BlockSpec(a.shape, lambda *_: (0,) * a.ndim)
    avg = jnp.asarray(np.kron(np.eye(RET_HEADS), np.full((RET_DK, RET_DK), 1.0 / RET_DK)).astype(np.float32)).astype(BF16)
    mbd = jnp.asarray(np.kron(np.eye(4), np.ones((RET_DK, RET_DK))).astype(np.float32))
    tabs = list(rtabs) + [mbd, avg]
    return pl.pallas_call(
        _retention_body,
        grid=(BATCH // 2, nj),
        in_specs=[ctx, ctx, ctx, lat(0), lat(0), lat(0), lat(1), lat(1), lat(1),
                  pl.BlockSpec((RET_SLAB, W), out_map)] + [full(t) for t in tabs],
        out_specs=pl.BlockSpec((RET_SLAB, W), out_map),
        out_shape=jax.ShapeDtypeStruct((R, W), BF16),
        scratch_shapes=[pltpu.VMEM((2 * (CTX_LEN + SEQ), W), F32), pltpu.VMEM((4, 256, 256), F32)],
        compiler_params=_params(("arbitrary", "arbitrary")),
        name="retention",
    )(qr, kr, vr, qr, kr, vr, qr, kr, vr, gr, *tabs)


def _pack_pair(a, b):
    ua = lax.bitcast_convert_type(a, jnp.uint32) >> 16
    ub = lax.bitcast_convert_type(b, jnp.uint32) & jnp.uint32(0xFFFF0000)
    return ua | ub


def _unpack_pair(w):
    a = lax.bitcast_convert_type(w << 16, F32)
    b = lax.bitcast_convert_type(w & jnp.uint32(0xFFFF0000), F32)
    return a, b


def _slot_onehot(s0, s1, n):
    srow = lax.broadcasted_iota(jnp.int32, (n, s0.shape[1]), 0)
    p0 = jnp.where(srow == s0, 1.0, 0.0).astype(BF16)
    p1 = jnp.where(srow == s1, 1.0, 0.0).astype(BF16)
    return p0, p1


def _pick(first, a_ref, b_ref):
    a = a_ref[...]
    flag = jnp.zeros(a.shape, jnp.int32) + first.astype(jnp.int32)
    return jnp.where(flag > 0, a, b_ref[...])


def _merge_body(*refs, layer0):
    if layer0:
        oac_ref, oal_ref, ofc_ref, ofl_ref, rt_ref, gm_ref, xc_ref, xl_ref = refs[:8]
        rest = refs[8:]
        is_ctx = pl.program_id(0) < RC // TM_MERGE
        oa_in = _pick(is_ctx, oac_ref, oal_ref)
        of_in = _pick(is_ctx, ofc_ref, ofl_ref)
        x_in = _pick(is_ctx, xc_ref, xl_ref)
    else:
        oal_ref, ofl_ref, rt_ref, gm_ref, x_ref = refs[:5]
        rest = refs[5:]
        oa_in = oal_ref[...]
        of_in = ofl_ref[...]
        x_in = x_ref[...]
    (mod_ref, gn_ref, wba_ref, wbf_ref, wbr_ref, wout_ref, wrh_ref, wrl_ref, br_ref, tri_ref, ltri_ref,
     xo_ref, xs_ref, ro_ref, nch_ref) = rest
    gm = gm_ref[...].astype(F32)
    z = (gm[:, 0:D_MODEL] * _dot(oa_in, wba_ref[...])
         + gm[:, D_MODEL:2 * D_MODEL] * _dot(of_in, wbf_ref[...])
         + gm[:, 2 * D_MODEL:3 * D_MODEL] * _dot(rt_ref[...], wbr_ref[...]))
    y = _dot(z.astype(BF16), wout_ref[...])
    x = x_in + mod_ref[2:3, :] * y
    xo_ref[...] = x
    ms = jnp.mean(x * x, axis=-1, keepdims=True)
    hn = x * lax.rsqrt(ms + NORM_EPS) * gn_ref[...]
    h2 = hn * (1.0 + mod_ref[4:5, :]) + mod_ref[3:4, :]
    hh, hl = _split(h2)
    wh, wl = wrh_ref[...], wrl_ref[...]
    lg = _dot_nt(wh, hh) + _dot_nt(wh, hl) + _dot_nt(wl, hh) + br_ref[...]
    tm = lg.shape[1]
    row8 = lax.broadcasted_iota(jnp.int32, (8, tm), 0)
    lgg = lg[0:8, :]
    mg = jnp.max(lgg, axis=0, keepdims=True)
    grp = jnp.min(jnp.where(lgg == mg, row8, 8), axis=0, keepdims=True)
    pg = 1.0 / jnp.sum(jnp.exp(lgg - mg), axis=0, keepdims=True)
    lin = jnp.zeros((8, tm), F32)
    for g in range(N_GROUPS):
        lin = jnp.where(grp == g, lg[8 + 8 * g:16 + 8 * g, :], lin)
    v1 = jnp.max(lin, axis=0, keepdims=True)
    i1 = jnp.min(jnp.where(lin == v1, row8, 8), axis=0, keepdims=True)
    rest = jnp.where(row8 == i1, -jnp.inf, lin)
    v2 = jnp.max(rest, axis=0, keepdims=True)
    i2 = jnp.min(jnp.where(rest == v2, row8, 8), axis=0, keepdims=True)
    e2 = jnp.exp(v2 - v1)
    w1 = pg / (1.0 + e2)
    w2 = pg * e2 / (1.0 + e2)
    e_1 = grp * EXPERTS_PER_GROUP + i1
    e_2 = grp * EXPERTS_PER_GROUP + i2

    row32 = lax.broadcasted_iota(jnp.int32, (N_EXPERTS, tm), 0)
    oh0 = jnp.where(row32 == e_1, 1.0, 0.0)
    oh1 = jnp.where(row32 == e_2, 1.0, 0.0)
    tri = tri_ref[...]
    cum0 = _dot(oh0.astype(BF16), tri)
    cum1 = _dot(oh1.astype(BF16), tri)
    tot0 = jnp.sum(oh0, axis=1, keepdims=True)
    tot1 = jnp.sum(oh1, axis=1, keepdims=True)
    nch = ((tot0 + tot1).astype(jnp.int32) + (CHUNK - 1)) >> CHUNK_SHIFT
    nch_b = jnp.broadcast_to(nch.astype(F32), (N_EXPERTS, 128))
    nch_ref[...] = nch_b.astype(jnp.int32)
    base = CHUNK * _dot(ltri_ref[...], nch_b.astype(BF16))[:, 0:1]
    s0 = jnp.sum(oh0 * (base + cum0), axis=0, keepdims=True).astype(jnp.int32)
    s1 = jnp.sum(oh1 * (base + tot0 + cum1), axis=0, keepdims=True).astype(jnp.int32)
    p0, p1 = _slot_onehot(s0, s1, SLOTS)
    xs = _dot(p0 + p1, hh)
    xs_ref[:, 0:D_MODEL // 2] = _pack_pair(xs[:, 0:D_MODEL // 2], xs[:, D_MODEL // 2:D_MODEL])

    def wrows(w):
        hi, lo = _split(w)
        return jnp.where(row8 == 0, hi.astype(F32), jnp.where(row8 == 1, lo.astype(F32), 0.0)).astype(BF16)

    wc = _dot_nt(p0, wrows(w1)) + _dot_nt(p1, wrows(w2))
    wcol = jnp.broadcast_to(wc[:, 0:1] + wc[:, 1:2], (SLOTS, 128))
    xs_ref[:, D_MODEL // 2:XS_COLS] = lax.bitcast_convert_type(wcol, jnp.uint32)
    s0f, s1f = s0.astype(F32), s1.astype(F32)
    ro_ref[...] = jnp.where(row8 == 0, s0f, jnp.where(row8 == 1, s1f, 0.0))


def _merge(oa_c, oa_l, of_c, of_l, ret, gm, xs_in, mod_l, gnorm, wba, wbf, wbr, wout, wrh, wrl, brb, layer0):
    tm = TM_MERGE
    row0 = 0 if layer0 else RC
    rm = R - row0
    nt = rm // tm
    off = row0 // tm
    nc = RC // tm
    src = lambda n: pl.BlockSpec((tm, n), lambda i: (i + off, 0))
    dst = lambda n: pl.BlockSpec((tm, n), lambda i: (i, 0))
    ctx_rows = lambda n: pl.BlockSpec((tm, n), lambda i: (jnp.minimum(i, nc - 1), 0))
    lat_rows = lambda n: pl.BlockSpec((tm, n), lambda i: (jnp.maximum(i - nc, 0), 0))
    full = lambda a: pl.BlockSpec(a.shape, lambda *_: (0,) * a.ndim, pipeline_mode=pl.Buffered(1))
    mrow = _mod_row(row0, tm)
    tri = jnp.asarray(np.triu(np.ones((tm, tm), np.float32), 1)).astype(BF16)
    ltri = jnp.asarray(np.tril(np.ones((N_EXPERTS, N_EXPERTS), np.float32), -1)).astype(BF16)
    if layer0:
        acts = [oa_c, oa_l, of_c, of_l, ret, gm, xs_in[0], xs_in[1]]
        act_specs = [ctx_rows(W), lat_rows(W), ctx_rows(W), lat_rows(W), src(W), src(3 * D_MODEL),
                     ctx_rows(D_MODEL), lat_rows(D_MODEL)]
    else:
        acts = [oa_l, of_l, ret, gm, xs_in]
        act_specs = [dst(W), dst(W), src(W), src(3 * D_MODEL), src(D_MODEL)]
    return pl.pallas_call(
        functools.partial(_merge_body, layer0=layer0),
        grid=(nt,),
        in_specs=act_specs + [
                  pl.BlockSpec((None, 6, D_MODEL), lambda i: (mrow(i), 0, 0)),
                  full(gnorm), full(wba), full(wbf), full(wbr), full(wout), full(wrh), full(wrl), full(brb),
                  full(tri), full(ltri)],
        out_specs=[dst(D_MODEL), pl.BlockSpec((SLOTS, XS_COLS), lambda i: (i, 0)),
                   pl.BlockSpec((8, tm), lambda i: (0, i)),
                   pl.BlockSpec((None, N_EXPERTS, 128), lambda i: (i, 0, 0))],
        out_shape=[jax.ShapeDtypeStruct((rm, D_MODEL), F32),
                   jax.ShapeDtypeStruct((nt * SLOTS, XS_COLS), jnp.uint32),
                   jax.ShapeDtypeStruct((8, rm), F32),
                   jax.ShapeDtypeStruct((nt, N_EXPERTS, 128), jnp.int32)],
        compiler_params=_params(("arbitrary",)),
        name="merge_router",
    )(*acts, mod_l, gnorm, wba, wbf, wbr, wout, wrh, wrl, brb, tri, ltri)


def _moe_plan(nch, nb):
    nt = nch.shape[0]
    choff = jnp.cumsum(nch, axis=1) - nch
    used_ch = jnp.sum(nch, axis=1)
    cum_t = jnp.cumsum(nch, axis=0)
    tot = cum_t[-1]
    ptot = (tot + CPB - 1) // CPB * CPB
    pend = jnp.cumsum(ptot)
    pstart = pend - ptot
    n_used = pend[-1] // CPB
    blk = jnp.arange(nb, dtype=jnp.int32)
    lane = jnp.arange(CPB, dtype=jnp.int32)
    blk_e = jnp.minimum(jnp.sum((blk[:, None] * CPB >= pend[None, :]).astype(jnp.int32), axis=1), N_EXPERTS - 1)
    oe = (blk_e[:, None] == jnp.arange(N_EXPERTS, dtype=jnp.int32)[None, :]).astype(jnp.int32)
    sel = lambda tab: jnp.sum(oe[:, :, None] * tab.T[None, :, :], axis=1)
    pstart_b = jnp.sum(oe * pstart[None, :], axis=1)
    tot_b = jnp.sum(oe * tot[None, :], axis=1)
    cum_b, nch_b, choff_b = sel(cum_t), sel(nch), sel(choff)
    i = blk[:, None] * CPB + lane[None, :] - pstart_b[:, None]
    valid = (i < tot_b[:, None]) & (blk[:, None] < n_used)
    t = jnp.minimum(jnp.sum((i[:, :, None] >= cum_b[:, None, :]).astype(jnp.int32), axis=2), nt - 1)
    tiles = jnp.arange(nt, dtype=jnp.int32)[None, None, :]
    before = jnp.sum(jnp.where(tiles < t[:, :, None], nch_b[:, None, :], 0), axis=2)
    coff = jnp.sum(jnp.where(tiles == t[:, :, None], choff_b[:, None, :], 0), axis=2)
    row = t * SLOTS + CHUNK * (coff + i - before)
    src = jnp.where(valid, row, SLOTS - CHUNK)
    dummy = nt * SLOTS + CHUNK * ((blk[:, None] % 2) * CPB + lane[None, :])
    dst = jnp.where(valid, row, dummy)
    blk_start = jnp.concatenate([pstart, pend[-1:]]) // CPB
    return (blk_start.astype(jnp.int32), n_used.astype(jnp.int32).reshape(1), src.reshape(-1).astype(jnp.int32),
            dst.reshape(-1).astype(jnp.int32), used_ch.astype(jnp.int32))


def _ffn_body(bs_ref, nu_ref, src_ref, dst_ref, uc_ref, xs_ref, wg_ref, wu_ref, wd_ref, ys_ref,
              xbuf, ybuf, zbuf, wgb, wub, wdb, sem_in, sem_out, sem_z, *, nt):
    e = pl.program_id(0)
    nu = nu_ref[0]
    half = D_MODEL // 2

    def gather(blk, sl):
        for c in range(CPB):
            r = pl.multiple_of(src_ref[blk * CPB + c], CHUNK)
            pltpu.make_async_copy(xs_ref.at[pl.ds(r, CHUNK)], xbuf.at[sl, pl.ds(c * CHUNK, CHUNK)],
                                  sem_in.at[sl]).start(priority=1)

    def scatter(blk, sl):
        for c in range(CPB):
            r = pl.multiple_of(dst_ref[blk * CPB + c], CHUNK)
            pltpu.make_async_copy(ybuf.at[sl, pl.ds(c * CHUNK, CHUNK)], ys_ref.at[pl.ds(r, CHUNK)],
                                  sem_out.at[sl]).start(priority=1)

    def wait_gather(sl):
        pltpu.make_async_copy(xs_ref.at[pl.ds(0, MOE_BM)], xbuf.at[sl], sem_in.at[sl]).wait()

    def wait_scatter(sl):
        pltpu.make_async_copy(ybuf.at[sl], ys_ref.at[pl.ds(0, MOE_BM)], sem_out.at[sl]).wait()

    def zero_copy(r):
        return pltpu.make_async_copy(zbuf, ys_ref.at[pl.ds(pl.multiple_of(r, CHUNK), CHUNK)], sem_z)

    @pl.when(e == 0)
    def _():
        zbuf[...] = jnp.zeros_like(zbuf)

        def tails(fn):
            def per_tile(t, carry):
                def per_chunk(c, carry2):
                    fn(t * SLOTS + c * CHUNK)
                    return carry2
                lax.fori_loop(uc_ref[t], SLOTS // CHUNK, per_chunk, 0)
                return carry
            lax.fori_loop(0, nt, per_tile, 0)
            for c in range(2 * CPB):
                fn(nt * SLOTS + c * CHUNK)

        tails(lambda r: zero_copy(r).start())
        tails(lambda r: zero_copy(r).wait())
        gather(0, 0)
        for d in range(1, GATHER_DEPTH - 1):
            @pl.when(nu > d)
            def _():
                gather(d, d)

    b0, b1 = bs_ref[e], bs_ref[e + 1]

    @pl.when(b1 > b0)
    def _():
        wgb[...] = wg_ref[...].astype(BF16)
        wub[...] = wu_ref[...].astype(BF16)
        wdb[...] = wd_ref[...].astype(BF16)

        def block(b, carry):
            slot = b % 2
            xslot = lax.rem(b, GATHER_DEPTH)

            @pl.when(b + GATHER_DEPTH - 1 < nu)
            def _():
                gather(b + GATHER_DEPTH - 1, lax.rem(b + GATHER_DEPTH - 1, GATHER_DEPTH))

            wait_gather(xslot)

            @pl.when(b >= 2)
            def _():
                wait_scatter(slot)

            xw = xbuf[xslot]
            xa, xb = _unpack_pair(xw[:, 0:half])
            x = jnp.concatenate([xa, xb], axis=1).astype(BF16)
            wt = lax.bitcast_convert_type(xw[:, half:XS_COLS], F32)
            g = _dot(x, wgb[...])
            u = _dot(x, wub[...])
            hmid = (g * _sigmoid(g) * u).astype(BF16)
            y = _dot(hmid, wdb[...]) * jnp.concatenate([wt] * (D_MODEL // 128), axis=1)
            yb = y.astype(BF16).astype(F32)
            ybuf[slot] = _pack_pair(yb[:, 0:half], yb[:, half:D_MODEL])
            scatter(b, slot)
            return carry

        lax.fori_loop(b0, b1, block, 0)

    @pl.when(e == N_EXPERTS - 1)
    def _():
        wait_scatter((nu - 1) % 2)

        @pl.when(nu >= 2)
        def _():
            wait_scatter(nu % 2)


def _ffn(plan, xs, w_g, w_u, w_d, layer, nt):
    wmap = lambda e, *_: (layer, e, 0, 0)
    half = D_MODEL // 2
    grid_spec = pltpu.PrefetchScalarGridSpec(
        num_scalar_prefetch=5,
        grid=(N_EXPERTS,),
        in_specs=[pl.BlockSpec(memory_space=pl.ANY),
                  pl.BlockSpec((None, None, D_MODEL, EXPERT_HIDDEN), wmap),
                  pl.BlockSpec((None, None, D_MODEL, EXPERT_HIDDEN), wmap),
                  pl.BlockSpec((None, None, EXPERT_HIDDEN, D_MODEL), wmap)],
        out_specs=pl.BlockSpec(memory_space=pl.ANY),
        scratch_shapes=[pltpu.VMEM((GATHER_DEPTH, MOE_BM, XS_COLS), jnp.uint32),
                        pltpu.VMEM((2, MOE_BM, half), jnp.uint32),
                        pltpu.VMEM((CHUNK, half), jnp.uint32),
                        pltpu.VMEM((D_MODEL, EXPERT_HIDDEN), BF16), pltpu.VMEM((D_MODEL, EXPERT_HIDDEN), BF16),
                        pltpu.VMEM((EXPERT_HIDDEN, D_MODEL), BF16),
                        pltpu.SemaphoreType.DMA((GATHER_DEPTH,)), pltpu.SemaphoreType.DMA((2,)),
                        pltpu.SemaphoreType.DMA(())],
    )
    return pl.pallas_call(
        functools.partial(_ffn_body, nt=nt),
        grid_spec=grid_spec,
        out_shape=jax.ShapeDtypeStruct((nt * SLOTS + 2 * CPB * CHUNK, half), jnp.uint32),
        compiler_params=_params(("arbitrary",)),
        name="moe_experts",
    )(*plan, xs, w_g, w_u, w_d)


def _combine_body(ys_ref, ro_ref, x_ref, mod_ref, gn_ref, o_ref, *, final):
    s = ro_ref[...]
    p0, p1 = _slot_onehot(s[0:1, :].astype(jnp.int32), s[1:2, :].astype(jnp.int32), SLOTS)
    ya, yb = _unpack_pair(ys_ref[...])
    y = jnp.concatenate([ya, yb], axis=1).astype(BF16)
    f = _dot_tn(p0 + p1, y)
    x = x_ref[...] + mod_ref[5:6, :] * f
    if final:
        ms = jnp.mean(x * x, axis=-1, keepdims=True)
        x = x * lax.rsqrt(ms + NORM_EPS) * gn_ref[...]
    o_ref[...] = x


def _combine(ys, route, x, mod_l, gnorm, row0, final):
    tm = TM_MERGE
    rm = x.shape[0]
    mrow = _mod_row(row0, tm)
    return pl.pallas_call(
        functools.partial(_combine_body, final=final),
        grid=(rm // tm,),
        in_specs=[pl.BlockSpec((SLOTS, D_MODEL // 2), lambda i: (i, 0)),
                  pl.BlockSpec((8, tm), lambda i: (0, i)),
                  pl.BlockSpec((tm, D_MODEL), lambda i: (i, 0)),
                  pl.BlockSpec((None, 6, D_MODEL), lambda i: (mrow(i), 0, 0)),
                  pl.BlockSpec((1, D_MODEL), lambda i: (0, 0))],
        out_specs=pl.BlockSpec((tm, D_MODEL), lambda i: (i, 0)),
        out_shape=jax.ShapeDtypeStruct((rm, D_MODEL), F32),
        compiler_params=_params(("arbitrary",)),
        name="moe_combine",
    )(ys, route, x, mod_l, gnorm)


def _moe(xs, route, nch3, x, mod_l, gnorm, w_g, w_u, w_d, row0, final, layer):
    nt = nch3.shape[0]
    max_chunks = nt * ((2 * TM_MERGE + N_EXPERTS * (CHUNK - 1)) // CHUNK)
    nb = -(-max_chunks // CPB) + N_EXPERTS
    plan = _moe_plan(nch3[:, :, 0], nb)
    ys = _ffn(plan, xs, w_g, w_u, w_d, layer, nt)
    return _combine(ys, route, x, mod_l, gnorm, row0, final)


def kernel(x, c, ctx, c_ctx, norm_mix, norm_ffn, w_ada, b_ada, w_in, attn_sink, ret_decay_fwd, ret_decay_bwd,
           w_branch_attn, w_branch_fourier, w_branch_ret, w_out, w_router_group, b_router_group,
           w_router_expert, b_router_expert, w_exp_gate, w_exp_up, w_exp_down, norm_final):
    tabs = [jnp.asarray(t) for t in _rope_tables()]
    dft = [jnp.asarray(t).astype(BF16) for t in _dft_tables()]

    cc = jnp.zeros((MOD_ROWS, D_MODEL), F32).at[0:BATCH].set(c).at[CTX_MOD_ROW].set(c_ctx)
    mod = _ada(cc, w_ada, b_ada).reshape(DEPTH, MOD_ROWS, 6, D_MODEL)

    xf = (ctx.reshape(RC, D_MODEL), x.reshape(RL, D_MODEL))
    w_in_bf = w_in.astype(BF16)
    for l in range(DEPTH):
        need_ctx = l < DEPTH - 1
        row0 = 0 if need_ctx else RC
        mod_l = mod[l]
        qa, ka, va, qr, kr, vr, gr, fu, gm = _proj(xf, mod_l, norm_mix[l][None, :], w_in_bf, tabs, l)
        oa_c, oa_l = _attention(attn_sink[l], qa, ka, va, need_ctx)
        of_c, of_l = _fourier(fu, dft, need_ctx)
        ret = _retention(qr, kr, vr, gr, _retention_tables(ret_decay_fwd[l], ret_decay_bwd[l]))
        wr = jnp.zeros((ROUTER_ROWS, D_MODEL), F32)
        wr = wr.at[0:N_GROUPS].set(w_router_group[l].T).at[8:8 + N_EXPERTS].set(w_router_expert[l].T)
        br = jnp.full((ROUTER_ROWS,), NEG, F32)
        br = br.at[0:N_GROUPS].set(b_router_group[l]).at[8:8 + N_EXPERTS].set(b_router_expert[l])
        wrh, wrl = _split(wr)
        brb = jnp.broadcast_to(br[:, None], (ROUTER_ROWS, TM_MERGE))
        x_mid, xs, route, nch3 = _merge(oa_c, oa_l, of_c, of_l, ret, gm, xf, mod_l, norm_ffn[l][None, :],
                                        w_branch_attn[l].astype(BF16), w_branch_fourier[l].astype(BF16),
                                        w_branch_ret[l].astype(BF16), w_out[l].astype(BF16), wrh, wrl, brb,
                                        need_ctx)
        final = l == DEPTH - 1
        xf = _moe(xs, route, nch3, x_mid, mod_l, norm_final[None, :], w_exp_gate, w_exp_up, w_exp_down,
                  row0, final, l)
    return xf.reshape(BATCH, SEQ, D_MODEL)
```

```python
import functools

import numpy as np
import jax
import jax.numpy as jnp
from jax import lax
from jax.experimental import pallas as pl
from jax.experimental.pallas import tpu as pltpu

F32 = jnp.float32
BF16 = jnp.bfloat16

D_MODEL = 1024
BATCH = 8
SEQ = 2048
DEPTH = 2
CTX_LEN = 256
GRID_W = 64
HEAD_DIM = 64
ATTN_HEADS = 8
ATTN_KV_HEADS = 2
ATTN_GROUP = ATTN_HEADS // ATTN_KV_HEADS
ATTN_GW = ATTN_GROUP * HEAD_DIM
ATTN_BLOCK = 128
ATTN_QB = 16
RET_HEADS = 8
RET_DK = 64
RET_GH = 4
RET_GW = RET_GH * RET_DK
RET_NG = RET_HEADS // RET_GH
RET_CHUNK = 256
FOURIER_GROUPS = 4
FOURIER_DIM = 128
N_GROUPS = 4
EXPERTS_PER_GROUP = 8
N_EXPERTS = N_GROUPS * EXPERTS_PER_GROUP
EXPERT_HIDDEN = 512
ROPE_BASE = 10000.0
NORM_EPS = 1e-6
GN_EPS = 1e-5

W = 512
IN_COLS = 6400
RC = BATCH * CTX_LEN
RL = BATCH * SEQ
R = RC + RL
MOD_ROWS = 16
CTX_MOD_ROW = 8

VMEM_LIMIT = 52 * 1024 * 1024

TM_PROJ = 512
TM_MERGE = 512
TN_ADA = 1536
TR_FOURIER = 2048
RET_SLAB = 2 * CTX_LEN
MOE_BM = 512
CHUNK = 8
CHUNK_SHIFT = 3
CPB = MOE_BM // CHUNK
GATHER_DEPTH = 3
SLOTS = 1280
XS_COLS = D_MODEL // 2 + 128
NEG = -1e30
LOG2E = 1.4426950408889634
ROUTER_ROWS = 40


def _dot(a, b):
    return jnp.dot(a, b, preferred_element_type=F32)


def _dot_nt(a, b):
    return lax.dot_general(a, b, (((1,), (1,)), ((), ())), preferred_element_type=F32)


def _dot_tn(a, b):
    return lax.dot_general(a, b, (((0,), (0,)), ((), ())), preferred_element_type=F32)


def _split(x):
    hi = x.astype(BF16)
    lo = (x - hi.astype(F32)).astype(BF16)
    return hi, lo


def _sigmoid(x):
    return 1.0 / (1.0 + jnp.exp(-x))


def _params(sem, vmem=VMEM_LIMIT):
    return pltpu.CompilerParams(dimension_semantics=sem, vmem_limit_bytes=vmem)


def _mod_row(row0, tm):
    def f(i):
        g0 = i * tm + row0
        return jnp.where(g0 < RC, CTX_MOD_ROW, (g0 - RC) // SEQ)
    return f


def _rope_tables():
    pos = np.arange(SEQ, dtype=np.float64)
    row = np.floor(pos / GRID_W)
    col = pos % GRID_W

    def cs(p, nf):
        inv = ROPE_BASE ** (-np.arange(nf, dtype=np.float64) / nf)
        ang = p[:, None] * inv[None, :]
        return np.cos(ang), np.sin(ang)

    rc, rs = cs(row, HEAD_DIM // 4)
    cc, cs_ = cs(col, HEAD_DIM // 4)
    cos_a = np.concatenate([rc, rc, cc, cc], axis=1)
    sin_a = np.concatenate([-rs, rs, -cs_, cs_], axis=1)
    tc, ts = cs(pos, RET_DK // 2)
    cos_r = np.concatenate([tc, tc], axis=1)
    sin_r = np.concatenate([-ts, ts], axis=1)

    def full(t, ident):
        t2 = np.concatenate([t, t], axis=1)
        return np.concatenate([np.full_like(t2, ident), t2], axis=0).astype(np.float32)

    return full(cos_a, 1.0), full(sin_a, 0.0), full(cos_r, 1.0), full(sin_r, 0.0)


def _dft_tables():
    def cs(n):
        k = np.arange(n, dtype=np.int64)
        m = (k[:, None] * k[None, :]) % n
        ang = 2.0 * np.pi * m.astype(np.float64) / n
        return np.cos(ang), np.sin(ang)

    c128, s128 = cs(FOURIER_DIM)
    eye = np.eye(FOURIER_GROUPS)
    bdc = np.kron(eye, c128).astype(np.float32)
    bds = np.kron(eye, s128).astype(np.float32)
    cn, sn = cs(SEQ)
    hn = SEQ // 2
    w2 = np.concatenate([cn[:, :hn], -sn[:, :hn]], axis=1).astype(np.float32)
    cl, sl = cs(CTX_LEN)
    w2c = np.concatenate([cl, -sl], axis=1).astype(np.float32)
    return bdc, bds, w2, w2c


def _retention_tables(dec_f, dec_b):
    lg_f = jax.nn.log_sigmoid(dec_f.astype(F32))
    lg_b = jax.nn.log_sigmoid(dec_b.astype(F32))
    i = jnp.arange(RET_CHUNK)
    diff = (i[:, None] - i[None, :]).astype(F32)
    fwd = jnp.exp(jnp.maximum(diff, 0.0)[None] * lg_f[:, None, None])
    bwd = jnp.exp(jnp.maximum(-diff, 0.0)[None] * lg_b[:, None, None])
    dcomb = jnp.where((diff >= 0)[None], fwd, bwd).reshape(RET_NG, RET_GH * RET_CHUNK, RET_CHUNK)
    fi = i.astype(F32)
    lanes = lambda t: jnp.repeat(t, RET_DK, axis=1)
    xi_f = lanes(jnp.exp((fi + 1.0)[:, None] * lg_f[None, :]))
    zt_f = lanes(jnp.exp((RET_CHUNK - 1 - fi)[:, None] * lg_f[None, :]))
    xi_b = lanes(jnp.exp((RET_CHUNK - fi)[:, None] * lg_b[None, :]))
    zt_b = lanes(jnp.exp(fi[:, None] * lg_b[None, :]))
    g_f = jnp.repeat(jnp.exp(RET_CHUNK * lg_f), RET_DK).reshape(RET_NG, RET_GW, 1)
    g_b = jnp.repeat(jnp.exp(RET_CHUNK * lg_b), RET_DK).reshape(RET_NG, RET_GW, 1)
    g_f = jnp.broadcast_to(g_f, (RET_NG, RET_GW, RET_GW))
    g_b = jnp.broadcast_to(g_b, (RET_NG, RET_GW, RET_GW))
    return dcomb, xi_f, zt_f, xi_b, zt_b, g_f, g_b


def _ada_body(c_ref, w_ref, b_ref, o_ref):
    c = c_ref[...]
    s = c * _sigmoid(c)
    sh, sl = _split(s)
    wh, wl = _split(w_ref[...])
    o_ref[...] = _dot(sh, wh) + _dot(sl, wh) + _dot(sh, wl) + b_ref[...]


def _ada(cc, w_ada, b_ada):
    nt = 6 * D_MODEL // TN_ADA
    return pl.pallas_call(
        _ada_body,
        grid=(DEPTH, nt),
        in_specs=[
            pl.BlockSpec((MOD_ROWS, D_MODEL), lambda l, j: (0, 0)),
            pl.BlockSpec((None, D_MODEL, TN_ADA), lambda l, j: (l, 0, j)),
            pl.BlockSpec((None, 1, TN_ADA), lambda l, j: (l, 0, j)),
        ],
        out_specs=pl.BlockSpec((None, MOD_ROWS, TN_ADA), lambda l, j: (l, 0, j)),
        out_shape=jax.ShapeDtypeStruct((DEPTH, MOD_ROWS, 6 * D_MODEL), F32),
        compiler_params=_params(("arbitrary", "arbitrary")),
        name="ada_mod",
    )(cc, w_ada, b_ada.reshape(DEPTH, 1, 6 * D_MODEL))


def _rope(xc, cos, sin, half):
    fwd = pltpu.roll(xc, 128 - half, axis=1)
    bwd = pltpu.roll(xc, half, axis=1)
    lane = lax.broadcasted_iota(jnp.int32, xc.shape, 1)
    first = (lane & (2 * half - 1)) < half
    return xc * cos + jnp.where(first, fwd, bwd) * sin


def _proj_body(*refs, split):
    if split:
        x = _pick(pl.program_id(0) < RC // TM_PROJ, refs[0], refs[1])
        refs = refs[2:]
    else:
        x = refs[0][...]
        refs = refs[1:]
    (mod_ref, gn_ref, w_ref, ca_ref, sa_ref, cr_ref, sr_ref,
     qa_ref, ka_ref, va_ref, qr_ref, kr_ref, vr_ref, gr_ref, fu_ref, gm_ref) = refs
    ms = jnp.mean(x * x, axis=-1, keepdims=True)
    y = x * lax.rsqrt(ms + NORM_EPS) * gn_ref[...]
    h = y * (1.0 + mod_ref[1:2, :]) + mod_ref[0:1, :]
    hb = h.astype(BF16)

    def proj(c0, width):
        return _dot(hb, w_ref[:, c0:c0 + width])

    ca, sa, cr, sr = ca_ref[...], sa_ref[...], cr_ref[...], sr_ref[...]

    for c in range(3):
        gm_ref[:, c * D_MODEL:(c + 1) * D_MODEL] = _sigmoid(proj(3328 + c * D_MODEL, D_MODEL)).astype(BF16)
    g = proj(2304, W)
    gr_ref[...] = (g * _sigmoid(g)).astype(BF16)
    qa = proj(0, W) * (HEAD_DIM ** -0.5 * LOG2E)
    for c in range(W // 128):
        qa_ref[:, c * 128:(c + 1) * 128] = _rope(qa[:, c * 128:(c + 1) * 128], ca, sa, 16).astype(BF16)
    kv = proj(W, 256)
    ka = _rope(kv[:, 0:128], ca, sa, 16).astype(BF16)
    ka_ref[0] = ka[:, 0:64]
    ka_ref[1] = ka[:, 64:128]
    va = kv[:, 128:256].astype(BF16)
    ones_col = jnp.where(lax.broadcasted_iota(jnp.int32, (va.shape[0], 64), 1) == 0, 1.0, 0.0).astype(BF16)
    va_ref[0] = jnp.concatenate([va[:, 0:64], ones_col], axis=1)
    va_ref[1] = jnp.concatenate([va[:, 64:128], ones_col], axis=1)
    qr = proj(768, W)
    kr = proj(1280, W) * (RET_DK ** -0.5)
    for c in range(W // 128):
        sl = slice(c * 128, (c + 1) * 128)
        qr_ref[:, sl] = _rope(qr[:, sl], cr, sr, 32).astype(BF16)
        kr_ref[:, sl] = _rope(kr[:, sl], cr, sr, 32).astype(BF16)
    vr_ref[...] = proj(1792, W).astype(BF16)
    fu_ref[...] = proj(2816, W).astype(BF16)


def _proj(x, mod_l, gnorm, w_in_bf, tabs, layer):
    tm = TM_PROJ
    nt = R // tm
    nc = RC // tm
    split = isinstance(x, tuple)
    if split:
        xs = list(x)
        x_specs = [pl.BlockSpec((tm, D_MODEL), lambda i: (jnp.minimum(i, nc - 1), 0)),
                   pl.BlockSpec((tm, D_MODEL), lambda i: (jnp.maximum(i - nc, 0), 0))]
    else:
        xs = [x]
        x_specs = [pl.BlockSpec((tm, D_MODEL), lambda i: (i, 0))]

    def tab_map(i):
        return (jnp.where(i < nc, i, nc + (i - nc) % (SEQ // tm)), 0)

    row = lambda i: (i, 0)
    wide = lambda n: pl.BlockSpec((tm, n), row)
    kv_spec = lambda n: pl.BlockSpec((2, tm, n), lambda i: (0, i, 0))
    sds = lambda n: jax.ShapeDtypeStruct((R, n), BF16)
    kv_sds = lambda n: jax.ShapeDtypeStruct((2, R, n), BF16)
    mrow = _mod_row(0, tm)
    return pl.pallas_call(
        functools.partial(_proj_body, split=split),
        grid=(nt,),
        in_specs=x_specs + [
            pl.BlockSpec((None, 6, D_MODEL), lambda i: (mrow(i), 0, 0)),
            pl.BlockSpec((1, D_MODEL), lambda i: (0, 0)),
            pl.BlockSpec((None, D_MODEL, IN_COLS), lambda i: (layer, 0, 0), pipeline_mode=pl.Buffered(1)),
        ] + [pl.BlockSpec((tm, 128), tab_map)] * 4,
        out_specs=[wide(W), kv_spec(64), kv_spec(128), wide(W), wide(W), wide(W), wide(W), wide(W),
                   wide(3 * D_MODEL)],
        out_shape=[sds(W), kv_sds(64), kv_sds(128), sds(W), sds(W), sds(W), sds(W), sds(W), sds(3 * D_MODEL)],
        compiler_params=_params(("arbitrary",)),
        name="in_proj",
    )(*xs, mod_l, gnorm, w_in_bf, *tabs)


def _attend(sink_ref, q_ref, o_ref, sub, pieces):
    rows = slice(sub * ATTN_BLOCK, (sub + 1) * ATTN_BLOCK)
    groups = [slice(g * ATTN_BLOCK, (g + 1) * ATTN_BLOCK) for g in range(ATTN_GROUP)]
    outs = []
    for h in range(ATTN_KV_HEADS):
        q = q_ref[rows, h * ATTN_GW:(h + 1) * ATTN_GW]
        q4 = jnp.concatenate([q[:, g * HEAD_DIM:(g + 1) * HEAD_DIM] for g in range(ATTN_GROUP)], axis=0)
        sinks = [sink_ref[h * ATTN_GROUP + g] * LOG2E for g in range(ATTN_GROUP)]
        k_all = jnp.concatenate([k_ref[h] for k_ref, _, _ in pieces], axis=0)
        v_all = jnp.concatenate([v_ref[h] for _, v_ref, _ in pieces], axis=0)
        s = _dot_nt(q4, k_all)
        cols, c0 = [], 0
        for k_ref, _, ok in pieces:
            n = k_ref.shape[1]
            cols.append(s[:, c0:c0 + n] if ok is None else jnp.where(ok, s[:, c0:c0 + n], NEG))
            c0 += n
        s = jnp.concatenate(cols, axis=1)
        mxs, ps = [], []
        for g, r in enumerate(groups):
            mx = jnp.maximum(jnp.max(s[r], axis=-1, keepdims=True), sinks[g])
            mxs.append(mx)
            ps.append(jnp.exp2(s[r] - mx).astype(BF16))
        oa = _dot(jnp.concatenate(ps, axis=0), v_all)
        for g, r in enumerate(groups):
            den = oa[r, HEAD_DIM:HEAD_DIM + 1] + jnp.exp2(sinks[g] - mxs[g])
            outs.append(oa[r, 0:HEAD_DIM] / den)
    o_ref[rows, :] = jnp.concatenate(outs, axis=1).astype(BF16)


def _attn_lat_body(sink_ref, q_ref, kc_ref, *refs):
    nk = ATTN_QB + 2
    k_refs, vc_ref, v_refs, o_ref = refs[:nk], refs[nk], refs[nk + 1:2 * nk + 1], refs[2 * nk + 1]
    p = pl.program_id(1)
    nr = ATTN_GROUP * ATTN_BLOCK
    ri = lax.broadcasted_iota(jnp.int32, (nr, ATTN_BLOCK), 0) & (ATTN_BLOCK - 1)
    ci = lax.broadcasted_iota(jnp.int32, (nr, ATTN_BLOCK), 1)
    far = 4 * ATTN_BLOCK
    first_prev = ci >= ri + jnp.where(p >= 1, 0, far)
    last_next = ci + jnp.where(p <= SEQ // (ATTN_QB * ATTN_BLOCK) - 2, 0, far) <= ri
    for sub in range(ATTN_QB):
        prev_ok = first_prev if sub == 0 else ci >= ri
        next_ok = last_next if sub == ATTN_QB - 1 else ci <= ri
        _attend(sink_ref, q_ref, o_ref, sub,
                [(kc_ref, vc_ref, None), (k_refs[sub], v_refs[sub], prev_ok), (k_refs[sub + 1], v_refs[sub + 1], None),
                 (k_refs[sub + 2], v_refs[sub + 2], next_ok)])


def _attn_ctx_body(sink_ref, q_ref, kc_ref, vc_ref, o_ref):
    for sub in range(CTX_LEN // ATTN_BLOCK):
        _attend(sink_ref, q_ref, o_ref, sub, [(kc_ref, vc_ref, None)])


def _attention(sink, qa, ka, va, need_ctx):
    nb = SEQ // ATTN_BLOCK
    nstep = nb // ATTN_QB
    tq = ATTN_QB * ATTN_BLOCK
    smem = pl.BlockSpec(memory_space=pltpu.SMEM)

    def loc(delta):
        def f(b, p):
            m = jnp.clip(ATTN_QB * p + delta, 0, nb - 1)
            return (0, RC // ATTN_BLOCK + b * nb + m, 0)
        return f

    def kv_specs(width):
        ctx_spec = pl.BlockSpec((ATTN_KV_HEADS, CTX_LEN, width), lambda b, p: (0, b, 0))
        return [ctx_spec] + [pl.BlockSpec((ATTN_KV_HEADS, ATTN_BLOCK, width), loc(d)) for d in range(-1, ATTN_QB + 1)]

    nloc = ATTN_QB + 3
    oa_l = pl.pallas_call(
        _attn_lat_body,
        grid=(BATCH, nstep),
        in_specs=[smem, pl.BlockSpec((tq, W), lambda b, p: (RC // tq + b * nstep + p, 0))]
                 + kv_specs(HEAD_DIM) + kv_specs(128),
        out_specs=pl.BlockSpec((tq, W), lambda b, p: (b * nstep + p, 0)),
        out_shape=jax.ShapeDtypeStruct((RL, W), BF16),
        compiler_params=_params(("arbitrary", "arbitrary")),
        name="window_attn",
    )(sink, qa, *([ka] * nloc), *([va] * nloc))
    if not need_ctx:
        return None, oa_l
    oa_c = pl.pallas_call(
        _attn_ctx_body,
        grid=(BATCH,),
        in_specs=[smem, pl.BlockSpec((CTX_LEN, W), lambda b: (b, 0)),
                  pl.BlockSpec((ATTN_KV_HEADS, CTX_LEN, HEAD_DIM), lambda b: (0, b, 0)),
                  pl.BlockSpec((ATTN_KV_HEADS, CTX_LEN, 128), lambda b: (0, b, 0))],
        out_specs=pl.BlockSpec((CTX_LEN, W), lambda b: (b, 0)),
        out_shape=jax.ShapeDtypeStruct((RC, W), BF16),
        compiler_params=_params(("arbitrary",)),
        name="ctx_attn",
    )(sink, qa, ka, va)
    return oa_c, oa_l


def _fourier_body(*refs, has_ctx):
    if has_ctx:
        uc_ref, ul_ref, jrev_ref, bdc_ref, bds_ref, w2_ref, w2c_ref, oc_ref, ol_ref, as_ref, mid_ref = refs
    else:
        ul_ref, jrev_ref, bdc_ref, bds_ref, w2_ref, ol_ref, as_ref, mid_ref = refs
    j = pl.program_id(1)
    first = 1 if has_ctx else 0
    hn = SEQ // 2

    if has_ctx:
        @pl.when(j == 0)
        def _():
            u = uc_ref[...]
            a = _dot(u, bdc_ref[...]).astype(BF16)
            s = _dot(u, bds_ref[...]).astype(BF16)
            z = _dot(w2c_ref[...], jnp.concatenate([a, s], axis=0))
            oc_ref[...] = (z * ((CTX_LEN * FOURIER_DIM) ** -0.5)).astype(BF16)

    @pl.when(j == first)
    def _():
        uh = ul_ref[0:hn, :].astype(F32)
        ur = _dot(jrev_ref[...], ul_ref[hn:SEQ, :])
        row = lax.broadcasted_iota(jnp.int32, (hn, W), 0)
        vm = jnp.where(row == 0, 0.0, uh - ur)
        as_ref[0:hn, :] = _dot((uh + ur).astype(BF16), bdc_ref[...]).astype(BF16)
        as_ref[hn:SEQ, :] = _dot(vm.astype(BF16), bds_ref[...]).astype(BF16)
        mid_ref[...] = _dot(ul_ref[hn:hn + 8, :], bdc_ref[...])

    @pl.when(j >= first)
    def _():
        z = _dot(w2_ref[...], as_ref[...])
        k = lax.broadcasted_iota(jnp.int32, (z.shape[0], 1), 0)
        sign = (1 - 2 * (k & 1)).astype(F32)
        ol_ref[...] = ((z + sign * mid_ref[0:1, :]) * ((SEQ * FOURIER_DIM) ** -0.5)).astype(BF16)


def _fourier(fu, dft, need_ctx):
    bdc, bds, w2, w2c = dft
    tr = TR_FOURIER
    nj = SEQ // tr
    hn = SEQ // 2
    first = 1 if need_ctx else 0
    jrev_np = np.zeros((hn, hn), np.float32)
    jrev_np[np.arange(1, hn), hn - np.arange(1, hn)] = 1.0
    jrev = jnp.asarray(jrev_np).astype(BF16)
    full = lambda a: pl.BlockSpec(a.shape, lambda *_: (0,) * a.ndim)
    lat_tile = lambda j: jnp.maximum(j - first, 0)
    ul_spec = pl.BlockSpec((SEQ, W), lambda b, j: (1 + b, 0))
    w2_spec = pl.BlockSpec((tr, SEQ), lambda b, j: (lat_tile(j), 0))
    ol_spec = pl.BlockSpec((tr, W), lambda b, j: (b * nj + lat_tile(j), 0))
    ol_shape = jax.ShapeDtypeStruct((RL, W), BF16)
    if need_ctx:
        in_specs = [pl.BlockSpec((CTX_LEN, W), lambda b, j: (b, 0)), ul_spec, full(jrev), full(bdc), full(bds),
                    w2_spec, full(w2c)]
        out_specs = [pl.BlockSpec((CTX_LEN, W), lambda b, j: (b, 0)), ol_spec]
        out_shape = [jax.ShapeDtypeStruct((RC, W), BF16), ol_shape]
        args = (fu, fu, jrev, bdc, bds, w2, w2c)
    else:
        in_specs = [ul_spec, full(jrev), full(bdc), full(bds), w2_spec]
        out_specs = [ol_spec]
        out_shape = [ol_shape]
        args = (fu, jrev, bdc, bds, w2)
    outs = pl.pallas_call(
        functools.partial(_fourier_body, has_ctx=need_ctx),
        grid=(BATCH, nj + first),
        in_specs=in_specs,
        out_specs=out_specs,
        out_shape=out_shape,
        scratch_shapes=[pltpu.VMEM((SEQ, W), BF16), pltpu.VMEM((8, W), F32)],
        compiler_params=_params(("arbitrary", "arbitrary")),
        name="fourier_mix",
    )(*args)
    return (outs[0], outs[1]) if need_ctx else (None, outs[0])


def _retention_body(qc_ref, kc_ref, vc_ref, ql0_ref, kl0_ref, vl0_ref, ql1_ref, kl1_ref, vl1_ref, g_ref,
                    dcomb_ref, xif_ref, ztf_ref, xib_ref, ztb_ref, gf_ref, gb_ref, mbd_ref, avg_ref,
                    o_ref, os_ref, st_ref):
    j = pl.program_id(1)
    C = RET_CHUNK
    nl = SEQ // C
    lat_refs = ((ql0_ref, kl0_ref, vl0_ref), (ql1_ref, kl1_ref, vl1_ref))

    own = ((lax.broadcasted_iota(jnp.int32, (RET_GH * C, RET_GW), 0) >> (C.bit_length() - 1))
           == (lax.broadcasted_iota(jnp.int32, (RET_GH * C, RET_GW), 1) >> (RET_DK.bit_length() - 1)))

    def chunk_fwd(bb, q, k, v, r0):
        for gi in range(RET_NG):
            sl = slice(gi * RET_GW, (gi + 1) * RET_GW)
            q4, k4, v4 = q[:, sl], k[:, sl], v[:, sl]
            s_prev = st_ref[RET_NG * bb + gi]
            q4f = q4.astype(F32)
            o4 = _dot((q4f * xif_ref[:, sl]).astype(BF16), s_prev.astype(BF16))
            qstack = jnp.where(own, jnp.concatenate([q4f] * RET_GH, axis=0), 0.0).astype(BF16)
            p = (_dot_nt(qstack, k4) * dcomb_ref[gi]).astype(BF16)
            ov = jnp.where(own, _dot(p, v4), 0.0)
            intra = sum(ov[hh * C:(hh + 1) * C] for hh in range(RET_GH))
            os_ref[pl.ds(r0, C), sl] = o4 + intra
            u = _dot_tn(k4, (v4.astype(F32) * ztf_ref[:, sl]).astype(BF16))
            st_ref[RET_NG * bb + gi] = gf_ref[gi] * s_prev + mbd_ref[...] * u

    def chunk_bwd(bb, q, k, v, r0):
        for gi in range(RET_NG):
            sl = slice(gi * RET_GW, (gi + 1) * RET_GW)
            q4, k4, v4 = q[:, sl], k[:, sl], v[:, sl]
            s_prev = st_ref[RET_NG * bb + gi]
            cross = _dot((q4.astype(F32) * xib_ref[:, sl]).astype(BF16), s_prev.astype(BF16))
            os_ref[pl.ds(r0, C), sl] = os_ref[pl.ds(r0, C), sl] + cross
            u = _dot_tn(k4, (v4.astype(F32) * ztb_ref[:, sl]).astype(BF16))
            st_ref[RET_NG * bb + gi] = gb_ref[gi] * s_prev + mbd_ref[...] * u

    def scan(chunk, ctx_order, lat_index):
        st_ref[...] = jnp.zeros_like(st_ref)
        for c in ctx_order:
            for bb in range(2):
                rs = slice(bb * CTX_LEN + c * C, bb * CTX_LEN + (c + 1) * C)
                chunk(bb, qc_ref[rs, :], kc_ref[rs, :], vc_ref[rs, :], bb * CTX_LEN + c * C)

        def body(t, carry):
            r0 = pl.multiple_of(lat_index(t) * C, C)
            for bb, (q_ref, k_ref, v_ref) in enumerate(lat_refs):
                rs = pl.ds(r0, C)
                chunk(bb, q_ref[rs, :], k_ref[rs, :], v_ref[rs, :], 2 * CTX_LEN + bb * SEQ + r0)
            return carry

        lax.fori_loop(0, nl, body, 0)

    @pl.when(j == 0)
    def _():
        scan(chunk_fwd, range(CTX_LEN // C), lambda t: t)
        scan(chunk_bwd, reversed(range(CTX_LEN // C)), lambda t: nl - 1 - t)

    o = os_ref[pl.ds(pl.multiple_of(j * RET_SLAB, RET_SLAB), RET_SLAB), :]
    avg = avg_ref[...]
    oh, ol = _split(o)
    mu = _dot(oh, avg) + _dot(ol, avg)
    d = o - mu
    var = _dot((d * d).astype(BF16), avg)
    o_ref[...] = (g_ref[...].astype(F32) * d * lax.rsqrt(var + GN_EPS)).astype(BF16)


def _retention(qr, kr, vr, gr, rtabs):
    nslab = SEQ // RET_SLAB
    nj = 1 + 2 * nslab

    def out_map(b2, j):
        return (jnp.where(j == 0, b2, RC // RET_SLAB + 2 * b2 * nslab + j - 1), 0)

    ctx = pl.BlockSpec((2 * CTX_LEN, W), lambda b2, j: (b2, 0))
    lat = lambda k: pl.BlockSpec((SEQ, W), lambda b2, j: (1 + 2 * b2 + k, 0))
    full = lambda a: pl.BlockSpec(a.shape, lambda *_: (0,) * a.ndim)
    avg = jnp.asarray(np.kron(np.eye(RET_HEADS), np.full((RET_DK, RET_DK), 1.0 / RET_DK)).astype(np.float32)).astype(BF16)
    mbd = jnp.asarray(np.kron(np.eye(RET_GH), np.ones((RET_DK, RET_DK))).astype(np.float32))
    tabs = list(rtabs) + [mbd, avg]
    return pl.pallas_call(
        _retention_body,
        grid=(BATCH // 2, nj),
        in_specs=[ctx, ctx, ctx, lat(0), lat(0), lat(0), lat(1), lat(1), lat(1),
                  pl.BlockSpec((RET_SLAB, W), out_map)] + [full(t) for t in tabs],
        out_specs=pl.BlockSpec((RET_SLAB, W), out_map),
        out_shape=jax.ShapeDtypeStruct((R, W), BF16),
        scratch_shapes=[pltpu.VMEM((2 * (CTX_LEN + SEQ), W), F32), pltpu.VMEM((2 * RET_NG, RET_GW, RET_GW), F32)],
        compiler_params=_params(("arbitrary", "arbitrary")),
        name="retention",
    )(qr, kr, vr, qr, kr, vr, qr, kr, vr, gr, *tabs)


def _pack_pair(a, b):
    ua = lax.bitcast_convert_type(a, jnp.uint32) >> 16
    ub = lax.bitcast_convert_type(b, jnp.uint32) & jnp.uint32(0xFFFF0000)
    return ua | ub


def _unpack_pair(w):
    a = lax.bitcast_convert_type(w << 16, F32)
    b = lax.bitcast_convert_type(w & jnp.uint32(0xFFFF0000), F32)
    return a, b


def _slot_onehot(s0, s1, n):
    srow = lax.broadcasted_iota(jnp.int32, (n, s0.shape[1]), 0)
    p0 = jnp.where(srow == s0, 1.0, 0.0).astype(BF16)
    p1 = jnp.where(srow == s1, 1.0, 0.0).astype(BF16)
    return p0, p1


def _pick(first, a_ref, b_ref):
    a = a_ref[...]
    flag = jnp.zeros(a.shape, jnp.int32) + first.astype(jnp.int32)
    return jnp.where(flag > 0, a, b_ref[...])


def _merge_body(*refs, layer0):
    if layer0:
        oac_ref, oal_ref, ofc_ref, ofl_ref, rt_ref, gm_ref, xc_ref, xl_ref = refs[:8]
        rest = refs[8:]
        is_ctx = pl.program_id(0) < RC // TM_MERGE
        oa_in = _pick(is_ctx, oac_ref, oal_ref)
        of_in = _pick(is_ctx, ofc_ref, ofl_ref)
        x_in = _pick(is_ctx, xc_ref, xl_ref)
    else:
        oal_ref, ofl_ref, rt_ref, gm_ref, x_ref = refs[:5]
        rest = refs[5:]
        oa_in = oal_ref[...]
        of_in = ofl_ref[...]
        x_in = x_ref[...]
    (mod_ref, gn_ref, wba_ref, wbf_ref, wbr_ref, wout_ref, wrh_ref, wrl_ref, br_ref, tri_ref, ltri_ref,
     xo_ref, xs_ref, ro_ref, nch_ref) = rest
    gm = gm_ref[...].astype(F32)
    z = (gm[:, 0:D_MODEL] * _dot(oa_in, wba_ref[...])
         + gm[:, D_MODEL:2 * D_MODEL] * _dot(of_in, wbf_ref[...])
         + gm[:, 2 * D_MODEL:3 * D_MODEL] * _dot(rt_ref[...], wbr_ref[...]))
    y = _dot(z.astype(BF16), wout_ref[...])
    x = x_in + mod_ref[2:3, :] * y
    xo_ref[...] = x
    ms = jnp.mean(x * x, axis=-1, keepdims=True)
    hn = x * lax.rsqrt(ms + NORM_EPS) * gn_ref[...]
    h2 = hn * (1.0 + mod_ref[4:5, :]) + mod_ref[3:4, :]
    hh, hl = _split(h2)
    wh, wl = wrh_ref[...], wrl_ref[...]
    lg = _dot_nt(wh, hh) + _dot_nt(wh, hl) + _dot_nt(wl, hh) + br_ref[...]
    tm = lg.shape[1]
    row8 = lax.broadcasted_iota(jnp.int32, (8, tm), 0)
    lgg = lg[0:8, :]
    mg = jnp.max(lgg, axis=0, keepdims=True)
    grp = jnp.min(jnp.where(lgg == mg, row8, 8), axis=0, keepdims=True)
    pg = 1.0 / jnp.sum(jnp.exp(lgg - mg), axis=0, keepdims=True)
    lin = jnp.zeros((8, tm), F32)
    for g in range(N_GROUPS):
        lin = jnp.where(grp == g, lg[8 + 8 * g:16 + 8 * g, :], lin)
    v1 = jnp.max(lin, axis=0, keepdims=True)
    i1 = jnp.min(jnp.where(lin == v1, row8, 8), axis=0, keepdims=True)
    rest = jnp.where(row8 == i1, -jnp.inf, lin)
    v2 = jnp.max(rest, axis=0, keepdims=True)
    i2 = jnp.min(jnp.where(rest == v2, row8, 8), axis=0, keepdims=True)
    e2 = jnp.exp(v2 - v1)
    w1 = pg / (1.0 + e2)
    w2 = pg * e2 / (1.0 + e2)
    e_1 = grp * EXPERTS_PER_GROUP + i1
    e_2 = grp * EXPERTS_PER_GROUP + i2

    row32 = lax.broadcasted_iota(jnp.int32, (N_EXPERTS, tm), 0)
    oh0 = jnp.where(row32 == e_1, 1.0, 0.0)
    oh1 = jnp.where(row32 == e_2, 1.0, 0.0)
    tri = tri_ref[...]
    cum0 = _dot(oh0.astype(BF16), tri)
    cum1 = _dot(oh1.astype(BF16), tri)
    tot0 = jnp.sum(oh0, axis=1, keepdims=True)
    tot1 = jnp.sum(oh1, axis=1, keepdims=True)
    nch = ((tot0 + tot1).astype(jnp.int32) + (CHUNK - 1)) >> CHUNK_SHIFT
    nch_b = jnp.broadcast_to(nch.astype(F32), (N_EXPERTS, 128))
    nch_ref[...] = nch_b.astype(jnp.int32)
    base = CHUNK * _dot(ltri_ref[...], nch_b.astype(BF16))[:, 0:1]
    s0 = jnp.sum(oh0 * (base + cum0), axis=0, keepdims=True).astype(jnp.int32)
    s1 = jnp.sum(oh1 * (base + tot0 + cum1), axis=0, keepdims=True).astype(jnp.int32)
    p0, p1 = _slot_onehot(s0, s1, SLOTS)
    xs = _dot(p0 + p1, hh)
    xs_ref[:, 0:D_MODEL // 2] = _pack_pair(xs[:, 0:D_MODEL // 2], xs[:, D_MODEL // 2:D_MODEL])

    def wrows(w):
        hi, lo = _split(w)
        return jnp.where(row8 == 0, hi.astype(F32), jnp.where(row8 == 1, lo.astype(F32), 0.0)).astype(BF16)

    wc = _dot_nt(p0, wrows(w1)) + _dot_nt(p1, wrows(w2))
    wcol = jnp.broadcast_to(wc[:, 0:1] + wc[:, 1:2], (SLOTS, 128))
    xs_ref[:, D_MODEL // 2:XS_COLS] = lax.bitcast_convert_type(wcol, jnp.uint32)
    s0f, s1f = s0.astype(F32), s1.astype(F32)
    ro_ref[...] = jnp.where(row8 == 0, s0f, jnp.where(row8 == 1, s1f, 0.0))


def _merge(oa_c, oa_l, of_c, of_l, ret, gm, xs_in, mod_l, gnorm, wba, wbf, wbr, wout, wrh, wrl, brb, layer0):
    tm = TM_MERGE
    row0 = 0 if layer0 else RC
    rm = R - row0
    nt = rm // tm
    off = row0 // tm
    nc = RC // tm
    src = lambda n: pl.BlockSpec((tm, n), lambda i: (i + off, 0))
    dst = lambda n: pl.BlockSpec((tm, n), lambda i: (i, 0))
    ctx_rows = lambda n: pl.BlockSpec((tm, n), lambda i: (jnp.minimum(i, nc - 1), 0))
    lat_rows = lambda n: pl.BlockSpec((tm, n), lambda i: (jnp.maximum(i - nc, 0), 0))
    full = lambda a: pl.BlockSpec(a.shape, lambda *_: (0,) * a.ndim, pipeline_mode=pl.Buffered(1))
    mrow = _mod_row(row0, tm)
    tri = jnp.asarray(np.triu(np.ones((tm, tm), np.float32), 1)).astype(BF16)
    ltri = jnp.asarray(np.tril(np.ones((N_EXPERTS, N_EXPERTS), np.float32), -1)).astype(BF16)
    if layer0:
        acts = [oa_c, oa_l, of_c, of_l, ret, gm, xs_in[0], xs_in[1]]
        act_specs = [ctx_rows(W), lat_rows(W), ctx_rows(W), lat_rows(W), src(W), src(3 * D_MODEL),
                     ctx_rows(D_MODEL), lat_rows(D_MODEL)]
    else:
        acts = [oa_l, of_l, ret, gm, xs_in]
        act_specs = [dst(W), dst(W), src(W), src(3 * D_MODEL), src(D_MODEL)]
    return pl.pallas_call(
        functools.partial(_merge_body, layer0=layer0),
        grid=(nt,),
        in_specs=act_specs + [
                  pl.BlockSpec((None, 6, D_MODEL), lambda i: (mrow(i), 0, 0)),
                  full(gnorm), full(wba), full(wbf), full(wbr), full(wout), full(wrh), full(wrl), full(brb),
                  full(tri), full(ltri)],
        out_specs=[dst(D_MODEL), pl.BlockSpec((SLOTS, XS_COLS), lambda i: (i, 0)),
                   pl.BlockSpec((8, tm), lambda i: (0, i)),
                   pl.BlockSpec((None, N_EXPERTS, 128), lambda i: (i, 0, 0))],
        out_shape=[jax.ShapeDtypeStruct((rm, D_MODEL), F32),
                   jax.ShapeDtypeStruct((nt * SLOTS, XS_COLS), jnp.uint32),
                   jax.ShapeDtypeStruct((8, rm), F32),
                   jax.ShapeDtypeStruct((nt, N_EXPERTS, 128), jnp.int32)],
        compiler_params=_params(("arbitrary",)),
        name="merge_router",
    )(*acts, mod_l, gnorm, wba, wbf, wbr, wout, wrh, wrl, brb, tri, ltri)


def _moe_plan(nch, nb):
    nt = nch.shape[0]
    choff = jnp.cumsum(nch, axis=1) - nch
    used_ch = jnp.sum(nch, axis=1)
    cum_t = jnp.cumsum(nch, axis=0)
    tot = cum_t[-1]
    ptot = (tot + CPB - 1) // CPB * CPB
    pend = jnp.cumsum(ptot)
    pstart = pend - ptot
    n_used = pend[-1] // CPB
    blk = jnp.arange(nb, dtype=jnp.int32)
    lane = jnp.arange(CPB, dtype=jnp.int32)
    blk_e = jnp.minimum(jnp.sum((blk[:, None] * CPB >= pend[None, :]).astype(jnp.int32), axis=1), N_EXPERTS - 1)
    oe = (blk_e[:, None] == jnp.arange(N_EXPERTS, dtype=jnp.int32)[None, :]).astype(jnp.int32)
    sel = lambda tab: jnp.sum(oe[:, :, None] * tab.T[None, :, :], axis=1)
    pstart_b = jnp.sum(oe * pstart[None, :], axis=1)
    tot_b = jnp.sum(oe * tot[None, :], axis=1)
    cum_b, nch_b, choff_b = sel(cum_t), sel(nch), sel(choff)
    i = blk[:, None] * CPB + lane[None, :] - pstart_b[:, None]
    valid = (i < tot_b[:, None]) & (blk[:, None] < n_used)
    t = jnp.minimum(jnp.sum((i[:, :, None] >= cum_b[:, None, :]).astype(jnp.int32), axis=2), nt - 1)
    tiles = jnp.arange(nt, dtype=jnp.int32)[None, None, :]
    before = jnp.sum(jnp.where(tiles < t[:, :, None], nch_b[:, None, :], 0), axis=2)
    coff = jnp.sum(jnp.where(tiles == t[:, :, None], choff_b[:, None, :], 0), axis=2)
    row = t * SLOTS + CHUNK * (coff + i - before)
    src = jnp.where(valid, row, SLOTS - CHUNK)
    dummy = nt * SLOTS + CHUNK * ((blk[:, None] % 2) * CPB + lane[None, :])
    dst = jnp.where(valid, row, dummy)
    blk_start = jnp.concatenate([pstart, pend[-1:]]) // CPB
    return (blk_start.astype(jnp.int32), n_used.astype(jnp.int32).reshape(1), src.reshape(-1).astype(jnp.int32),
            dst.reshape(-1).astype(jnp.int32), used_ch.astype(jnp.int32))


def _ffn_body(bs_ref, nu_ref, src_ref, dst_ref, uc_ref, xs_ref, wg_ref, wu_ref, wd_ref, ys_ref,
              xbuf, ybuf, zbuf, wgb, wub, wdb, sem_in, sem_out, sem_z, *, nt):
    e = pl.program_id(0)
    nu = nu_ref[0]
    half = D_MODEL // 2

    def gather(blk, sl):
        for c in range(CPB):
            r = pl.multiple_of(src_ref[blk * CPB + c], CHUNK)
            pltpu.make_async_copy(xs_ref.at[pl.ds(r, CHUNK)], xbuf.at[sl, pl.ds(c * CHUNK, CHUNK)],
                                  sem_in.at[sl]).start()

    def scatter(blk, sl):
        for c in range(CPB):
            r = pl.multiple_of(dst_ref[blk * CPB + c], CHUNK)
            pltpu.make_async_copy(ybuf.at[sl, pl.ds(c * CHUNK, CHUNK)], ys_ref.at[pl.ds(r, CHUNK)],
                                  sem_out.at[sl]).start()

    def wait_gather(sl):
        pltpu.make_async_copy(xs_ref.at[pl.ds(0, MOE_BM)], xbuf.at[sl], sem_in.at[sl]).wait()

    def wait_scatter(sl):
        pltpu.make_async_copy(ybuf.at[sl], ys_ref.at[pl.ds(0, MOE_BM)], sem_out.at[sl]).wait()

    def zero_copy(r):
        return pltpu.make_async_copy(zbuf, ys_ref.at[pl.ds(pl.multiple_of(r, CHUNK), CHUNK)], sem_z)

    @pl.when(e == 0)
    def _():
        zbuf[...] = jnp.zeros_like(zbuf)

        def tails(fn):
            def per_tile(t, carry):
                def per_chunk(c, carry2):
                    fn(t * SLOTS + c * CHUNK)
                    return carry2
                lax.fori_loop(uc_ref[t], SLOTS // CHUNK, per_chunk, 0)
                return carry
            lax.fori_loop(0, nt, per_tile, 0)
            for c in range(2 * CPB):
                fn(nt * SLOTS + c * CHUNK)

        tails(lambda r: zero_copy(r).start())
        tails(lambda r: zero_copy(r).wait())
        gather(0, 0)
        for d in range(1, GATHER_DEPTH - 1):
            @pl.when(nu > d)
            def _():
                gather(d, d)

    b0, b1 = bs_ref[e], bs_ref[e + 1]

    @pl.when(b1 > b0)
    def _():
        wgb[...] = wg_ref[...].astype(BF16)
        wub[...] = wu_ref[...].astype(BF16)
        wdb[...] = wd_ref[...].astype(BF16)

        def block(b, carry):
            slot = b % 2
            xslot = lax.rem(b, GATHER_DEPTH)

            @pl.when(b + GATHER_DEPTH - 1 < nu)
            def _():
                gather(b + GATHER_DEPTH - 1, lax.rem(b + GATHER_DEPTH - 1, GATHER_DEPTH))

            wait_gather(xslot)

            @pl.when(b >= 2)
            def _():
                wait_scatter(slot)

            xw = xbuf[xslot]
            xa, xb = _unpack_pair(xw[:, 0:half])
            x = jnp.concatenate([xa, xb], axis=1).astype(BF16)
            wt = lax.bitcast_convert_type(xw[:, half:XS_COLS], F32)
            g = _dot(x, wgb[...])
            u = _dot(x, wub[...])
            hmid = (g * _sigmoid(g) * u).astype(BF16)
            y = _dot(hmid, wdb[...]) * jnp.concatenate([wt] * (D_MODEL // 128), axis=1)
            yb = y.astype(BF16).astype(F32)
            ybuf[slot] = _pack_pair(yb[:, 0:half], yb[:, half:D_MODEL])
            scatter(b, slot)
            return carry

        lax.fori_loop(b0, b1, block, 0)

    @pl.when(e == N_EXPERTS - 1)
    def _():
        wait_scatter((nu - 1) % 2)

        @pl.when(nu >= 2)
        def _():
            wait_scatter(nu % 2)


def _ffn(plan, xs, w_g, w_u, w_d, layer, nt):
    wmap = lambda e, *_: (layer, e, 0, 0)
    half = D_MODEL // 2
    grid_spec = pltpu.PrefetchScalarGridSpec(
        num_scalar_prefetch=5,
        grid=(N_EXPERTS,),
        in_specs=[pl.BlockSpec(memory_space=pl.ANY),
                  pl.BlockSpec((None, None, D_MODEL, EXPERT_HIDDEN), wmap),
                  pl.BlockSpec((None, None, D_MODEL, EXPERT_HIDDEN), wmap),
                  pl.BlockSpec((None, None, EXPERT_HIDDEN, D_MODEL), wmap)],
        out_specs=pl.BlockSpec(memory_space=pl.ANY),
        scratch_shapes=[pltpu.VMEM((GATHER_DEPTH, MOE_BM, XS_COLS), jnp.uint32),
                        pltpu.VMEM((2, MOE_BM, half), jnp.uint32),
                        pltpu.VMEM((CHUNK, half), jnp.uint32),
                        pltpu.VMEM((D_MODEL, EXPERT_HIDDEN), BF16), pltpu.VMEM((D_MODEL, EXPERT_HIDDEN), BF16),
                        pltpu.VMEM((EXPERT_HIDDEN, D_MODEL), BF16),
                        pltpu.SemaphoreType.DMA((GATHER_DEPTH,)), pltpu.SemaphoreType.DMA((2,)),
                        pltpu.SemaphoreType.DMA(())],
    )
    return pl.pallas_call(
        functools.partial(_ffn_body, nt=nt),
        grid_spec=grid_spec,
        out_shape=jax.ShapeDtypeStruct((nt * SLOTS + 2 * CPB * CHUNK, half), jnp.uint32),
        compiler_params=_params(("arbitrary",)),
        name="moe_experts",
    )(*plan, xs, w_g, w_u, w_d)


def _combine_body(ys_ref, ro_ref, x_ref, mod_ref, gn_ref, o_ref, *, final):
    s = ro_ref[...]
    p0, p1 = _slot_onehot(s[0:1, :].astype(jnp.int32), s[1:2, :].astype(jnp.int32), SLOTS)
    ya, yb = _unpack_pair(ys_ref[...])
    y = jnp.concatenate([ya, yb], axis=1).astype(BF16)
    f = _dot_tn(p0 + p1, y)
    x = x_ref[...] + mod_ref[5:6, :] * f
    if final:
        ms = jnp.mean(x * x, axis=-1, keepdims=True)
        x = x * lax.rsqrt(ms + NORM_EPS) * gn_ref[...]
    o_ref[...] = x


def _combine(ys, route, x, mod_l, gnorm, row0, final):
    tm = TM_MERGE
    rm = x.shape[0]
    mrow = _mod_row(row0, tm)
    return pl.pallas_call(
        functools.partial(_combine_body, final=final),
        grid=(rm // tm,),
        in_specs=[pl.BlockSpec((SLOTS, D_MODEL // 2), lambda i: (i, 0)),
                  pl.BlockSpec((8, tm), lambda i: (0, i)),
                  pl.BlockSpec((tm, D_MODEL), lambda i: (i, 0)),
                  pl.BlockSpec((None, 6, D_MODEL), lambda i: (mrow(i), 0, 0)),
                  pl.BlockSpec((1, D_MODEL), lambda i: (0, 0))],
        out_specs=pl.BlockSpec((tm, D_MODEL), lambda i: (i, 0)),
        out_shape=jax.ShapeDtypeStruct((rm, D_MODEL), F32),
        compiler_params=_params(("arbitrary",)),
        name="moe_combine",
    )(ys, route, x, mod_l, gnorm)


def _moe(xs, route, nch3, x, mod_l, gnorm, w_g, w_u, w_d, row0, final, layer):
    nt = nch3.shape[0]
    max_chunks = nt * ((2 * TM_MERGE + N_EXPERTS * (CHUNK - 1)) // CHUNK)
    nb = -(-max_chunks // CPB) + N_EXPERTS
    plan = _moe_plan(nch3[:, :, 0], nb)
    ys = _ffn(plan, xs, w_g, w_u, w_d, layer, nt)
    return _combine(ys, route, x, mod_l, gnorm, row0, final)


def kernel(x, c, ctx, c_ctx, norm_mix, norm_ffn, w_ada, b_ada, w_in, attn_sink, ret_decay_fwd, ret_decay_bwd,
           w_branch_attn, w_branch_fourier, w_branch_ret, w_out, w_router_group, b_router_group,
           w_router_expert, b_router_expert, w_exp_gate, w_exp_up, w_exp_down, norm_final):
    assert DEPTH == 2, "layer 0 reads (ctx, x) separately and keeps its context rows; the last layer does not"
    tabs = [jnp.asarray(t) for t in _rope_tables()]
    dft = [jnp.asarray(t).astype(BF16) for t in _dft_tables()]

    cc = jnp.zeros((MOD_ROWS, D_MODEL), F32).at[0:BATCH].set(c).at[CTX_MOD_ROW].set(c_ctx)
    mod = _ada(cc, w_ada, b_ada).reshape(DEPTH, MOD_ROWS, 6, D_MODEL)

    xf = (ctx.reshape(RC, D_MODEL), x.reshape(RL, D_MODEL))
    w_in_bf = w_in.astype(BF16)
    for l in range(DEPTH):
        need_ctx = l < DEPTH - 1
        row0 = 0 if need_ctx else RC
        mod_l = mod[l]
        qa, ka, va, qr, kr, vr, gr, fu, gm = _proj(xf, mod_l, norm_mix[l][None, :], w_in_bf, tabs, l)
        oa_c, oa_l = _attention(attn_sink[l], qa, ka, va, need_ctx)
        of_c, of_l = _fourier(fu, dft, need_ctx)
        ret = _retention(qr, kr, vr, gr, _retention_tables(ret_decay_fwd[l], ret_decay_bwd[l]))
        wr = jnp.zeros((ROUTER_ROWS, D_MODEL), F32)
        wr = wr.at[0:N_GROUPS].set(w_router_group[l].T).at[8:8 + N_EXPERTS].set(w_router_expert[l].T)
        br = jnp.full((ROUTER_ROWS,), NEG, F32)
        br = br.at[0:N_GROUPS].set(b_router_group[l]).at[8:8 + N_EXPERTS].set(b_router_expert[l])
        wrh, wrl = _split(wr)
        brb = jnp.broadcast_to(br[:, None], (ROUTER_ROWS, TM_MERGE))
        x_mid, xs, route, nch3 = _merge(oa_c, oa_l, of_c, of_l, ret, gm, xf, mod_l, norm_ffn[l][None, :],
                                        w_branch_attn[l].astype(BF16), w_branch_fourier[l].astype(BF16),
                                        w_branch_ret[l].astype(BF16), w_out[l].astype(BF16), wrh, wrl, brb,
                                        need_ctx)
        final = l == DEPTH - 1
        xf = _moe(xs, route, nch3, x_mid, mod_l, norm_final[None, :], w_exp_gate, w_exp_up, w_exp_down,
                  row0, final, l)
    return xf.reshape(BATCH, SEQ, D_MODEL)
```

```python
import functools

import numpy as np
import jax
import jax.numpy as jnp
from jax import lax
from jax.experimental import pallas as pl
from jax.experimental.pallas import tpu as pltpu

F32 = jnp.float32
BF16 = jnp.bfloat16

D_MODEL = 1024
BATCH = 8
SEQ = 2048
DEPTH = 2
CTX_LEN = 256
GRID_W = 64
HEAD_DIM = 64
ATTN_HEADS = 8
ATTN_KV_HEADS = 2
ATTN_GROUP = ATTN_HEADS // ATTN_KV_HEADS
ATTN_GW = ATTN_GROUP * HEAD_DIM
ATTN_BLOCK = 128
ATTN_QB = 16
RET_HEADS = 8
RET_DK = 64
RET_GH = 4
RET_GW = RET_GH * RET_DK
RET_NG = RET_HEADS // RET_GH
RET_CHUNK = 256
FOURIER_GROUPS = 4
FOURIER_DIM = 128
N_GROUPS = 4
EXPERTS_PER_GROUP = 8
N_EXPERTS = N_GROUPS * EXPERTS_PER_GROUP
EXPERT_HIDDEN = 512
ROPE_BASE = 10000.0
NORM_EPS = 1e-6
GN_EPS = 1e-5

W = 512
IN_COLS = 6400
RC = BATCH * CTX_LEN
RL = BATCH * SEQ
R = RC + RL
MOD_ROWS = 16
CTX_MOD_ROW = 8

VMEM_LIMIT = 52 * 1024 * 1024

TM_PROJ = 512
TM_MERGE = 512
COMBINE_TILES = 2
TN_ADA = 1536
TR_FOURIER = 2048
RET_SLAB = 2 * CTX_LEN
MOE_BM = 512
CHUNK = 8
CHUNK_SHIFT = 3
CPB = MOE_BM // CHUNK
GATHER_DEPTH = 3
SLOTS = 1280
XS_COLS = D_MODEL // 2 + 128
NEG = -1e30
LOG2E = 1.4426950408889634
ROUTER_ROWS = 40


def _dot(a, b):
    return jnp.dot(a, b, preferred_element_type=F32)


def _dot_nt(a, b):
    return lax.dot_general(a, b, (((1,), (1,)), ((), ())), preferred_element_type=F32)


def _dot_tn(a, b):
    return lax.dot_general(a, b, (((0,), (0,)), ((), ())), preferred_element_type=F32)


def _split(x):
    hi = x.astype(BF16)
    lo = (x - hi.astype(F32)).astype(BF16)
    return hi, lo


def _sigmoid(x):
    return 1.0 / (1.0 + jnp.exp(-x))


def _params(sem, vmem=VMEM_LIMIT):
    return pltpu.CompilerParams(dimension_semantics=sem, vmem_limit_bytes=vmem)


def _mod_row(row0, tm):
    def f(i):
        g0 = i * tm + row0
        return jnp.where(g0 < RC, CTX_MOD_ROW, (g0 - RC) // SEQ)
    return f


def _rope_tables():
    pos = np.arange(SEQ, dtype=np.float64)
    row = np.floor(pos / GRID_W)
    col = pos % GRID_W

    def cs(p, nf):
        inv = ROPE_BASE ** (-np.arange(nf, dtype=np.float64) / nf)
        ang = p[:, None] * inv[None, :]
        return np.cos(ang), np.sin(ang)

    rc, rs = cs(row, HEAD_DIM // 4)
    cc, cs_ = cs(col, HEAD_DIM // 4)
    cos_a = np.concatenate([rc, rc, cc, cc], axis=1)
    sin_a = np.concatenate([-rs, rs, -cs_, cs_], axis=1)
    tc, ts = cs(pos, RET_DK // 2)
    cos_r = np.concatenate([tc, tc], axis=1)
    sin_r = np.concatenate([-ts, ts], axis=1)

    def full(t, ident):
        t2 = np.concatenate([t, t], axis=1)
        return np.concatenate([np.full_like(t2, ident), t2], axis=0).astype(np.float32)

    return full(cos_a, 1.0), full(sin_a, 0.0), full(cos_r, 1.0), full(sin_r, 0.0)


def _dft_tables():
    def cs(n):
        k = np.arange(n, dtype=np.int64)
        m = (k[:, None] * k[None, :]) % n
        ang = 2.0 * np.pi * m.astype(np.float64) / n
        return np.cos(ang), np.sin(ang)

    c128, s128 = cs(FOURIER_DIM)
    eye = np.eye(FOURIER_GROUPS)
    bdc = np.kron(eye, c128).astype(np.float32)
    bds = np.kron(eye, s128).astype(np.float32)
    cn, sn = cs(SEQ)
    hn = SEQ // 2
    w2 = np.concatenate([cn[:, :hn], -sn[:, :hn]], axis=1).astype(np.float32)
    cl, sl = cs(CTX_LEN)
    w2c = np.concatenate([cl, -sl], axis=1).astype(np.float32)
    return bdc, bds, w2, w2c


def _retention_tables(dec_f, dec_b):
    lg_f = jax.nn.log_sigmoid(dec_f.astype(F32))
    lg_b = jax.nn.log_sigmoid(dec_b.astype(F32))
    i = jnp.arange(RET_CHUNK)
    diff = (i[:, None] - i[None, :]).astype(F32)
    fwd = jnp.exp(jnp.maximum(diff, 0.0)[None] * lg_f[:, None, None])
    bwd = jnp.exp(jnp.maximum(-diff, 0.0)[None] * lg_b[:, None, None])
    dcomb = jnp.where((diff >= 0)[None], fwd, bwd).reshape(RET_NG, RET_GH * RET_CHUNK, RET_CHUNK)
    fi = i.astype(F32)
    lanes = lambda t: jnp.repeat(t, RET_DK, axis=1)
    xi_f = lanes(jnp.exp((fi + 1.0)[:, None] * lg_f[None, :]))
    zt_f = lanes(jnp.exp((RET_CHUNK - 1 - fi)[:, None] * lg_f[None, :]))
    xi_b = lanes(jnp.exp((RET_CHUNK - fi)[:, None] * lg_b[None, :]))
    zt_b = lanes(jnp.exp(fi[:, None] * lg_b[None, :]))
    g_f = jnp.repeat(jnp.exp(RET_CHUNK * lg_f), RET_DK).reshape(RET_NG, RET_GW, 1)
    g_b = jnp.repeat(jnp.exp(RET_CHUNK * lg_b), RET_DK).reshape(RET_NG, RET_GW, 1)
    g_f = jnp.broadcast_to(g_f, (RET_NG, RET_GW, RET_GW))
    g_b = jnp.broadcast_to(g_b, (RET_NG, RET_GW, RET_GW))
    return dcomb, xi_f, zt_f, xi_b, zt_b, g_f, g_b


def _ada_body(c_ref, w_ref, b_ref, o_ref):
    c = c_ref[...]
    s = c * _sigmoid(c)
    sh, sl = _split(s)
    wh, wl = _split(w_ref[...])
    o_ref[...] = _dot(sh, wh) + _dot(sl, wh) + _dot(sh, wl) + b_ref[...]


def _ada(cc, w_ada, b_ada):
    nt = 6 * D_MODEL // TN_ADA
    return pl.pallas_call(
        _ada_body,
        grid=(DEPTH, nt),
        in_specs=[
            pl.BlockSpec((MOD_ROWS, D_MODEL), lambda l, j: (0, 0)),
            pl.BlockSpec((None, D_MODEL, TN_ADA), lambda l, j: (l, 0, j)),
            pl.BlockSpec((None, 1, TN_ADA), lambda l, j: (l, 0, j)),
        ],
        out_specs=pl.BlockSpec((None, MOD_ROWS, TN_ADA), lambda l, j: (l, 0, j)),
        out_shape=jax.ShapeDtypeStruct((DEPTH, MOD_ROWS, 6 * D_MODEL), F32),
        compiler_params=_params(("arbitrary", "arbitrary")),
        name="ada_mod",
    )(cc, w_ada, b_ada.reshape(DEPTH, 1, 6 * D_MODEL))


def _rope(xc, cos, sin, half):
    fwd = pltpu.roll(xc, 128 - half, axis=1)
    bwd = pltpu.roll(xc, half, axis=1)
    lane = lax.broadcasted_iota(jnp.int32, xc.shape, 1)
    first = (lane & (2 * half - 1)) < half
    return xc * cos + jnp.where(first, fwd, bwd) * sin


def _proj_body(*refs, split):
    if split:
        x = _pick(pl.program_id(0) < RC // TM_PROJ, refs[0], refs[1])
        refs = refs[2:]
    else:
        x = refs[0][...]
        refs = refs[1:]
    (mod_ref, gn_ref, w_ref, ca_ref, sa_ref, cr_ref, sr_ref,
     qa_ref, ka_ref, va_ref, qr_ref, kr_ref, vr_ref, gr_ref, fu_ref, gm_ref) = refs
    ms = jnp.mean(x * x, axis=-1, keepdims=True)
    y = x * lax.rsqrt(ms + NORM_EPS) * gn_ref[...]
    h = y * (1.0 + mod_ref[1:2, :]) + mod_ref[0:1, :]
    hb = h.astype(BF16)

    def proj(c0, width):
        return _dot(hb, w_ref[:, c0:c0 + width])

    ca, sa, cr, sr = ca_ref[...], sa_ref[...], cr_ref[...], sr_ref[...]

    for c in range(3):
        gm_ref[:, c * D_MODEL:(c + 1) * D_MODEL] = _sigmoid(proj(3328 + c * D_MODEL, D_MODEL)).astype(BF16)
    g = proj(2304, W)
    gr_ref[...] = (g * _sigmoid(g)).astype(BF16)
    qa = proj(0, W) * (HEAD_DIM ** -0.5 * LOG2E)
    for c in range(W // 128):
        qa_ref[:, c * 128:(c + 1) * 128] = _rope(qa[:, c * 128:(c + 1) * 128], ca, sa, 16).astype(BF16)
    kv = proj(W, 256)
    ka = _rope(kv[:, 0:128], ca, sa, 16).astype(BF16)
    ka_ref[0] = ka[:, 0:64]
    ka_ref[1] = ka[:, 64:128]
    va = kv[:, 128:256].astype(BF16)
    ones_col = jnp.where(lax.broadcasted_iota(jnp.int32, (va.shape[0], 64), 1) == 0, 1.0, 0.0).astype(BF16)
    va_ref[0] = jnp.concatenate([va[:, 0:64], ones_col], axis=1)
    va_ref[1] = jnp.concatenate([va[:, 64:128], ones_col], axis=1)
    qr = proj(768, W)
    kr = proj(1280, W) * (RET_DK ** -0.5)
    for c in range(W // 128):
        sl = slice(c * 128, (c + 1) * 128)
        qr_ref[:, sl] = _rope(qr[:, sl], cr, sr, 32).astype(BF16)
        kr_ref[:, sl] = _rope(kr[:, sl], cr, sr, 32).astype(BF16)
    vr_ref[...] = proj(1792, W).astype(BF16)
    fu_ref[...] = proj(2816, W).astype(BF16)


def _proj(x, mod_l, gnorm, w_in_bf, tabs, layer):
    tm = TM_PROJ
    nt = R // tm
    nc = RC // tm
    split = isinstance(x, tuple)
    if split:
        xs = list(x)
        x_specs = [pl.BlockSpec((tm, D_MODEL), lambda i: (jnp.minimum(i, nc - 1), 0)),
                   pl.BlockSpec((tm, D_MODEL), lambda i: (jnp.maximum(i - nc, 0), 0))]
    else:
        xs = [x]
        x_specs = [pl.BlockSpec((tm, D_MODEL), lambda i: (i, 0))]

    def tab_map(i):
        return (jnp.where(i < nc, i, nc + (i - nc) % (SEQ // tm)), 0)

    row = lambda i: (i, 0)
    wide = lambda n: pl.BlockSpec((tm, n), row)
    kv_spec = lambda n: pl.BlockSpec((2, tm, n), lambda i: (0, i, 0))
    sds = lambda n: jax.ShapeDtypeStruct((R, n), BF16)
    kv_sds = lambda n: jax.ShapeDtypeStruct((2, R, n), BF16)
    mrow = _mod_row(0, tm)
    return pl.pallas_call(
        functools.partial(_proj_body, split=split),
        grid=(nt,),
        in_specs=x_specs + [
            pl.BlockSpec((None, 6, D_MODEL), lambda i: (mrow(i), 0, 0)),
            pl.BlockSpec((1, D_MODEL), lambda i: (0, 0)),
            pl.BlockSpec((None, D_MODEL, IN_COLS), lambda i: (layer, 0, 0), pipeline_mode=pl.Buffered(1)),
        ] + [pl.BlockSpec((tm, 128), tab_map)] * 4,
        out_specs=[wide(W), kv_spec(64), kv_spec(128), wide(W), wide(W), wide(W), wide(W), wide(W),
                   wide(3 * D_MODEL)],
        out_shape=[sds(W), kv_sds(64), kv_sds(128), sds(W), sds(W), sds(W), sds(W), sds(W), sds(3 * D_MODEL)],
        compiler_params=_params(("arbitrary",)),
        name="in_proj",
    )(*xs, mod_l, gnorm, w_in_bf, *tabs)


def _attend(sink_ref, q_ref, o_ref, sub, pieces):
    rows = slice(sub * ATTN_BLOCK, (sub + 1) * ATTN_BLOCK)
    groups = [slice(g * ATTN_BLOCK, (g + 1) * ATTN_BLOCK) for g in range(ATTN_GROUP)]
    outs = []
    for h in range(ATTN_KV_HEADS):
        q = q_ref[rows, h * ATTN_GW:(h + 1) * ATTN_GW]
        q4 = jnp.concatenate([q[:, g * HEAD_DIM:(g + 1) * HEAD_DIM] for g in range(ATTN_GROUP)], axis=0)
        sinks = [sink_ref[h * ATTN_GROUP + g] * LOG2E for g in range(ATTN_GROUP)]
        k_all = jnp.concatenate([k_ref[h] for k_ref, _, _ in pieces], axis=0)
        v_all = jnp.concatenate([v_ref[h] for _, v_ref, _ in pieces], axis=0)
        s = _dot_nt(q4, k_all)
        cols, c0 = [], 0
        for k_ref, _, ok in pieces:
            n = k_ref.shape[1]
            cols.append(s[:, c0:c0 + n] if ok is None else jnp.where(ok, s[:, c0:c0 + n], NEG))
            c0 += n
        s = jnp.concatenate(cols, axis=1)
        mxs, ps = [], []
        for g, r in enumerate(groups):
            mx = jnp.maximum(jnp.max(s[r], axis=-1, keepdims=True), sinks[g])
            mxs.append(mx)
            ps.append(jnp.exp2(s[r] - mx).astype(BF16))
        oa = _dot(jnp.concatenate(ps, axis=0), v_all)
        for g, r in enumerate(groups):
            den = oa[r, HEAD_DIM:HEAD_DIM + 1] + jnp.exp2(sinks[g] - mxs[g])
            outs.append(oa[r, 0:HEAD_DIM] / den)
    o_ref[rows, :] = jnp.concatenate(outs, axis=1).astype(BF16)


def _attn_lat_body(sink_ref, q_ref, kc_ref, *refs):
    nk = ATTN_QB + 2
    k_refs, vc_ref, v_refs, o_ref = refs[:nk], refs[nk], refs[nk + 1:2 * nk + 1], refs[2 * nk + 1]
    p = pl.program_id(1)
    nr = ATTN_GROUP * ATTN_BLOCK
    ri = lax.broadcasted_iota(jnp.int32, (nr, ATTN_BLOCK), 0) & (ATTN_BLOCK - 1)
    ci = lax.broadcasted_iota(jnp.int32, (nr, ATTN_BLOCK), 1)
    far = 4 * ATTN_BLOCK
    first_prev = ci >= ri + jnp.where(p >= 1, 0, far)
    last_next = ci + jnp.where(p <= SEQ // (ATTN_QB * ATTN_BLOCK) - 2, 0, far) <= ri
    for sub in range(ATTN_QB):
        prev_ok = first_prev if sub == 0 else ci >= ri
        next_ok = last_next if sub == ATTN_QB - 1 else ci <= ri
        _attend(sink_ref, q_ref, o_ref, sub,
                [(kc_ref, vc_ref, None), (k_refs[sub], v_refs[sub], prev_ok), (k_refs[sub + 1], v_refs[sub + 1], None),
                 (k_refs[sub + 2], v_refs[sub + 2], next_ok)])


def _attn_ctx_body(sink_ref, q_ref, kc_ref, vc_ref, o_ref):
    for sub in range(CTX_LEN // ATTN_BLOCK):
        _attend(sink_ref, q_ref, o_ref, sub, [(kc_ref, vc_ref, None)])


def _attention(sink, qa, ka, va, need_ctx):
    nb = SEQ // ATTN_BLOCK
    nstep = nb // ATTN_QB
    tq = ATTN_QB * ATTN_BLOCK
    smem = pl.BlockSpec(memory_space=pltpu.SMEM)

    def loc(delta):
        def f(b, p):
            m = jnp.clip(ATTN_QB * p + delta, 0, nb - 1)
            return (0, RC // ATTN_BLOCK + b * nb + m, 0)
        return f

    def kv_specs(width):
        ctx_spec = pl.BlockSpec((ATTN_KV_HEADS, CTX_LEN, width), lambda b, p: (0, b, 0))
        return [ctx_spec] + [pl.BlockSpec((ATTN_KV_HEADS, ATTN_BLOCK, width), loc(d)) for d in range(-1, ATTN_QB + 1)]

    nloc = ATTN_QB + 3
    oa_l = pl.pallas_call(
        _attn_lat_body,
        grid=(BATCH, nstep),
        in_specs=[smem, pl.BlockSpec((tq, W), lambda b, p: (RC // tq + b * nstep + p, 0))]
                 + kv_specs(HEAD_DIM) + kv_specs(128),
        out_specs=pl.BlockSpec((tq, W), lambda b, p: (b * nstep + p, 0)),
        out_shape=jax.ShapeDtypeStruct((RL, W), BF16),
        compiler_params=_params(("arbitrary", "arbitrary")),
        name="window_attn",
    )(sink, qa, *([ka] * nloc), *([va] * nloc))
    if not need_ctx:
        return None, oa_l
    oa_c = pl.pallas_call(
        _attn_ctx_body,
        grid=(BATCH,),
        in_specs=[smem, pl.BlockSpec((CTX_LEN, W), lambda b: (b, 0)),
                  pl.BlockSpec((ATTN_KV_HEADS, CTX_LEN, HEAD_DIM), lambda b: (0, b, 0)),
                  pl.BlockSpec((ATTN_KV_HEADS, CTX_LEN, 128), lambda b: (0, b, 0))],
        out_specs=pl.BlockSpec((CTX_LEN, W), lambda b: (b, 0)),
        out_shape=jax.ShapeDtypeStruct((RC, W), BF16),
        compiler_params=_params(("arbitrary",)),
        name="ctx_attn",
    )(sink, qa, ka, va)
    return oa_c, oa_l


def _fourier_body(*refs, has_ctx):
    if has_ctx:
        uc_ref, ul_ref, jrev_ref, bdc_ref, bds_ref, w2_ref, w2c_ref, oc_ref, ol_ref, as_ref, mid_ref = refs
    else:
        ul_ref, jrev_ref, bdc_ref, bds_ref, w2_ref, ol_ref, as_ref, mid_ref = refs
    j = pl.program_id(1)
    first = 1 if has_ctx else 0
    hn = SEQ // 2

    if has_ctx:
        @pl.when(j == 0)
        def _():
            u = uc_ref[...]
            a = _dot(u, bdc_ref[...]).astype(BF16)
            s = _dot(u, bds_ref[...]).astype(BF16)
            z = _dot(w2c_ref[...], jnp.concatenate([a, s], axis=0))
            oc_ref[...] = (z * ((CTX_LEN * FOURIER_DIM) ** -0.5)).astype(BF16)

    @pl.when(j == first)
    def _():
        uh = ul_ref[0:hn, :].astype(F32)
        ur = _dot(jrev_ref[...], ul_ref[hn:SEQ, :])
        row = lax.broadcasted_iota(jnp.int32, (hn, W), 0)
        vm = jnp.where(row == 0, 0.0, uh - ur)
        as_ref[0:hn, :] = _dot((uh + ur).astype(BF16), bdc_ref[...]).astype(BF16)
        as_ref[hn:SEQ, :] = _dot(vm.astype(BF16), bds_ref[...]).astype(BF16)
        mid_ref[...] = _dot(ul_ref[hn:hn + 8, :], bdc_ref[...])

    @pl.when(j >= first)
    def _():
        z = _dot(w2_ref[...], as_ref[...])
        k = lax.broadcasted_iota(jnp.int32, (z.shape[0], 1), 0)
        sign = (1 - 2 * (k & 1)).astype(F32)
        ol_ref[...] = ((z + sign * mid_ref[0:1, :]) * ((SEQ * FOURIER_DIM) ** -0.5)).astype(BF16)


def _fourier(fu, dft, need_ctx):
    bdc, bds, w2, w2c = dft
    tr = TR_FOURIER
    nj = SEQ // tr
    hn = SEQ // 2
    first = 1 if need_ctx else 0
    jrev_np = np.zeros((hn, hn), np.float32)
    jrev_np[np.arange(1, hn), hn - np.arange(1, hn)] = 1.0
    jrev = jnp.asarray(jrev_np).astype(BF16)
    full = lambda a: pl.BlockSpec(a.shape, lambda *_: (0,) * a.ndim)
    lat_tile = lambda j: jnp.maximum(j - first, 0)
    ul_spec = pl.BlockSpec((SEQ, W), lambda b, j: (1 + b, 0))
    w2_spec = pl.BlockSpec((tr, SEQ), lambda b, j: (lat_tile(j), 0))
    ol_spec = pl.BlockSpec((tr, W), lambda b, j: (b * nj + lat_tile(j), 0))
    ol_shape = jax.ShapeDtypeStruct((RL, W), BF16)
    if need_ctx:
        in_specs = [pl.BlockSpec((CTX_LEN, W), lambda b, j: (b, 0)), ul_spec, full(jrev), full(bdc), full(bds),
                    w2_spec, full(w2c)]
        out_specs = [pl.BlockSpec((CTX_LEN, W), lambda b, j: (b, 0)), ol_spec]
        out_shape = [jax.ShapeDtypeStruct((RC, W), BF16), ol_shape]
        args = (fu, fu, jrev, bdc, bds, w2, w2c)
    else:
        in_specs = [ul_spec, full(jrev), full(bdc), full(bds), w2_spec]
        out_specs = [ol_spec]
        out_shape = [ol_shape]
        args = (fu, jrev, bdc, bds, w2)
    outs = pl.pallas_call(
        functools.partial(_fourier_body, has_ctx=need_ctx),
        grid=(BATCH, nj + first),
        in_specs=in_specs,
        out_specs=out_specs,
        out_shape=out_shape,
        scratch_shapes=[pltpu.VMEM((SEQ, W), BF16), pltpu.VMEM((8, W), F32)],
        compiler_params=_params(("arbitrary", "arbitrary")),
        name="fourier_mix",
    )(*args)
    return (outs[0], outs[1]) if need_ctx else (None, outs[0])


def _retention_body(qc_ref, kc_ref, vc_ref, ql0_ref, kl0_ref, vl0_ref, ql1_ref, kl1_ref, vl1_ref, g_ref,
                    dcomb_ref, xif_ref, ztf_ref, xib_ref, ztb_ref, gf_ref, gb_ref, mbd_ref, avg_ref,
                    o_ref, os_ref, st_ref):
    j = pl.program_id(1)
    C = RET_CHUNK
    nl = SEQ // C
    lat_refs = ((ql0_ref, kl0_ref, vl0_ref), (ql1_ref, kl1_ref, vl1_ref))

    own = ((lax.broadcasted_iota(jnp.int32, (RET_GH * C, RET_GW), 0) >> (C.bit_length() - 1))
           == (lax.broadcasted_iota(jnp.int32, (RET_GH * C, RET_GW), 1) >> (RET_DK.bit_length() - 1)))

    def chunk_fwd(bb, q, k, v, r0):
        for gi in range(RET_NG):
            sl = slice(gi * RET_GW, (gi + 1) * RET_GW)
            q4, k4, v4 = q[:, sl], k[:, sl], v[:, sl]
            s_prev = st_ref[RET_NG * bb + gi]
            q4f = q4.astype(F32)
            o4 = _dot((q4f * xif_ref[:, sl]).astype(BF16), s_prev.astype(BF16))
            qstack = jnp.where(own, jnp.concatenate([q4f] * RET_GH, axis=0), 0.0).astype(BF16)
            p = (_dot_nt(qstack, k4) * dcomb_ref[gi]).astype(BF16)
            ov = jnp.where(own, _dot(p, v4), 0.0)
            intra = sum(ov[hh * C:(hh + 1) * C] for hh in range(RET_GH))
            os_ref[pl.ds(r0, C), sl] = o4 + intra
            u = _dot_tn(k4, (v4.astype(F32) * ztf_ref[:, sl]).astype(BF16))
            st_ref[RET_NG * bb + gi] = gf_ref[gi] * s_prev + mbd_ref[...] * u

    def chunk_bwd(bb, q, k, v, r0):
        for gi in range(RET_NG):
            sl = slice(gi * RET_GW, (gi + 1) * RET_GW)
            q4, k4, v4 = q[:, sl], k[:, sl], v[:, sl]
            s_prev = st_ref[RET_NG * bb + gi]
            cross = _dot((q4.astype(F32) * xib_ref[:, sl]).astype(BF16), s_prev.astype(BF16))
            os_ref[pl.ds(r0, C), sl] = os_ref[pl.ds(r0, C), sl] + cross
            u = _dot_tn(k4, (v4.astype(F32) * ztb_ref[:, sl]).astype(BF16))
            st_ref[RET_NG * bb + gi] = gb_ref[gi] * s_prev + mbd_ref[...] * u

    def scan(chunk, ctx_order, lat_index):
        st_ref[...] = jnp.zeros_like(st_ref)
        for c in ctx_order:
            for bb in range(2):
                rs = slice(bb * CTX_LEN + c * C, bb * CTX_LEN + (c + 1) * C)
                chunk(bb, qc_ref[rs, :], kc_ref[rs, :], vc_ref[rs, :], bb * CTX_LEN + c * C)

        def body(t, carry):
            r0 = pl.multiple_of(lat_index(t) * C, C)
            for bb, (q_ref, k_ref, v_ref) in enumerate(lat_refs):
                rs = pl.ds(r0, C)
                chunk(bb, q_ref[rs, :], k_ref[rs, :], v_ref[rs, :], 2 * CTX_LEN + bb * SEQ + r0)
            return carry

        lax.fori_loop(0, nl, body, 0)

    @pl.when(j == 0)
    def _():
        scan(chunk_fwd, range(CTX_LEN // C), lambda t: t)
        scan(chunk_bwd, reversed(range(CTX_LEN // C)), lambda t: nl - 1 - t)

    o = os_ref[pl.ds(pl.multiple_of(j * RET_SLAB, RET_SLAB), RET_SLAB), :]
    avg = avg_ref[...]
    oh, ol = _split(o)
    mu = _dot(oh, avg) + _dot(ol, avg)
    d = o - mu
    var = _dot((d * d).astype(BF16), avg)
    o_ref[...] = (g_ref[...].astype(F32) * d * lax.rsqrt(var + GN_EPS)).astype(BF16)


def _retention(qr, kr, vr, gr, rtabs):
    nslab = SEQ // RET_SLAB
    nj = 1 + 2 * nslab

    def out_map(b2, j):
        return (jnp.where(j == 0, b2, RC // RET_SLAB + 2 * b2 * nslab + j - 1), 0)

    ctx = pl.BlockSpec((2 * CTX_LEN, W), lambda b2, j: (b2, 0))
    lat = lambda k: pl.BlockSpec((SEQ, W), lambda b2, j: (1 + 2 * b2 + k, 0))
    full = lambda a: pl.BlockSpec(a.shape, lambda *_: (0,) * a.ndim)
    avg = jnp.asarray(np.kron(np.eye(RET_HEADS), np.full((RET_DK, RET_DK), 1.0 / RET_DK)).astype(np.float32)).astype(BF16)
    mbd = jnp.asarray(np.kron(np.eye(RET_GH), np.ones((RET_DK, RET_DK))).astype(np.float32))
    tabs = list(rtabs) + [mbd, avg]
    return pl.pallas_call(
        _retention_body,
        grid=(BATCH // 2, nj),
        in_specs=[ctx, ctx, ctx, lat(0), lat(0), lat(0), lat(1), lat(1), lat(1),
                  pl.BlockSpec((RET_SLAB, W), out_map)] + [full(t) for t in tabs],
        out_specs=pl.BlockSpec((RET_SLAB, W), out_map),
        out_shape=jax.ShapeDtypeStruct((R, W), BF16),
        scratch_shapes=[pltpu.VMEM((2 * (CTX_LEN + SEQ), W), F32), pltpu.VMEM((2 * RET_NG, RET_GW, RET_GW), F32)],
        compiler_params=_params(("arbitrary", "arbitrary")),
        name="retention",
    )(qr, kr, vr, qr, kr, vr, qr, kr, vr, gr, *tabs)


def _pack_pair(a, b):
    ua = lax.bitcast_convert_type(a, jnp.uint32) >> 16
    ub = lax.bitcast_convert_type(b, jnp.uint32) & jnp.uint32(0xFFFF0000)
    return ua | ub


def _unpack_pair(w):
    a = lax.bitcast_convert_type(w << 16, F32)
    b = lax.bitcast_convert_type(w & jnp.uint32(0xFFFF0000), F32)
    return a, b


def _slot_onehot(s0, s1, n):
    srow = lax.broadcasted_iota(jnp.int32, (n, s0.shape[1]), 0)
    p0 = jnp.where(srow == s0, 1.0, 0.0).astype(BF16)
    p1 = jnp.where(srow == s1, 1.0, 0.0).astype(BF16)
    return p0, p1


def _pick(first, a_ref, b_ref):
    a = a_ref[...]
    flag = jnp.zeros(a.shape, jnp.int32) + first.astype(jnp.int32)
    return jnp.where(flag > 0, a, b_ref[...])


def _merge_body(*refs, layer0):
    if layer0:
        oac_ref, oal_ref, ofc_ref, ofl_ref, rt_ref, gm_ref, xc_ref, xl_ref = refs[:8]
        rest = refs[8:]
        is_ctx = pl.program_id(0) < RC // TM_MERGE
        oa_in = _pick(is_ctx, oac_ref, oal_ref)
        of_in = _pick(is_ctx, ofc_ref, ofl_ref)
        x_in = _pick(is_ctx, xc_ref, xl_ref)
    else:
        oal_ref, ofl_ref, rt_ref, gm_ref, x_ref = refs[:5]
        rest = refs[5:]
        oa_in = oal_ref[...]
        of_in = ofl_ref[...]
        x_in = x_ref[...]
    (mod_ref, gn_ref, wba_ref, wbf_ref, wbr_ref, wout_ref, wrh_ref, wrl_ref, br_ref, tri_ref, ltri_ref,
     xo_ref, xs_ref, ro_ref, nch_ref) = rest
    gm = gm_ref[...].astype(F32)
    z = (gm[:, 0:D_MODEL] * _dot(oa_in, wba_ref[...])
         + gm[:, D_MODEL:2 * D_MODEL] * _dot(of_in, wbf_ref[...])
         + gm[:, 2 * D_MODEL:3 * D_MODEL] * _dot(rt_ref[...], wbr_ref[...]))
    y = _dot(z.astype(BF16), wout_ref[...])
    x = x_in + mod_ref[2:3, :] * y
    xo_ref[...] = x
    ms = jnp.mean(x * x, axis=-1, keepdims=True)
    hn = x * lax.rsqrt(ms + NORM_EPS) * gn_ref[...]
    h2 = hn * (1.0 + mod_ref[4:5, :]) + mod_ref[3:4, :]
    hh, hl = _split(h2)
    wh, wl = wrh_ref[...], wrl_ref[...]
    lg = _dot_nt(wh, hh) + _dot_nt(wh, hl) + _dot_nt(wl, hh) + br_ref[...]
    tm = lg.shape[1]
    row8 = lax.broadcasted_iota(jnp.int32, (8, tm), 0)
    lgg = lg[0:8, :]
    mg = jnp.max(lgg, axis=0, keepdims=True)
    grp = jnp.min(jnp.where(lgg == mg, row8, 8), axis=0, keepdims=True)
    pg = 1.0 / jnp.sum(jnp.exp(lgg - mg), axis=0, keepdims=True)
    lin = jnp.zeros((8, tm), F32)
    for g in range(N_GROUPS):
        lin = jnp.where(grp == g, lg[8 + 8 * g:16 + 8 * g, :], lin)
    v1 = jnp.max(lin, axis=0, keepdims=True)
    i1 = jnp.min(jnp.where(lin == v1, row8, 8), axis=0, keepdims=True)
    rest = jnp.where(row8 == i1, -jnp.inf, lin)
    v2 = jnp.max(rest, axis=0, keepdims=True)
    i2 = jnp.min(jnp.where(rest == v2, row8, 8), axis=0, keepdims=True)
    e2 = jnp.exp(v2 - v1)
    w1 = pg / (1.0 + e2)
    w2 = pg * e2 / (1.0 + e2)
    e_1 = grp * EXPERTS_PER_GROUP + i1
    e_2 = grp * EXPERTS_PER_GROUP + i2

    row32 = lax.broadcasted_iota(jnp.int32, (N_EXPERTS, tm), 0)
    oh0 = jnp.where(row32 == e_1, 1.0, 0.0)
    oh1 = jnp.where(row32 == e_2, 1.0, 0.0)
    tri = tri_ref[...]
    cum0 = _dot(oh0.astype(BF16), tri)
    cum1 = _dot(oh1.astype(BF16), tri)
    tot0 = jnp.sum(oh0, axis=1, keepdims=True)
    tot1 = jnp.sum(oh1, axis=1, keepdims=True)
    nch = ((tot0 + tot1).astype(jnp.int32) + (CHUNK - 1)) >> CHUNK_SHIFT
    nch_b = jnp.broadcast_to(nch.astype(F32), (N_EXPERTS, 128))
    nch_ref[...] = nch_b.astype(jnp.int32)
    base = CHUNK * _dot(ltri_ref[...], nch_b.astype(BF16))[:, 0:1]
    s0 = jnp.sum(oh0 * (base + cum0), axis=0, keepdims=True).astype(jnp.int32)
    s1 = jnp.sum(oh1 * (base + tot0 + cum1), axis=0, keepdims=True).astype(jnp.int32)
    p0, p1 = _slot_onehot(s0, s1, SLOTS)
    xs = _dot(p0 + p1, hh)
    xs_ref[:, 0:D_MODEL // 2] = _pack_pair(xs[:, 0:D_MODEL // 2], xs[:, D_MODEL // 2:D_MODEL])

    def wrows(w):
        hi, lo = _split(w)
        return jnp.where(row8 == 0, hi.astype(F32), jnp.where(row8 == 1, lo.astype(F32), 0.0)).astype(BF16)

    wc = _dot_nt(p0, wrows(w1)) + _dot_nt(p1, wrows(w2))
    wcol = jnp.broadcast_to(wc[:, 0:1] + wc[:, 1:2], (SLOTS, 128))
    xs_ref[:, D_MODEL // 2:XS_COLS] = lax.bitcast_convert_type(wcol, jnp.uint32)
    s0f, s1f = s0.astype(F32), s1.astype(F32)
    ro_ref[...] = jnp.where(row8 == 0, s0f, jnp.where(row8 == 1, s1f, 0.0))


def _merge(oa_c, oa_l, of_c, of_l, ret, gm, xs_in, mod_l, gnorm, wba, wbf, wbr, wout, wrh, wrl, brb, layer0):
    tm = TM_MERGE
    row0 = 0 if layer0 else RC
    rm = R - row0
    nt = rm // tm
    off = row0 // tm
    nc = RC // tm
    src = lambda n: pl.BlockSpec((tm, n), lambda i: (i + off, 0))
    dst = lambda n: pl.BlockSpec((tm, n), lambda i: (i, 0))
    ctx_rows = lambda n: pl.BlockSpec((tm, n), lambda i: (jnp.minimum(i, nc - 1), 0))
    lat_rows = lambda n: pl.BlockSpec((tm, n), lambda i: (jnp.maximum(i - nc, 0), 0))
    full = lambda a: pl.BlockSpec(a.shape, lambda *_: (0,) * a.ndim, pipeline_mode=pl.Buffered(1))
    mrow = _mod_row(row0, tm)
    tri = jnp.asarray(np.triu(np.ones((tm, tm), np.float32), 1)).astype(BF16)
    ltri = jnp.asarray(np.tril(np.ones((N_EXPERTS, N_EXPERTS), np.float32), -1)).astype(BF16)
    if layer0:
        acts = [oa_c, oa_l, of_c, of_l, ret, gm, xs_in[0], xs_in[1]]
        act_specs = [ctx_rows(W), lat_rows(W), ctx_rows(W), lat_rows(W), src(W), src(3 * D_MODEL),
                     ctx_rows(D_MODEL), lat_rows(D_MODEL)]
    else:
        acts = [oa_l, of_l, ret, gm, xs_in]
        act_specs = [dst(W), dst(W), src(W), src(3 * D_MODEL), src(D_MODEL)]
    return pl.pallas_call(
        functools.partial(_merge_body, layer0=layer0),
        grid=(nt,),
        in_specs=act_specs + [
                  pl.BlockSpec((None, 6, D_MODEL), lambda i: (mrow(i), 0, 0)),
                  full(gnorm), full(wba), full(wbf), full(wbr), full(wout), full(wrh), full(wrl), full(brb),
                  full(tri), full(ltri)],
        out_specs=[dst(D_MODEL), pl.BlockSpec((SLOTS, XS_COLS), lambda i: (i, 0)),
                   pl.BlockSpec((8, tm), lambda i: (0, i)),
                   pl.BlockSpec((None, N_EXPERTS, 128), lambda i: (i, 0, 0))],
        out_shape=[jax.ShapeDtypeStruct((rm, D_MODEL), F32),
                   jax.ShapeDtypeStruct((nt * SLOTS, XS_COLS), jnp.uint32),
                   jax.ShapeDtypeStruct((8, rm), F32),
                   jax.ShapeDtypeStruct((nt, N_EXPERTS, 128), jnp.int32)],
        compiler_params=_params(("arbitrary",)),
        name="merge_router",
    )(*acts, mod_l, gnorm, wba, wbf, wbr, wout, wrh, wrl, brb, tri, ltri)


def _moe_plan(nch, nb):
    nt = nch.shape[0]
    choff = jnp.cumsum(nch, axis=1) - nch
    used_ch = jnp.sum(nch, axis=1)
    cum_t = jnp.cumsum(nch, axis=0)
    tot = cum_t[-1]
    ptot = (tot + CPB - 1) // CPB * CPB
    pend = jnp.cumsum(ptot)
    pstart = pend - ptot
    n_used = pend[-1] // CPB
    blk = jnp.arange(nb, dtype=jnp.int32)
    lane = jnp.arange(CPB, dtype=jnp.int32)
    blk_e = jnp.minimum(jnp.sum((blk[:, None] * CPB >= pend[None, :]).astype(jnp.int32), axis=1), N_EXPERTS - 1)
    oe = (blk_e[:, None] == jnp.arange(N_EXPERTS, dtype=jnp.int32)[None, :]).astype(jnp.int32)
    sel = lambda tab: jnp.sum(oe[:, :, None] * tab.T[None, :, :], axis=1)
    pstart_b = jnp.sum(oe * pstart[None, :], axis=1)
    tot_b = jnp.sum(oe * tot[None, :], axis=1)
    cum_b, nch_b, choff_b = sel(cum_t), sel(nch), sel(choff)
    i = blk[:, None] * CPB + lane[None, :] - pstart_b[:, None]
    valid = (i < tot_b[:, None]) & (blk[:, None] < n_used)
    t = jnp.minimum(jnp.sum((i[:, :, None] >= cum_b[:, None, :]).astype(jnp.int32), axis=2), nt - 1)
    tiles = jnp.arange(nt, dtype=jnp.int32)[None, None, :]
    before = jnp.sum(jnp.where(tiles < t[:, :, None], nch_b[:, None, :], 0), axis=2)
    coff = jnp.sum(jnp.where(tiles == t[:, :, None], choff_b[:, None, :], 0), axis=2)
    row = t * SLOTS + CHUNK * (coff + i - before)
    src = jnp.where(valid, row, SLOTS - CHUNK)
    dummy = nt * SLOTS + CHUNK * ((blk[:, None] % 2) * CPB + lane[None, :])
    dst = jnp.where(valid, row, dummy)
    blk_start = jnp.concatenate([pstart, pend[-1:]]) // CPB
    return (blk_start.astype(jnp.int32), n_used.astype(jnp.int32).reshape(1), src.reshape(-1).astype(jnp.int32),
            dst.reshape(-1).astype(jnp.int32), used_ch.astype(jnp.int32))


def _ffn_body(bs_ref, nu_ref, src_ref, dst_ref, uc_ref, xs_ref, wg_ref, wu_ref, wd_ref, ys_ref,
              xbuf, ybuf, zbuf, wgb, wub, wdb, sem_in, sem_out, sem_z, *, nt):
    e = pl.program_id(0)
    nu = nu_ref[0]
    half = D_MODEL // 2

    def gather(blk, sl):
        for c in range(CPB):
            r = pl.multiple_of(src_ref[blk * CPB + c], CHUNK)
            pltpu.make_async_copy(xs_ref.at[pl.ds(r, CHUNK)], xbuf.at[sl, pl.ds(c * CHUNK, CHUNK)],
                                  sem_in.at[sl]).start()

    def scatter(blk, sl):
        for c in range(CPB):
            r = pl.multiple_of(dst_ref[blk * CPB + c], CHUNK)
            pltpu.make_async_copy(ybuf.at[sl, pl.ds(c * CHUNK, CHUNK)], ys_ref.at[pl.ds(r, CHUNK)],
                                  sem_out.at[sl]).start()

    def wait_gather(sl):
        pltpu.make_async_copy(xs_ref.at[pl.ds(0, MOE_BM)], xbuf.at[sl], sem_in.at[sl]).wait()

    def wait_scatter(sl):
        pltpu.make_async_copy(ybuf.at[sl], ys_ref.at[pl.ds(0, MOE_BM)], sem_out.at[sl]).wait()

    def zero_copy(r):
        return pltpu.make_async_copy(zbuf, ys_ref.at[pl.ds(pl.multiple_of(r, CHUNK), CHUNK)], sem_z)

    @pl.when(e == 0)
    def _():
        zbuf[...] = jnp.zeros_like(zbuf)

        def tails(fn):
            def per_tile(t, carry):
                def per_chunk(c, carry2):
                    fn(t * SLOTS + c * CHUNK)
                    return carry2
                lax.fori_loop(uc_ref[t], SLOTS // CHUNK, per_chunk, 0)
                return carry
            lax.fori_loop(0, nt, per_tile, 0)
            for c in range(2 * CPB):
                fn(nt * SLOTS + c * CHUNK)

        tails(lambda r: zero_copy(r).start())
        tails(lambda r: zero_copy(r).wait())
        gather(0, 0)
        for d in range(1, GATHER_DEPTH - 1):
            @pl.when(nu > d)
            def _():
                gather(d, d)

    b0, b1 = bs_ref[e], bs_ref[e + 1]

    @pl.when(b1 > b0)
    def _():
        wgb[...] = wg_ref[...].astype(BF16)
        wub[...] = wu_ref[...].astype(BF16)
        wdb[...] = wd_ref[...].astype(BF16)

        def block(b, carry):
            slot = b % 2
            xslot = lax.rem(b, GATHER_DEPTH)

            @pl.when(b + GATHER_DEPTH - 1 < nu)
            def _():
                gather(b + GATHER_DEPTH - 1, lax.rem(b + GATHER_DEPTH - 1, GATHER_DEPTH))

            wait_gather(xslot)

            @pl.when(b >= 2)
            def _():
                wait_scatter(slot)

            xw = xbuf[xslot]
            xa, xb = _unpack_pair(xw[:, 0:half])
            x = jnp.concatenate([xa, xb], axis=1).astype(BF16)
            wt = lax.bitcast_convert_type(xw[:, half:XS_COLS], F32)
            g = _dot(x, wgb[...])
            u = _dot(x, wub[...])
            hmid = (g * _sigmoid(g) * u).astype(BF16)
            y = _dot(hmid, wdb[...]) * jnp.concatenate([wt] * (D_MODEL // 128), axis=1)
            yb = y.astype(BF16).astype(F32)
            ybuf[slot] = _pack_pair(yb[:, 0:half], yb[:, half:D_MODEL])
            scatter(b, slot)
            return carry

        lax.fori_loop(b0, b1, block, 0)

    @pl.when(e == N_EXPERTS - 1)
    def _():
        wait_scatter((nu - 1) % 2)

        @pl.when(nu >= 2)
        def _():
            wait_scatter(nu % 2)


def _ffn(plan, xs, w_g, w_u, w_d, layer, nt):
    wmap = lambda e, *_: (layer, e, 0, 0)
    half = D_MODEL // 2
    grid_spec = pltpu.PrefetchScalarGridSpec(
        num_scalar_prefetch=5,
        grid=(N_EXPERTS,),
        in_specs=[pl.BlockSpec(memory_space=pl.ANY),
                  pl.BlockSpec((None, None, D_MODEL, EXPERT_HIDDEN), wmap),
                  pl.BlockSpec((None, None, D_MODEL, EXPERT_HIDDEN), wmap),
                  pl.BlockSpec((None, None, EXPERT_HIDDEN, D_MODEL), wmap)],
        out_specs=pl.BlockSpec(memory_space=pl.ANY),
        scratch_shapes=[pltpu.VMEM((GATHER_DEPTH, MOE_BM, XS_COLS), jnp.uint32),
                        pltpu.VMEM((2, MOE_BM, half), jnp.uint32),
                        pltpu.VMEM((CHUNK, half), jnp.uint32),
                        pltpu.VMEM((D_MODEL, EXPERT_HIDDEN), BF16), pltpu.VMEM((D_MODEL, EXPERT_HIDDEN), BF16),
                        pltpu.VMEM((EXPERT_HIDDEN, D_MODEL), BF16),
                        pltpu.SemaphoreType.DMA((GATHER_DEPTH,)), pltpu.SemaphoreType.DMA((2,)),
                        pltpu.SemaphoreType.DMA(())],
    )
    return pl.pallas_call(
        functools.partial(_ffn_body, nt=nt),
        grid_spec=grid_spec,
        out_shape=jax.ShapeDtypeStruct((nt * SLOTS + 2 * CPB * CHUNK, half), jnp.uint32),
        compiler_params=_params(("arbitrary",)),
        name="moe_experts",
    )(*plan, xs, w_g, w_u, w_d)


def _combine_body(ys_ref, ro_ref, x_ref, mod_ref, gn_ref, o_ref, *, final):
    tm = TM_MERGE
    for t in range(COMBINE_TILES):
        cols = slice(t * tm, (t + 1) * tm)
        s = ro_ref[:, cols]
        p0, p1 = _slot_onehot(s[0:1, :].astype(jnp.int32), s[1:2, :].astype(jnp.int32), SLOTS)
        ya, yb = _unpack_pair(ys_ref[t * SLOTS:(t + 1) * SLOTS, :])
        y = jnp.concatenate([ya, yb], axis=1).astype(BF16)
        f = _dot_tn(p0 + p1, y)
        x = x_ref[cols, :] + mod_ref[5:6, :] * f
        if final:
            ms = jnp.mean(x * x, axis=-1, keepdims=True)
            x = x * lax.rsqrt(ms + NORM_EPS) * gn_ref[...]
        o_ref[cols, :] = x


def _combine(ys, route, x, mod_l, gnorm, row0, final):
    tm = COMBINE_TILES * TM_MERGE
    rm = x.shape[0]
    mrow = _mod_row(row0, tm)
    return pl.pallas_call(
        functools.partial(_combine_body, final=final),
        grid=(rm // tm,),
        in_specs=[pl.BlockSpec((COMBINE_TILES * SLOTS, D_MODEL // 2), lambda i: (i, 0)),
                  pl.BlockSpec((8, tm), lambda i: (0, i)),
                  pl.BlockSpec((tm, D_MODEL), lambda i: (i, 0)),
                  pl.BlockSpec((None, 6, D_MODEL), lambda i: (mrow(i), 0, 0)),
                  pl.BlockSpec((1, D_MODEL), lambda i: (0, 0))],
        out_specs=pl.BlockSpec((tm, D_MODEL), lambda i: (i, 0)),
        out_shape=jax.ShapeDtypeStruct((rm, D_MODEL), F32),
        compiler_params=_params(("arbitrary",)),
        name="moe_combine",
    )(ys, route, x, mod_l, gnorm)


def _moe(xs, route, nch3, x, mod_l, gnorm, w_g, w_u, w_d, row0, final, layer):
    nt = nch3.shape[0]
    max_chunks = nt * ((2 * TM_MERGE + N_EXPERTS * (CHUNK - 1)) // CHUNK)
    nb = -(-max_chunks // CPB) + N_EXPERTS
    plan = _moe_plan(nch3[:, :, 0], nb)
    ys = _ffn(plan, xs, w_g, w_u, w_d, layer, nt)
    return _combine(ys, route, x, mod_l, gnorm, row0, final)


def kernel(x, c, ctx, c_ctx, norm_mix, norm_ffn, w_ada, b_ada, w_in, attn_sink, ret_decay_fwd, ret_decay_bwd,
           w_branch_attn, w_branch_fourier, w_branch_ret, w_out, w_router_group, b_router_group,
           w_router_expert, b_router_expert, w_exp_gate, w_exp_up, w_exp_down, norm_final):
    assert DEPTH == 2, "layer 0 reads (ctx, x) separately and keeps its context rows; the last layer does not"
    tabs = [jnp.asarray(t) for t in _rope_tables()]
    dft = [jnp.asarray(t).astype(BF16) for t in _dft_tables()]

    cc = jnp.zeros((MOD_ROWS, D_MODEL), F32).at[0:BATCH].set(c).at[CTX_MOD_ROW].set(c_ctx)
    mod = _ada(cc, w_ada, b_ada).reshape(DEPTH, MOD_ROWS, 6, D_MODEL)

    xf = (ctx.reshape(RC, D_MODEL), x.reshape(RL, D_MODEL))
    w_in_bf = w_in.astype(BF16)
    for l in range(DEPTH):
        need_ctx = l < DEPTH - 1
        row0 = 0 if need_ctx else RC
        mod_l = mod[l]
        qa, ka, va, qr, kr, vr, gr, fu, gm = _proj(xf, mod_l, norm_mix[l][None, :], w_in_bf, tabs, l)
        oa_c, oa_l = _attention(attn_sink[l], qa, ka, va, need_ctx)
        of_c, of_l = _fourier(fu, dft, need_ctx)
        ret = _retention(qr, kr, vr, gr, _retention_tables(ret_decay_fwd[l], ret_decay_bwd[l]))
        wr = jnp.zeros((ROUTER_ROWS, D_MODEL), F32)
        wr = wr.at[0:N_GROUPS].set(w_router_group[l].T).at[8:8 + N_EXPERTS].set(w_router_expert[l].T)
        br = jnp.full((ROUTER_ROWS,), NEG, F32)
        br = br.at[0:N_GROUPS].set(b_router_group[l]).at[8:8 + N_EXPERTS].set(b_router_expert[l])
        wrh, wrl = _split(wr)
        brb = jnp.broadcast_to(br[:, None], (ROUTER_ROWS, TM_MERGE))
        x_mid, xs, route, nch3 = _merge(oa_c, oa_l, of_c, of_l, ret, gm, xf, mod_l, norm_ffn[l][None, :],
                                        w_branch_attn[l].astype(BF16), w_branch_fourier[l].astype(BF16),
                                        w_branch_ret[l].astype(BF16), w_out[l].astype(BF16), wrh, wrl, brb,
                                        need_ctx)
        final = l == DEPTH - 1
        xf = _moe(xs, route, nch3, x_mid, mod_l, norm_final[None, :], w_exp_gate, w_exp_up, w_exp_down,
                  row0, final, l)
    return xf.reshape(BATCH, SEQ, D_MODEL)
```

```python
import functools

import numpy as np
import jax
import jax.numpy as jnp
from jax import lax
from jax.experimental import pallas as pl
from jax.experimental.pallas import tpu as pltpu

F32 = jnp.float32
BF16 = jnp.bfloat16

D_MODEL = 1024
BATCH = 8
SEQ = 2048
DEPTH = 2
CTX_LEN = 256
GRID_W = 64
HEAD_DIM = 64
ATTN_HEADS = 8
ATTN_KV_HEADS = 2
ATTN_GROUP = ATTN_HEADS // ATTN_KV_HEADS
ATTN_GW = ATTN_GROUP * HEAD_DIM
ATTN_BLOCK = 128
ATTN_QB = 16
RET_HEADS = 8
RET_DK = 64
RET_GH = 4
RET_GW = RET_GH * RET_DK
RET_NG = RET_HEADS // RET_GH
RET_CHUNK = 256
FOURIER_GROUPS = 4
FOURIER_DIM = 128
N_GROUPS = 4
EXPERTS_PER_GROUP = 8
N_EXPERTS = N_GROUPS * EXPERTS_PER_GROUP
EXPERT_HIDDEN = 512
ROPE_BASE = 10000.0
NORM_EPS = 1e-6
GN_EPS = 1e-5

W = 512
IN_COLS = 6400
RC = BATCH * CTX_LEN
RL = BATCH * SEQ
R = RC + RL
MOD_ROWS = 16
CTX_MOD_ROW = 8

VMEM_LIMIT = 52 * 1024 * 1024

TM_PROJ = 512
TM_MERGE = 512
COMBINE_TILES = 2
TN_ADA = 3072
TR_FOURIER = 2048
RET_SLAB = 2 * CTX_LEN
MOE_BM = 512
CHUNK = 8
CHUNK_SHIFT = 3
CPB = MOE_BM // CHUNK
GATHER_DEPTH = 3
SLOTS = 1280
XS_COLS = D_MODEL // 2 + 128
NEG = -1e30
LOG2E = 1.4426950408889634
ROUTER_ROWS = 40


def _dot(a, b):
    return jnp.dot(a, b, preferred_element_type=F32)


def _dot_nt(a, b):
    return lax.dot_general(a, b, (((1,), (1,)), ((), ())), preferred_element_type=F32)


def _dot_tn(a, b):
    return lax.dot_general(a, b, (((0,), (0,)), ((), ())), preferred_element_type=F32)


def _split(x):
    hi = x.astype(BF16)
    lo = (x - hi.astype(F32)).astype(BF16)
    return hi, lo


def _sigmoid(x):
    return 1.0 / (1.0 + jnp.exp(-x))


def _params(sem, vmem=VMEM_LIMIT):
    return pltpu.CompilerParams(dimension_semantics=sem, vmem_limit_bytes=vmem)


def _mod_row(row0, tm):
    def f(i):
        g0 = i * tm + row0
        return jnp.where(g0 < RC, CTX_MOD_ROW, (g0 - RC) // SEQ)
    return f


def _rope_tables():
    pos = np.arange(SEQ, dtype=np.float64)
    row = np.floor(pos / GRID_W)
    col = pos % GRID_W

    def cs(p, nf):
        inv = ROPE_BASE ** (-np.arange(nf, dtype=np.float64) / nf)
        ang = p[:, None] * inv[None, :]
        return np.cos(ang), np.sin(ang)

    rc, rs = cs(row, HEAD_DIM // 4)
    cc, cs_ = cs(col, HEAD_DIM // 4)
    cos_a = np.concatenate([rc, rc, cc, cc], axis=1)
    sin_a = np.concatenate([-rs, rs, -cs_, cs_], axis=1)
    tc, ts = cs(pos, RET_DK // 2)
    cos_r = np.concatenate([tc, tc], axis=1)
    sin_r = np.concatenate([-ts, ts], axis=1)

    def full(t, ident):
        t2 = np.concatenate([t, t], axis=1)
        return np.concatenate([np.full_like(t2, ident), t2], axis=0).astype(np.float32)

    return full(cos_a, 1.0), full(sin_a, 0.0), full(cos_r, 1.0), full(sin_r, 0.0)


def _dft_tables():
    def cs(n):
        k = np.arange(n, dtype=np.int64)
        m = (k[:, None] * k[None, :]) % n
        ang = 2.0 * np.pi * m.astype(np.float64) / n
        return np.cos(ang), np.sin(ang)

    c128, s128 = cs(FOURIER_DIM)
    eye = np.eye(FOURIER_GROUPS)
    bdc = np.kron(eye, c128).astype(np.float32)
    bds = np.kron(eye, s128).astype(np.float32)
    cn, sn = cs(SEQ)
    hn = SEQ // 2
    w2 = np.concatenate([cn[:, :hn], -sn[:, :hn]], axis=1).astype(np.float32)
    cl, sl = cs(CTX_LEN)
    w2c = np.concatenate([cl, -sl], axis=1).astype(np.float32)
    return bdc, bds, w2, w2c


def _retention_tables(dec_f, dec_b):
    lg_f = jax.nn.log_sigmoid(dec_f.astype(F32))
    lg_b = jax.nn.log_sigmoid(dec_b.astype(F32))
    i = jnp.arange(RET_CHUNK)
    diff = (i[:, None] - i[None, :]).astype(F32)
    fwd = jnp.exp(jnp.maximum(diff, 0.0)[None] * lg_f[:, None, None])
    bwd = jnp.exp(jnp.maximum(-diff, 0.0)[None] * lg_b[:, None, None])
    dcomb = jnp.where((diff >= 0)[None], fwd, bwd).reshape(RET_NG, RET_GH * RET_CHUNK, RET_CHUNK)
    fi = i.astype(F32)
    lanes = lambda t: jnp.repeat(t, RET_DK, axis=1)
    xi_f = lanes(jnp.exp((fi + 1.0)[:, None] * lg_f[None, :]))
    zt_f = lanes(jnp.exp((RET_CHUNK - 1 - fi)[:, None] * lg_f[None, :]))
    xi_b = lanes(jnp.exp((RET_CHUNK - fi)[:, None] * lg_b[None, :]))
    zt_b = lanes(jnp.exp(fi[:, None] * lg_b[None, :]))
    g_f = jnp.repeat(jnp.exp(RET_CHUNK * lg_f), RET_DK).reshape(RET_NG, RET_GW, 1)
    g_b = jnp.repeat(jnp.exp(RET_CHUNK * lg_b), RET_DK).reshape(RET_NG, RET_GW, 1)
    g_f = jnp.broadcast_to(g_f, (RET_NG, RET_GW, RET_GW))
    g_b = jnp.broadcast_to(g_b, (RET_NG, RET_GW, RET_GW))
    return dcomb, xi_f, zt_f, xi_b, zt_b, g_f, g_b


def _ada_body(c_ref, w_ref, b_ref, o_ref):
    c = c_ref[...]
    s = c * _sigmoid(c)
    sh, sl = _split(s)
    wh, wl = _split(w_ref[...])
    o_ref[...] = _dot(sh, wh) + _dot(sl, wh) + _dot(sh, wl) + b_ref[...]


def _ada(cc, w_ada, b_ada):
    nt = 6 * D_MODEL // TN_ADA
    return pl.pallas_call(
        _ada_body,
        grid=(DEPTH, nt),
        in_specs=[
            pl.BlockSpec((MOD_ROWS, D_MODEL), lambda l, j: (0, 0)),
            pl.BlockSpec((None, D_MODEL, TN_ADA), lambda l, j: (l, 0, j)),
            pl.BlockSpec((None, 1, TN_ADA), lambda l, j: (l, 0, j)),
        ],
        out_specs=pl.BlockSpec((None, MOD_ROWS, TN_ADA), lambda l, j: (l, 0, j)),
        out_shape=jax.ShapeDtypeStruct((DEPTH, MOD_ROWS, 6 * D_MODEL), F32),
        compiler_params=_params(("arbitrary", "arbitrary")),
        name="ada_mod",
    )(cc, w_ada, b_ada.reshape(DEPTH, 1, 6 * D_MODEL))


def _rope(xc, cos, sin, half):
    fwd = pltpu.roll(xc, 128 - half, axis=1)
    bwd = pltpu.roll(xc, half, axis=1)
    lane = lax.broadcasted_iota(jnp.int32, xc.shape, 1)
    first = (lane & (2 * half - 1)) < half
    return xc * cos + jnp.where(first, fwd, bwd) * sin


def _proj_body(*refs, split):
    if split:
        x = _pick(pl.program_id(0) < RC // TM_PROJ, refs[0], refs[1])
        refs = refs[2:]
    else:
        x = refs[0][...]
        refs = refs[1:]
    (mod_ref, gn_ref, w_ref, ca_ref, sa_ref, cr_ref, sr_ref,
     qa_ref, ka_ref, va_ref, qr_ref, kr_ref, vr_ref, gr_ref, fu_ref, gm_ref) = refs
    ms = jnp.mean(x * x, axis=-1, keepdims=True)
    y = x * lax.rsqrt(ms + NORM_EPS) * gn_ref[...]
    h = y * (1.0 + mod_ref[1:2, :]) + mod_ref[0:1, :]
    hb = h.astype(BF16)

    def proj(c0, width):
        return _dot(hb, w_ref[:, c0:c0 + width])

    ca, sa, cr, sr = ca_ref[...], sa_ref[...], cr_ref[...], sr_ref[...]

    for c in range(3):
        gm_ref[:, c * D_MODEL:(c + 1) * D_MODEL] = _sigmoid(proj(3328 + c * D_MODEL, D_MODEL)).astype(BF16)
    g = proj(2304, W)
    gr_ref[...] = (g * _sigmoid(g)).astype(BF16)
    qa = proj(0, W) * (HEAD_DIM ** -0.5 * LOG2E)
    for c in range(W // 128):
        qa_ref[:, c * 128:(c + 1) * 128] = _rope(qa[:, c * 128:(c + 1) * 128], ca, sa, 16).astype(BF16)
    kv = proj(W, 256)
    ka = _rope(kv[:, 0:128], ca, sa, 16).astype(BF16)
    ka_ref[0] = ka[:, 0:64]
    ka_ref[1] = ka[:, 64:128]
    va = kv[:, 128:256].astype(BF16)
    ones_col = jnp.where(lax.broadcasted_iota(jnp.int32, (va.shape[0], 64), 1) == 0, 1.0, 0.0).astype(BF16)
    va_ref[0] = jnp.concatenate([va[:, 0:64], ones_col], axis=1)
    va_ref[1] = jnp.concatenate([va[:, 64:128], ones_col], axis=1)
    qr = proj(768, W)
    kr = proj(1280, W) * (RET_DK ** -0.5)
    for c in range(W // 128):
        sl = slice(c * 128, (c + 1) * 128)
        qr_ref[:, sl] = _rope(qr[:, sl], cr, sr, 32).astype(BF16)
        kr_ref[:, sl] = _rope(kr[:, sl], cr, sr, 32).astype(BF16)
    vr_ref[...] = proj(1792, W).astype(BF16)
    fu_ref[...] = proj(2816, W).astype(BF16)


def _proj(x, mod_l, gnorm, w_in_bf, tabs, layer):
    tm = TM_PROJ
    nt = R // tm
    nc = RC // tm
    split = isinstance(x, tuple)
    if split:
        xs = list(x)
        x_specs = [pl.BlockSpec((tm, D_MODEL), lambda i: (jnp.minimum(i, nc - 1), 0)),
                   pl.BlockSpec((tm, D_MODEL), lambda i: (jnp.maximum(i - nc, 0), 0))]
    else:
        xs = [x]
        x_specs = [pl.BlockSpec((tm, D_MODEL), lambda i: (i, 0))]

    def tab_map(i):
        return (jnp.where(i < nc, i, nc + (i - nc) % (SEQ // tm)), 0)

    row = lambda i: (i, 0)
    wide = lambda n: pl.BlockSpec((tm, n), row)
    kv_spec = lambda n: pl.BlockSpec((2, tm, n), lambda i: (0, i, 0))
    sds = lambda n: jax.ShapeDtypeStruct((R, n), BF16)
    kv_sds = lambda n: jax.ShapeDtypeStruct((2, R, n), BF16)
    mrow = _mod_row(0, tm)
    return pl.pallas_call(
        functools.partial(_proj_body, split=split),
        grid=(nt,),
        in_specs=x_specs + [
            pl.BlockSpec((None, 6, D_MODEL), lambda i: (mrow(i), 0, 0)),
            pl.BlockSpec((1, D_MODEL), lambda i: (0, 0)),
            pl.BlockSpec((None, D_MODEL, IN_COLS), lambda i: (layer, 0, 0), pipeline_mode=pl.Buffered(1)),
        ] + [pl.BlockSpec((tm, 128), tab_map)] * 4,
        out_specs=[wide(W), kv_spec(64), kv_spec(128), wide(W), wide(W), wide(W), wide(W), wide(W),
                   wide(3 * D_MODEL)],
        out_shape=[sds(W), kv_sds(64), kv_sds(128), sds(W), sds(W), sds(W), sds(W), sds(W), sds(3 * D_MODEL)],
        compiler_params=_params(("arbitrary",)),
        name="in_proj",
    )(*xs, mod_l, gnorm, w_in_bf, *tabs)


def _attend(sink_ref, q_ref, o_ref, sub, pieces):
    rows = slice(sub * ATTN_BLOCK, (sub + 1) * ATTN_BLOCK)
    groups = [slice(g * ATTN_BLOCK, (g + 1) * ATTN_BLOCK) for g in range(ATTN_GROUP)]
    outs = []
    for h in range(ATTN_KV_HEADS):
        q = q_ref[rows, h * ATTN_GW:(h + 1) * ATTN_GW]
        q4 = jnp.concatenate([q[:, g * HEAD_DIM:(g + 1) * HEAD_DIM] for g in range(ATTN_GROUP)], axis=0)
        sinks = [sink_ref[h * ATTN_GROUP + g] * LOG2E for g in range(ATTN_GROUP)]
        k_all = jnp.concatenate([k_ref[h] for k_ref, _, _ in pieces], axis=0)
        v_all = jnp.concatenate([v_ref[h] for _, v_ref, _ in pieces], axis=0)
        s = _dot_nt(q4, k_all)
        cols, c0 = [], 0
        for k_ref, _, ok in pieces:
            n = k_ref.shape[1]
            cols.append(s[:, c0:c0 + n] if ok is None else jnp.where(ok, s[:, c0:c0 + n], NEG))
            c0 += n
        s = jnp.concatenate(cols, axis=1)
        mxs, ps = [], []
        for g, r in enumerate(groups):
            mx = jnp.maximum(jnp.max(s[r], axis=-1, keepdims=True), sinks[g])
            mxs.append(mx)
            ps.append(jnp.exp2(s[r] - mx).astype(BF16))
        oa = _dot(jnp.concatenate(ps, axis=0), v_all)
        for g, r in enumerate(groups):
            den = oa[r, HEAD_DIM:HEAD_DIM + 1] + jnp.exp2(sinks[g] - mxs[g])
            outs.append(oa[r, 0:HEAD_DIM] / den)
    o_ref[rows, :] = jnp.concatenate(outs, axis=1).astype(BF16)


def _attn_lat_body(sink_ref, q_ref, kc_ref, *refs):
    nk = ATTN_QB + 2
    k_refs, vc_ref, v_refs, o_ref = refs[:nk], refs[nk], refs[nk + 1:2 * nk + 1], refs[2 * nk + 1]
    p = pl.program_id(1)
    nr = ATTN_GROUP * ATTN_BLOCK
    ri = lax.broadcasted_iota(jnp.int32, (nr, ATTN_BLOCK), 0) & (ATTN_BLOCK - 1)
    ci = lax.broadcasted_iota(jnp.int32, (nr, ATTN_BLOCK), 1)
    far = 4 * ATTN_BLOCK
    first_prev = ci >= ri + jnp.where(p >= 1, 0, far)
    last_next = ci + jnp.where(p <= SEQ // (ATTN_QB * ATTN_BLOCK) - 2, 0, far) <= ri
    for sub in range(ATTN_QB):
        prev_ok = first_prev if sub == 0 else ci >= ri
        next_ok = last_next if sub == ATTN_QB - 1 else ci <= ri
        _attend(sink_ref, q_ref, o_ref, sub,
                [(kc_ref, vc_ref, None), (k_refs[sub], v_refs[sub], prev_ok), (k_refs[sub + 1], v_refs[sub + 1], None),
                 (k_refs[sub + 2], v_refs[sub + 2], next_ok)])


def _attn_ctx_body(sink_ref, q_ref, kc_ref, vc_ref, o_ref):
    for sub in range(CTX_LEN // ATTN_BLOCK):
        _attend(sink_ref, q_ref, o_ref, sub, [(kc_ref, vc_ref, None)])


def _attention(sink, qa, ka, va, need_ctx):
    nb = SEQ // ATTN_BLOCK
    nstep = nb // ATTN_QB
    tq = ATTN_QB * ATTN_BLOCK
    smem = pl.BlockSpec(memory_space=pltpu.SMEM)

    def loc(delta):
        def f(b, p):
            m = jnp.clip(ATTN_QB * p + delta, 0, nb - 1)
            return (0, RC // ATTN_BLOCK + b * nb + m, 0)
        return f

    def kv_specs(width):
        ctx_spec = pl.BlockSpec((ATTN_KV_HEADS, CTX_LEN, width), lambda b, p: (0, b, 0))
        return [ctx_spec] + [pl.BlockSpec((ATTN_KV_HEADS, ATTN_BLOCK, width), loc(d)) for d in range(-1, ATTN_QB + 1)]

    nloc = ATTN_QB + 3
    oa_l = pl.pallas_call(
        _attn_lat_body,
        grid=(BATCH, nstep),
        in_specs=[smem, pl.BlockSpec((tq, W), lambda b, p: (RC // tq + b * nstep + p, 0))]
                 + kv_specs(HEAD_DIM) + kv_specs(128),
        out_specs=pl.BlockSpec((tq, W), lambda b, p: (b * nstep + p, 0)),
        out_shape=jax.ShapeDtypeStruct((RL, W), BF16),
        compiler_params=_params(("arbitrary", "arbitrary")),
        name="window_attn",
    )(sink, qa, *([ka] * nloc), *([va] * nloc))
    if not need_ctx:
        return None, oa_l
    oa_c = pl.pallas_call(
        _attn_ctx_body,
        grid=(BATCH,),
        in_specs=[smem, pl.BlockSpec((CTX_LEN, W), lambda b: (b, 0)),
                  pl.BlockSpec((ATTN_KV_HEADS, CTX_LEN, HEAD_DIM), lambda b: (0, b, 0)),
                  pl.BlockSpec((ATTN_KV_HEADS, CTX_LEN, 128), lambda b: (0, b, 0))],
        out_specs=pl.BlockSpec((CTX_LEN, W), lambda b: (b, 0)),
        out_shape=jax.ShapeDtypeStruct((RC, W), BF16),
        compiler_params=_params(("arbitrary",)),
        name="ctx_attn",
    )(sink, qa, ka, va)
    return oa_c, oa_l


def _fourier_body(*refs, has_ctx):
    if has_ctx:
        uc_ref, ul_ref, jrev_ref, bdc_ref, bds_ref, w2_ref, w2c_ref, oc_ref, ol_ref, as_ref, mid_ref = refs
    else:
        ul_ref, jrev_ref, bdc_ref, bds_ref, w2_ref, ol_ref, as_ref, mid_ref = refs
    j = pl.program_id(1)
    first = 1 if has_ctx else 0
    hn = SEQ // 2

    if has_ctx:
        @pl.when(j == 0)
        def _():
            u = uc_ref[...]
            a = _dot(u, bdc_ref[...]).astype(BF16)
            s = _dot(u, bds_ref[...]).astype(BF16)
            z = _dot(w2c_ref[...], jnp.concatenate([a, s], axis=0))
            oc_ref[...] = (z * ((CTX_LEN * FOURIER_DIM) ** -0.5)).astype(BF16)

    @pl.when(j == first)
    def _():
        uh = ul_ref[0:hn, :].astype(F32)
        ur = _dot(jrev_ref[...], ul_ref[hn:SEQ, :])
        row = lax.broadcasted_iota(jnp.int32, (hn, W), 0)
        vm = jnp.where(row == 0, 0.0, uh - ur)
        as_ref[0:hn, :] = _dot((uh + ur).astype(BF16), bdc_ref[...]).astype(BF16)
        as_ref[hn:SEQ, :] = _dot(vm.astype(BF16), bds_ref[...]).astype(BF16)
        mid_ref[...] = _dot(ul_ref[hn:hn + 8, :], bdc_ref[...])

    @pl.when(j >= first)
    def _():
        z = _dot(w2_ref[...], as_ref[...])
        k = lax.broadcasted_iota(jnp.int32, (z.shape[0], 1), 0)
        sign = (1 - 2 * (k & 1)).astype(F32)
        ol_ref[...] = ((z + sign * mid_ref[0:1, :]) * ((SEQ * FOURIER_DIM) ** -0.5)).astype(BF16)


def _fourier(fu, dft, need_ctx):
    bdc, bds, w2, w2c = dft
    tr = TR_FOURIER
    nj = SEQ // tr
    hn = SEQ // 2
    first = 1 if need_ctx else 0
    jrev_np = np.zeros((hn, hn), np.float32)
    jrev_np[np.arange(1, hn), hn - np.arange(1, hn)] = 1.0
    jrev = jnp.asarray(jrev_np).astype(BF16)
    full = lambda a: pl.BlockSpec(a.shape, lambda *_: (0,) * a.ndim)
    lat_tile = lambda j: jnp.maximum(j - first, 0)
    ul_spec = pl.BlockSpec((SEQ, W), lambda b, j: (1 + b, 0))
    w2_spec = pl.BlockSpec((tr, SEQ), lambda b, j: (lat_tile(j), 0))
    ol_spec = pl.BlockSpec((tr, W), lambda b, j: (b * nj + lat_tile(j), 0))
    ol_shape = jax.ShapeDtypeStruct((RL, W), BF16)
    if need_ctx:
        in_specs = [pl.BlockSpec((CTX_LEN, W), lambda b, j: (b, 0)), ul_spec, full(jrev), full(bdc), full(bds),
                    w2_spec, full(w2c)]
        out_specs = [pl.BlockSpec((CTX_LEN, W), lambda b, j: (b, 0)), ol_spec]
        out_shape = [jax.ShapeDtypeStruct((RC, W), BF16), ol_shape]
        args = (fu, fu, jrev, bdc, bds, w2, w2c)
    else:
        in_specs = [ul_spec, full(jrev), full(bdc), full(bds), w2_spec]
        out_specs = [ol_spec]
        out_shape = [ol_shape]
        args = (fu, jrev, bdc, bds, w2)
    outs = pl.pallas_call(
        functools.partial(_fourier_body, has_ctx=need_ctx),
        grid=(BATCH, nj + first),
        in_specs=in_specs,
        out_specs=out_specs,
        out_shape=out_shape,
        scratch_shapes=[pltpu.VMEM((SEQ, W), BF16), pltpu.VMEM((8, W), F32)],
        compiler_params=_params(("arbitrary", "arbitrary")),
        name="fourier_mix",
    )(*args)
    return (outs[0], outs[1]) if need_ctx else (None, outs[0])


def _retention_body(qc_ref, kc_ref, vc_ref, ql0_ref, kl0_ref, vl0_ref, ql1_ref, kl1_ref, vl1_ref, g_ref,
                    dcomb_ref, xif_ref, ztf_ref, xib_ref, ztb_ref, gf_ref, gb_ref, mbd_ref, avg_ref,
                    o_ref, os_ref, st_ref):
    j = pl.program_id(1)
    C = RET_CHUNK
    nl = SEQ // C
    lat_refs = ((ql0_ref, kl0_ref, vl0_ref), (ql1_ref, kl1_ref, vl1_ref))

    own = ((lax.broadcasted_iota(jnp.int32, (RET_GH * C, RET_GW), 0) >> (C.bit_length() - 1))
           == (lax.broadcasted_iota(jnp.int32, (RET_GH * C, RET_GW), 1) >> (RET_DK.bit_length() - 1)))

    def chunk_fwd(bb, q, k, v, r0):
        for gi in range(RET_NG):
            sl = slice(gi * RET_GW, (gi + 1) * RET_GW)
            q4, k4, v4 = q[:, sl], k[:, sl], v[:, sl]
            s_prev = st_ref[RET_NG * bb + gi]
            q4f = q4.astype(F32)
            o4 = _dot((q4f * xif_ref[:, sl]).astype(BF16), s_prev.astype(BF16))
            qstack = jnp.where(own, jnp.concatenate([q4f] * RET_GH, axis=0), 0.0).astype(BF16)
            p = (_dot_nt(qstack, k4) * dcomb_ref[gi]).astype(BF16)
            ov = jnp.where(own, _dot(p, v4), 0.0)
            intra = sum(ov[hh * C:(hh + 1) * C] for hh in range(RET_GH))
            os_ref[pl.ds(r0, C), sl] = o4 + intra
            u = _dot_tn(k4, (v4.astype(F32) * ztf_ref[:, sl]).astype(BF16))
            st_ref[RET_NG * bb + gi] = gf_ref[gi] * s_prev + mbd_ref[...] * u

    def chunk_bwd(bb, q, k, v, r0):
        for gi in range(RET_NG):
            sl = slice(gi * RET_GW, (gi + 1) * RET_GW)
            q4, k4, v4 = q[:, sl], k[:, sl], v[:, sl]
            s_prev = st_ref[RET_NG * bb + gi]
            cross = _dot((q4.astype(F32) * xib_ref[:, sl]).astype(BF16), s_prev.astype(BF16))
            os_ref[pl.ds(r0, C), sl] = os_ref[pl.ds(r0, C), sl] + cross
            u = _dot_tn(k4, (v4.astype(F32) * ztb_ref[:, sl]).astype(BF16))
            st_ref[RET_NG * bb + gi] = gb_ref[gi] * s_prev + mbd_ref[...] * u

    def scan(chunk, ctx_order, lat_index):
        st_ref[...] = jnp.zeros_like(st_ref)
        for c in ctx_order:
            for bb in range(2):
                rs = slice(bb * CTX_LEN + c * C, bb * CTX_LEN + (c + 1) * C)
                chunk(bb, qc_ref[rs, :], kc_ref[rs, :], vc_ref[rs, :], bb * CTX_LEN + c * C)

        def body(t, carry):
            r0 = pl.multiple_of(lat_index(t) * C, C)
            for bb, (q_ref, k_ref, v_ref) in enumerate(lat_refs):
                rs = pl.ds(r0, C)
                chunk(bb, q_ref[rs, :], k_ref[rs, :], v_ref[rs, :], 2 * CTX_LEN + bb * SEQ + r0)
            return carry

        lax.fori_loop(0, nl, body, 0)

    @pl.when(j == 0)
    def _():
        scan(chunk_fwd, range(CTX_LEN // C), lambda t: t)
        scan(chunk_bwd, reversed(range(CTX_LEN // C)), lambda t: nl - 1 - t)

    o = os_ref[pl.ds(pl.multiple_of(j * RET_SLAB, RET_SLAB), RET_SLAB), :]
    avg = avg_ref[...]
    oh, ol = _split(o)
    mu = _dot(oh, avg) + _dot(ol, avg)
    d = o - mu
    var = _dot((d * d).astype(BF16), avg)
    o_ref[...] = (g_ref[...].astype(F32) * d * lax.rsqrt(var + GN_EPS)).astype(BF16)


def _retention(qr, kr, vr, gr, rtabs):
    nslab = SEQ // RET_SLAB
    nj = 1 + 2 * nslab

    def out_map(b2, j):
        return (jnp.where(j == 0, b2, RC // RET_SLAB + 2 * b2 * nslab + j - 1), 0)

    ctx = pl.BlockSpec((2 * CTX_LEN, W), lambda b2, j: (b2, 0))
    lat = lambda k: pl.BlockSpec((SEQ, W), lambda b2, j: (1 + 2 * b2 + k, 0))
    full = lambda a: pl.BlockSpec(a.shape, lambda *_: (0,) * a.ndim)
    avg = jnp.asarray(np.kron(np.eye(RET_HEADS), np.full((RET_DK, RET_DK), 1.0 / RET_DK)).astype(np.float32)).astype(BF16)
    mbd = jnp.asarray(np.kron(np.eye(RET_GH), np.ones((RET_DK, RET_DK))).astype(np.float32))
    tabs = list(rtabs) + [mbd, avg]
    return pl.pallas_call(
        _retention_body,
        grid=(BATCH // 2, nj),
        in_specs=[ctx, ctx, ctx, lat(0), lat(0), lat(0), lat(1), lat(1), lat(1),
                  pl.BlockSpec((RET_SLAB, W), out_map)] + [full(t) for t in tabs],
        out_specs=pl.BlockSpec((RET_SLAB, W), out_map),
        out_shape=jax.ShapeDtypeStruct((R, W), BF16),
        scratch_shapes=[pltpu.VMEM((2 * (CTX_LEN + SEQ), W), F32), pltpu.VMEM((2 * RET_NG, RET_GW, RET_GW), F32)],
        compiler_params=_params(("arbitrary", "arbitrary")),
        name="retention",
    )(qr, kr, vr, qr, kr, vr, qr, kr, vr, gr, *tabs)


def _pack_pair(a, b):
    ua = lax.bitcast_convert_type(a, jnp.uint32) >> 16
    ub = lax.bitcast_convert_type(b, jnp.uint32) & jnp.uint32(0xFFFF0000)
    return ua | ub


def _unpack_pair(w):
    a = lax.bitcast_convert_type(w << 16, F32)
    b = lax.bitcast_convert_type(w & jnp.uint32(0xFFFF0000), F32)
    return a, b


def _slot_onehot(s0, s1, n):
    srow = lax.broadcasted_iota(jnp.int32, (n, s0.shape[1]), 0)
    p0 = jnp.where(srow == s0, 1.0, 0.0).astype(BF16)
    p1 = jnp.where(srow == s1, 1.0, 0.0).astype(BF16)
    return p0, p1


def _pick(first, a_ref, b_ref):
    a = a_ref[...]
    flag = jnp.zeros(a.shape, jnp.int32) + first.astype(jnp.int32)
    return jnp.where(flag > 0, a, b_ref[...])


def _merge_body(*refs, layer0):
    if layer0:
        oac_ref, oal_ref, ofc_ref, ofl_ref, rt_ref, gm_ref, xc_ref, xl_ref = refs[:8]
        rest = refs[8:]
        is_ctx = pl.program_id(0) < RC // TM_MERGE
        oa_in = _pick(is_ctx, oac_ref, oal_ref)
        of_in = _pick(is_ctx, ofc_ref, ofl_ref)
        x_in = _pick(is_ctx, xc_ref, xl_ref)
    else:
        oal_ref, ofl_ref, rt_ref, gm_ref, x_ref = refs[:5]
        rest = refs[5:]
        oa_in = oal_ref[...]
        of_in = ofl_ref[...]
        x_in = x_ref[...]
    (mod_ref, gn_ref, wba_ref, wbf_ref, wbr_ref, wout_ref, wrh_ref, wrl_ref, br_ref, tri_ref, ltri_ref,
     xo_ref, xs_ref, ro_ref, nch_ref) = rest
    gm = gm_ref[...].astype(F32)
    z = (gm[:, 0:D_MODEL] * _dot(oa_in, wba_ref[...])
         + gm[:, D_MODEL:2 * D_MODEL] * _dot(of_in, wbf_ref[...])
         + gm[:, 2 * D_MODEL:3 * D_MODEL] * _dot(rt_ref[...], wbr_ref[...]))
    y = _dot(z.astype(BF16), wout_ref[...])
    x = x_in + mod_ref[2:3, :] * y
    xo_ref[...] = x
    ms = jnp.mean(x * x, axis=-1, keepdims=True)
    hn = x * lax.rsqrt(ms + NORM_EPS) * gn_ref[...]
    h2 = hn * (1.0 + mod_ref[4:5, :]) + mod_ref[3:4, :]
    hh, hl = _split(h2)
    wh, wl = wrh_ref[...], wrl_ref[...]
    lg = _dot_nt(wh, hh) + _dot_nt(wh, hl) + _dot_nt(wl, hh) + br_ref[...]
    tm = lg.shape[1]
    row8 = lax.broadcasted_iota(jnp.int32, (8, tm), 0)
    lgg = lg[0:8, :]
    mg = jnp.max(lgg, axis=0, keepdims=True)
    grp = jnp.min(jnp.where(lgg == mg, row8, 8), axis=0, keepdims=True)
    pg = 1.0 / jnp.sum(jnp.exp(lgg - mg), axis=0, keepdims=True)
    lin = jnp.zeros((8, tm), F32)
    for g in range(N_GROUPS):
        lin = jnp.where(grp == g, lg[8 + 8 * g:16 + 8 * g, :], lin)
    v1 = jnp.max(lin, axis=0, keepdims=True)
    i1 = jnp.min(jnp.where(lin == v1, row8, 8), axis=0, keepdims=True)
    rest = jnp.where(row8 == i1, -jnp.inf, lin)
    v2 = jnp.max(rest, axis=0, keepdims=True)
    i2 = jnp.min(jnp.where(rest == v2, row8, 8), axis=0, keepdims=True)
    e2 = jnp.exp(v2 - v1)
    w1 = pg / (1.0 + e2)
    w2 = pg * e2 / (1.0 + e2)
    e_1 = grp * EXPERTS_PER_GROUP + i1
    e_2 = grp * EXPERTS_PER_GROUP + i2

    row32 = lax.broadcasted_iota(jnp.int32, (N_EXPERTS, tm), 0)
    oh0 = jnp.where(row32 == e_1, 1.0, 0.0)
    oh1 = jnp.where(row32 == e_2, 1.0, 0.0)
    tri = tri_ref[...]
    cum0 = _dot(oh0.astype(BF16), tri)
    cum1 = _dot(oh1.astype(BF16), tri)
    tot0 = jnp.sum(oh0, axis=1, keepdims=True)
    tot1 = jnp.sum(oh1, axis=1, keepdims=True)
    nch = ((tot0 + tot1).astype(jnp.int32) + (CHUNK - 1)) >> CHUNK_SHIFT
    nch_b = jnp.broadcast_to(nch.astype(F32), (N_EXPERTS, 128))
    nch_ref[...] = nch_b.astype(jnp.int32)
    base = CHUNK * _dot(ltri_ref[...], nch_b.astype(BF16))[:, 0:1]
    s0 = jnp.sum(oh0 * (base + cum0), axis=0, keepdims=True).astype(jnp.int32)
    s1 = jnp.sum(oh1 * (base + tot0 + cum1), axis=0, keepdims=True).astype(jnp.int32)
    p0, p1 = _slot_onehot(s0, s1, SLOTS)
    xs = _dot(p0 + p1, hh)
    xs_ref[:, 0:D_MODEL // 2] = _pack_pair(xs[:, 0:D_MODEL // 2], xs[:, D_MODEL // 2:D_MODEL])

    def wrows(w):
        hi, lo = _split(w)
        return jnp.where(row8 == 0, hi.astype(F32), jnp.where(row8 == 1, lo.astype(F32), 0.0)).astype(BF16)

    wc = _dot_nt(p0, wrows(w1)) + _dot_nt(p1, wrows(w2))
    wcol = jnp.broadcast_to(wc[:, 0:1] + wc[:, 1:2], (SLOTS, 128))
    xs_ref[:, D_MODEL // 2:XS_COLS] = lax.bitcast_convert_type(wcol, jnp.uint32)
    s0f, s1f = s0.astype(F32), s1.astype(F32)
    ro_ref[...] = jnp.where(row8 == 0, s0f, jnp.where(row8 == 1, s1f, 0.0))


def _merge(oa_c, oa_l, of_c, of_l, ret, gm, xs_in, mod_l, gnorm, wba, wbf, wbr, wout, wrh, wrl, brb, layer0):
    tm = TM_MERGE
    row0 = 0 if layer0 else RC
    rm = R - row0
    nt = rm // tm
    off = row0 // tm
    nc = RC // tm
    src = lambda n: pl.BlockSpec((tm, n), lambda i: (i + off, 0))
    dst = lambda n: pl.BlockSpec((tm, n), lambda i: (i, 0))
    ctx_rows = lambda n: pl.BlockSpec((tm, n), lambda i: (jnp.minimum(i, nc - 1), 0))
    lat_rows = lambda n: pl.BlockSpec((tm, n), lambda i: (jnp.maximum(i - nc, 0), 0))
    full = lambda a: pl.BlockSpec(a.shape, lambda *_: (0,) * a.ndim, pipeline_mode=pl.Buffered(1))
    mrow = _mod_row(row0, tm)
    tri = jnp.asarray(np.triu(np.ones((tm, tm), np.float32), 1)).astype(BF16)
    ltri = jnp.asarray(np.tril(np.ones((N_EXPERTS, N_EXPERTS), np.float32), -1)).astype(BF16)
    if layer0:
        acts = [oa_c, oa_l, of_c, of_l, ret, gm, xs_in[0], xs_in[1]]
        act_specs = [ctx_rows(W), lat_rows(W), ctx_rows(W), lat_rows(W), src(W), src(3 * D_MODEL),
                     ctx_rows(D_MODEL), lat_rows(D_MODEL)]
    else:
        acts = [oa_l, of_l, ret, gm, xs_in]
        act_specs = [dst(W), dst(W), src(W), src(3 * D_MODEL), src(D_MODEL)]
    return pl.pallas_call(
        functools.partial(_merge_body, layer0=layer0),
        grid=(nt,),
        in_specs=act_specs + [
                  pl.BlockSpec((None, 6, D_MODEL), lambda i: (mrow(i), 0, 0)),
                  full(gnorm), full(wba), full(wbf), full(wbr), full(wout), full(wrh), full(wrl), full(brb),
                  full(tri), full(ltri)],
        out_specs=[dst(D_MODEL), pl.BlockSpec((SLOTS, XS_COLS), lambda i: (i, 0)),
                   pl.BlockSpec((8, tm), lambda i: (0, i)),
                   pl.BlockSpec((None, N_EXPERTS, 128), lambda i: (i, 0, 0))],
        out_shape=[jax.ShapeDtypeStruct((rm, D_MODEL), F32),
                   jax.ShapeDtypeStruct((nt * SLOTS, XS_COLS), jnp.uint32),
                   jax.ShapeDtypeStruct((8, rm), F32),
                   jax.ShapeDtypeStruct((nt, N_EXPERTS, 128), jnp.int32)],
        compiler_params=_params(("arbitrary",)),
        name="merge_router",
    )(*acts, mod_l, gnorm, wba, wbf, wbr, wout, wrh, wrl, brb, tri, ltri)


def _moe_plan(nch, nb):
    nt = nch.shape[0]
    choff = jnp.cumsum(nch, axis=1) - nch
    used_ch = jnp.sum(nch, axis=1)
    cum_t = jnp.cumsum(nch, axis=0)
    tot = cum_t[-1]
    ptot = (tot + CPB - 1) // CPB * CPB
    pend = jnp.cumsum(ptot)
    pstart = pend - ptot
    n_used = pend[-1] // CPB
    blk = jnp.arange(nb, dtype=jnp.int32)
    lane = jnp.arange(CPB, dtype=jnp.int32)
    blk_e = jnp.minimum(jnp.sum((blk[:, None] * CPB >= pend[None, :]).astype(jnp.int32), axis=1), N_EXPERTS - 1)
    oe = (blk_e[:, None] == jnp.arange(N_EXPERTS, dtype=jnp.int32)[None, :]).astype(jnp.int32)
    sel = lambda tab: jnp.sum(oe[:, :, None] * tab.T[None, :, :], axis=1)
    pstart_b = jnp.sum(oe * pstart[None, :], axis=1)
    tot_b = jnp.sum(oe * tot[None, :], axis=1)
    cum_b, nch_b, choff_b = sel(cum_t), sel(nch), sel(choff)
    i = blk[:, None] * CPB + lane[None, :] - pstart_b[:, None]
    valid = (i < tot_b[:, None]) & (blk[:, None] < n_used)
    t = jnp.minimum(jnp.sum((i[:, :, None] >= cum_b[:, None, :]).astype(jnp.int32), axis=2), nt - 1)
    tiles = jnp.arange(nt, dtype=jnp.int32)[None, None, :]
    before = jnp.sum(jnp.where(tiles < t[:, :, None], nch_b[:, None, :], 0), axis=2)
    coff = jnp.sum(jnp.where(tiles == t[:, :, None], choff_b[:, None, :], 0), axis=2)
    row = t * SLOTS + CHUNK * (coff + i - before)
    src = jnp.where(valid, row, SLOTS - CHUNK)
    dummy = nt * SLOTS + CHUNK * ((blk[:, None] % 2) * CPB + lane[None, :])
    dst = jnp.where(valid, row, dummy)
    blk_start = jnp.concatenate([pstart, pend[-1:]]) // CPB
    return (blk_start.astype(jnp.int32), n_used.astype(jnp.int32).reshape(1), src.reshape(-1).astype(jnp.int32),
            dst.reshape(-1).astype(jnp.int32), used_ch.astype(jnp.int32))


def _ffn_body(bs_ref, nu_ref, src_ref, dst_ref, uc_ref, xs_ref, wg_ref, wu_ref, wd_ref, ys_ref,
              xbuf, ybuf, zbuf, wgb, wub, wdb, sem_in, sem_out, sem_z, *, nt):
    e = pl.program_id(0)
    nu = nu_ref[0]
    half = D_MODEL // 2

    def gather(blk, sl):
        for c in range(CPB):
            r = pl.multiple_of(src_ref[blk * CPB + c], CHUNK)
            pltpu.make_async_copy(xs_ref.at[pl.ds(r, CHUNK)], xbuf.at[sl, pl.ds(c * CHUNK, CHUNK)],
                                  sem_in.at[sl]).start()

    def scatter(blk, sl):
        for c in range(CPB):
            r = pl.multiple_of(dst_ref[blk * CPB + c], CHUNK)
            pltpu.make_async_copy(ybuf.at[sl, pl.ds(c * CHUNK, CHUNK)], ys_ref.at[pl.ds(r, CHUNK)],
                                  sem_out.at[sl]).start()

    def wait_gather(sl):
        pltpu.make_async_copy(xs_ref.at[pl.ds(0, MOE_BM)], xbuf.at[sl], sem_in.at[sl]).wait()

    def wait_scatter(sl):
        pltpu.make_async_copy(ybuf.at[sl], ys_ref.at[pl.ds(0, MOE_BM)], sem_out.at[sl]).wait()

    def zero_copy(r):
        return pltpu.make_async_copy(zbuf, ys_ref.at[pl.ds(pl.multiple_of(r, CHUNK), CHUNK)], sem_z)

    @pl.when(e == 0)
    def _():
        zbuf[...] = jnp.zeros_like(zbuf)

        def tails(fn):
            def per_tile(t, carry):
                def per_chunk(c, carry2):
                    fn(t * SLOTS + c * CHUNK)
                    return carry2
                lax.fori_loop(uc_ref[t], SLOTS // CHUNK, per_chunk, 0)
                return carry
            lax.fori_loop(0, nt, per_tile, 0)
            for c in range(2 * CPB):
                fn(nt * SLOTS + c * CHUNK)

        tails(lambda r: zero_copy(r).start())
        gather(0, 0)
        for d in range(1, GATHER_DEPTH - 1):
            @pl.when(nu > d)
            def _():
                gather(d, d)
        tails(lambda r: zero_copy(r).wait())

    b0, b1 = bs_ref[e], bs_ref[e + 1]

    @pl.when(b1 > b0)
    def _():
        wgb[...] = wg_ref[...].astype(BF16)
        wub[...] = wu_ref[...].astype(BF16)
        wdb[...] = wd_ref[...].astype(BF16)

        def block(b, carry):
            slot = b % 2
            xslot = lax.rem(b, GATHER_DEPTH)

            @pl.when(b + GATHER_DEPTH - 1 < nu)
            def _():
                gather(b + GATHER_DEPTH - 1, lax.rem(b + GATHER_DEPTH - 1, GATHER_DEPTH))

            wait_gather(xslot)

            @pl.when(b >= 2)
            def _():
                wait_scatter(slot)

            xw = xbuf[xslot]
            xa, xb = _unpack_pair(xw[:, 0:half])
            x = jnp.concatenate([xa, xb], axis=1).astype(BF16)
            wt = lax.bitcast_convert_type(xw[:, half:XS_COLS], F32)
            g = _dot(x, wgb[...])
            u = _dot(x, wub[...])
            hmid = (g * _sigmoid(g) * u).astype(BF16)
            y = _dot(hmid, wdb[...]) * jnp.concatenate([wt] * (D_MODEL // 128), axis=1)
            yb = y.astype(BF16).astype(F32)
            ybuf[slot] = _pack_pair(yb[:, 0:half], yb[:, half:D_MODEL])
            scatter(b, slot)
            return carry

        lax.fori_loop(b0, b1, block, 0)

    @pl.when(e == N_EXPERTS - 1)
    def _():
        wait_scatter((nu - 1) % 2)

        @pl.when(nu >= 2)
        def _():
            wait_scatter(nu % 2)


def _ffn(plan, xs, w_g, w_u, w_d, layer, nt):
    wmap = lambda e, *_: (layer, e, 0, 0)
    half = D_MODEL // 2
    grid_spec = pltpu.PrefetchScalarGridSpec(
        num_scalar_prefetch=5,
        grid=(N_EXPERTS,),
        in_specs=[pl.BlockSpec(memory_space=pl.ANY),
                  pl.BlockSpec((None, None, D_MODEL, EXPERT_HIDDEN), wmap),
                  pl.BlockSpec((None, None, D_MODEL, EXPERT_HIDDEN), wmap),
                  pl.BlockSpec((None, None, EXPERT_HIDDEN, D_MODEL), wmap)],
        out_specs=pl.BlockSpec(memory_space=pl.ANY),
        scratch_shapes=[pltpu.VMEM((GATHER_DEPTH, MOE_BM, XS_COLS), jnp.uint32),
                        pltpu.VMEM((2, MOE_BM, half), jnp.uint32),
                        pltpu.VMEM((CHUNK, half), jnp.uint32),
                        pltpu.VMEM((D_MODEL, EXPERT_HIDDEN), BF16), pltpu.VMEM((D_MODEL, EXPERT_HIDDEN), BF16),
                        pltpu.VMEM((EXPERT_HIDDEN, D_MODEL), BF16),
                        pltpu.SemaphoreType.DMA((GATHER_DEPTH,)), pltpu.SemaphoreType.DMA((2,)),
                        pltpu.SemaphoreType.DMA(())],
    )
    return pl.pallas_call(
        functools.partial(_ffn_body, nt=nt),
        grid_spec=grid_spec,
        out_shape=jax.ShapeDtypeStruct((nt * SLOTS + 2 * CPB * CHUNK, half), jnp.uint32),
        compiler_params=_params(("arbitrary",)),
        name="moe_experts",
    )(*plan, xs, w_g, w_u, w_d)


def _combine_body(ys_ref, ro_ref, x_ref, mod_ref, gn_ref, o_ref, *, final):
    tm = TM_MERGE
    for t in range(COMBINE_TILES):
        cols = slice(t * tm, (t + 1) * tm)
        s = ro_ref[:, cols]
        p0, p1 = _slot_onehot(s[0:1, :].astype(jnp.int32), s[1:2, :].astype(jnp.int32), SLOTS)
        ya, yb = _unpack_pair(ys_ref[t * SLOTS:(t + 1) * SLOTS, :])
        y = jnp.concatenate([ya, yb], axis=1).astype(BF16)
        f = _dot_tn(p0 + p1, y)
        x = x_ref[cols, :] + mod_ref[5:6, :] * f
        if final:
            ms = jnp.mean(x * x, axis=-1, keepdims=True)
            x = x * lax.rsqrt(ms + NORM_EPS) * gn_ref[...]
        o_ref[cols, :] = x


def _combine(ys, route, x, mod_l, gnorm, row0, final):
    tm = COMBINE_TILES * TM_MERGE
    rm = x.shape[0]
    mrow = _mod_row(row0, tm)
    return pl.pallas_call(
        functools.partial(_combine_body, final=final),
        grid=(rm // tm,),
        in_specs=[pl.BlockSpec((COMBINE_TILES * SLOTS, D_MODEL // 2), lambda i: (i, 0)),
                  pl.BlockSpec((8, tm), lambda i: (0, i)),
                  pl.BlockSpec((tm, D_MODEL), lambda i: (i, 0)),
                  pl.BlockSpec((None, 6, D_MODEL), lambda i: (mrow(i), 0, 0)),
                  pl.BlockSpec((1, D_MODEL), lambda i: (0, 0))],
        out_specs=pl.BlockSpec((tm, D_MODEL), lambda i: (i, 0)),
        out_shape=jax.ShapeDtypeStruct((rm, D_MODEL), F32),
        compiler_params=_params(("arbitrary",)),
        name="moe_combine",
    )(ys, route, x, mod_l, gnorm)


def _moe(xs, route, nch3, x, mod_l, gnorm, w_g, w_u, w_d, row0, final, layer):
    nt = nch3.shape[0]
    max_chunks = nt * ((2 * TM_MERGE + N_EXPERTS * (CHUNK - 1)) // CHUNK)
    nb = -(-max_chunks // CPB) + N_EXPERTS
    plan = _moe_plan(nch3[:, :, 0], nb)
    ys = _ffn(plan, xs, w_g, w_u, w_d, layer, nt)
    return _combine(ys, route, x, mod_l, gnorm, row0, final)


def kernel(x, c, ctx, c_ctx, norm_mix, norm_ffn, w_ada, b_ada, w_in, attn_sink, ret_decay_fwd, ret_decay_bwd,
           w_branch_attn, w_branch_fourier, w_branch_ret, w_out, w_router_group, b_router_group,
           w_router_expert, b_router_expert, w_exp_gate, w_exp_up, w_exp_down, norm_final):
    assert DEPTH == 2, "layer 0 reads (ctx, x) separately and keeps its context rows; the last layer does not"
    tabs = [jnp.asarray(t) for t in _rope_tables()]
    dft = [jnp.asarray(t).astype(BF16) for t in _dft_tables()]

    cc = jnp.zeros((MOD_ROWS, D_MODEL), F32).at[0:BATCH].set(c).at[CTX_MOD_ROW].set(c_ctx)
    mod = _ada(cc, w_ada, b_ada).reshape(DEPTH, MOD_ROWS, 6, D_MODEL)

    xf = (ctx.reshape(RC, D_MODEL), x.reshape(RL, D_MODEL))
    w_in_bf = w_in.astype(BF16)
    for l in range(DEPTH):
        need_ctx = l < DEPTH - 1
        row0 = 0 if need_ctx else RC
        mod_l = mod[l]
        qa, ka, va, qr, kr, vr, gr, fu, gm = _proj(xf, mod_l, norm_mix[l][None, :], w_in_bf, tabs, l)
        oa_c, oa_l = _attention(attn_sink[l], qa, ka, va, need_ctx)
        of_c, of_l = _fourier(fu, dft, need_ctx)
        ret = _retention(qr, kr, vr, gr, _retention_tables(ret_decay_fwd[l], ret_decay_bwd[l]))
        wr = jnp.zeros((ROUTER_ROWS, D_MODEL), F32)
        wr = wr.at[0:N_GROUPS].set(w_router_group[l].T).at[8:8 + N_EXPERTS].set(w_router_expert[l].T)
        br = jnp.full((ROUTER_ROWS,), NEG, F32)
        br = br.at[0:N_GROUPS].set(b_router_group[l]).at[8:8 + N_EXPERTS].set(b_router_expert[l])
        wrh, wrl = _split(wr)
        brb = jnp.broadcast_to(br[:, None], (ROUTER_ROWS, TM_MERGE))
        x_mid, xs, route, nch3 = _merge(oa_c, oa_l, of_c, of_l, ret, gm, xf, mod_l, norm_ffn[l][None, :],
                                        w_branch_attn[l].astype(BF16), w_branch_fourier[l].astype(BF16),
                                        w_branch_ret[l].astype(BF16), w_out[l].astype(BF16), wrh, wrl, brb,
                                        need_ctx)
        final = l == DEPTH - 1
        xf = _moe(xs, route, nch3, x_mid, mod_l, norm_final[None, :], w_exp_gate, w_exp_up, w_exp_down,
                  row0, final, l)
    return xf.reshape(BATCH, SEQ, D_MODEL)
```
